```python
import math
import jax
import jax.numpy as jnp
from jax import lax
import numpy as np

D_MODEL = 2048
BATCH = 1
SEQ = 8192
DEPTH = 2
DEC_BATCH = 128
DEC_SEQ = 8
PAST_LEN = 2048
PAGE_SIZE = 128

N_HEADS = 16
HEAD_DIM = D_MODEL // N_HEADS
N_KV_HEADS = 4
MOBA_BLOCK = 256
MOBA_TOPK = 3
Q_CHUNK = 16
NUM_BUCKETS = 32
MAX_DISTANCE = 128
SSM_GROUP = 16
N_SSM_GROUPS = D_MODEL // SSM_GROUP
SSM_STATE = 64
SCAN_CHUNK = 512
D_FF = 7 * D_MODEL // 2
N_EXPERTS = 8
TOP_K = 2
N_EVEN = (DEPTH + 1) // 2
N_ODD = DEPTH // 2
ALPHA = (2 * DEPTH) ** 0.25
BETA = (8 * DEPTH) ** -0.25
LN_EPS = 1e-5
NEG_INF = -1e30

kernel_name = 'moba_s5_hybrid_decode_step'


def layer_norm(x, g, b):
    xf = x.astype(jnp.float32)
    xc = xf - xf.mean(-1, keepdims=True)
    var = jnp.mean(xc * xc, -1, keepdims=True)
    return (xc * lax.rsqrt(var + LN_EPS) * g.astype(jnp.float32) + b.astype(jnp.float32)).astype(x.dtype)


def t5_bucket(dist):
    n = jnp.maximum(dist, 0)
    max_exact = NUM_BUCKETS // 2
    nf = jnp.maximum(n, 1).astype(jnp.float32)
    large = max_exact + (jnp.log(nf / max_exact) / math.log(MAX_DISTANCE / max_exact)
                         * (NUM_BUCKETS - max_exact)).astype(jnp.int32)
    large = jnp.minimum(large, NUM_BUCKETS - 1)
    return jnp.where(n < max_exact, n, large)


def moba_core(q, k, v, q_pos, rel_bias):
    L = k.shape[0]
    nb = max(-(-L // MOBA_BLOCK), MOBA_TOPK)
    pad = nb * MOBA_BLOCK - L
    k = jnp.pad(k, ((0, pad), (0, 0), (0, 0)))
    v = jnp.pad(v, ((0, pad), (0, 0), (0, 0)))
    kb = k.reshape(nb, MOBA_BLOCK, N_KV_HEADS, HEAD_DIM).transpose(2, 0, 1, 3)
    vb = v.reshape(nb, MOBA_BLOCK, N_KV_HEADS, HEAD_DIM).transpose(2, 0, 1, 3)
    head_kv = jnp.arange(N_HEADS) // (N_HEADS // N_KV_HEADS)
    kmean = kb.astype(jnp.float32).mean(axis=2)[head_kv]
    scale = HEAD_DIM ** -0.5
    bias_f = rel_bias.astype(jnp.float32)
    tq = q.shape[0]
    qc = math.gcd(tq, Q_CHUNK)

    def chunk(args):
        qch, pos = args
        c = qch.shape[0]
        own = pos // MOBA_BLOCK
        gate = jnp.einsum('chd,hnd->chn', qch.astype(jnp.float32), kmean)
        is_past = jnp.arange(nb)[None, None, :] < own[:, None, None]
        gate = jnp.where(is_past, gate, NEG_INF)
        _, top = lax.top_k(gate, MOBA_TOPK)
        own_b = jnp.broadcast_to(own[:, None, None], (c, N_HEADS, 1)).astype(top.dtype)
        blocks = jnp.concatenate([top, own_b], axis=-1)
        kg = kb[head_kv[None, :, None], blocks]
        vg = vb[head_kv[None, :, None], blocks]
        kpos = blocks[..., None] * MOBA_BLOCK + jnp.arange(MOBA_BLOCK)
        dist = pos[:, None, None, None] - kpos
        rank = jnp.arange(MOBA_TOPK + 1)[None, None, :, None]
        valid = jnp.where(rank < MOBA_TOPK, rank < own[:, None, None, None], dist >= 0)
        logits = jnp.einsum('chd,chsjd->chsj', qch, kg).astype(jnp.float32) * scale
        logits = logits + bias_f[jnp.arange(N_HEADS)[None, :, None, None], t5_bucket(dist)]
        logits = jnp.where(valid, logits, NEG_INF)
        p = jax.nn.softmax(logits.reshape(c, N_HEADS, -1), axis=-1).reshape(logits.shape)
        return jnp.einsum('chsj,chsjd->chd', p.astype(vg.dtype), vg)

    out = lax.map(chunk, (q.reshape(tq // qc, qc, N_HEADS, HEAD_DIM), q_pos.reshape(tq // qc, qc)))
    return out.reshape(tq, N_HEADS, HEAD_DIM)


def qkv_split(x, w_qkv):
    b, t, _ = x.shape
    dq = N_HEADS * HEAD_DIM
    dkv = N_KV_HEADS * HEAD_DIM
    qkv = x @ w_qkv
    q = qkv[..., :dq].reshape(b, t, N_HEADS, HEAD_DIM)
    k = qkv[..., dq:dq + dkv].reshape(b, t, N_KV_HEADS, HEAD_DIM)
    v = qkv[..., dq + dkv:].reshape(b, t, N_KV_HEADS, HEAD_DIM)
    return q, k, v


def moba_prompt(x, w_qkv, w_o, rel_bias):
    b, t, _ = x.shape
    q, k, v = qkv_split(x, w_qkv)
    pos = jnp.arange(t, dtype=jnp.int32)
    o = lax.map(lambda a: moba_core(a[0], a[1], a[2], pos, rel_bias), (q, k, v))
    return o.reshape(b, t, -1) @ w_o, k, v


def moba_sample(x, cache_k, cache_v, page_table, w_qkv, w_o, rel_bias):
    b, s, _ = x.shape
    q, k, v = qkv_split(x, w_qkv)
    past_len = page_table.shape[1] * cache_k.shape[1]
    pos = past_len + jnp.arange(s, dtype=jnp.int32)

    def one(a):
        pt, q1, k1, v1 = a
        kp = cache_k[pt].reshape(past_len, N_KV_HEADS, HEAD_DIM).astype(k1.dtype)
        vp = cache_v[pt].reshape(past_len, N_KV_HEADS, HEAD_DIM).astype(v1.dtype)
        return moba_core(q1, jnp.concatenate([kp, k1], 0), jnp.concatenate([vp, v1], 0), pos, rel_bias)

    o = lax.map(one, (page_table, q, k, v))
    return o.reshape(b, s, -1) @ w_o, k, v


def s5_mixer(x, h0_re, h0_im, w_in, a_re, a_im, log_dt, b_re, b_im, c_re, c_im, d_skip, w_glu_v, w_glu_g):
    f32 = jnp.float32
    bsz, t, _ = x.shape
    u = (x @ w_in).astype(f32).reshape(bsz, t, N_SSM_GROUPS, SSM_GROUP)
    lam_re = jnp.minimum(a_re.astype(f32), -1e-4)
    lam_im = a_im.astype(f32)
    dt = jnp.exp(log_dt.astype(f32))[:, None]
    decay = jnp.exp(lam_re * dt)
    abar_re = decay * jnp.cos(lam_im * dt)
    abar_im = decay * jnp.sin(lam_im * dt)
    num_re = abar_re - 1.0
    den = lam_re * lam_re + lam_im * lam_im
    f_re = (num_re * lam_re + abar_im * lam_im) / den
    f_im = (abar_im * lam_re - num_re * lam_im) / den
    b_re = b_re.astype(f32)
    b_im = b_im.astype(f32)
    bbar_re = f_re[..., None] * b_re - f_im[..., None] * b_im
    bbar_im = f_re[..., None] * b_im + f_im[..., None] * b_re
    c_re = c_re.astype(f32)
    c_im = c_im.astype(f32)
    chunk = math.gcd(t, SCAN_CHUNK)
    u_c = u.reshape(bsz, t // chunk, chunk, N_SSM_GROUPS, SSM_GROUP).transpose(1, 2, 0, 3, 4)

    def combine(e1, e2):
        a1r, a1i, b1r, b1i = e1
        a2r, a2i, b2r, b2i = e2
        return (a2r * a1r - a2i * a1i, a2r * a1i + a2i * a1r,
                a2r * b1r - a2i * b1i + b2r, a2r * b1i + a2i * b1r + b2i)

    def step(h, uc):
        h_re, h_im = h
        bu_re = jnp.einsum('tbgc,gpc->tbgp', uc, bbar_re)
        bu_im = jnp.einsum('tbgc,gpc->tbgp', uc, bbar_im)
        a_r = jnp.broadcast_to(abar_re, bu_re.shape)
        a_i = jnp.broadcast_to(abar_im, bu_re.shape)
        cr, ci, sr, si = lax.associative_scan(combine, (a_r, a_i, bu_re, bu_im), axis=0)
        s_re = cr * h_re - ci * h_im + sr
        s_im = cr * h_im + ci * h_re + si
        y = jnp.einsum('tbgp,gcp->tbgc', s_re, c_re) - jnp.einsum('tbgp,gcp->tbgc', s_im, c_im)
        return (s_re[-1], s_im[-1]), y

    (h_re, h_im), y = lax.scan(step, (h0_re.astype(f32), h0_im.astype(f32)), u_c)
    y = y.transpose(2, 0, 1, 3, 4).reshape(bsz, t, N_SSM_GROUPS, SSM_GROUP) + d_skip.astype(f32) * u
    y = jax.nn.gelu(y.reshape(bsz, t, D_MODEL)).astype(x.dtype)
    z = (y @ w_glu_v) * jax.nn.sigmoid(y @ w_glu_g)
    return z, h_re, h_im


def swiglu(x, w_gate, w_up, w_down):
    return (jax.nn.silu(x @ w_gate) * (x @ w_up)) @ w_down


def moe_swiglu(x, w_router, w_gate, w_up, w_down):
    logits = (x @ w_router).astype(jnp.float32)
    top_logit, top_idx = lax.top_k(logits, TOP_K)
    gates = jax.nn.softmax(top_logit, axis=-1)
    comb = jnp.sum(jax.nn.one_hot(top_idx, N_EXPERTS, dtype=jnp.float32) * gates[..., None], axis=-2).astype(x.dtype)
    out = jnp.zeros_like(x)
    for e in range(N_EXPERTS):
        out = out + comb[..., e:e + 1] * swiglu(x, w_gate[e], w_up[e], w_down[e])
    return out


def setup_inputs(seed: int = 0) -> dict:
    key = jax.random.key(seed)
    keys = jax.random.split(key, 32)

    def nrm(i, shape, scale):
        return jax.random.normal(keys[i], shape, jnp.float32) * scale

    n_pages = PAST_LEN // PAGE_SIZE
    n_used = DEC_BATCH * n_pages
    n_pool = n_used + max(1, n_used // 4)
    page_table = jax.random.permutation(keys[0], n_pool)[:n_used].reshape(DEC_BATCH, n_pages).astype(jnp.int32)
    dq = N_HEADS * HEAD_DIM
    dkv = N_KV_HEADS * HEAD_DIM
    inv_d = D_MODEL ** -0.5
    w_qkv = jnp.concatenate([nrm(1, (N_EVEN, D_MODEL, dq), inv_d),
                             nrm(2, (N_EVEN, D_MODEL, dkv), inv_d),
                             nrm(3, (N_EVEN, D_MODEL, dkv), inv_d * BETA)], axis=-1)
    n_state = jnp.arange(SSM_STATE, dtype=jnp.float32)
    ssm_shape = (N_ODD, N_SSM_GROUPS, SSM_STATE)
    return {
        'x_prompt': nrm(4, (BATCH, SEQ, D_MODEL), 1.0),
        'x_sample': nrm(5, (DEC_BATCH, DEC_SEQ, D_MODEL), 1.0),
        'cache_k': nrm(6, (N_EVEN, n_pool, PAGE_SIZE, N_KV_HEADS, HEAD_DIM), 1.0),
        'cache_v': nrm(7, (N_EVEN, n_pool, PAGE_SIZE, N_KV_HEADS, HEAD_DIM), BETA),
        'state_ssm_re': nrm(8, (N_ODD, DEC_BATCH, N_SSM_GROUPS, SSM_STATE), 0.5),
        'state_ssm_im': nrm(9, (N_ODD, DEC_BATCH, N_SSM_GROUPS, SSM_STATE), 0.5),
        'page_table': page_table,
        'rel_bias': nrm(10, (N_HEADS, NUM_BUCKETS), 0.5),
        'ln_g': 1.0 + nrm(11, (DEPTH, 2, D_MODEL), 0.02),
        'ln_b': nrm(12, (DEPTH, 2, D_MODEL), 0.02),
        'w_qkv': w_qkv,
        'w_o': nrm(13, (N_EVEN, dq, D_MODEL), dq ** -0.5 * BETA),
        'w_ssm_in': nrm(14, (N_ODD, D_MODEL, D_MODEL), inv_d),
        'ssm_a_re': -0.5 + nrm(15, ssm_shape, 0.01),
        'ssm_a_im': math.pi * n_state + nrm(16, ssm_shape, 0.01),
        'ssm_log_dt': jax.random.uniform(keys[17], (N_ODD, N_SSM_GROUPS), jnp.float32, math.log(1e-3), math.log(1e-1)),
        'ssm_b_re': nrm(18, (N_ODD, N_SSM_GROUPS, SSM_STATE, SSM_GROUP), (2 * SSM_GROUP) ** -0.5),
        'ssm_b_im': nrm(19, (N_ODD, N_SSM_GROUPS, SSM_STATE, SSM_GROUP), (2 * SSM_GROUP) ** -0.5),
        'ssm_c_re': nrm(20, (N_ODD, N_SSM_GROUPS, SSM_GROUP, SSM_STATE), (2 * SSM_STATE) ** -0.5),
        'ssm_c_im': nrm(21, (N_ODD, N_SSM_GROUPS, SSM_GROUP, SSM_STATE), (2 * SSM_STATE) ** -0.5),
        'ssm_d': nrm(22, (N_ODD, N_SSM_GROUPS, SSM_GROUP), 1.0),
        'w_glu_v': nrm(23, (N_ODD, D_MODEL, D_MODEL), inv_d * BETA),
        'w_glu_g': nrm(24, (N_ODD, D_MODEL, D_MODEL), inv_d),
        'w_ff_gate': nrm(25, (N_EVEN, D_MODEL, D_FF), inv_d),
        'w_ff_up': nrm(26, (N_EVEN, D_MODEL, D_FF), inv_d),
        'w_ff_down': nrm(27, (N_EVEN, D_FF, D_MODEL), D_FF ** -0.5 * BETA),
        'w_router': nrm(28, (N_ODD, D_MODEL, N_EXPERTS), inv_d),
        'w_moe_gate': nrm(29, (N_ODD, N_EXPERTS, D_MODEL, D_FF), inv_d),
        'w_moe_up': nrm(30, (N_ODD, N_EXPERTS, D_MODEL, D_FF), inv_d),
        'w_moe_down': nrm(31, (N_ODD, N_EXPERTS, D_FF, D_MODEL), D_FF ** -0.5 * BETA),
    }


def reference(x_prompt, x_sample, cache_k, cache_v, state_ssm_re, state_ssm_im, page_table,
              rel_bias, ln_g, ln_b, w_qkv, w_o,
              w_ssm_in, ssm_a_re, ssm_a_im, ssm_log_dt, ssm_b_re, ssm_b_im, ssm_c_re, ssm_c_im, ssm_d,
              w_glu_v, w_glu_g, w_ff_gate, w_ff_up, w_ff_down,
              w_router, w_moe_gate, w_moe_up, w_moe_down):
    yp, ys = x_prompt, x_sample
    kp_l, vp_l, ks_l, vs_l = [], [], [], []
    hrp_l, hip_l, hrs_l, his_l = [], [], [], []
    for i in range(DEPTH):
        j = i // 2
        if i % 2 == 0:
            mp, kp, vp = moba_prompt(yp, w_qkv[j], w_o[j], rel_bias)
            ms, ks, vs = moba_sample(ys, cache_k[j], cache_v[j], page_table, w_qkv[j], w_o[j], rel_bias)
            kp_l.append(kp)
            vp_l.append(vp)
            ks_l.append(ks)
            vs_l.append(vs)
        else:
            zero = jnp.zeros((yp.shape[0], N_SSM_GROUPS, SSM_STATE), jnp.float32)
            mp, hrp, hip = s5_mixer(yp, zero, zero, w_ssm_in[j], ssm_a_re[j], ssm_a_im[j], ssm_log_dt[j],
                                    ssm_b_re[j], ssm_b_im[j], ssm_c_re[j], ssm_c_im[j], ssm_d[j],
                                    w_glu_v[j], w_glu_g[j])
            ms, hrs, his = s5_mixer(ys, state_ssm_re[j], state_ssm_im[j], w_ssm_in[j], ssm_a_re[j], ssm_a_im[j],
                                    ssm_log_dt[j], ssm_b_re[j], ssm_b_im[j], ssm_c_re[j], ssm_c_im[j], ssm_d[j],
                                    w_glu_v[j], w_glu_g[j])
            hrp_l.append(hrp)
            hip_l.append(hip)
            hrs_l.append(hrs)
            his_l.append(his)
        yp = layer_norm(ALPHA * yp + mp, ln_g[i, 0], ln_b[i, 0])
        ys = layer_norm(ALPHA * ys + ms, ln_g[i, 0], ln_b[i, 0])
        if i % 2 == 0:
            fp = swiglu(yp, w_ff_gate[j], w_ff_up[j], w_ff_down[j])
            fs = swiglu(ys, w_ff_gate[j], w_ff_up[j], w_ff_down[j])
        else:
            fp = moe_swiglu(yp, w_router[j], w_moe_gate[j], w_moe_up[j], w_moe_down[j])
            fs = moe_swiglu(ys, w_router[j], w_moe_gate[j], w_moe_up[j], w_moe_down[j])
        yp = layer_norm(ALPHA * yp + fp, ln_g[i, 1], ln_b[i, 1])
        ys = layer_norm(ALPHA * ys + fs, ln_g[i, 1], ln_b[i, 1])
    return (yp, ys, jnp.stack(kp_l), jnp.stack(vp_l), jnp.stack(ks_l), jnp.stack(vs_l),
            jnp.stack(hrp_l), jnp.stack(hip_l), jnp.stack(hrs_l), jnp.stack(his_l))
```

```python
import functools
import math

import numpy as np
import jax
import jax.numpy as jnp
from jax import lax
from jax.experimental import pallas as pl
from jax.experimental.pallas import tpu as pltpu

D_MODEL = 2048
N_HEADS = 16
HEAD_DIM = D_MODEL // N_HEADS
N_KV_HEADS = 4
HEADS_PER_KV = N_HEADS // N_KV_HEADS
MOBA_BLOCK = 256
MOBA_TOPK = 3
NUM_BUCKETS = 32
MAX_DISTANCE = 128
SSM_GROUP = 16
N_SSM_GROUPS = D_MODEL // SSM_GROUP
SSM_STATE = 64
SSM_CHUNK = 8
D_FF = 7 * D_MODEL // 2
N_EXPERTS = 8
TOP_K = 2
DEPTH = 2
ALPHA = (2 * DEPTH) ** 0.25
LN_EPS = 1e-5
NEG_INF = -1e30

LANES = 128
SUBLANES = 8
VMEM_LIMIT = 56 * 1024 * 1024

F32 = jnp.float32
BF16 = jnp.bfloat16
HIGHEST = lax.Precision.HIGHEST
_NT = (((1,), (1,)), ((), ()))


def _params(*sem):
    return pltpu.CompilerParams(dimension_semantics=sem, vmem_limit_bytes=VMEM_LIMIT)


def _dot(a, b):
    return jnp.dot(a, b, preferred_element_type=F32)


def _sigmoid(x):
    return 1.0 / (1.0 + jnp.exp(-x))


def _layer_norm(y, g, b):
    mean = jnp.mean(y, axis=-1, keepdims=True)
    yc = y - mean
    var = jnp.mean(yc * yc, axis=-1, keepdims=True)
    return yc * lax.rsqrt(var + LN_EPS) * g + b


def _mm_body(x_ref, w_ref, o_ref):
    o_ref[...] = _dot(x_ref[...], w_ref[...])


def _matmul(x, w, tm, tn):
    m, k = x.shape
    n = w.shape[1]
    return pl.pallas_call(
        _mm_body,
        grid=(m // tm, n // tn),
        in_specs=[pl.BlockSpec((tm, k), lambda i, j: (i, 0)),
                  pl.BlockSpec((k, tn), lambda i, j: (0, j))],
        out_specs=pl.BlockSpec((tm, tn), lambda i, j: (i, j)),
        out_shape=jax.ShapeDtypeStruct((m, n), F32),
        compiler_params=_params("parallel", "parallel"),
        name="matmul",
    )(x, w)


def _qkv_body(x_ref, w_ref, of_ref, ob_ref, *, nq_tiles, scale):
    acc = _dot(x_ref[...], w_ref[...])
    of_ref[...] = acc
    s = jnp.where(pl.program_id(1) < nq_tiles, scale, 1.0).astype(F32)
    ob_ref[...] = (acc * s).astype(BF16)


def _qkv_proj(x, w, tm, tn):
    m, k = x.shape
    n = w.shape[1]
    body = functools.partial(_qkv_body, nq_tiles=(N_HEADS * HEAD_DIM) // tn, scale=HEAD_DIM ** -0.5)
    return pl.pallas_call(
        body,
        grid=(m // tm, n // tn),
        in_specs=[pl.BlockSpec((tm, k), lambda i, j: (i, 0)),
                  pl.BlockSpec((k, tn), lambda i, j: (0, j))],
        out_specs=[pl.BlockSpec((tm, tn), lambda i, j: (i, j)),
                   pl.BlockSpec((tm, tn), lambda i, j: (i, j))],
        out_shape=[jax.ShapeDtypeStruct((m, n), F32), jax.ShapeDtypeStruct((m, n), BF16)],
        compiler_params=_params("parallel", "parallel"),
        name="qkv_proj",
    )(x, w)


def _kmean_body(k_ref, o_ref):
    i = pl.program_id(0)

    @pl.when(i == 0)
    def _():
        o_ref[...] = jnp.zeros_like(o_ref)

    mean = jnp.sum(k_ref[...], axis=0, keepdims=True) * (1.0 / MOBA_BLOCK)
    rows = lax.broadcasted_iota(jnp.int32, o_ref.shape, 0)
    o_ref[...] = jnp.where(rows == i, mean, o_ref[...])


def _block_means(qkv_f32, n_blocks):
    dkv = N_KV_HEADS * HEAD_DIM
    kcol = (N_HEADS * HEAD_DIM) // dkv
    return pl.pallas_call(
        _kmean_body,
        grid=(n_blocks,),
        in_specs=[pl.BlockSpec((MOBA_BLOCK, dkv), lambda i: (i, kcol))],
        out_specs=pl.BlockSpec((LANES, dkv), lambda i: (0, 0)),
        out_shape=jax.ShapeDtypeStruct((LANES, dkv), F32),
        compiler_params=_params("arbitrary"),
        name="moba_block_means",
    )(qkv_f32)


def _t5_bucket_np(dist):
    n = np.maximum(dist, 0)
    max_exact = NUM_BUCKETS // 2
    nf = np.maximum(n, 1).astype(np.float32)
    large = max_exact + (np.log(nf / np.float32(max_exact)) / np.float32(math.log(MAX_DISTANCE / max_exact))
                         * np.float32(NUM_BUCKETS - max_exact)).astype(np.int32)
    large = np.minimum(large, NUM_BUCKETS - 1)
    return np.where(n < max_exact, n, large).astype(np.int32)


def _bucket_tables():
    q = np.arange(MOBA_BLOCK)[:, None]
    k = np.arange(MOBA_BLOCK)[None, :]
    own = np.where(q - k >= 0, _t5_bucket_np(q - k), -1)
    prev = _t5_bucket_np(q - k + MOBA_BLOCK)
    return np.stack([own, prev]).astype(np.int32)


def _bias_body(rb_ref, idx_ref, o_ref, os_ref):
    h = pl.program_id(0)
    far = rb_ref[h, NUM_BUCKETS - 1]
    tabs = []
    for m in range(2):
        idx = idx_ref[m]
        acc = jnp.full(idx.shape, NEG_INF, F32)
        for b in range(NUM_BUCKETS):
            acc = jnp.where(idx == b, rb_ref[h, b] - far, acc)
        o_ref[0, m] = acc
        tabs.append(acc)
    os_ref[0] = jnp.concatenate([tabs[1][:SUBLANES, :], tabs[0][:SUBLANES, :LANES]], axis=1)


def _bias_tables(rel_bias):
    idx = jnp.asarray(_bucket_tables())
    return pl.pallas_call(
        _bias_body,
        grid=(N_HEADS,),
        in_specs=[pl.BlockSpec(memory_space=pltpu.SMEM),
                  pl.BlockSpec((2, MOBA_BLOCK, MOBA_BLOCK), lambda h: (0, 0, 0))],
        out_specs=[pl.BlockSpec((1, 2, MOBA_BLOCK, MOBA_BLOCK), lambda h: (h, 0, 0, 0)),
                   pl.BlockSpec((1, SUBLANES, MOBA_BLOCK + LANES), lambda h: (h, 0, 0))],
        out_shape=[jax.ShapeDtypeStruct((N_HEADS, 2, MOBA_BLOCK, MOBA_BLOCK), F32),
                   jax.ShapeDtypeStruct((N_HEADS, SUBLANES, MOBA_BLOCK + LANES), F32)],
        compiler_params=_params("parallel"),
        name="moba_bias_tables",
    )(rel_bias, idx)


def _select_blocks(gate, n_past):
    lane = lax.broadcasted_iota(jnp.int32, gate.shape, 1)
    lane_f = lane.astype(F32)
    g = jnp.where(lane < n_past, gate, NEG_INF)
    sel = jnp.full(gate.shape, NEG_INF, F32)
    for _ in range(MOBA_TOPK):
        mx = jnp.max(g, axis=1, keepdims=True)
        idx = jnp.min(jnp.where(g == mx, lane_f, float(gate.shape[1])), axis=1, keepdims=True)
        pick = lane_f == idx
        sel = jnp.where(pick, 0.0, sel)
        g = jnp.where(pick, -jnp.inf, g)
    return jnp.where(lane < n_past, sel, 0.0)


def _attn_prompt_body(qf_ref, qb_ref, k_ref, v_ref, km_ref, tab_ref, o_ref, qa_scr, m_scr, l_scr, acc_scr):
    i = pl.program_id(1)
    rows = HEADS_PER_KV * MOBA_BLOCK
    km = km_ref[...]
    for hh in range(HEADS_PER_KV):
        cs = slice(hh * HEAD_DIM, (hh + 1) * HEAD_DIM)
        rs = slice(hh * MOBA_BLOCK, (hh + 1) * MOBA_BLOCK)
        gate = lax.dot_general(qf_ref[:, cs], km, _NT, precision=HIGHEST, preferred_element_type=F32)
        qa_scr[rs, :HEAD_DIM] = qb_ref[:, cs]
        qa_scr[rs, HEAD_DIM:] = _select_blocks(gate, i).astype(BF16)
    qa = qa_scr[...]
    lane = lax.broadcasted_iota(jnp.int32, (MOBA_BLOCK, LANES), 1)

    def scores(j, onehot):
        start = pl.multiple_of(j * MOBA_BLOCK, MOBA_BLOCK)
        kj = k_ref[pl.ds(start, MOBA_BLOCK), :]
        vj = v_ref[pl.ds(start, MOBA_BLOCK), :]
        rhs = jnp.concatenate([kj, onehot], axis=1)
        return lax.dot_general(qa, rhs, _NT, preferred_element_type=F32), vj

    def online_update(s, vj):
        m_prev = m_scr[...]
        m_new = jnp.maximum(m_prev, jnp.max(s, axis=1, keepdims=True))
        alpha = jnp.exp(m_prev - m_new)
        p = jnp.exp(s - jnp.concatenate([m_new, m_new], axis=1))
        l_scr[...] = alpha * l_scr[...] + jnp.sum(p, axis=1, keepdims=True)
        acc_scr[...] = alpha * acc_scr[...] + _dot(p.astype(BF16), vj)
        m_scr[...] = m_new

    s, vj = scores(i, jnp.zeros((MOBA_BLOCK, LANES), BF16))
    s = s + tab_ref[:, 0].reshape(rows, MOBA_BLOCK)
    m0 = jnp.max(s, axis=1, keepdims=True)
    p = jnp.exp(s - m0)
    m_scr[...] = jnp.broadcast_to(m0, (rows, LANES))
    l_scr[...] = jnp.broadcast_to(jnp.sum(p, axis=1, keepdims=True), (rows, LANES))
    acc_scr[...] = _dot(p.astype(BF16), vj)

    @pl.when(i >= 1)
    def _():
        j = i - 1
        s, vj = scores(j, (lane == j).astype(BF16))
        online_update(s + tab_ref[:, 1].reshape(rows, MOBA_BLOCK), vj)

    def far_block(j, carry):
        s, vj = scores(j, (lane == j).astype(BF16))
        online_update(s, vj)
        return carry

    lax.fori_loop(0, jnp.maximum(i - 1, 0), far_block, 0)

    out = acc_scr[...] / l_scr[...]
    for hh in range(HEADS_PER_KV):
        o_ref[:, hh * HEAD_DIM:(hh + 1) * HEAD_DIM] = out[hh * MOBA_BLOCK:(hh + 1) * MOBA_BLOCK]


def _attn_prompt(qkv_f32, qkv_b, kmean, tabs, seq, m_total):
    nblk = seq // MOBA_BLOCK
    gw = HEADS_PER_KV * HEAD_DIM
    kcol = (N_HEADS * HEAD_DIM) // HEAD_DIM
    vcol = kcol + N_KV_HEADS
    rows = HEADS_PER_KV * MOBA_BLOCK
    return pl.pallas_call(
        _attn_prompt_body,
        grid=(N_KV_HEADS, nblk),
        in_specs=[pl.BlockSpec((MOBA_BLOCK, gw), lambda g, i: (i, g)),
                  pl.BlockSpec((MOBA_BLOCK, gw), lambda g, i: (i, g)),
                  pl.BlockSpec((seq, HEAD_DIM), lambda g, i: (0, kcol + g)),
                  pl.BlockSpec((seq, HEAD_DIM), lambda g, i: (0, vcol + g)),
                  pl.BlockSpec((LANES, HEAD_DIM), lambda g, i: (0, g)),
                  pl.BlockSpec((HEADS_PER_KV, 2, MOBA_BLOCK, MOBA_BLOCK), lambda g, i: (g, 0, 0, 0))],
        out_specs=pl.BlockSpec((MOBA_BLOCK, gw), lambda g, i: (i, g)),
        out_shape=jax.ShapeDtypeStruct((m_total, N_HEADS * HEAD_DIM), F32),
        scratch_shapes=[pltpu.VMEM((rows, 2 * HEAD_DIM), BF16),
                        pltpu.VMEM((rows, LANES), F32),
                        pltpu.VMEM((rows, LANES), F32),
                        pltpu.VMEM((rows, HEAD_DIM), F32)],
        compiler_params=_params("parallel", "arbitrary"),
        name="moba_attn_prompt",
    )(qkv_f32, qkv_b, qkv_b, qkv_b, kmean, tabs)


def _attn_sample_body(pt_ref, *refs, n_pages, page, dec_seq):
    del pt_ref
    qkv_ref = refs[0]
    kp = refs[1:1 + n_pages]
    vp = refs[1 + n_pages:1 + 2 * n_pages]
    tab_ref = refs[1 + 2 * n_pages]
    o_ref = refs[3 + 2 * n_pages]
    kall, vall = refs[4 + 2 * n_pages:]
    past = n_pages * page
    n_past_blocks = past // MOBA_BLOCK
    pages_per_block = MOBA_BLOCK // page
    near0 = past - MOBA_BLOCK
    total = past + LANES
    dq = N_HEADS * HEAD_DIM
    dkv = N_KV_HEADS * HEAD_DIM
    hq = HEADS_PER_KV * dec_seq
    pad = jnp.zeros((LANES - dec_seq, HEAD_DIM), F32)
    key_blk = lax.broadcasted_iota(jnp.int32, (LANES, total), 1) // MOBA_BLOCK
    expand = (key_blk == lax.broadcasted_iota(jnp.int32, (LANES, total), 0)).astype(BF16)
    scale = HEAD_DIM ** -0.5

    for g in range(N_KV_HEADS):
        means = []
        for blk in range(n_past_blocks):
            tot = jnp.zeros((1, HEAD_DIM), F32)
            for pp in range(pages_per_block):
                pg = blk * pages_per_block + pp
                kk = kp[pg][pl.ds(g, page, stride=N_KV_HEADS), :]
                vv = vp[pg][pl.ds(g, page, stride=N_KV_HEADS), :]
                tot = tot + jnp.sum(kk, axis=0, keepdims=True)
                kall[pg * page:(pg + 1) * page, :] = kk.astype(BF16)
                vall[pg * page:(pg + 1) * page, :] = vv.astype(BF16)
            means.append(tot * (1.0 / MOBA_BLOCK))
        kmean = jnp.concatenate(means + [jnp.zeros((LANES - n_past_blocks, HEAD_DIM), F32)], axis=0)
        knew = qkv_ref[:, dq + g * HEAD_DIM:dq + (g + 1) * HEAD_DIM]
        vnew = qkv_ref[:, dq + dkv + g * HEAD_DIM:dq + dkv + (g + 1) * HEAD_DIM]
        kall[past:total, :] = jnp.concatenate([knew, pad], axis=0).astype(BF16)
        vall[past:total, :] = jnp.concatenate([vnew, pad], axis=0).astype(BF16)

        qs = jnp.concatenate([qkv_ref[:, (g * HEADS_PER_KV + hh) * HEAD_DIM:(g * HEADS_PER_KV + hh + 1) * HEAD_DIM]
                              for hh in range(HEADS_PER_KV)], axis=0)
        gate = lax.dot_general(qs, kmean, _NT, precision=HIGHEST, preferred_element_type=F32)
        selm = _select_blocks(gate, n_past_blocks).astype(BF16)
        s = lax.dot_general((qs * scale).astype(BF16), kall[...], _NT, preferred_element_type=F32)
        s = s + _dot(selm, expand)
        tab = tab_ref[g * HEADS_PER_KV:(g + 1) * HEADS_PER_KV].reshape(hq, MOBA_BLOCK + LANES)
        s_far = s[:, :near0]
        s_near = s[:, near0:] + tab
        m = jnp.maximum(jnp.max(s_far, axis=1, keepdims=True), jnp.max(s_near, axis=1, keepdims=True))
        p_far = jnp.exp(s_far - m)
        p_near = jnp.exp(s_near - m)
        l = jnp.sum(p_far, axis=1, keepdims=True) + jnp.sum(p_near, axis=1, keepdims=True)
        out = _dot(p_far.astype(BF16), vall[:near0, :]) + _dot(p_near.astype(BF16), vall[near0:, :])
        out = out / l
        for hh in range(HEADS_PER_KV):
            c0 = (g * HEADS_PER_KV + hh) * HEAD_DIM
            o_ref[:, c0:c0 + HEAD_DIM] = out[hh * dec_seq:(hh + 1) * dec_seq]


def _attn_sample(page_table, qkv_f32, cache_k, cache_v, tab_s, attn, row0, dec_seq):
    dec_batch, n_pages = page_table.shape
    page = cache_k.shape[1] // N_KV_HEADS
    total = n_pages * page + LANES
    blk0 = row0 // dec_seq

    def page_spec(p):
        return pl.BlockSpec((None, page * N_KV_HEADS, HEAD_DIM), lambda b, pt, p=p: (pt[b, p], 0, 0))

    body = functools.partial(_attn_sample_body, n_pages=n_pages, page=page, dec_seq=dec_seq)
    grid_spec = pltpu.PrefetchScalarGridSpec(
        num_scalar_prefetch=1,
        grid=(dec_batch,),
        in_specs=([pl.BlockSpec((dec_seq, qkv_f32.shape[1]), lambda b, pt: (blk0 + b, 0))]
                  + [page_spec(p) for p in range(n_pages)]
                  + [page_spec(p) for p in range(n_pages)]
                  + [pl.BlockSpec(tab_s.shape, lambda b, pt: (0, 0, 0)),
                     pl.BlockSpec(memory_space=pl.ANY)]),
        out_specs=pl.BlockSpec((dec_seq, attn.shape[1]), lambda b, pt: (blk0 + b, 0)),
        scratch_shapes=[pltpu.VMEM((total, HEAD_DIM), BF16), pltpu.VMEM((total, HEAD_DIM), BF16)],
    )
    n_in = 1 + 1 + 2 * n_pages + 1
    return pl.pallas_call(
        body,
        grid_spec=grid_spec,
        out_shape=jax.ShapeDtypeStruct(attn.shape, attn.dtype),
        input_output_aliases={n_in: 0},
        compiler_params=_params("arbitrary"),
        name="moba_attn_sample",
    )(page_table, qkv_f32, *([cache_k] * n_pages), *([cache_v] * n_pages), tab_s, attn)


def _proj_ln_body(*refs, glu, n_tiles):
    if glu:
        a_ref, w_ref, w2_ref, x_ref, g_ref, b_ref, of_ref, ob_ref, z_scr = refs
    else:
        a_ref, w_ref, x_ref, g_ref, b_ref, of_ref, ob_ref, z_scr = refs
    j = pl.program_id(1)
    a = a_ref[...].astype(BF16)
    z = _dot(a, w_ref[...])
    if glu:
        z = z * _sigmoid(_dot(a, w2_ref[...]))
    z_scr[j] = z

    @pl.when(j == n_tiles - 1)
    def _():
        zfull = jnp.concatenate([z_scr[t] for t in range(n_tiles)], axis=1)
        out = _layer_norm(ALPHA * x_ref[...] + zfull, g_ref[...], b_ref[...])
        of_ref[...] = out
        ob_ref[...] = out.astype(BF16)


def _proj_ln(a, ws, x, g, b, tm, tn):
    m, k = a.shape
    n = ws[0].shape[1]
    n_tiles = n // tn
    glu = len(ws) == 2
    body = functools.partial(_proj_ln_body, glu=glu, n_tiles=n_tiles)
    row = pl.BlockSpec((tm, n), lambda i, j: (i, 0))
    vec = pl.BlockSpec((1, n), lambda i, j: (0, 0))
    return pl.pallas_call(
        body,
        grid=(m // tm, n_tiles),
        in_specs=([pl.BlockSpec((tm, k), lambda i, j: (i, 0))]
                  + [pl.BlockSpec((k, tn), lambda i, j: (0, j)) for _ in ws]
                  + [row, vec, vec]),
        out_specs=[row, row],
        out_shape=[jax.ShapeDtypeStruct((m, n), F32), jax.ShapeDtypeStruct((m, n), BF16)],
        scratch_shapes=[pltpu.VMEM((n_tiles, tm, tn), F32)],
        compiler_params=_params("parallel", "arbitrary"),
        name="glu_res_ln" if glu else "proj_res_ln",
    )(a, *ws, x, g.reshape(1, n), b.reshape(1, n))


def _ffn_body(te_ref, tv_ref, *refs, n_f, dense):
    del te_ref
    if dense:
        x_ref, wg_ref, wu_ref, wd_ref, xr_ref, g_ref, b_ref, of_ref, ob_ref, acc = refs
    else:
        x_ref, wg_ref, wu_ref, wd_ref, gate_ref, o_ref, acc = refs
    t = pl.program_id(0)
    f = pl.program_id(1)

    @pl.when(f == 0)
    def _():
        acc[...] = jnp.zeros_like(acc)

    @pl.when(tv_ref[t] > 0)
    def _():
        x = x_ref[...]
        hg = _dot(x, wg_ref[...])
        hu = _dot(x, wu_ref[...])
        h = (hg * _sigmoid(hg)) * hu
        acc[...] += _dot(h.astype(BF16), wd_ref[...])

    @pl.when(f == n_f - 1)
    def _():
        if dense:
            out = _layer_norm(ALPHA * xr_ref[...] + acc[...], g_ref[...], b_ref[...])
            of_ref[...] = out
            ob_ref[...] = out.astype(BF16)
        else:
            o_ref[...] = acc[...] * gate_ref[...]


def _ffn_specs(d, tm, tf):
    x_spec = pl.BlockSpec((tm, d), lambda t, f, te, tv: (t, 0))
    wgu = pl.BlockSpec((None, d, tf), lambda t, f, te, tv: (te[t], 0, f * tv[t]))
    wd = pl.BlockSpec((None, tf, d), lambda t, f, te, tv: (te[t], f * tv[t], 0))
    return x_spec, wgu, wd


def _ffn_dense(xb, wg, wu, wd, xres, g, b, tm, tf):
    m, d = xb.shape
    n_t = m // tm
    n_f = wg.shape[-1] // tf
    x_spec, wgu, wds = _ffn_specs(d, tm, tf)
    vec = pl.BlockSpec((1, d), lambda t, f, te, tv: (0, 0))
    grid_spec = pltpu.PrefetchScalarGridSpec(
        num_scalar_prefetch=2, grid=(n_t, n_f),
        in_specs=[x_spec, wgu, wgu, wds, x_spec, vec, vec],
        out_specs=[x_spec, x_spec],
        scratch_shapes=[pltpu.VMEM((tm, d), F32)])
    return pl.pallas_call(
        functools.partial(_ffn_body, n_f=n_f, dense=True),
        grid_spec=grid_spec,
        out_shape=[jax.ShapeDtypeStruct((m, d), F32), jax.ShapeDtypeStruct((m, d), BF16)],
        compiler_params=_params("parallel", "arbitrary"),
        name="ffn_res_ln",
    )(jnp.zeros((n_t,), jnp.int32), jnp.ones((n_t,), jnp.int32), xb, wg, wu, wd, xres,
      g.reshape(1, d), b.reshape(1, d))


def _ffn_moe(tile_expert, tile_valid, xs, wg, wu, wd, gates, tm, tf):
    s, d = xs.shape
    n_t = s // tm
    n_f = wg.shape[-1] // tf
    x_spec, wgu, wds = _ffn_specs(d, tm, tf)
    grid_spec = pltpu.PrefetchScalarGridSpec(
        num_scalar_prefetch=2, grid=(n_t, n_f),
        in_specs=[x_spec, wgu, wgu, wds, pl.BlockSpec((tm, 1), lambda t, f, te, tv: (t, 0))],
        out_specs=x_spec,
        scratch_shapes=[pltpu.VMEM((tm, d), F32)])
    return pl.pallas_call(
        functools.partial(_ffn_body, n_f=n_f, dense=False),
        grid_spec=grid_spec,
        out_shape=jax.ShapeDtypeStruct((s, d), F32),
        compiler_params=_params("parallel", "arbitrary"),
        name="moe_ffn",
    )(tile_expert, tile_valid, xs, wg, wu, wd, gates)


def _router_body(y_ref, w_ref, i_ref, g_ref):
    logits = jnp.dot(y_ref[...], w_ref[...], precision=HIGHEST, preferred_element_type=F32)
    lane = lax.broadcasted_iota(jnp.int32, logits.shape, 1)
    lane_f = lane.astype(F32)
    l1 = jnp.where(lane < N_EXPERTS, logits, -jnp.inf)
    m1 = jnp.max(l1, axis=1, keepdims=True)
    i1 = jnp.min(jnp.where(l1 == m1, lane_f, float(LANES)), axis=1, keepdims=True)
    l2 = jnp.where(lane_f == i1, -jnp.inf, l1)
    m2 = jnp.max(l2, axis=1, keepdims=True)
    i2 = jnp.min(jnp.where(l2 == m2, lane_f, float(LANES)), axis=1, keepdims=True)
    e = jnp.exp(m2 - m1)
    g1 = 1.0 / (1.0 + e)
    g2 = e / (1.0 + e)
    i_ref[...] = jnp.where(lane == 0, i1, jnp.where(lane == 1, i2, 0.0)).astype(jnp.int32)
    g_ref[...] = jnp.where(lane == 0, g1, jnp.where(lane == 1, g2, 0.0))


def _router(y, w_router, tm):
    m, d = y.shape
    wr = jnp.zeros((d, LANES), F32).at[:, :N_EXPERTS].set(w_router)
    row = pl.BlockSpec((tm, LANES), lambda i: (i, 0))
    return pl.pallas_call(
        _router_body,
        grid=(m // tm,),
        in_specs=[pl.BlockSpec((tm, d), lambda i: (i, 0)), pl.BlockSpec((d, LANES), lambda i: (0, 0))],
        out_specs=[row, row],
        out_shape=[jax.ShapeDtypeStruct((m, LANES), jnp.int32), jax.ShapeDtypeStruct((m, LANES), F32)],
        compiler_params=_params("parallel"),
        name="moe_router",
    )(y, wr)


def _combine_ln_body(x_ref, a_ref, b2_ref, g_ref, b_ref, o_ref):
    o_ref[...] = _layer_norm(ALPHA * x_ref[...] + a_ref[...] + b2_ref[...], g_ref[...], b_ref[...])


def _combine_ln(x, pairs, g, b, tm):
    m, d = x.shape
    vec = pl.BlockSpec((1, d), lambda i: (0, 0))
    return pl.pallas_call(
        _combine_ln_body,
        grid=(m // tm,),
        in_specs=[pl.BlockSpec((tm, d), lambda i: (i, 0)),
                  pl.BlockSpec((tm, d), lambda i: (i, 0)),
                  pl.BlockSpec((tm, d), lambda i: (i, 1)),
                  vec, vec],
        out_specs=pl.BlockSpec((tm, d), lambda i: (i, 0)),
        out_shape=jax.ShapeDtypeStruct((m, d), F32),
        compiler_params=_params("parallel"),
        name="moe_combine_ln",
    )(x, pairs, pairs, g.reshape(1, d), b.reshape(1, d))


def _moe_routing(top_idx, top_gate, tm):
    m = top_idx.shape[0]
    n_pairs = m * TOP_K
    n_tiles = n_pairs // tm + N_EXPERTS
    e_flat = top_idx.reshape(-1)
    g_flat = top_gate.reshape(-1)
    order = jnp.argsort(e_flat, stable=True).astype(jnp.int32)
    e_sorted = e_flat[order]
    counts = jnp.sum((e_flat[:, None] == jnp.arange(N_EXPERTS)[None, :]).astype(jnp.int32), axis=0)
    padded = ((counts + tm - 1) // tm) * tm
    ends_padded = jnp.cumsum(padded)
    starts_padded = ends_padded - padded
    starts = jnp.cumsum(counts) - counts
    dest = starts_padded[e_sorted] + jnp.arange(n_pairs, dtype=jnp.int32) - starts[e_sorted]
    slot_token = jnp.zeros((n_tiles * tm,), jnp.int32).at[dest].set(order // TOP_K)
    slot_gate = jnp.zeros((n_tiles * tm,), F32).at[dest].set(g_flat[order])
    pair_slot = jnp.zeros((n_pairs,), jnp.int32).at[order].set(dest)
    tile_start = jnp.arange(n_tiles, dtype=jnp.int32) * tm
    tile_expert = jnp.minimum(jnp.searchsorted(ends_padded, tile_start, side="right"), N_EXPERTS - 1)
    tile_valid = (tile_start < ends_padded[-1]).astype(jnp.int32)
    return tile_expert.astype(jnp.int32), tile_valid, slot_token, slot_gate, pair_slot


def _gelu_tanh(x):
    return 0.5 * x * (1.0 + jnp.tanh(math.sqrt(2.0 / math.pi) * (x + 0.044715 * (x * x * x))))


def _ssm_body(u_ref, bd_ref, wbr_ref, wbi_ref, vcr_ref, vci_ref, a8r_ref, a8i_ref, d_ref, h0r_ref, h0i_ref,
              y_ref, hr_ref, hi_ref, *scr, sequential):
    n_chunks = u_ref.shape[0]
    ub = [u_ref[:, s, :].astype(BF16) for s in range(SSM_CHUNK)]
    hl_r = _dot(ub[0], wbr_ref[0])
    hl_i = _dot(ub[0], wbi_ref[0])
    for s in range(1, SSM_CHUNK):
        hl_r = hl_r + _dot(ub[s], wbr_ref[s])
        hl_i = hl_i + _dot(ub[s], wbi_ref[s])
    a8r = a8r_ref[...]
    a8i = a8i_ref[...]
    if sequential:
        hlr_scr, hli_scr, hinr_scr, hini_scr = scr
        hlr_scr[...] = hl_r
        hli_scr[...] = hl_i

        def tile_step(kb, carry):
            hr, hi = carry
            base = pl.multiple_of(kb * SUBLANES, SUBLANES)
            tr = hlr_scr[pl.ds(base, SUBLANES), :]
            ti = hli_scr[pl.ds(base, SUBLANES), :]
            rows_r, rows_i = [], []
            for r in range(SUBLANES):
                rows_r.append(hr)
                rows_i.append(hi)
                hr, hi = (a8r * hr - a8i * hi + tr[r:r + 1], a8r * hi + a8i * hr + ti[r:r + 1])
            hinr_scr[pl.ds(base, SUBLANES), :] = jnp.concatenate(rows_r, axis=0)
            hini_scr[pl.ds(base, SUBLANES), :] = jnp.concatenate(rows_i, axis=0)
            return hr, hi

        hr, hi = lax.fori_loop(0, n_chunks // SUBLANES, tile_step, (h0r_ref[...], h0i_ref[...]))
        hr_ref[...] = hr
        hi_ref[...] = hi
        hin_r = hinr_scr[...]
        hin_i = hini_scr[...]
    else:
        hin_r = h0r_ref[...]
        hin_i = h0i_ref[...]
        hr_ref[...] = a8r * hin_r - a8i * hin_i + hl_r
        hi_ref[...] = a8r * hin_i + a8i * hin_r + hl_i
    hb_r = hin_r.astype(BF16)
    hb_i = hin_i.astype(BF16)
    d = d_ref[...]
    for t in range(SSM_CHUNK):
        y = _dot(hb_r, vcr_ref[t]) + _dot(hb_i, vci_ref[t])
        for s in range(t + 1):
            y = y + _dot(ub[s], bd_ref[t - s])
        y_ref[:, t, :] = _gelu_tanh(y + d * u_ref[:, t, :])


def _ssm_operators(a_re, a_im, log_dt, b_re, b_im, c_re, c_im):
    g_n, p_n = a_re.shape
    gpb = LANES // SSM_GROUP
    nb = g_n // gpb
    lam_re = jnp.minimum(a_re, -1e-4)
    lam_im = a_im
    dt = jnp.exp(log_dt)[:, None]
    decay = jnp.exp(lam_re * dt)
    ar = decay * jnp.cos(lam_im * dt)
    ai = decay * jnp.sin(lam_im * dt)
    num_re = ar - 1.0
    den = lam_re * lam_re + lam_im * lam_im
    f_re = (num_re * lam_re + ai * lam_im) / den
    f_im = (ai * lam_re - num_re * lam_im) / den
    bb_re = f_re[..., None] * b_re - f_im[..., None] * b_im
    bb_im = f_re[..., None] * b_im + f_im[..., None] * b_re
    pw_re, pw_im = [jnp.ones_like(ar)], [jnp.zeros_like(ar)]
    for _ in range(SSM_CHUNK):
        pr, pi = pw_re[-1], pw_im[-1]
        pw_re.append(pr * ar - pi * ai)
        pw_im.append(pr * ai + pi * ar)
    pw_re = jnp.stack(pw_re)
    pw_im = jnp.stack(pw_im)
    ab_re = pw_re[:SSM_CHUNK, :, :, None] * bb_re - pw_im[:SSM_CHUNK, :, :, None] * bb_im
    ab_im = pw_re[:SSM_CHUNK, :, :, None] * bb_im + pw_im[:SSM_CHUNK, :, :, None] * bb_re
    conv = (jnp.einsum("gcp,tgpd->tgdc", c_re, ab_re, precision=HIGHEST)
            - jnp.einsum("gcp,tgpd->tgdc", c_im, ab_im, precision=HIGHEST))
    eye = jnp.eye(gpb, dtype=F32)

    def blockdiag(x):
        t_n, _, a_n, b_n = x.shape
        x = x.reshape(t_n, nb, gpb, a_n, b_n)
        x = x[:, :, :, :, None, :] * eye[None, None, :, None, :, None]
        return x.reshape(t_n, nb, gpb * a_n, gpb * b_n).transpose(1, 0, 2, 3)

    bd = blockdiag(conv).astype(BF16)
    wb_re = blockdiag(ab_re[::-1].transpose(0, 1, 3, 2)).astype(BF16)
    wb_im = blockdiag(ab_im[::-1].transpose(0, 1, 3, 2)).astype(BF16)
    pr = pw_re[1:, :, None, :]
    pi = pw_im[1:, :, None, :]
    vc_re = blockdiag((c_re[None] * pr - c_im[None] * pi).transpose(0, 1, 3, 2)).astype(BF16)
    vc_im = blockdiag((-(c_re[None] * pi + c_im[None] * pr)).transpose(0, 1, 3, 2)).astype(BF16)
    a8r = pw_re[SSM_CHUNK].reshape(1, g_n * p_n)
    a8i = pw_im[SSM_CHUNK].reshape(1, g_n * p_n)
    return bd, wb_re, wb_im, vc_re, vc_im, a8r, a8i


def _ssm_scan(u3, ops, d_skip, h0_re, h0_im, chunk0, n_chunks, sequential):
    bd, wb_re, wb_im, vc_re, vc_im, a8r, a8i = ops
    nb = bd.shape[0]
    sw = wb_re.shape[-1]
    cblk = chunk0 // n_chunks
    n_state_rows = 1 if sequential else n_chunks
    op4 = lambda shp: pl.BlockSpec((None,) + shp, lambda j: (j, 0, 0, 0))
    srow = pl.BlockSpec((1, sw), lambda j: (0, j))
    hspec = pl.BlockSpec((n_state_rows, sw), lambda j: (0, j))
    scratch = [pltpu.VMEM((n_chunks, sw), F32)] * 4 if sequential else []
    return pl.pallas_call(
        functools.partial(_ssm_body, sequential=sequential),
        grid=(nb,),
        in_specs=[pl.BlockSpec((n_chunks, SSM_CHUNK, LANES), lambda j: (cblk, 0, j)),
                  op4(bd.shape[1:]), op4(wb_re.shape[1:]), op4(wb_im.shape[1:]),
                  op4(vc_re.shape[1:]), op4(vc_im.shape[1:]),
                  srow, srow, pl.BlockSpec((1, LANES), lambda j: (0, j)), hspec, hspec],
        out_specs=[pl.BlockSpec((n_chunks, SSM_CHUNK, LANES), lambda j: (0, 0, j)), hspec, hspec],
        out_shape=[jax.ShapeDtypeStruct((n_chunks, SSM_CHUNK, u3.shape[2]), F32),
                   jax.ShapeDtypeStruct((n_state_rows, h0_re.shape[1]), F32),
                   jax.ShapeDtypeStruct((n_state_rows, h0_re.shape[1]), F32)],
        scratch_shapes=scratch,
        compiler_params=_params("parallel"),
        name="s5_scan_seq" if sequential else "s5_scan_step",
    )(u3, bd, wb_re, wb_im, vc_re, vc_im, a8r, a8i, d_skip.reshape(1, -1), h0_re, h0_im)


def kernel(x_prompt, x_sample, cache_k, cache_v, state_ssm_re, state_ssm_im, page_table, rel_bias, ln_g, ln_b,
           w_qkv, w_o, w_ssm_in, ssm_a_re, ssm_a_im, ssm_log_dt, ssm_b_re, ssm_b_im, ssm_c_re, ssm_c_im, ssm_d,
           w_glu_v, w_glu_g, w_ff_gate, w_ff_up, w_ff_down, w_router, w_moe_gate, w_moe_up, w_moe_down):
    batch, seq, d = x_prompt.shape
    dec_batch, dec_seq, _ = x_sample.shape
    assert batch == 1 and d == D_MODEL and dec_seq == SSM_CHUNK and seq % MOBA_BLOCK == 0
    m_p = batch * seq
    m_s = dec_batch * dec_seq
    m = m_p + m_s
    dq = N_HEADS * HEAD_DIM
    dkv = N_KV_HEADS * HEAD_DIM
    n_pool, page = cache_k.shape[1], cache_k.shape[2]

    x = jnp.concatenate([x_prompt.reshape(m_p, d), x_sample.reshape(m_s, d)], axis=0)
    xb = x.astype(BF16)

    qkv_f32, qkv_b = _qkv_proj(xb, w_qkv[0].astype(BF16), tm=1024, tn=512)
    kmean = _block_means(qkv_f32, seq // MOBA_BLOCK)
    tabs, tab_s = _bias_tables(rel_bias)
    attn = _attn_prompt(qkv_f32, qkv_b, kmean, tabs, seq, m)
    ck = cache_k[0].reshape(n_pool, page * N_KV_HEADS, HEAD_DIM)
    cv = cache_v[0].reshape(n_pool, page * N_KV_HEADS, HEAD_DIM)
    attn = _attn_sample(page_table, qkv_f32, ck, cv, tab_s, attn, m_p, dec_seq)
    y_f, y_b = _proj_ln(attn, [w_o[0].astype(BF16)], x, ln_g[0, 0], ln_b[0, 0], tm=512, tn=512)
    y_f, y_b = _ffn_dense(y_b, w_ff_gate.astype(BF16), w_ff_up.astype(BF16), w_ff_down.astype(BF16),
                          y_f, ln_g[0, 1], ln_b[0, 1], tm=512, tf=512)

    k_all = qkv_f32[:, dq:dq + dkv]
    v_all = qkv_f32[:, dq + dkv:]
    k_prompt = k_all[:m_p].reshape(1, batch, seq, N_KV_HEADS, HEAD_DIM)
    v_prompt = v_all[:m_p].reshape(1, batch, seq, N_KV_HEADS, HEAD_DIM)
    k_sample = k_all[m_p:].reshape(1, dec_batch, dec_seq, N_KV_HEADS, HEAD_DIM)
    v_sample = v_all[m_p:].reshape(1, dec_batch, dec_seq, N_KV_HEADS, HEAD_DIM)

    u = _matmul(y_b, w_ssm_in[0].astype(BF16), tm=1024, tn=512)
    u3 = u.reshape(m // SSM_CHUNK, SSM_CHUNK, d)
    ops = _ssm_operators(ssm_a_re[0], ssm_a_im[0], ssm_log_dt[0], ssm_b_re[0], ssm_b_im[0],
                         ssm_c_re[0], ssm_c_im[0])
    n_state = N_SSM_GROUPS * SSM_STATE
    zero = jnp.zeros((batch, n_state), F32)
    yg_p, hrp, hip = _ssm_scan(u3, ops, ssm_d[0], zero, zero, 0, m_p // SSM_CHUNK, True)
    yg_s, hrs, his = _ssm_scan(u3, ops, ssm_d[0], state_ssm_re[0].reshape(dec_batch, n_state),
                               state_ssm_im[0].reshape(dec_batch, n_state), m_p // SSM_CHUNK, dec_batch, False)
    yg = jnp.concatenate([yg_p.reshape(m_p, d), yg_s.reshape(m_s, d)], axis=0)
    y_f, y_b = _proj_ln(yg, [w_glu_v[0].astype(BF16), w_glu_g[0].astype(BF16)], y_f, ln_g[1, 0], ln_b[1, 0],
                        tm=512, tn=512)

    tm_moe = 512
    top_idx, top_gate = _router(y_f, w_router[0], tm=512)
    tile_expert, tile_valid, slot_token, slot_gate, pair_slot = _moe_routing(
        top_idx[:, :TOP_K], top_gate[:, :TOP_K], tm_moe)
    xs = jnp.take(y_b, slot_token, axis=0)
    ys = _ffn_moe(tile_expert, tile_valid, xs, w_moe_gate[0].astype(BF16), w_moe_up[0].astype(BF16),
                  w_moe_down[0].astype(BF16), slot_gate.reshape(-1, 1), tm_moe, tf=512)
    pairs = jnp.take(ys, pair_slot, axis=0).reshape(m, TOP_K * d)
    out = _combine_ln(y_f, pairs, ln_g[1, 1], ln_b[1, 1], tm=512)

    return (out[:m_p].reshape(batch, seq, d), out[m_p:].reshape(dec_batch, dec_seq, d),
            k_prompt, v_prompt, k_sample, v_sample,
            hrp.reshape(1, batch, N_SSM_GROUPS, SSM_STATE), hip.reshape(1, batch, N_SSM_GROUPS, SSM_STATE),
            hrs.reshape(1, dec_batch, N_SSM_GROUPS, SSM_STATE), his.reshape(1, dec_batch, N_SSM_GROUPS, SSM_STATE))
```

```python
import functools
import math

import numpy as np
import jax
import jax.numpy as jnp
from jax import lax
from jax.experimental import pallas as pl
from jax.experimental.pallas import tpu as pltpu

D_MODEL = 2048
N_HEADS = 16
HEAD_DIM = D_MODEL // N_HEADS
N_KV_HEADS = 4
HEADS_PER_KV = N_HEADS // N_KV_HEADS
MOBA_BLOCK = 256
MOBA_TOPK = 3
NUM_BUCKETS = 32
MAX_DISTANCE = 128
SSM_GROUP = 16
N_SSM_GROUPS = D_MODEL // SSM_GROUP
SSM_STATE = 64
SSM_CHUNK = 8
D_FF = 7 * D_MODEL // 2
N_EXPERTS = 8
TOP_K = 2
DEPTH = 2
ALPHA = (2 * DEPTH) ** 0.25
LN_EPS = 1e-5
NEG_INF = -1e30

LANES = 128
SUBLANES = 8
VMEM_LIMIT = 60 * 1024 * 1024

F32 = jnp.float32
BF16 = jnp.bfloat16
HIGHEST = lax.Precision.HIGHEST
_NT = (((1,), (1,)), ((), ()))


def _params(*sem):
    return pltpu.CompilerParams(dimension_semantics=sem, vmem_limit_bytes=VMEM_LIMIT)


def _dot(a, b):
    return jnp.dot(a, b, preferred_element_type=F32)


def _sigmoid(x):
    return 1.0 / (1.0 + jnp.exp(-x))


def _layer_norm(y, g, b):
    mean = jnp.mean(y, axis=-1, keepdims=True)
    yc = y - mean
    var = jnp.mean(yc * yc, axis=-1, keepdims=True)
    return yc * lax.rsqrt(var + LN_EPS) * g + b


def _mm_body(x_ref, w_ref, o_ref, xb_scr):
    @pl.when(pl.program_id(1) == 0)
    def _():
        xb_scr[...] = x_ref[...].astype(BF16)

    o_ref[...] = _dot(xb_scr[...], w_ref[...])


def _matmul(x, w, tm, tn):
    m, k = x.shape
    n = w.shape[1]
    return pl.pallas_call(
        _mm_body,
        grid=(m // tm, n // tn),
        in_specs=[pl.BlockSpec((tm, k), lambda i, j: (i, 0)),
                  pl.BlockSpec((k, tn), lambda i, j: (0, j))],
        out_specs=pl.BlockSpec((tm, tn), lambda i, j: (i, j)),
        out_shape=jax.ShapeDtypeStruct((m, n), F32),
        scratch_shapes=[pltpu.VMEM((tm, k), BF16)],
        compiler_params=_params("parallel", "arbitrary"),
        name="matmul",
    )(x, w)


def _qkv_body(x_ref, w_ref, of_ref, ob_ref, *, nq_tiles, scale):
    acc = _dot(x_ref[...], w_ref[...])
    of_ref[...] = acc
    s = jnp.where(pl.program_id(1) < nq_tiles, scale, 1.0).astype(F32)
    ob_ref[...] = (acc * s).astype(BF16)


def _qkv_proj(x, w, tm, tn):
    m, k = x.shape
    n = w.shape[1]
    body = functools.partial(_qkv_body, nq_tiles=(N_HEADS * HEAD_DIM) // tn, scale=HEAD_DIM ** -0.5)
    return pl.pallas_call(
        body,
        grid=(m // tm, n // tn),
        in_specs=[pl.BlockSpec((tm, k), lambda i, j: (i, 0)),
                  pl.BlockSpec((k, tn), lambda i, j: (0, j))],
        out_specs=[pl.BlockSpec((tm, tn), lambda i, j: (i, j)),
                   pl.BlockSpec((tm, tn), lambda i, j: (i, j))],
        out_shape=[jax.ShapeDtypeStruct((m, n), F32), jax.ShapeDtypeStruct((m, n), BF16)],
        compiler_params=_params("parallel", "parallel"),
        name="qkv_proj",
    )(x, w)


def _kmean_body(k_ref, o_ref):
    i = pl.program_id(0)

    @pl.when(i == 0)
    def _():
        o_ref[...] = jnp.zeros_like(o_ref)

    mean = jnp.sum(k_ref[...], axis=0, keepdims=True) * (1.0 / MOBA_BLOCK)
    rows = lax.broadcasted_iota(jnp.int32, o_ref.shape, 0)
    o_ref[...] = jnp.where(rows == i, mean, o_ref[...])


def _block_means(qkv_f32, n_blocks):
    dkv = N_KV_HEADS * HEAD_DIM
    kcol = (N_HEADS * HEAD_DIM) // dkv
    return pl.pallas_call(
        _kmean_body,
        grid=(n_blocks,),
        in_specs=[pl.BlockSpec((MOBA_BLOCK, dkv), lambda i: (i, kcol))],
        out_specs=pl.BlockSpec((LANES, dkv), lambda i: (0, 0)),
        out_shape=jax.ShapeDtypeStruct((LANES, dkv), F32),
        compiler_params=_params("arbitrary"),
        name="moba_block_means",
    )(qkv_f32)


def _t5_bucket_np(dist):
    n = np.maximum(dist, 0)
    max_exact = NUM_BUCKETS // 2
    nf = np.maximum(n, 1).astype(np.float32)
    large = max_exact + (np.log(nf / np.float32(max_exact)) / np.float32(math.log(MAX_DISTANCE / max_exact))
                         * np.float32(NUM_BUCKETS - max_exact)).astype(np.int32)
    large = np.minimum(large, NUM_BUCKETS - 1)
    return np.where(n < max_exact, n, large).astype(np.int32)


def _bucket_tables():
    q = np.arange(MOBA_BLOCK)[:, None]
    k = np.arange(MOBA_BLOCK)[None, :]
    own = np.where(q - k >= 0, _t5_bucket_np(q - k), -1)
    prev = _t5_bucket_np(q - k + MOBA_BLOCK)
    return np.stack([own, prev]).astype(np.int32)


def _bias_body(rb_ref, idx_ref, o_ref, os_ref):
    h = pl.program_id(0)
    far = rb_ref[h, NUM_BUCKETS - 1]
    tabs = []
    for m in range(2):
        idx = idx_ref[m]
        acc = jnp.full(idx.shape, NEG_INF, F32)
        for b in range(NUM_BUCKETS):
            acc = jnp.where(idx == b, rb_ref[h, b] - far, acc)
        o_ref[0, m] = acc
        tabs.append(acc)
    os_ref[0] = jnp.concatenate([tabs[1][:SUBLANES, :], tabs[0][:SUBLANES, :LANES]], axis=1)


def _bias_tables(rel_bias):
    idx = jnp.asarray(_bucket_tables())
    return pl.pallas_call(
        _bias_body,
        grid=(N_HEADS,),
        in_specs=[pl.BlockSpec(memory_space=pltpu.SMEM),
                  pl.BlockSpec((2, MOBA_BLOCK, MOBA_BLOCK), lambda h: (0, 0, 0))],
        out_specs=[pl.BlockSpec((1, 2, MOBA_BLOCK, MOBA_BLOCK), lambda h: (h, 0, 0, 0)),
                   pl.BlockSpec((1, SUBLANES, MOBA_BLOCK + LANES), lambda h: (h, 0, 0))],
        out_shape=[jax.ShapeDtypeStruct((N_HEADS, 2, MOBA_BLOCK, MOBA_BLOCK), F32),
                   jax.ShapeDtypeStruct((N_HEADS, SUBLANES, MOBA_BLOCK + LANES), F32)],
        compiler_params=_params("parallel"),
        name="moba_bias_tables",
    )(rel_bias, idx)


def _select_blocks(gate, n_past):
    lane = lax.broadcasted_iota(jnp.int32, gate.shape, 1)
    lane_f = lane.astype(F32)
    g = jnp.where(lane < n_past, gate, NEG_INF)
    sel = jnp.full(gate.shape, NEG_INF, F32)
    for _ in range(MOBA_TOPK):
        mx = jnp.max(g, axis=1, keepdims=True)
        idx = jnp.min(jnp.where(g == mx, lane_f, float(gate.shape[1])), axis=1, keepdims=True)
        pick = lane_f == idx
        sel = jnp.where(pick, 0.0, sel)
        g = jnp.where(pick, -jnp.inf, g)
    return jnp.where(lane < n_past, sel, 0.0)


def _attn_prompt_body(qf_ref, qb_ref, k_ref, v_ref, km_ref, tab_ref, o_ref, qa_scr, m_scr, l_scr, acc_scr):
    i = pl.program_id(1)
    rows = HEADS_PER_KV * MOBA_BLOCK
    km = km_ref[...]
    lane = lax.broadcasted_iota(jnp.int32, (MOBA_BLOCK, LANES), 1)
    dummy = LANES - 1
    for hh in range(HEADS_PER_KV):
        cs = slice(hh * HEAD_DIM, (hh + 1) * HEAD_DIM)
        rs = slice(hh * MOBA_BLOCK, (hh + 1) * MOBA_BLOCK)
        gate = lax.dot_general(qf_ref[:, cs], km, _NT, precision=HIGHEST, preferred_element_type=F32)
        mask = jnp.where(lane == dummy, NEG_INF, _select_blocks(gate, i))
        qa_scr[rs, :HEAD_DIM] = qb_ref[:, cs]
        qa_scr[rs, HEAD_DIM:] = mask.astype(BF16)
    qa = qa_scr[...]

    def scores(j, mask_lane):
        start = pl.multiple_of(j * MOBA_BLOCK, MOBA_BLOCK)
        kj = k_ref[pl.ds(start, MOBA_BLOCK), :]
        vj = v_ref[pl.ds(start, MOBA_BLOCK), :]
        rhs = jnp.concatenate([kj, (lane == mask_lane).astype(BF16)], axis=1)
        return lax.dot_general(qa, rhs, _NT, preferred_element_type=F32), vj

    def softmax_parts(parts, m_new):
        m2 = jnp.concatenate([m_new, m_new], axis=1)
        l_add = None
        acc_add = None
        for s, vj in parts:
            p = jnp.exp(s - m2)
            ls = jnp.sum(p, axis=1, keepdims=True)
            pv = _dot(p.astype(BF16), vj)
            l_add = ls if l_add is None else l_add + ls
            acc_add = pv if acc_add is None else acc_add + pv
        return l_add, acc_add

    has_prev = i >= 1
    s_own, v_own = scores(i, -1)
    s_own = s_own + tab_ref[:, 0].reshape(rows, MOBA_BLOCK)
    s_prev, v_prev = scores(jnp.maximum(i - 1, 0), jnp.where(has_prev, i - 1, dummy))
    s_prev = s_prev + tab_ref[:, 1].reshape(rows, MOBA_BLOCK)
    m0 = jnp.maximum(jnp.max(s_own, axis=1, keepdims=True), jnp.max(s_prev, axis=1, keepdims=True))
    m0 = jnp.broadcast_to(m0, (rows, LANES))
    l0, acc0 = softmax_parts([(s_own, v_own), (s_prev, v_prev)], m0)
    m_scr[...] = m0
    l_scr[...] = jnp.broadcast_to(l0, (rows, LANES))
    acc_scr[...] = acc0

    n_far = jnp.maximum(i - 1, 0)

    def far_pair(t, carry):
        j0 = 2 * t
        j1 = j0 + 1
        ok1 = j1 < n_far
        parts = [scores(j0, j0), scores(jnp.where(ok1, j1, 0), jnp.where(ok1, j1, dummy))]
        m_prev = m_scr[...]
        m_new = m_prev
        for s, _ in parts:
            m_new = jnp.maximum(m_new, jnp.max(s, axis=1, keepdims=True))
        alpha = jnp.exp(m_prev - m_new)
        l_add, acc_add = softmax_parts(parts, m_new)
        l_scr[...] = alpha * l_scr[...] + l_add
        acc_scr[...] = alpha * acc_scr[...] + acc_add
        m_scr[...] = m_new
        return carry

    lax.fori_loop(0, (n_far + 1) // 2, far_pair, 0)

    out = acc_scr[...] / l_scr[...]
    for hh in range(HEADS_PER_KV):
        o_ref[:, hh * HEAD_DIM:(hh + 1) * HEAD_DIM] = out[hh * MOBA_BLOCK:(hh + 1) * MOBA_BLOCK]


def _attn_prompt(qkv_f32, qkv_b, kmean, tabs, seq, m_total):
    nblk = seq // MOBA_BLOCK
    assert nblk < LANES - 1
    gw = HEADS_PER_KV * HEAD_DIM
    kcol = (N_HEADS * HEAD_DIM) // HEAD_DIM
    vcol = kcol + N_KV_HEADS
    rows = HEADS_PER_KV * MOBA_BLOCK
    return pl.pallas_call(
        _attn_prompt_body,
        grid=(N_KV_HEADS, nblk),
        in_specs=[pl.BlockSpec((MOBA_BLOCK, gw), lambda g, i: (i, g)),
                  pl.BlockSpec((MOBA_BLOCK, gw), lambda g, i: (i, g)),
                  pl.BlockSpec((seq, HEAD_DIM), lambda g, i: (0, kcol + g)),
                  pl.BlockSpec((seq, HEAD_DIM), lambda g, i: (0, vcol + g)),
                  pl.BlockSpec((LANES, HEAD_DIM), lambda g, i: (0, g)),
                  pl.BlockSpec((HEADS_PER_KV, 2, MOBA_BLOCK, MOBA_BLOCK), lambda g, i: (g, 0, 0, 0))],
        out_specs=pl.BlockSpec((MOBA_BLOCK, gw), lambda g, i: (i, g)),
        out_shape=jax.ShapeDtypeStruct((m_total, N_HEADS * HEAD_DIM), F32),
        scratch_shapes=[pltpu.VMEM((rows, 2 * HEAD_DIM), BF16),
                        pltpu.VMEM((rows, LANES), F32),
                        pltpu.VMEM((rows, LANES), F32),
                        pltpu.VMEM((rows, HEAD_DIM), F32)],
        compiler_params=_params("parallel", "arbitrary"),
        name="moba_attn_prompt",
    )(qkv_f32, qkv_b, qkv_b, qkv_b, kmean, tabs)


def _attn_sample_body(pt_ref, *refs, n_pages, page, dec_seq):
    del pt_ref
    qkv_ref = refs[0]
    kp = refs[1:1 + n_pages]
    vp = refs[1 + n_pages:1 + 2 * n_pages]
    tab_ref = refs[1 + 2 * n_pages]
    o_ref = refs[3 + 2 * n_pages]
    kall, vall = refs[4 + 2 * n_pages:]
    past = n_pages * page
    n_past_blocks = past // MOBA_BLOCK
    pages_per_block = MOBA_BLOCK // page
    near0 = past - MOBA_BLOCK
    total = past + LANES
    dq = N_HEADS * HEAD_DIM
    dkv = N_KV_HEADS * HEAD_DIM
    hq = HEADS_PER_KV * dec_seq
    pad = jnp.zeros((LANES - dec_seq, HEAD_DIM), F32)
    key_blk = lax.broadcasted_iota(jnp.int32, (LANES, total), 1) // MOBA_BLOCK
    expand = (key_blk == lax.broadcasted_iota(jnp.int32, (LANES, total), 0)).astype(BF16)
    scale = HEAD_DIM ** -0.5

    for g in range(N_KV_HEADS):
        means = []
        for blk in range(n_past_blocks):
            tot = jnp.zeros((1, HEAD_DIM), F32)
            for pp in range(pages_per_block):
                pg = blk * pages_per_block + pp
                kk = kp[pg][pl.ds(g, page, stride=N_KV_HEADS), :]
                vv = vp[pg][pl.ds(g, page, stride=N_KV_HEADS), :]
                tot = tot + jnp.sum(kk, axis=0, keepdims=True)
                kall[g, pg * page:(pg + 1) * page, :] = kk.astype(BF16)
                vall[g, pg * page:(pg + 1) * page, :] = vv.astype(BF16)
            means.append(tot * (1.0 / MOBA_BLOCK))
        kmean = jnp.concatenate(means + [jnp.zeros((LANES - n_past_blocks, HEAD_DIM), F32)], axis=0)
        knew = qkv_ref[:, dq + g * HEAD_DIM:dq + (g + 1) * HEAD_DIM]
        vnew = qkv_ref[:, dq + dkv + g * HEAD_DIM:dq + dkv + (g + 1) * HEAD_DIM]
        kall[g, past:total, :] = jnp.concatenate([knew, pad], axis=0).astype(BF16)
        vall[g, past:total, :] = jnp.concatenate([vnew, pad], axis=0).astype(BF16)

        qs = jnp.concatenate([qkv_ref[:, (g * HEADS_PER_KV + hh) * HEAD_DIM:(g * HEADS_PER_KV + hh + 1) * HEAD_DIM]
                              for hh in range(HEADS_PER_KV)], axis=0)
        gate = lax.dot_general(qs, kmean, _NT, precision=HIGHEST, preferred_element_type=F32)
        selm = _select_blocks(gate, n_past_blocks).astype(BF16)
        s = lax.dot_general((qs * scale).astype(BF16), kall[g], _NT, preferred_element_type=F32)
        s = s + _dot(selm, expand)
        tab = tab_ref[g * HEADS_PER_KV:(g + 1) * HEADS_PER_KV].reshape(hq, MOBA_BLOCK + LANES)
        s_far = s[:, :near0]
        s_near = s[:, near0:] + tab
        m = jnp.maximum(jnp.max(s_far, axis=1, keepdims=True), jnp.max(s_near, axis=1, keepdims=True))
        p_far = jnp.exp(s_far - m)
        p_near = jnp.exp(s_near - m)
        l = jnp.sum(p_far, axis=1, keepdims=True) + jnp.sum(p_near, axis=1, keepdims=True)
        out = _dot(p_far.astype(BF16), vall[g, :near0, :]) + _dot(p_near.astype(BF16), vall[g, near0:, :])
        out = out / l
        for hh in range(HEADS_PER_KV):
            c0 = (g * HEADS_PER_KV + hh) * HEAD_DIM
            o_ref[:, c0:c0 + HEAD_DIM] = out[hh * dec_seq:(hh + 1) * dec_seq]


def _attn_sample(page_table, qkv_f32, cache_k, cache_v, tab_s, attn, row0, dec_seq):
    dec_batch, n_pages = page_table.shape
    page = cache_k.shape[1] // N_KV_HEADS
    total = n_pages * page + LANES
    blk0 = row0 // dec_seq

    def page_spec(p):
        return pl.BlockSpec((None, page * N_KV_HEADS, HEAD_DIM), lambda b, pt, p=p: (pt[b, p], 0, 0))

    body = functools.partial(_attn_sample_body, n_pages=n_pages, page=page, dec_seq=dec_seq)
    grid_spec = pltpu.PrefetchScalarGridSpec(
        num_scalar_prefetch=1,
        grid=(dec_batch,),
        in_specs=([pl.BlockSpec((dec_seq, qkv_f32.shape[1]), lambda b, pt: (blk0 + b, 0))]
                  + [page_spec(p) for p in range(n_pages)]
                  + [page_spec(p) for p in range(n_pages)]
                  + [pl.BlockSpec(tab_s.shape, lambda b, pt: (0, 0, 0)),
                     pl.BlockSpec(memory_space=pl.ANY)]),
        out_specs=pl.BlockSpec((dec_seq, attn.shape[1]), lambda b, pt: (blk0 + b, 0)),
        scratch_shapes=[pltpu.VMEM((N_KV_HEADS, total, HEAD_DIM), BF16),
                        pltpu.VMEM((N_KV_HEADS, total, HEAD_DIM), BF16)],
    )
    n_in = 1 + 1 + 2 * n_pages + 1
    return pl.pallas_call(
        body,
        grid_spec=grid_spec,
        out_shape=jax.ShapeDtypeStruct(attn.shape, attn.dtype),
        input_output_aliases={n_in: 0},
        compiler_params=_params("arbitrary"),
        name="moba_attn_sample",
    )(page_table, qkv_f32, *([cache_k] * n_pages), *([cache_v] * n_pages), tab_s, attn)


def _proj_ln_body(*refs, glu, n_tiles):
    if glu:
        a_ref, w_ref, w2_ref, x_ref, g_ref, b_ref, o_ref, a_scr, z_scr = refs
    else:
        a_ref, w_ref, x_ref, g_ref, b_ref, o_ref, a_scr, z_scr = refs
    j = pl.program_id(1)

    @pl.when(j == 0)
    def _():
        a_scr[...] = a_ref[...].astype(BF16)

    a = a_scr[...]
    z = _dot(a, w_ref[...])
    if glu:
        z = z * _sigmoid(_dot(a, w2_ref[...]))
    z_scr[j] = z

    @pl.when(j == n_tiles - 1)
    def _():
        zfull = jnp.concatenate([z_scr[t] for t in range(n_tiles)], axis=1)
        o_ref[...] = _layer_norm(ALPHA * x_ref[...] + zfull, g_ref[...], b_ref[...])


def _proj_ln(a, ws, x, g, b, tm, tn):
    m, k = a.shape
    n = ws[0].shape[1]
    n_tiles = n // tn
    glu = len(ws) == 2
    body = functools.partial(_proj_ln_body, glu=glu, n_tiles=n_tiles)
    row = pl.BlockSpec((tm, n), lambda i, j: (i, 0))
    vec = pl.BlockSpec((1, n), lambda i, j: (0, 0))
    return pl.pallas_call(
        body,
        grid=(m // tm, n_tiles),
        in_specs=([pl.BlockSpec((tm, k), lambda i, j: (i, 0))]
                  + [pl.BlockSpec((k, tn), lambda i, j: (0, j)) for _ in ws]
                  + [row, vec, vec]),
        out_specs=row,
        out_shape=jax.ShapeDtypeStruct((m, n), F32),
        scratch_shapes=[pltpu.VMEM((tm, k), BF16), pltpu.VMEM((n_tiles, tm, tn), F32)],
        compiler_params=_params("parallel", "arbitrary"),
        name="glu_res_ln" if glu else "proj_res_ln",
    )(a, *ws, x, g.reshape(1, n), b.reshape(1, n))


def _ffn_body(te_ref, tr_ref, *refs, n_f, dense, half):
    del te_ref
    if dense:
        x_ref, wg_ref, wu_ref, wd_ref, g_ref, b_ref, o_ref, xb, wgb, wub, wdb = refs
    else:
        x_ref, wg_ref, wu_ref, wd_ref, gate_ref, o_ref, xb, wgb, wub, wdb = refs
    t = pl.program_id(0)
    f = pl.program_id(1)
    n_rows = tr_ref[t]

    @pl.when(f == 0)
    def _():
        xb[...] = x_ref[...].astype(BF16)
        o_ref[...] = jnp.zeros_like(o_ref)

    @pl.when(n_rows > 0)
    def _():
        wgb[...] = wg_ref[...].astype(BF16)
        wub[...] = wu_ref[...].astype(BF16)
        wdb[...] = wd_ref[...].astype(BF16)

    for hh in range(x_ref.shape[0] // half):
        @pl.when(n_rows > hh * half)
        def _(hh=hh):
            rs = slice(hh * half, (hh + 1) * half)
            x = xb[rs, :]
            hg = _dot(x, wgb[...])
            hu = _dot(x, wub[...])
            h = (hg * _sigmoid(hg)) * hu
            o_ref[rs, :] += _dot(h.astype(BF16), wdb[...])

    @pl.when(f == n_f - 1)
    def _():
        if dense:
            o_ref[...] = _layer_norm(ALPHA * x_ref[...] + o_ref[...], g_ref[...], b_ref[...])
        else:
            o_ref[...] = o_ref[...] * gate_ref[...]


def _ffn_call(tile_expert, tile_rows, x, wg, wu, wd, extra, extra_specs, tm, tf, half, dense, name):
    s, d = x.shape
    n_t = s // tm
    n_f = wg.shape[-1] // tf
    live = lambda t, tr: jnp.minimum(tr[t], 1)
    x_spec = pl.BlockSpec((tm, d), lambda t, f, te, tr: (t, 0))
    wgu = pl.BlockSpec((None, d, tf), lambda t, f, te, tr: (te[t], 0, f * live(t, tr)))
    wds = pl.BlockSpec((None, tf, d), lambda t, f, te, tr: (te[t], f * live(t, tr), 0))
    grid_spec = pltpu.PrefetchScalarGridSpec(
        num_scalar_prefetch=2, grid=(n_t, n_f),
        in_specs=[x_spec, wgu, wgu, wds] + extra_specs,
        out_specs=x_spec,
        scratch_shapes=[pltpu.VMEM((tm, d), BF16), pltpu.VMEM((d, tf), BF16),
                        pltpu.VMEM((d, tf), BF16), pltpu.VMEM((tf, d), BF16)])
    return pl.pallas_call(
        functools.partial(_ffn_body, n_f=n_f, dense=dense, half=half),
        grid_spec=grid_spec,
        out_shape=jax.ShapeDtypeStruct((s, d), F32),
        compiler_params=_params("parallel", "arbitrary"),
        name=name,
    )(tile_expert, tile_rows, x, wg, wu, wd, *extra)


def _ffn_dense(x, wg, wu, wd, g, b, tm, tf):
    m, d = x.shape
    n_t = m // tm
    vec = pl.BlockSpec((1, d), lambda t, f, te, tr: (0, 0))
    return _ffn_call(jnp.zeros((n_t,), jnp.int32), jnp.full((n_t,), tm, jnp.int32), x, wg, wu, wd,
                     [g.reshape(1, d), b.reshape(1, d)], [vec, vec], tm, tf, tm, True, "ffn_res_ln")


def _ffn_moe(tile_expert, tile_rows, xs, wg, wu, wd, gates, tm, tf, half):
    gate_spec = pl.BlockSpec((tm, 1), lambda t, f, te, tr: (t, 0))
    return _ffn_call(tile_expert, tile_rows, xs, wg, wu, wd, [gates], [gate_spec], tm, tf, half, False, "moe_ffn")


def _router_body(y_ref, w_ref, i_ref, g_ref):
    logits = jnp.dot(y_ref[...], w_ref[...], precision=HIGHEST, preferred_element_type=F32)
    lane = lax.broadcasted_iota(jnp.int32, logits.shape, 1)
    lane_f = lane.astype(F32)
    l1 = jnp.where(lane < N_EXPERTS, logits, -jnp.inf)
    m1 = jnp.max(l1, axis=1, keepdims=True)
    i1 = jnp.min(jnp.where(l1 == m1, lane_f, float(LANES)), axis=1, keepdims=True)
    l2 = jnp.where(lane_f == i1, -jnp.inf, l1)
    m2 = jnp.max(l2, axis=1, keepdims=True)
    i2 = jnp.min(jnp.where(l2 == m2, lane_f, float(LANES)), axis=1, keepdims=True)
    e = jnp.exp(m2 - m1)
    g1 = 1.0 / (1.0 + e)
    g2 = e / (1.0 + e)
    i_ref[...] = jnp.where(lane == 0, i1, jnp.where(lane == 1, i2, 0.0)).astype(jnp.int32)
    g_ref[...] = jnp.where(lane == 0, g1, jnp.where(lane == 1, g2, 0.0))


def _router(y, w_router, tm):
    m, d = y.shape
    wr = jnp.zeros((d, LANES), F32).at[:, :N_EXPERTS].set(w_router)
    row = pl.BlockSpec((tm, LANES), lambda i: (i, 0))
    return pl.pallas_call(
        _router_body,
        grid=(m // tm,),
        in_specs=[pl.BlockSpec((tm, d), lambda i: (i, 0)), pl.BlockSpec((d, LANES), lambda i: (0, 0))],
        out_specs=[row, row],
        out_shape=[jax.ShapeDtypeStruct((m, LANES), jnp.int32), jax.ShapeDtypeStruct((m, LANES), F32)],
        compiler_params=_params("parallel"),
        name="moe_router",
    )(y, wr)


def _combine_ln_body(x_ref, a_ref, b2_ref, g_ref, b_ref, op_ref, os_ref, *, n_first):
    i = pl.program_id(0)
    out = _layer_norm(ALPHA * x_ref[...] + a_ref[...] + b2_ref[...], g_ref[...], b_ref[...])

    @pl.when(i < n_first)
    def _():
        op_ref[...] = out

    @pl.when(i >= n_first)
    def _():
        os_ref[...] = out


def _combine_ln(x, pairs, g, b, m_first, tm):
    m, d = x.shape
    n_first = m_first // tm
    vec = pl.BlockSpec((1, d), lambda i: (0, 0))
    return pl.pallas_call(
        functools.partial(_combine_ln_body, n_first=n_first),
        grid=(m // tm,),
        in_specs=[pl.BlockSpec((tm, d), lambda i: (i, 0)),
                  pl.BlockSpec((None, tm, d), lambda i: (0, i, 0)),
                  pl.BlockSpec((None, tm, d), lambda i: (1, i, 0)),
                  vec, vec],
        out_specs=[pl.BlockSpec((tm, d), lambda i: (jnp.minimum(i, n_first - 1), 0)),
                   pl.BlockSpec((tm, d), lambda i: (jnp.maximum(i - n_first, 0), 0))],
        out_shape=[jax.ShapeDtypeStruct((m_first, d), F32), jax.ShapeDtypeStruct((m - m_first, d), F32)],
        compiler_params=_params("arbitrary"),
        name="moe_combine_ln",
    )(x, pairs, pairs, g.reshape(1, d), b.reshape(1, d))


def _moe_routing(top_idx, top_gate, tm):
    m = top_idx.shape[0]
    n_pairs = m * TOP_K
    n_tiles = n_pairs // tm + N_EXPERTS
    i32 = jnp.int32
    e_flat = top_idx.reshape(-1)
    g_flat = top_gate.reshape(-1)
    order = jnp.argsort(e_flat, stable=True).astype(i32)
    inv = jnp.argsort(order).astype(i32)
    counts = jnp.sum((e_flat[:, None] == jnp.arange(N_EXPERTS)[None, :]).astype(i32), axis=0)
    padded = ((counts + tm - 1) // tm) * tm
    ends_padded = jnp.cumsum(padded)
    starts_padded = ends_padded - padded
    starts = jnp.cumsum(counts) - counts
    tile_start = jnp.arange(n_tiles, dtype=i32) * tm
    tile_expert = jnp.minimum(jnp.searchsorted(ends_padded, tile_start, side="right"), N_EXPERTS - 1).astype(i32)
    tile_rows = jnp.clip(counts[tile_expert] - (tile_start - starts_padded[tile_expert]), 0, tm).astype(i32)
    slot = jnp.arange(n_tiles * tm, dtype=i32)
    slot_e = jnp.repeat(tile_expert, tm)
    rank = slot - starts_padded[slot_e]
    live = (rank >= 0) & (rank < counts[slot_e])
    src = order[jnp.clip(starts[slot_e] + rank, 0, n_pairs - 1)]
    slot_token = jnp.where(live, src // TOP_K, 0)
    slot_gate = jnp.where(live, g_flat[src], 0.0)
    pair_slot = starts_padded[e_flat] + inv - starts[e_flat]
    return tile_expert, tile_rows, slot_token, slot_gate, pair_slot


def _gelu_tanh(x):
    return 0.5 * x * (1.0 + jnp.tanh(math.sqrt(2.0 / math.pi) * (x + 0.044715 * (x * x * x))))


def _ssm_body(*refs, sequential):
    (u_ref, bd_ref, wbr_ref, wbi_ref, vcr_ref, vci_ref, a8r_ref, a8i_ref, d_ref, h0r_ref, h0i_ref) = refs[:11]
    refs = refs[11:]
    if not sequential:
        refs = refs[1:]
    y_ref, hr_ref, hi_ref = refs[:3]
    scr = refs[3:]
    n_chunks = u_ref.shape[0]
    ub = [u_ref[:, s, :].astype(BF16) for s in range(SSM_CHUNK)]
    hl_r = _dot(ub[0], wbr_ref[0])
    hl_i = _dot(ub[0], wbi_ref[0])
    for s in range(1, SSM_CHUNK):
        hl_r = hl_r + _dot(ub[s], wbr_ref[s])
        hl_i = hl_i + _dot(ub[s], wbi_ref[s])
    a8r = a8r_ref[...]
    a8i = a8i_ref[...]
    if sequential:
        hlr_scr, hli_scr, hinr_scr, hini_scr = scr
        hlr_scr[...] = hl_r
        hli_scr[...] = hl_i

        def tile_step(kb, carry):
            hr, hi = carry
            base = pl.multiple_of(kb * SUBLANES, SUBLANES)
            tr = hlr_scr[pl.ds(base, SUBLANES), :]
            ti = hli_scr[pl.ds(base, SUBLANES), :]
            rows_r, rows_i = [], []
            for r in range(SUBLANES):
                rows_r.append(hr)
                rows_i.append(hi)
                hr, hi = (a8r * hr - a8i * hi + tr[r:r + 1], a8r * hi + a8i * hr + ti[r:r + 1])
            hinr_scr[pl.ds(base, SUBLANES), :] = jnp.concatenate(rows_r, axis=0)
            hini_scr[pl.ds(base, SUBLANES), :] = jnp.concatenate(rows_i, axis=0)
            return hr, hi

        hr, hi = lax.fori_loop(0, n_chunks // SUBLANES, tile_step, (h0r_ref[...], h0i_ref[...]))
        hr_ref[...] = hr
        hi_ref[...] = hi
        hin_r = hinr_scr[...]
        hin_i = hini_scr[...]
    else:
        hin_r = h0r_ref[...]
        hin_i = h0i_ref[...]
        hr_ref[...] = a8r * hin_r - a8i * hin_i + hl_r
        hi_ref[...] = a8r * hin_i + a8i * hin_r + hl_i
    hb_r = hin_r.astype(BF16)
    hb_i = hin_i.astype(BF16)
    d = d_ref[...]
    for t in range(SSM_CHUNK):
        y = _dot(hb_r, vcr_ref[t]) + _dot(hb_i, vci_ref[t])
        for s in range(t + 1):
            y = y + _dot(ub[s], bd_ref[t - s])
        y_ref[:, t, :] = _gelu_tanh(y + d * u_ref[:, t, :])


def _group_blockdiag(x, rows_per_group, lanes_per_group):
    t_n, g_n, a_n, b_n = x.shape
    gpb = LANES // SSM_GROUP
    nb = g_n // gpb
    assert a_n == rows_per_group and b_n == lanes_per_group
    x = x.reshape(t_n, nb, gpb * a_n, b_n)
    tile = np.tile(np.eye(b_n, dtype=np.float32), (1, gpb))
    x = jnp.einsum("tnrb,bl->ntrl", x, jnp.asarray(tile), precision=HIGHEST)
    r_grp = np.arange(gpb * a_n)[:, None] // a_n
    l_grp = np.arange(gpb * b_n)[None, :] // b_n
    return jnp.where(jnp.asarray(r_grp == l_grp), x, 0.0).astype(BF16)


def _ssm_operators(a_re, a_im, log_dt, b_re, b_im, c_re, c_im):
    g_n, p_n = a_re.shape
    lam_re = jnp.minimum(a_re, -1e-4)
    lam_im = a_im
    dt = jnp.exp(log_dt)[:, None]
    decay = jnp.exp(lam_re * dt)
    ar = decay * jnp.cos(lam_im * dt)
    ai = decay * jnp.sin(lam_im * dt)
    num_re = ar - 1.0
    den = lam_re * lam_re + lam_im * lam_im
    f_re = (num_re * lam_re + ai * lam_im) / den
    f_im = (ai * lam_re - num_re * lam_im) / den
    bt_re = jnp.swapaxes(b_re, 1, 2)
    bt_im = jnp.swapaxes(b_im, 1, 2)
    bb_re = f_re[:, None, :] * bt_re - f_im[:, None, :] * bt_im
    bb_im = f_re[:, None, :] * bt_im + f_im[:, None, :] * bt_re
    pw_re, pw_im = [jnp.ones_like(ar)], [jnp.zeros_like(ar)]
    for _ in range(SSM_CHUNK):
        pr, pi = pw_re[-1], pw_im[-1]
        pw_re.append(pr * ar - pi * ai)
        pw_im.append(pr * ai + pi * ar)
    pw_re = jnp.stack(pw_re)
    pw_im = jnp.stack(pw_im)
    pk_re = pw_re[:SSM_CHUNK, :, None, :]
    pk_im = pw_im[:SSM_CHUNK, :, None, :]
    ab_re = pk_re * bb_re - pk_im * bb_im
    ab_im = pk_re * bb_im + pk_im * bb_re
    conv = (jnp.einsum("gcp,tgdp->tgdc", c_re, ab_re, precision=HIGHEST)
            - jnp.einsum("gcp,tgdp->tgdc", c_im, ab_im, precision=HIGHEST))
    bd = _group_blockdiag(conv, SSM_GROUP, SSM_GROUP)
    wb_re = _group_blockdiag(ab_re[::-1], SSM_GROUP, p_n)
    wb_im = _group_blockdiag(ab_im[::-1], SSM_GROUP, p_n)
    pr = pw_re[1:, :, :, None]
    pi = pw_im[1:, :, :, None]
    ct_re = jnp.swapaxes(c_re, 1, 2)[None]
    ct_im = jnp.swapaxes(c_im, 1, 2)[None]
    vc_re = _group_blockdiag(ct_re * pr - ct_im * pi, p_n, SSM_GROUP)
    vc_im = _group_blockdiag(-(ct_re * pi + ct_im * pr), p_n, SSM_GROUP)
    a8r = pw_re[SSM_CHUNK].reshape(1, g_n * p_n)
    a8i = pw_im[SSM_CHUNK].reshape(1, g_n * p_n)
    return bd, wb_re, wb_im, vc_re, vc_im, a8r, a8i


def _ssm_scan(u3, ops, d_skip, h0_re, h0_im, chunk0, n_chunks, y_prev):
    bd, wb_re, wb_im, vc_re, vc_im, a8r, a8i = ops
    sequential = y_prev is None
    nb = bd.shape[0]
    sw = wb_re.shape[-1]
    cblk = chunk0 // n_chunks
    n_state_rows = 1 if sequential else n_chunks
    op4 = lambda shp: pl.BlockSpec((None,) + shp, lambda j: (j, 0, 0, 0))
    srow = pl.BlockSpec((1, sw), lambda j: (0, j))
    hspec = pl.BlockSpec((n_state_rows, sw), lambda j: (0, j))
    in_specs = [pl.BlockSpec((n_chunks, SSM_CHUNK, LANES), lambda j: (cblk, 0, j)),
                op4(bd.shape[1:]), op4(wb_re.shape[1:]), op4(wb_im.shape[1:]),
                op4(vc_re.shape[1:]), op4(vc_im.shape[1:]),
                srow, srow, pl.BlockSpec((1, LANES), lambda j: (0, j)), hspec, hspec]
    args = [u3, bd, wb_re, wb_im, vc_re, vc_im, a8r, a8i, d_skip.reshape(1, -1), h0_re, h0_im]
    aliases = {}
    if not sequential:
        in_specs.append(pl.BlockSpec(memory_space=pl.ANY))
        args.append(y_prev)
        aliases = {len(args) - 1: 0}
    scratch = [pltpu.VMEM((n_chunks, sw), F32)] * 4 if sequential else []
    return pl.pallas_call(
        functools.partial(_ssm_body, sequential=sequential),
        grid=(nb,),
        in_specs=in_specs,
        out_specs=[pl.BlockSpec((n_chunks, SSM_CHUNK, LANES), lambda j: (cblk, 0, j)), hspec, hspec],
        out_shape=[jax.ShapeDtypeStruct(u3.shape, F32),
                   jax.ShapeDtypeStruct((n_state_rows, h0_re.shape[1]), F32),
                   jax.ShapeDtypeStruct((n_state_rows, h0_re.shape[1]), F32)],
        scratch_shapes=scratch,
        input_output_aliases=aliases,
        compiler_params=_params("parallel"),
        name="s5_scan_seq" if sequential else "s5_scan_step",
    )(*args)


def kernel(x_prompt, x_sample, cache_k, cache_v, state_ssm_re, state_ssm_im, page_table, rel_bias, ln_g, ln_b,
           w_qkv, w_o, w_ssm_in, ssm_a_re, ssm_a_im, ssm_log_dt, ssm_b_re, ssm_b_im, ssm_c_re, ssm_c_im, ssm_d,
           w_glu_v, w_glu_g, w_ff_gate, w_ff_up, w_ff_down, w_router, w_moe_gate, w_moe_up, w_moe_down):
    batch, seq, d = x_prompt.shape
    dec_batch, dec_seq, _ = x_sample.shape
    assert batch == 1 and d == D_MODEL and dec_seq == SSM_CHUNK and seq % MOBA_BLOCK == 0
    m_p = batch * seq
    m_s = dec_batch * dec_seq
    m = m_p + m_s
    dq = N_HEADS * HEAD_DIM
    dkv = N_KV_HEADS * HEAD_DIM
    n_pool, page = cache_k.shape[1], cache_k.shape[2]

    x = jnp.concatenate([x_prompt.reshape(m_p, d), x_sample.reshape(m_s, d)], axis=0)

    qkv_f32, qkv_b = _qkv_proj(x.astype(BF16), w_qkv[0].astype(BF16), tm=1024, tn=512)
    kmean = _block_means(qkv_f32, seq // MOBA_BLOCK)
    tabs, tab_s = _bias_tables(rel_bias)
    attn = _attn_prompt(qkv_f32, qkv_b, kmean, tabs, seq, m)
    ck = cache_k[0].reshape(n_pool, page * N_KV_HEADS, HEAD_DIM)
    cv = cache_v[0].reshape(n_pool, page * N_KV_HEADS, HEAD_DIM)
    attn = _attn_sample(page_table, qkv_f32, ck, cv, tab_s, attn, m_p, dec_seq)
    y = _proj_ln(attn, [w_o[0].astype(BF16)], x, ln_g[0, 0], ln_b[0, 0], tm=512, tn=512)
    y = _ffn_dense(y, w_ff_gate, w_ff_up, w_ff_down, ln_g[0, 1], ln_b[0, 1], tm=1024, tf=256)

    k_all = qkv_f32[:, dq:dq + dkv]
    v_all = qkv_f32[:, dq + dkv:]
    k_prompt = k_all[:m_p].reshape(1, batch, seq, N_KV_HEADS, HEAD_DIM)
    v_prompt = v_all[:m_p].reshape(1, batch, seq, N_KV_HEADS, HEAD_DIM)
    k_sample = k_all[m_p:].reshape(1, dec_batch, dec_seq, N_KV_HEADS, HEAD_DIM)
    v_sample = v_all[m_p:].reshape(1, dec_batch, dec_seq, N_KV_HEADS, HEAD_DIM)

    u = _matmul(y, w_ssm_in[0].astype(BF16), tm=1024, tn=512)
    u3 = u.reshape(m // SSM_CHUNK, SSM_CHUNK, d)
    ops = _ssm_operators(ssm_a_re[0], ssm_a_im[0], ssm_log_dt[0], ssm_b_re[0], ssm_b_im[0],
                         ssm_c_re[0], ssm_c_im[0])
    n_state = N_SSM_GROUPS * SSM_STATE
    zero = jnp.zeros((batch, n_state), F32)
    yg, hrp, hip = _ssm_scan(u3, ops, ssm_d[0], zero, zero, 0, m_p // SSM_CHUNK, None)
    yg, hrs, his = _ssm_scan(u3, ops, ssm_d[0], state_ssm_re[0].reshape(dec_batch, n_state),
                             state_ssm_im[0].reshape(dec_batch, n_state), m_p // SSM_CHUNK, dec_batch, yg)
    y = _proj_ln(yg.reshape(m, d), [w_glu_v[0].astype(BF16), w_glu_g[0].astype(BF16)], y,
                 ln_g[1, 0], ln_b[1, 0], tm=512, tn=512)

    tm_moe = 1024
    top_idx, top_gate = _router(y, w_router[0], tm=512)
    tile_expert, tile_rows, slot_token, slot_gate, pair_slot = _moe_routing(
        top_idx[:, :TOP_K], top_gate[:, :TOP_K], tm_moe)
    xs = jnp.take(y, slot_token, axis=0)
    ys = _ffn_moe(tile_expert, tile_rows, xs, w_moe_gate[0], w_moe_up[0], w_moe_down[0],
                  slot_gate.reshape(-1, 1), tm_moe, tf=256, half=512)
    pairs = jnp.take(ys, pair_slot.reshape(m, TOP_K).T.reshape(-1), axis=0).reshape(TOP_K, m, d)
    out_p, out_s = _combine_ln(y, pairs, ln_g[1, 1], ln_b[1, 1], m_p, tm=512)

    return (out_p.reshape(batch, seq, d), out_s.reshape(dec_batch, dec_seq, d),
            k_prompt, v_prompt, k_sample, v_sample,
            hrp.reshape(1, batch, N_SSM_GROUPS, SSM_STATE), hip.reshape(1, batch, N_SSM_GROUPS, SSM_STATE),
            hrs.reshape(1, dec_batch, N_SSM_GROUPS, SSM_STATE), his.reshape(1, dec_batch, N_SSM_GROUPS, SSM_STATE))
```

```python
import functools
import math

import numpy as np
import jax
import jax.numpy as jnp
from jax import lax
from jax.experimental import pallas as pl
from jax.experimental.pallas import tpu as pltpu

D_MODEL = 2048
N_HEADS = 16
HEAD_DIM = D_MODEL // N_HEADS
N_KV_HEADS = 4
HEADS_PER_KV = N_HEADS // N_KV_HEADS
MOBA_BLOCK = 256
MOBA_TOPK = 3
NUM_BUCKETS = 32
MAX_DISTANCE = 128
SSM_GROUP = 16
N_SSM_GROUPS = D_MODEL // SSM_GROUP
SSM_STATE = 64
SSM_CHUNK = 8
D_FF = 7 * D_MODEL // 2
N_EXPERTS = 8
TOP_K = 2
DEPTH = 2
ALPHA = (2 * DEPTH) ** 0.25
LN_EPS = 1e-5
NEG_INF = -1e30

LANES = 128
SUBLANES = 8
VMEM_LIMIT = 60 * 1024 * 1024

F32 = jnp.float32
BF16 = jnp.bfloat16
HIGHEST = lax.Precision.HIGHEST
_NT = (((1,), (1,)), ((), ()))


def _params(*sem):
    return pltpu.CompilerParams(dimension_semantics=sem, vmem_limit_bytes=VMEM_LIMIT)


def _dot(a, b):
    return jnp.dot(a, b, preferred_element_type=F32)


def _sigmoid(x):
    return 1.0 / (1.0 + jnp.exp(-x))


def _layer_norm(y, g, b):
    mean = jnp.mean(y, axis=-1, keepdims=True)
    yc = y - mean
    var = jnp.mean(yc * yc, axis=-1, keepdims=True)
    return yc * lax.rsqrt(var + LN_EPS) * g + b


def _mm_body(x_ref, w_ref, o_ref, xb_scr):
    @pl.when(pl.program_id(1) == 0)
    def _():
        xb_scr[...] = x_ref[...].astype(BF16)

    o_ref[...] = _dot(xb_scr[...], w_ref[...])


def _matmul(x, w, tm, tn):
    m, k = x.shape
    n = w.shape[1]
    return pl.pallas_call(
        _mm_body,
        grid=(m // tm, n // tn),
        in_specs=[pl.BlockSpec((tm, k), lambda i, j: (i, 0)),
                  pl.BlockSpec((k, tn), lambda i, j: (0, j))],
        out_specs=pl.BlockSpec((tm, tn), lambda i, j: (i, j)),
        out_shape=jax.ShapeDtypeStruct((m, n), F32),
        scratch_shapes=[pltpu.VMEM((tm, k), BF16)],
        compiler_params=_params("parallel", "arbitrary"),
        name="matmul",
    )(x, w)


def _qkv_body(x_ref, w_ref, of_ref, ob_ref, *, nq_tiles, scale):
    acc = _dot(x_ref[...], w_ref[...])
    of_ref[...] = acc
    s = jnp.where(pl.program_id(1) < nq_tiles, scale, 1.0).astype(F32)
    ob_ref[...] = (acc * s).astype(BF16)


def _qkv_proj(x, w, tm, tn):
    m, k = x.shape
    n = w.shape[1]
    body = functools.partial(_qkv_body, nq_tiles=(N_HEADS * HEAD_DIM) // tn, scale=HEAD_DIM ** -0.5)
    return pl.pallas_call(
        body,
        grid=(m // tm, n // tn),
        in_specs=[pl.BlockSpec((tm, k), lambda i, j: (i, 0)),
                  pl.BlockSpec((k, tn), lambda i, j: (0, j))],
        out_specs=[pl.BlockSpec((tm, tn), lambda i, j: (i, j)),
                   pl.BlockSpec((tm, tn), lambda i, j: (i, j))],
        out_shape=[jax.ShapeDtypeStruct((m, n), F32), jax.ShapeDtypeStruct((m, n), BF16)],
        compiler_params=_params("parallel", "parallel"),
        name="qkv_proj",
    )(x, w)


def _kmean_body(k_ref, o_ref):
    i = pl.program_id(0)

    @pl.when(i == 0)
    def _():
        o_ref[...] = jnp.zeros_like(o_ref)

    mean = jnp.sum(k_ref[...], axis=0, keepdims=True) * (1.0 / MOBA_BLOCK)
    rows = lax.broadcasted_iota(jnp.int32, o_ref.shape, 0)
    o_ref[...] = jnp.where(rows == i, mean, o_ref[...])


def _block_means(qkv_f32, n_blocks):
    dkv = N_KV_HEADS * HEAD_DIM
    kcol = (N_HEADS * HEAD_DIM) // dkv
    return pl.pallas_call(
        _kmean_body,
        grid=(n_blocks,),
        in_specs=[pl.BlockSpec((MOBA_BLOCK, dkv), lambda i: (i, kcol))],
        out_specs=pl.BlockSpec((LANES, dkv), lambda i: (0, 0)),
        out_shape=jax.ShapeDtypeStruct((LANES, dkv), F32),
        compiler_params=_params("arbitrary"),
        name="moba_block_means",
    )(qkv_f32)


def _t5_bucket_np(dist):
    n = np.maximum(dist, 0)
    max_exact = NUM_BUCKETS // 2
    nf = np.maximum(n, 1).astype(np.float32)
    large = max_exact + (np.log(nf / np.float32(max_exact)) / np.float32(math.log(MAX_DISTANCE / max_exact))
                         * np.float32(NUM_BUCKETS - max_exact)).astype(np.int32)
    large = np.minimum(large, NUM_BUCKETS - 1)
    return np.where(n < max_exact, n, large).astype(np.int32)


def _bucket_tables():
    q = np.arange(MOBA_BLOCK)[:, None]
    k = np.arange(MOBA_BLOCK)[None, :]
    own = np.where(q - k >= 0, _t5_bucket_np(q - k), -1)
    prev = _t5_bucket_np(q - k + MOBA_BLOCK)
    return np.stack([own, prev]).astype(np.int32)


def _bias_body(rb_ref, idx_ref, o_ref, os_ref):
    h = pl.program_id(0)
    far = rb_ref[h, NUM_BUCKETS - 1]
    tabs = []
    for m in range(2):
        idx = idx_ref[m]
        acc = jnp.full(idx.shape, NEG_INF, F32)
        for b in range(NUM_BUCKETS):
            acc = jnp.where(idx == b, rb_ref[h, b] - far, acc)
        o_ref[0, m] = acc
        tabs.append(acc)
    os_ref[0] = jnp.concatenate([tabs[1][:SUBLANES, :], tabs[0][:SUBLANES, :LANES]], axis=1)


def _bias_tables(rel_bias):
    idx = jnp.asarray(_bucket_tables())
    return pl.pallas_call(
        _bias_body,
        grid=(N_HEADS,),
        in_specs=[pl.BlockSpec(memory_space=pltpu.SMEM),
                  pl.BlockSpec((2, MOBA_BLOCK, MOBA_BLOCK), lambda h: (0, 0, 0))],
        out_specs=[pl.BlockSpec((1, 2, MOBA_BLOCK, MOBA_BLOCK), lambda h: (h, 0, 0, 0)),
                   pl.BlockSpec((1, SUBLANES, MOBA_BLOCK + LANES), lambda h: (h, 0, 0))],
        out_shape=[jax.ShapeDtypeStruct((N_HEADS, 2, MOBA_BLOCK, MOBA_BLOCK), F32),
                   jax.ShapeDtypeStruct((N_HEADS, SUBLANES, MOBA_BLOCK + LANES), F32)],
        compiler_params=_params("parallel"),
        name="moba_bias_tables",
    )(rel_bias, idx)


def _select_blocks(gate, n_past):
    lane = lax.broadcasted_iota(jnp.int32, gate.shape, 1)
    lane_f = lane.astype(F32)
    g = jnp.where(lane < n_past, gate, NEG_INF)
    sel = jnp.full(gate.shape, NEG_INF, F32)
    for _ in range(MOBA_TOPK):
        mx = jnp.max(g, axis=1, keepdims=True)
        idx = jnp.min(jnp.where(g == mx, lane_f, float(gate.shape[1])), axis=1, keepdims=True)
        pick = lane_f == idx
        sel = jnp.where(pick, 0.0, sel)
        g = jnp.where(pick, -jnp.inf, g)
    return jnp.where(lane < n_past, sel, 0.0)


def _attn_prompt_body(qf_ref, qb_ref, k_ref, v_ref, km_ref, tab_ref, o_ref, qa_scr, m_scr, l_scr, acc_scr):
    i = pl.program_id(1)
    rows = HEADS_PER_KV * MOBA_BLOCK
    km = km_ref[...]
    lane = lax.broadcasted_iota(jnp.int32, (MOBA_BLOCK, LANES), 1)
    dummy = LANES - 1
    for hh in range(HEADS_PER_KV):
        cs = slice(hh * HEAD_DIM, (hh + 1) * HEAD_DIM)
        rs = slice(hh * MOBA_BLOCK, (hh + 1) * MOBA_BLOCK)
        gate = lax.dot_general(qf_ref[:, cs], km, _NT, precision=HIGHEST, preferred_element_type=F32)
        mask = jnp.where(lane == dummy, NEG_INF, _select_blocks(gate, i))
        qa_scr[rs, :HEAD_DIM] = qb_ref[:, cs]
        qa_scr[rs, HEAD_DIM:] = mask.astype(BF16)
    qa = qa_scr[...]

    def scores(j, mask_lane):
        start = pl.multiple_of(j * MOBA_BLOCK, MOBA_BLOCK)
        kj = k_ref[pl.ds(start, MOBA_BLOCK), :]
        vj = v_ref[pl.ds(start, MOBA_BLOCK), :]
        rhs = jnp.concatenate([kj, (lane == mask_lane).astype(BF16)], axis=1)
        return lax.dot_general(qa, rhs, _NT, preferred_element_type=F32), vj

    def softmax_parts(parts, m_new):
        m2 = jnp.concatenate([m_new, m_new], axis=1)
        l_add = None
        acc_add = None
        for s, vj in parts:
            p = jnp.exp(s - m2)
            ls = jnp.sum(p, axis=1, keepdims=True)
            pv = _dot(p.astype(BF16), vj)
            l_add = ls if l_add is None else l_add + ls
            acc_add = pv if acc_add is None else acc_add + pv
        return l_add, acc_add

    has_prev = i >= 1
    s_own, v_own = scores(i, -1)
    s_own = s_own + tab_ref[:, 0].reshape(rows, MOBA_BLOCK)
    s_prev, v_prev = scores(jnp.maximum(i - 1, 0), jnp.where(has_prev, i - 1, dummy))
    s_prev = s_prev + tab_ref[:, 1].reshape(rows, MOBA_BLOCK)
    m0 = jnp.maximum(jnp.max(s_own, axis=1, keepdims=True), jnp.max(s_prev, axis=1, keepdims=True))
    m0 = jnp.broadcast_to(m0, (rows, LANES))
    l0, acc0 = softmax_parts([(s_own, v_own), (s_prev, v_prev)], m0)
    m_scr[...] = m0
    l_scr[...] = jnp.broadcast_to(l0, (rows, LANES))
    acc_scr[...] = acc0

    n_far = jnp.maximum(i - 1, 0)

    def far_pair(t, carry):
        j0 = 2 * t
        j1 = j0 + 1
        ok1 = j1 < n_far
        parts = [scores(j0, j0), scores(jnp.where(ok1, j1, 0), jnp.where(ok1, j1, dummy))]
        m_prev = m_scr[...]
        m_new = m_prev
        for s, _ in parts:
            m_new = jnp.maximum(m_new, jnp.max(s, axis=1, keepdims=True))
        alpha = jnp.exp(m_prev - m_new)
        l_add, acc_add = softmax_parts(parts, m_new)
        l_scr[...] = alpha * l_scr[...] + l_add
        acc_scr[...] = alpha * acc_scr[...] + acc_add
        m_scr[...] = m_new
        return carry

    lax.fori_loop(0, (n_far + 1) // 2, far_pair, 0)

    out = acc_scr[...] / l_scr[...]
    for hh in range(HEADS_PER_KV):
        o_ref[:, hh * HEAD_DIM:(hh + 1) * HEAD_DIM] = out[hh * MOBA_BLOCK:(hh + 1) * MOBA_BLOCK]


def _attn_prompt(qkv_f32, qkv_b, kmean, tabs, seq):
    nblk = seq // MOBA_BLOCK
    assert nblk < LANES - 1
    gw = HEADS_PER_KV * HEAD_DIM
    kcol = (N_HEADS * HEAD_DIM) // HEAD_DIM
    vcol = kcol + N_KV_HEADS
    rows = HEADS_PER_KV * MOBA_BLOCK
    return pl.pallas_call(
        _attn_prompt_body,
        grid=(N_KV_HEADS, nblk),
        in_specs=[pl.BlockSpec((MOBA_BLOCK, gw), lambda g, i: (i, g)),
                  pl.BlockSpec((MOBA_BLOCK, gw), lambda g, i: (i, g)),
                  pl.BlockSpec((seq, HEAD_DIM), lambda g, i: (0, kcol + g)),
                  pl.BlockSpec((seq, HEAD_DIM), lambda g, i: (0, vcol + g)),
                  pl.BlockSpec((LANES, HEAD_DIM), lambda g, i: (0, g)),
                  pl.BlockSpec((HEADS_PER_KV, 2, MOBA_BLOCK, MOBA_BLOCK), lambda g, i: (g, 0, 0, 0))],
        out_specs=pl.BlockSpec((MOBA_BLOCK, gw), lambda g, i: (i, g)),
        out_shape=jax.ShapeDtypeStruct((seq, N_HEADS * HEAD_DIM), F32),
        scratch_shapes=[pltpu.VMEM((rows, 2 * HEAD_DIM), BF16),
                        pltpu.VMEM((rows, LANES), F32),
                        pltpu.VMEM((rows, LANES), F32),
                        pltpu.VMEM((rows, HEAD_DIM), F32)],
        compiler_params=_params("parallel", "arbitrary"),
        name="moba_attn_prompt",
    )(qkv_f32, qkv_b, qkv_b, qkv_b, kmean, tabs)


def _attn_sample_body(pt_ref, *refs, n_pages, page, dec_seq):
    del pt_ref
    qkv_ref = refs[0]
    kp = refs[1:1 + n_pages]
    vp = refs[1 + n_pages:1 + 2 * n_pages]
    tab_ref = refs[1 + 2 * n_pages]
    o_ref = refs[2 + 2 * n_pages]
    kall, vall, expand = refs[3 + 2 * n_pages:]
    past = n_pages * page
    n_past_blocks = past // MOBA_BLOCK
    pages_per_block = MOBA_BLOCK // page
    near0 = past - MOBA_BLOCK
    total = past + LANES
    dq = N_HEADS * HEAD_DIM
    dkv = N_KV_HEADS * HEAD_DIM
    hq = HEADS_PER_KV * dec_seq
    rows = N_HEADS * dec_seq
    pad = jnp.zeros((LANES - dec_seq, HEAD_DIM), F32)

    @pl.when(pl.program_id(0) == 0)
    def _():
        key_blk = lax.broadcasted_iota(jnp.int32, (LANES, total), 1) // MOBA_BLOCK
        expand[...] = (key_blk == lax.broadcasted_iota(jnp.int32, (LANES, total), 0)).astype(BF16)

    kmeans = []
    for g in range(N_KV_HEADS):
        means = []
        for blk in range(n_past_blocks):
            tot = jnp.zeros((1, HEAD_DIM), F32)
            for pp in range(pages_per_block):
                pg = blk * pages_per_block + pp
                kk = kp[pg][pl.ds(g, page, stride=N_KV_HEADS), :]
                vv = vp[pg][pl.ds(g, page, stride=N_KV_HEADS), :]
                tot = tot + jnp.sum(kk, axis=0, keepdims=True)
                kall[g, pg * page:(pg + 1) * page, :] = kk.astype(BF16)
                vall[g, pg * page:(pg + 1) * page, :] = vv.astype(BF16)
            means.append(tot * (1.0 / MOBA_BLOCK))
        kmeans.append(jnp.concatenate(means + [jnp.zeros((LANES - n_past_blocks, HEAD_DIM), F32)], axis=0))
        knew = qkv_ref[:, dq + g * HEAD_DIM:dq + (g + 1) * HEAD_DIM]
        vnew = qkv_ref[:, dq + dkv + g * HEAD_DIM:dq + dkv + (g + 1) * HEAD_DIM]
        kall[g, past:total, :] = jnp.concatenate([knew, pad], axis=0).astype(BF16)
        vall[g, past:total, :] = jnp.concatenate([vnew, pad], axis=0).astype(BF16)

    qs = jnp.concatenate([qkv_ref[:, h * HEAD_DIM:(h + 1) * HEAD_DIM] for h in range(N_HEADS)], axis=0)
    gate_all = lax.dot_general(qs, jnp.concatenate(kmeans, axis=0), _NT, precision=HIGHEST,
                               preferred_element_type=F32)
    row_g = lax.broadcasted_iota(jnp.int32, (rows, LANES), 0) // hq
    gate = gate_all[:, :LANES]
    for g in range(1, N_KV_HEADS):
        gate = jnp.where(row_g == g, gate_all[:, g * LANES:(g + 1) * LANES], gate)
    selm = _select_blocks(gate, n_past_blocks).astype(BF16)

    qb = (qs * (HEAD_DIM ** -0.5)).astype(BF16)
    s = jnp.concatenate([lax.dot_general(qb[g * hq:(g + 1) * hq], kall[g], _NT, preferred_element_type=F32)
                         for g in range(N_KV_HEADS)], axis=0)
    s = s + _dot(selm, expand[...])
    s_far = s[:, :near0]
    s_near = s[:, near0:] + tab_ref[...].reshape(rows, MOBA_BLOCK + LANES)
    m = jnp.maximum(jnp.max(s_far, axis=1, keepdims=True), jnp.max(s_near, axis=1, keepdims=True))
    p_far = jnp.exp(s_far - m)
    p_near = jnp.exp(s_near - m)
    inv_l = 1.0 / (jnp.sum(p_far, axis=1, keepdims=True) + jnp.sum(p_near, axis=1, keepdims=True))
    p_far = p_far.astype(BF16)
    p_near = p_near.astype(BF16)
    for g in range(N_KV_HEADS):
        rs = slice(g * hq, (g + 1) * hq)
        out = (_dot(p_far[rs], vall[g, :near0, :]) + _dot(p_near[rs], vall[g, near0:, :])) * inv_l[rs]
        for hh in range(HEADS_PER_KV):
            c0 = (g * HEADS_PER_KV + hh) * HEAD_DIM
            o_ref[:, c0:c0 + HEAD_DIM] = out[hh * dec_seq:(hh + 1) * dec_seq]


def _attn_sample(page_table, qkv_f32, cache_k, cache_v, tab_s, row0, dec_seq):
    dec_batch, n_pages = page_table.shape
    page = cache_k.shape[1] // N_KV_HEADS
    total = n_pages * page + LANES
    blk0 = row0 // dec_seq

    def page_spec(p):
        return pl.BlockSpec((None, page * N_KV_HEADS, HEAD_DIM), lambda b, pt, p=p: (pt[b, p], 0, 0))

    body = functools.partial(_attn_sample_body, n_pages=n_pages, page=page, dec_seq=dec_seq)
    grid_spec = pltpu.PrefetchScalarGridSpec(
        num_scalar_prefetch=1,
        grid=(dec_batch,),
        in_specs=([pl.BlockSpec((dec_seq, qkv_f32.shape[1]), lambda b, pt: (blk0 + b, 0))]
                  + [page_spec(p) for p in range(n_pages)]
                  + [page_spec(p) for p in range(n_pages)]
                  + [pl.BlockSpec(tab_s.shape, lambda b, pt: (0, 0, 0))]),
        out_specs=pl.BlockSpec((dec_seq, N_HEADS * HEAD_DIM), lambda b, pt: (b, 0)),
        scratch_shapes=[pltpu.VMEM((N_KV_HEADS, total, HEAD_DIM), BF16),
                        pltpu.VMEM((N_KV_HEADS, total, HEAD_DIM), BF16),
                        pltpu.VMEM((LANES, total), BF16)],
    )
    return pl.pallas_call(
        body,
        grid_spec=grid_spec,
        out_shape=jax.ShapeDtypeStruct((dec_batch * dec_seq, N_HEADS * HEAD_DIM), F32),
        compiler_params=_params("arbitrary"),
        name="moba_attn_sample",
    )(page_table, qkv_f32, *([cache_k] * n_pages), *([cache_v] * n_pages), tab_s)


def _proj_ln_body(*refs, glu, n_tiles, n_first):
    if glu:
        a1_ref, a2_ref, w_ref, w2_ref, x_ref, g_ref, b_ref, o_ref, a_scr, z_scr = refs
    else:
        a1_ref, a2_ref, w_ref, x_ref, g_ref, b_ref, o_ref, a_scr, z_scr = refs
    i = pl.program_id(0)
    j = pl.program_id(1)

    @pl.when((j == 0) & (i < n_first))
    def _():
        a_scr[...] = a1_ref[...].astype(BF16)

    @pl.when((j == 0) & (i >= n_first))
    def _():
        a_scr[...] = a2_ref[...].astype(BF16)

    a = a_scr[...]
    z = _dot(a, w_ref[...])
    if glu:
        z = z * _sigmoid(_dot(a, w2_ref[...]))
    z_scr[j] = z

    @pl.when(j == n_tiles - 1)
    def _():
        zfull = jnp.concatenate([z_scr[t] for t in range(n_tiles)], axis=1)
        o_ref[...] = _layer_norm(ALPHA * x_ref[...] + zfull, g_ref[...], b_ref[...])


def _proj_ln(a1, a2, ws, x, g, b, tm, tn):
    m, n = x.shape
    k = a1.shape[1]
    n_first = a1.shape[0] // tm
    assert a1.shape[0] % tm == 0 and a2.shape[0] % tm == 0 and a1.shape[0] + a2.shape[0] == m
    n_tiles = n // tn
    glu = len(ws) == 2
    body = functools.partial(_proj_ln_body, glu=glu, n_tiles=n_tiles, n_first=n_first)
    row = pl.BlockSpec((tm, n), lambda i, j: (i, 0))
    vec = pl.BlockSpec((1, n), lambda i, j: (0, 0))
    return pl.pallas_call(
        body,
        grid=(m // tm, n_tiles),
        in_specs=([pl.BlockSpec((tm, k), lambda i, j: (jnp.minimum(i, n_first - 1), 0)),
                   pl.BlockSpec((tm, k), lambda i, j: (jnp.maximum(i - n_first, 0), 0))]
                  + [pl.BlockSpec((k, tn), lambda i, j: (0, j)) for _ in ws]
                  + [row, vec, vec]),
        out_specs=row,
        out_shape=jax.ShapeDtypeStruct((m, n), F32),
        scratch_shapes=[pltpu.VMEM((tm, k), BF16), pltpu.VMEM((n_tiles, tm, tn), F32)],
        compiler_params=_params("parallel", "arbitrary"),
        name="glu_res_ln" if glu else "proj_res_ln",
    )(a1, a2, *ws, x, g.reshape(1, n), b.reshape(1, n))


def _ffn_body(te_ref, tr_ref, *refs, n_f, dense, half):
    del te_ref
    if dense:
        x_ref, wg_ref, wu_ref, wd_ref, g_ref, b_ref, o_ref, xb = refs[:8]
    else:
        x_ref, wg_ref, wu_ref, wd_ref, o_ref, xb = refs[:6]
    n_half = x_ref.shape[0] // half
    w_scr = refs[-3:] if n_half > 1 else None
    t = pl.program_id(0)
    f = pl.program_id(1)
    n_rows = tr_ref[t]

    @pl.when(f == 0)
    def _():
        xb[...] = x_ref[...].astype(BF16)
        o_ref[...] = jnp.zeros_like(o_ref)

    for hh in range(n_half):
        @pl.when(n_rows > hh * half)
        def _(hh=hh):
            rs = slice(hh * half, (hh + 1) * half)
            x = xb[rs, :]
            if hh == 0:
                wg = wg_ref[...].astype(BF16)
                wu = wu_ref[...].astype(BF16)
                wd = wd_ref[...].astype(BF16)
                if w_scr is not None:
                    w_scr[0][...] = wg
                    w_scr[1][...] = wu
                    w_scr[2][...] = wd
            else:
                wg, wu, wd = w_scr[0][...], w_scr[1][...], w_scr[2][...]
            hg = _dot(x, wg)
            hu = _dot(x, wu)
            h = (hg * _sigmoid(hg)) * hu
            o_ref[rs, :] += _dot(h.astype(BF16), wd)

    if dense:
        @pl.when(f == n_f - 1)
        def _():
            o_ref[...] = _layer_norm(ALPHA * x_ref[...] + o_ref[...], g_ref[...], b_ref[...])


def _ffn_call(tile_expert, tile_rows, x, wg, wu, wd, extra, extra_specs, tm, tf, half, dense, name):
    s, d = x.shape
    n_t = s // tm
    n_f = wg.shape[-1] // tf
    live = lambda t, tr: jnp.minimum(tr[t], 1)
    x_spec = pl.BlockSpec((tm, d), lambda t, f, te, tr: (t, 0))
    wgu = pl.BlockSpec((None, d, tf), lambda t, f, te, tr: (te[t], 0, f * live(t, tr)))
    wds = pl.BlockSpec((None, tf, d), lambda t, f, te, tr: (te[t], f * live(t, tr), 0))
    grid_spec = pltpu.PrefetchScalarGridSpec(
        num_scalar_prefetch=2, grid=(n_t, n_f),
        in_specs=[x_spec, wgu, wgu, wds] + extra_specs,
        out_specs=x_spec,
        scratch_shapes=[pltpu.VMEM((tm, d), BF16)] + ([] if half == tm else [
            pltpu.VMEM((d, tf), BF16), pltpu.VMEM((d, tf), BF16), pltpu.VMEM((tf, d), BF16)]))
    return pl.pallas_call(
        functools.partial(_ffn_body, n_f=n_f, dense=dense, half=half),
        grid_spec=grid_spec,
        out_shape=jax.ShapeDtypeStruct((s, d), F32),
        compiler_params=_params("parallel", "arbitrary"),
        name=name,
    )(tile_expert, tile_rows, x, wg, wu, wd, *extra)


def _ffn_dense(x, wg, wu, wd, g, b, tm, tf):
    m, d = x.shape
    n_t = m // tm
    vec = pl.BlockSpec((1, d), lambda t, f, te, tr: (0, 0))
    return _ffn_call(jnp.zeros((n_t,), jnp.int32), jnp.full((n_t,), tm, jnp.int32), x, wg, wu, wd,
                     [g.reshape(1, d), b.reshape(1, d)], [vec, vec], tm, tf, tm, True, "ffn_res_ln")


def _ffn_moe(tile_expert, tile_rows, xs, wg, wu, wd, tm, tf, half):
    return _ffn_call(tile_expert, tile_rows, xs, wg, wu, wd, [], [], tm, tf, half, False, "moe_ffn")


def _router_body(y_ref, w_ref, i_ref, g_ref):
    logits = jnp.dot(y_ref[...], w_ref[...], precision=HIGHEST, preferred_element_type=F32)
    lane = lax.broadcasted_iota(jnp.int32, logits.shape, 1)
    lane_f = lane.astype(F32)
    l1 = jnp.where(lane < N_EXPERTS, logits, -jnp.inf)
    m1 = jnp.max(l1, axis=1, keepdims=True)
    i1 = jnp.min(jnp.where(l1 == m1, lane_f, float(LANES)), axis=1, keepdims=True)
    l2 = jnp.where(lane_f == i1, -jnp.inf, l1)
    m2 = jnp.max(l2, axis=1, keepdims=True)
    i2 = jnp.min(jnp.where(l2 == m2, lane_f, float(LANES)), axis=1, keepdims=True)
    e = jnp.exp(m2 - m1)
    g1 = 1.0 / (1.0 + e)
    g2 = e / (1.0 + e)
    i_ref[...] = jnp.where(lane == 0, i1, jnp.where(lane == 1, i2, 0.0)).astype(jnp.int32)
    g_ref[...] = jnp.where(lane == 0, g1, jnp.where(lane == 1, g2, 0.0))


def _router(y, w_router, tm):
    m, d = y.shape
    wr = jnp.zeros((d, LANES), F32).at[:, :N_EXPERTS].set(w_router)
    row = pl.BlockSpec((tm, LANES), lambda i: (i, 0))
    return pl.pallas_call(
        _router_body,
        grid=(m // tm,),
        in_specs=[pl.BlockSpec((tm, d), lambda i: (i, 0)), pl.BlockSpec((d, LANES), lambda i: (0, 0))],
        out_specs=[row, row],
        out_shape=[jax.ShapeDtypeStruct((m, LANES), jnp.int32), jax.ShapeDtypeStruct((m, LANES), F32)],
        compiler_params=_params("parallel"),
        name="moe_router",
    )(y, wr)


def _combine_ln_body(x_ref, a_ref, b2_ref, tg_ref, g_ref, b_ref, op_ref, os_ref, *, n_first):
    i = pl.program_id(0)
    tg = tg_ref[...]
    moe = tg[:, 0:1] * a_ref[...] + tg[:, 1:2] * b2_ref[...]
    out = _layer_norm(ALPHA * x_ref[...] + moe, g_ref[...], b_ref[...])

    @pl.when(i < n_first)
    def _():
        op_ref[...] = out

    @pl.when(i >= n_first)
    def _():
        os_ref[...] = out


def _combine_ln(x, pairs, top_gate, g, b, m_first, tm):
    m, d = x.shape
    n_first = m_first // tm
    vec = pl.BlockSpec((1, d), lambda i: (0, 0))
    return pl.pallas_call(
        functools.partial(_combine_ln_body, n_first=n_first),
        grid=(m // tm,),
        in_specs=[pl.BlockSpec((tm, d), lambda i: (i, 0)),
                  pl.BlockSpec((None, tm, d), lambda i: (0, i, 0)),
                  pl.BlockSpec((None, tm, d), lambda i: (1, i, 0)),
                  pl.BlockSpec((tm, LANES), lambda i: (i, 0)),
                  vec, vec],
        out_specs=[pl.BlockSpec((tm, d), lambda i: (jnp.minimum(i, n_first - 1), 0)),
                   pl.BlockSpec((tm, d), lambda i: (jnp.maximum(i - n_first, 0), 0))],
        out_shape=[jax.ShapeDtypeStruct((m_first, d), F32), jax.ShapeDtypeStruct((m - m_first, d), F32)],
        compiler_params=_params("arbitrary"),
        name="moe_combine_ln",
    )(x, pairs, pairs, top_gate, g.reshape(1, d), b.reshape(1, d))


def _moe_routing(top_idx, tm):
    m = top_idx.shape[0]
    n_pairs = m * TOP_K
    n_tiles = n_pairs // tm + N_EXPERTS
    i32 = jnp.int32
    e_flat = jnp.concatenate([top_idx[:, k] for k in range(TOP_K)])
    order = jnp.argsort(e_flat, stable=True).astype(i32)
    inv = jnp.argsort(order).astype(i32)
    counts = jnp.sum((e_flat[None, :] == jnp.arange(N_EXPERTS, dtype=i32)[:, None]).astype(i32), axis=1)
    padded = ((counts + tm - 1) // tm) * tm
    ends_padded = jnp.cumsum(padded)
    starts_padded = ends_padded - padded
    starts = jnp.cumsum(counts) - counts
    tile_start = jnp.arange(n_tiles, dtype=i32) * tm
    tile_expert = jnp.minimum(jnp.searchsorted(ends_padded, tile_start, side="right"), N_EXPERTS - 1).astype(i32)
    tile_rows = jnp.clip(counts[tile_expert] - (tile_start - starts_padded[tile_expert]), 0, tm).astype(i32)
    slot = jnp.arange(n_tiles * tm, dtype=i32)
    slot_e = jnp.repeat(tile_expert, tm)
    rank = slot - starts_padded[slot_e]
    live = (rank >= 0) & (rank < counts[slot_e])
    src = order[jnp.clip(starts[slot_e] + rank, 0, n_pairs - 1)]
    slot_token = jnp.where(live, src % m, slot % m)
    pair_slot = starts_padded[e_flat] + inv - starts[e_flat]
    return tile_expert, tile_rows, slot_token, pair_slot


def _gelu_tanh(x):
    return 0.5 * x * (1.0 + jnp.tanh(math.sqrt(2.0 / math.pi) * (x + 0.044715 * (x * x * x))))


def _ssm_body(*refs, sequential):
    (u_ref, bd_ref, wbr_ref, wbi_ref, vcr_ref, vci_ref, a8r_ref, a8i_ref, d_ref, h0r_ref, h0i_ref,
     y_ref, hr_ref, hi_ref) = refs[:14]
    scr = refs[14:]
    n_chunks = u_ref.shape[0]
    ub = [u_ref[:, s, :].astype(BF16) for s in range(SSM_CHUNK)]
    hl_r = _dot(ub[0], wbr_ref[0])
    hl_i = _dot(ub[0], wbi_ref[0])
    for s in range(1, SSM_CHUNK):
        hl_r = hl_r + _dot(ub[s], wbr_ref[s])
        hl_i = hl_i + _dot(ub[s], wbi_ref[s])
    a8r = a8r_ref[...]
    a8i = a8i_ref[...]
    if sequential:
        hlr_scr, hli_scr, hinr_scr, hini_scr = scr
        hlr_scr[...] = hl_r
        hli_scr[...] = hl_i

        def tile_step(kb, carry):
            hr, hi = carry
            base = pl.multiple_of(kb * SUBLANES, SUBLANES)
            tr = hlr_scr[pl.ds(base, SUBLANES), :]
            ti = hli_scr[pl.ds(base, SUBLANES), :]
            rows_r, rows_i = [], []
            for r in range(SUBLANES):
                rows_r.append(hr)
                rows_i.append(hi)
                hr, hi = (a8r * hr - a8i * hi + tr[r:r + 1], a8r * hi + a8i * hr + ti[r:r + 1])
            hinr_scr[pl.ds(base, SUBLANES), :] = jnp.concatenate(rows_r, axis=0)
            hini_scr[pl.ds(base, SUBLANES), :] = jnp.concatenate(rows_i, axis=0)
            return hr, hi

        hr, hi = lax.fori_loop(0, n_chunks // SUBLANES, tile_step, (h0r_ref[...], h0i_ref[...]))
        hr_ref[...] = hr
        hi_ref[...] = hi
        hin_r = hinr_scr[...]
        hin_i = hini_scr[...]
    else:
        hin_r = h0r_ref[...]
        hin_i = h0i_ref[...]
        hr_ref[...] = a8r * hin_r - a8i * hin_i + hl_r
        hi_ref[...] = a8r * hin_i + a8i * hin_r + hl_i
    hb_r = hin_r.astype(BF16)
    hb_i = hin_i.astype(BF16)
    d = d_ref[...]
    for t in range(SSM_CHUNK):
        y = _dot(hb_r, vcr_ref[t]) + _dot(hb_i, vci_ref[t])
        for s in range(t + 1):
            y = y + _dot(ub[s], bd_ref[t - s])
        y_ref[:, t, :] = _gelu_tanh(y + d * u_ref[:, t, :])


def _group_blockdiag(x, rows_per_group, lanes_per_group):
    t_n, g_n, a_n, b_n = x.shape
    gpb = LANES // SSM_GROUP
    nb = g_n // gpb
    assert a_n == rows_per_group and b_n == lanes_per_group
    x = x.reshape(t_n, nb, gpb * a_n, b_n)
    tile = np.tile(np.eye(b_n, dtype=np.float32), (1, gpb))
    x = jnp.einsum("tnrb,bl->ntrl", x, jnp.asarray(tile), precision=HIGHEST)
    r_grp = np.arange(gpb * a_n)[:, None] // a_n
    l_grp = np.arange(gpb * b_n)[None, :] // b_n
    return jnp.where(jnp.asarray(r_grp == l_grp), x, 0.0).astype(BF16)


def _ssm_operators(a_re, a_im, log_dt, b_re, b_im, c_re, c_im):
    g_n, p_n = a_re.shape
    lam_re = jnp.minimum(a_re, -1e-4)
    lam_im = a_im
    dt = jnp.exp(log_dt)[:, None]
    decay = jnp.exp(lam_re * dt)
    ar = decay * jnp.cos(lam_im * dt)
    ai = decay * jnp.sin(lam_im * dt)
    num_re = ar - 1.0
    den = lam_re * lam_re + lam_im * lam_im
    f_re = (num_re * lam_re + ai * lam_im) / den
    f_im = (ai * lam_re - num_re * lam_im) / den
    bt_re = jnp.swapaxes(b_re, 1, 2)
    bt_im = jnp.swapaxes(b_im, 1, 2)
    bb_re = f_re[:, None, :] * bt_re - f_im[:, None, :] * bt_im
    bb_im = f_re[:, None, :] * bt_im + f_im[:, None, :] * bt_re
    pw_re, pw_im = [jnp.ones_like(ar)], [jnp.zeros_like(ar)]
    for _ in range(SSM_CHUNK):
        pr, pi = pw_re[-1], pw_im[-1]
        pw_re.append(pr * ar - pi * ai)
        pw_im.append(pr * ai + pi * ar)
    pw_re = jnp.stack(pw_re)
    pw_im = jnp.stack(pw_im)
    pk_re = pw_re[:SSM_CHUNK, :, None, :]
    pk_im = pw_im[:SSM_CHUNK, :, None, :]
    ab_re = pk_re * bb_re - pk_im * bb_im
    ab_im = pk_re * bb_im + pk_im * bb_re
    conv = (jnp.einsum("gcp,tgdp->tgdc", c_re, ab_re, precision=HIGHEST)
            - jnp.einsum("gcp,tgdp->tgdc", c_im, ab_im, precision=HIGHEST))
    bd = _group_blockdiag(conv, SSM_GROUP, SSM_GROUP)
    wb_re = _group_blockdiag(ab_re[::-1], SSM_GROUP, p_n)
    wb_im = _group_blockdiag(ab_im[::-1], SSM_GROUP, p_n)
    pr = pw_re[1:, :, :, None]
    pi = pw_im[1:, :, :, None]
    ct_re = jnp.swapaxes(c_re, 1, 2)[None]
    ct_im = jnp.swapaxes(c_im, 1, 2)[None]
    vc_re = _group_blockdiag(ct_re * pr - ct_im * pi, p_n, SSM_GROUP)
    vc_im = _group_blockdiag(-(ct_re * pi + ct_im * pr), p_n, SSM_GROUP)
    a8r = pw_re[SSM_CHUNK].reshape(1, g_n * p_n)
    a8i = pw_im[SSM_CHUNK].reshape(1, g_n * p_n)
    return bd, wb_re, wb_im, vc_re, vc_im, a8r, a8i


def _ssm_scan(u3, ops, d_skip, h0_re, h0_im, chunk0, n_chunks, sequential):
    bd, wb_re, wb_im, vc_re, vc_im, a8r, a8i = ops
    nb = bd.shape[0]
    sw = wb_re.shape[-1]
    cblk = chunk0 // n_chunks
    n_state_rows = 1 if sequential else n_chunks
    op4 = lambda shp: pl.BlockSpec((None,) + shp, lambda j: (j, 0, 0, 0))
    srow = pl.BlockSpec((1, sw), lambda j: (0, j))
    hspec = pl.BlockSpec((n_state_rows, sw), lambda j: (0, j))
    scratch = [pltpu.VMEM((n_chunks, sw), F32)] * 4 if sequential else []
    return pl.pallas_call(
        functools.partial(_ssm_body, sequential=sequential),
        grid=(nb,),
        in_specs=[pl.BlockSpec((n_chunks, SSM_CHUNK, LANES), lambda j: (cblk, 0, j)),
                  op4(bd.shape[1:]), op4(wb_re.shape[1:]), op4(wb_im.shape[1:]),
                  op4(vc_re.shape[1:]), op4(vc_im.shape[1:]),
                  srow, srow, pl.BlockSpec((1, LANES), lambda j: (0, j)), hspec, hspec],
        out_specs=[pl.BlockSpec((n_chunks, SSM_CHUNK, LANES), lambda j: (0, 0, j)), hspec, hspec],
        out_shape=[jax.ShapeDtypeStruct((n_chunks, SSM_CHUNK, u3.shape[2]), F32),
                   jax.ShapeDtypeStruct((n_state_rows, h0_re.shape[1]), F32),
                   jax.ShapeDtypeStruct((n_state_rows, h0_re.shape[1]), F32)],
        scratch_shapes=scratch,
        compiler_params=_params("parallel"),
        name="s5_scan_seq" if sequential else "s5_scan_step",
    )(u3, bd, wb_re, wb_im, vc_re, vc_im, a8r, a8i, d_skip.reshape(1, -1), h0_re, h0_im)


def kernel(x_prompt, x_sample, cache_k, cache_v, state_ssm_re, state_ssm_im, page_table, rel_bias, ln_g, ln_b,
           w_qkv, w_o, w_ssm_in, ssm_a_re, ssm_a_im, ssm_log_dt, ssm_b_re, ssm_b_im, ssm_c_re, ssm_c_im, ssm_d,
           w_glu_v, w_glu_g, w_ff_gate, w_ff_up, w_ff_down, w_router, w_moe_gate, w_moe_up, w_moe_down):
    batch, seq, d = x_prompt.shape
    dec_batch, dec_seq, _ = x_sample.shape
    assert batch == 1 and d == D_MODEL and dec_seq == SSM_CHUNK and seq % MOBA_BLOCK == 0
    m_p = batch * seq
    m_s = dec_batch * dec_seq
    m = m_p + m_s
    dq = N_HEADS * HEAD_DIM
    dkv = N_KV_HEADS * HEAD_DIM
    n_pool, page = cache_k.shape[1], cache_k.shape[2]

    x = jnp.concatenate([x_prompt.reshape(m_p, d), x_sample.reshape(m_s, d)], axis=0)

    qkv_f32, qkv_b = _qkv_proj(x.astype(BF16), w_qkv[0].astype(BF16), tm=1024, tn=512)
    kmean = _block_means(qkv_f32, seq // MOBA_BLOCK)
    tabs, tab_s = _bias_tables(rel_bias)
    attn_p = _attn_prompt(qkv_f32, qkv_b, kmean, tabs, seq)
    ck = cache_k[0].reshape(n_pool, page * N_KV_HEADS, HEAD_DIM)
    cv = cache_v[0].reshape(n_pool, page * N_KV_HEADS, HEAD_DIM)
    attn_s = _attn_sample(page_table, qkv_f32, ck, cv, tab_s, m_p, dec_seq)
    y = _proj_ln(attn_p, attn_s, [w_o[0].astype(BF16)], x, ln_g[0, 0], ln_b[0, 0], tm=512, tn=512)
    y = _ffn_dense(y, w_ff_gate, w_ff_up, w_ff_down, ln_g[0, 1], ln_b[0, 1], tm=1024, tf=256)

    k_all = qkv_f32[:, dq:dq + dkv]
    v_all = qkv_f32[:, dq + dkv:]
    k_prompt = k_all[:m_p].reshape(1, batch, seq, N_KV_HEADS, HEAD_DIM)
    v_prompt = v_all[:m_p].reshape(1, batch, seq, N_KV_HEADS, HEAD_DIM)
    k_sample = k_all[m_p:].reshape(1, dec_batch, dec_seq, N_KV_HEADS, HEAD_DIM)
    v_sample = v_all[m_p:].reshape(1, dec_batch, dec_seq, N_KV_HEADS, HEAD_DIM)

    u = _matmul(y, w_ssm_in[0].astype(BF16), tm=1024, tn=512)
    u3 = u.reshape(m // SSM_CHUNK, SSM_CHUNK, d)
    ops = _ssm_operators(ssm_a_re[0], ssm_a_im[0], ssm_log_dt[0], ssm_b_re[0], ssm_b_im[0],
                         ssm_c_re[0], ssm_c_im[0])
    n_state = N_SSM_GROUPS * SSM_STATE
    zero = jnp.zeros((batch, n_state), F32)
    yg_p, hrp, hip = _ssm_scan(u3, ops, ssm_d[0], zero, zero, 0, m_p // SSM_CHUNK, True)
    yg_s, hrs, his = _ssm_scan(u3, ops, ssm_d[0], state_ssm_re[0].reshape(dec_batch, n_state),
                               state_ssm_im[0].reshape(dec_batch, n_state), m_p // SSM_CHUNK, dec_batch, False)
    y = _proj_ln(yg_p.reshape(m_p, d), yg_s.reshape(m_s, d), [w_glu_v[0].astype(BF16), w_glu_g[0].astype(BF16)],
                 y, ln_g[1, 0], ln_b[1, 0], tm=512, tn=512)

    tm_moe = 1024
    top_idx, top_gate = _router(y, w_router[0], tm=512)
    tile_expert, tile_rows, slot_token, pair_slot = _moe_routing(top_idx, tm_moe)
    xs = jnp.take(y, slot_token, axis=0)
    ys = _ffn_moe(tile_expert, tile_rows, xs, w_moe_gate[0], w_moe_up[0], w_moe_down[0], tm_moe, tf=256, half=512)
    pairs = jnp.take(ys, pair_slot, axis=0).reshape(TOP_K, m, d)
    out_p, out_s = _combine_ln(y, pairs, top_gate, ln_g[1, 1], ln_b[1, 1], m_p, tm=512)

    return (out_p.reshape(batch, seq, d), out_s.reshape(dec_batch, dec_seq, d),
            k_prompt, v_prompt, k_sample, v_sample,
            hrp.reshape(1, batch, N_SSM_GROUPS, SSM_STATE), hip.reshape(1, batch, N_SSM_GROUPS, SSM_STATE),
            hrs.reshape(1, dec_batch, N_SSM_GROUPS, SSM_STATE), his.reshape(1, dec_batch, N_SSM_GROUPS, SSM_STATE))
```

```python
import functools
import math

import numpy as np
import jax
import jax.numpy as jnp
from jax import lax
from jax.experimental import pallas as pl
from jax.experimental.pallas import tpu as pltpu

D_MODEL = 2048
N_HEADS = 16
HEAD_DIM = D_MODEL // N_HEADS
N_KV_HEADS = 4
HEADS_PER_KV = N_HEADS // N_KV_HEADS
MOBA_BLOCK = 256
MOBA_TOPK = 3
NUM_BUCKETS = 32
MAX_DISTANCE = 128
SSM_GROUP = 16
N_SSM_GROUPS = D_MODEL // SSM_GROUP
SSM_STATE = 64
SSM_CHUNK = 8
D_FF = 7 * D_MODEL // 2
N_EXPERTS = 8
TOP_K = 2
DEPTH = 2
ALPHA = (2 * DEPTH) ** 0.25
LN_EPS = 1e-5
NEG_INF = -1e30

LANES = 128
SUBLANES = 8
VMEM_LIMIT = 60 * 1024 * 1024

F32 = jnp.float32
BF16 = jnp.bfloat16
HIGHEST = lax.Precision.HIGHEST
_NT = (((1,), (1,)), ((), ()))


def _params(*sem):
    return pltpu.CompilerParams(dimension_semantics=sem, vmem_limit_bytes=VMEM_LIMIT)


def _dot(a, b):
    return jnp.dot(a, b, preferred_element_type=F32)


def _sigmoid(x):
    return 1.0 / (1.0 + jnp.exp(-x))


def _layer_norm(y, g, b):
    mean = jnp.mean(y, axis=-1, keepdims=True)
    yc = y - mean
    var = jnp.mean(yc * yc, axis=-1, keepdims=True)
    return yc * lax.rsqrt(var + LN_EPS) * g + b


def _mm_body(x_ref, w_ref, o_ref, xb_scr):
    @pl.when(pl.program_id(1) == 0)
    def _():
        xb_scr[...] = x_ref[...].astype(BF16)

    o_ref[...] = _dot(xb_scr[...], w_ref[...])


def _matmul(x, w, tm, tn):
    m, k = x.shape
    n = w.shape[1]
    return pl.pallas_call(
        _mm_body,
        grid=(m // tm, n // tn),
        in_specs=[pl.BlockSpec((tm, k), lambda i, j: (i, 0)),
                  pl.BlockSpec((k, tn), lambda i, j: (0, j))],
        out_specs=pl.BlockSpec((tm, tn), lambda i, j: (i, j)),
        out_shape=jax.ShapeDtypeStruct((m, n), F32),
        scratch_shapes=[pltpu.VMEM((tm, k), BF16)],
        compiler_params=_params("parallel", "arbitrary"),
        name="matmul",
    )(x, w)


def _split_rows(block, n_first):
    return [pl.BlockSpec(block, lambda i, j: (jnp.minimum(i, n_first - 1), 0)),
            pl.BlockSpec(block, lambda i, j: (jnp.maximum(i - n_first, 0), 0))]


def _load_split_bf16(dst, src1, src2, n_first):
    i = pl.program_id(0)
    j = pl.program_id(1)

    @pl.when((j == 0) & (i < n_first))
    def _():
        dst[...] = src1[...].astype(BF16)

    @pl.when((j == 0) & (i >= n_first))
    def _():
        dst[...] = src2[...].astype(BF16)


def _qkv_body(x1_ref, x2_ref, w_ref, of_ref, ob_ref, xb_scr, *, nq_tiles, scale, n_first):
    _load_split_bf16(xb_scr, x1_ref, x2_ref, n_first)
    acc = _dot(xb_scr[...], w_ref[...])
    of_ref[...] = acc
    s = jnp.where(pl.program_id(1) < nq_tiles, scale, 1.0).astype(F32)
    ob_ref[...] = (acc * s).astype(BF16)


def _qkv_proj(x1, x2, w, tm, tn):
    k = x1.shape[1]
    m = x1.shape[0] + x2.shape[0]
    n = w.shape[1]
    n_first = x1.shape[0] // tm
    assert x1.shape[0] % tm == 0 and x2.shape[0] % tm == 0
    body = functools.partial(_qkv_body, nq_tiles=(N_HEADS * HEAD_DIM) // tn, scale=HEAD_DIM ** -0.5,
                             n_first=n_first)
    return pl.pallas_call(
        body,
        grid=(m // tm, n // tn),
        in_specs=_split_rows((tm, k), n_first) + [pl.BlockSpec((k, tn), lambda i, j: (0, j))],
        out_specs=[pl.BlockSpec((tm, tn), lambda i, j: (i, j)),
                   pl.BlockSpec((tm, tn), lambda i, j: (i, j))],
        out_shape=[jax.ShapeDtypeStruct((m, n), F32), jax.ShapeDtypeStruct((m, n), BF16)],
        scratch_shapes=[pltpu.VMEM((tm, k), BF16)],
        compiler_params=_params("parallel", "arbitrary"),
        name="qkv_proj",
    )(x1, x2, w)


def _kmean_body(k_ref, o_ref):
    i = pl.program_id(0)

    @pl.when(i == 0)
    def _():
        o_ref[...] = jnp.zeros_like(o_ref)

    mean = jnp.sum(k_ref[...], axis=0, keepdims=True) * (1.0 / MOBA_BLOCK)
    rows = lax.broadcasted_iota(jnp.int32, o_ref.shape, 0)
    o_ref[...] = jnp.where(rows == i, mean, o_ref[...])


def _block_means(qkv_f32, n_blocks):
    dkv = N_KV_HEADS * HEAD_DIM
    kcol = (N_HEADS * HEAD_DIM) // dkv
    return pl.pallas_call(
        _kmean_body,
        grid=(n_blocks,),
        in_specs=[pl.BlockSpec((MOBA_BLOCK, dkv), lambda i: (i, kcol))],
        out_specs=pl.BlockSpec((LANES, dkv), lambda i: (0, 0)),
        out_shape=jax.ShapeDtypeStruct((LANES, dkv), F32),
        compiler_params=_params("arbitrary"),
        name="moba_block_means",
    )(qkv_f32)


def _t5_bucket_np(dist):
    n = np.maximum(dist, 0)
    max_exact = NUM_BUCKETS // 2
    nf = np.maximum(n, 1).astype(np.float32)
    large = max_exact + (np.log(nf / np.float32(max_exact)) / np.float32(math.log(MAX_DISTANCE / max_exact))
                         * np.float32(NUM_BUCKETS - max_exact)).astype(np.int32)
    large = np.minimum(large, NUM_BUCKETS - 1)
    return np.where(n < max_exact, n, large).astype(np.int32)


def _bucket_tables():
    q = np.arange(MOBA_BLOCK)[:, None]
    k = np.arange(MOBA_BLOCK)[None, :]
    own = np.where(q - k >= 0, _t5_bucket_np(q - k), -1)
    prev = _t5_bucket_np(q - k + MOBA_BLOCK)
    return np.stack([own, prev]).astype(np.int32)


def _bias_body(rb_ref, idx_ref, o_ref, os_ref):
    h = pl.program_id(0)
    far = rb_ref[h, NUM_BUCKETS - 1]
    tabs = []
    for m in range(2):
        idx = idx_ref[m]
        acc = jnp.full(idx.shape, NEG_INF, F32)
        for b in range(NUM_BUCKETS):
            acc = jnp.where(idx == b, rb_ref[h, b] - far, acc)
        o_ref[0, m] = acc
        tabs.append(acc)
    os_ref[0] = jnp.concatenate([tabs[1][:SUBLANES, :], tabs[0][:SUBLANES, :LANES]], axis=1)


def _bias_tables(rel_bias):
    idx = jnp.asarray(_bucket_tables())
    return pl.pallas_call(
        _bias_body,
        grid=(N_HEADS,),
        in_specs=[pl.BlockSpec(memory_space=pltpu.SMEM),
                  pl.BlockSpec((2, MOBA_BLOCK, MOBA_BLOCK), lambda h: (0, 0, 0))],
        out_specs=[pl.BlockSpec((1, 2, MOBA_BLOCK, MOBA_BLOCK), lambda h: (h, 0, 0, 0)),
                   pl.BlockSpec((1, SUBLANES, MOBA_BLOCK + LANES), lambda h: (h, 0, 0))],
        out_shape=[jax.ShapeDtypeStruct((N_HEADS, 2, MOBA_BLOCK, MOBA_BLOCK), F32),
                   jax.ShapeDtypeStruct((N_HEADS, SUBLANES, MOBA_BLOCK + LANES), F32)],
        compiler_params=_params("parallel"),
        name="moba_bias_tables",
    )(rel_bias, idx)


def _select_blocks(gate, n_past):
    lane = lax.broadcasted_iota(jnp.int32, gate.shape, 1)
    lane_f = lane.astype(F32)
    g = jnp.where(lane < n_past, gate, NEG_INF)
    sel = jnp.full(gate.shape, NEG_INF, F32)
    for _ in range(MOBA_TOPK):
        mx = jnp.max(g, axis=1, keepdims=True)
        idx = jnp.min(jnp.where(g == mx, lane_f, float(gate.shape[1])), axis=1, keepdims=True)
        pick = lane_f == idx
        sel = jnp.where(pick, 0.0, sel)
        g = jnp.where(pick, -jnp.inf, g)
    return jnp.where(lane < n_past, sel, 0.0)


def _attn_prompt_body(qf_ref, qb_ref, k_ref, v_ref, km_ref, tab_ref, o_ref, qa_scr, m_scr, l_scr, acc_scr):
    i = pl.program_id(1)
    rows = HEADS_PER_KV * MOBA_BLOCK
    km = km_ref[...]
    lane = lax.broadcasted_iota(jnp.int32, (MOBA_BLOCK, LANES), 1)
    dummy = LANES - 1
    for hh in range(HEADS_PER_KV):
        cs = slice(hh * HEAD_DIM, (hh + 1) * HEAD_DIM)
        rs = slice(hh * MOBA_BLOCK, (hh + 1) * MOBA_BLOCK)
        gate = lax.dot_general(qf_ref[:, cs], km, _NT, precision=HIGHEST, preferred_element_type=F32)
        mask = jnp.where(lane == dummy, NEG_INF, _select_blocks(gate, i))
        qa_scr[rs, :HEAD_DIM] = qb_ref[:, cs]
        qa_scr[rs, HEAD_DIM:] = mask.astype(BF16)
    qa = qa_scr[...]

    def scores(j, mask_lane):
        start = pl.multiple_of(j * MOBA_BLOCK, MOBA_BLOCK)
        kj = k_ref[pl.ds(start, MOBA_BLOCK), :]
        vj = v_ref[pl.ds(start, MOBA_BLOCK), :]
        rhs = jnp.concatenate([kj, (lane == mask_lane).astype(BF16)], axis=1)
        return lax.dot_general(qa, rhs, _NT, preferred_element_type=F32), vj

    def softmax_parts(parts, m_new):
        m2 = jnp.concatenate([m_new, m_new], axis=1)
        l_add = None
        acc_add = None
        for s, vj in parts:
            p = jnp.exp(s - m2)
            ls = jnp.sum(p, axis=1, keepdims=True)
            pv = _dot(p.astype(BF16), vj)
            l_add = ls if l_add is None else l_add + ls
            acc_add = pv if acc_add is None else acc_add + pv
        return l_add, acc_add

    has_prev = i >= 1
    s_own, v_own = scores(i, -1)
    s_own = s_own + tab_ref[:, 0].reshape(rows, MOBA_BLOCK)
    s_prev, v_prev = scores(jnp.maximum(i - 1, 0), jnp.where(has_prev, i - 1, dummy))
    s_prev = s_prev + tab_ref[:, 1].reshape(rows, MOBA_BLOCK)
    m0 = jnp.maximum(jnp.max(s_own, axis=1, keepdims=True), jnp.max(s_prev, axis=1, keepdims=True))
    m0 = jnp.broadcast_to(m0, (rows, LANES))
    l0, acc0 = softmax_parts([(s_own, v_own), (s_prev, v_prev)], m0)
    m_scr[...] = m0
    l_scr[...] = jnp.broadcast_to(l0, (rows, LANES))
    acc_scr[...] = acc0

    n_far = jnp.maximum(i - 1, 0)

    def far_pair(t, carry):
        j0 = 2 * t
        j1 = j0 + 1
        ok1 = j1 < n_far
        parts = [scores(j0, j0), scores(jnp.where(ok1, j1, 0), jnp.where(ok1, j1, dummy))]
        m_prev = m_scr[...]
        m_new = m_prev
        for s, _ in parts:
            m_new = jnp.maximum(m_new, jnp.max(s, axis=1, keepdims=True))
        alpha = jnp.exp(m_prev - m_new)
        l_add, acc_add = softmax_parts(parts, m_new)
        l_scr[...] = alpha * l_scr[...] + l_add
        acc_scr[...] = alpha * acc_scr[...] + acc_add
        m_scr[...] = m_new
        return carry

    lax.fori_loop(0, (n_far + 1) // 2, far_pair, 0)

    out = acc_scr[...] / l_scr[...]
    for hh in range(HEADS_PER_KV):
        o_ref[:, hh * HEAD_DIM:(hh + 1) * HEAD_DIM] = out[hh * MOBA_BLOCK:(hh + 1) * MOBA_BLOCK]


def _attn_prompt(qkv_f32, qkv_b, kmean, tabs, seq):
    nblk = seq // MOBA_BLOCK
    assert nblk < LANES - 1
    gw = HEADS_PER_KV * HEAD_DIM
    kcol = (N_HEADS * HEAD_DIM) // HEAD_DIM
    vcol = kcol + N_KV_HEADS
    rows = HEADS_PER_KV * MOBA_BLOCK
    return pl.pallas_call(
        _attn_prompt_body,
        grid=(N_KV_HEADS, nblk),
        in_specs=[pl.BlockSpec((MOBA_BLOCK, gw), lambda g, i: (i, g)),
                  pl.BlockSpec((MOBA_BLOCK, gw), lambda g, i: (i, g)),
                  pl.BlockSpec((seq, HEAD_DIM), lambda g, i: (0, kcol + g)),
                  pl.BlockSpec((seq, HEAD_DIM), lambda g, i: (0, vcol + g)),
                  pl.BlockSpec((LANES, HEAD_DIM), lambda g, i: (0, g)),
                  pl.BlockSpec((HEADS_PER_KV, 2, MOBA_BLOCK, MOBA_BLOCK), lambda g, i: (g, 0, 0, 0))],
        out_specs=pl.BlockSpec((MOBA_BLOCK, gw), lambda g, i: (i, g)),
        out_shape=jax.ShapeDtypeStruct((seq, N_HEADS * HEAD_DIM), F32),
        scratch_shapes=[pltpu.VMEM((rows, 2 * HEAD_DIM), BF16),
                        pltpu.VMEM((rows, LANES), F32),
                        pltpu.VMEM((rows, LANES), F32),
                        pltpu.VMEM((rows, HEAD_DIM), F32)],
        compiler_params=_params("parallel", "arbitrary"),
        name="moba_attn_prompt",
    )(qkv_f32, qkv_b, qkv_b, qkv_b, kmean, tabs)


def _attn_sample_body(pt_ref, *refs, n_pages, page, dec_seq):
    del pt_ref
    qkv_ref = refs[0]
    kp = refs[1:1 + n_pages]
    vp = refs[1 + n_pages:1 + 2 * n_pages]
    tab_ref = refs[1 + 2 * n_pages]
    o_ref = refs[2 + 2 * n_pages]
    kall, vall, expand = refs[3 + 2 * n_pages:]
    past = n_pages * page
    n_past_blocks = past // MOBA_BLOCK
    pages_per_block = MOBA_BLOCK // page
    near0 = past - MOBA_BLOCK
    total = past + LANES
    dq = N_HEADS * HEAD_DIM
    dkv = N_KV_HEADS * HEAD_DIM
    hq = HEADS_PER_KV * dec_seq
    rows = N_HEADS * dec_seq
    pad = jnp.zeros((LANES - dec_seq, HEAD_DIM), F32)

    @pl.when(pl.program_id(0) == 0)
    def _():
        key_blk = lax.broadcasted_iota(jnp.int32, (LANES, total), 1) // MOBA_BLOCK
        expand[...] = (key_blk == lax.broadcasted_iota(jnp.int32, (LANES, total), 0)).astype(BF16)

    kmeans = []
    for g in range(N_KV_HEADS):
        means = []
        for blk in range(n_past_blocks):
            tot = jnp.zeros((1, HEAD_DIM), F32)
            for pp in range(pages_per_block):
                pg = blk * pages_per_block + pp
                kk = kp[pg][pl.ds(g, page, stride=N_KV_HEADS), :]
                vv = vp[pg][pl.ds(g, page, stride=N_KV_HEADS), :]
                tot = tot + jnp.sum(kk, axis=0, keepdims=True)
                kall[g, pg * page:(pg + 1) * page, :] = kk.astype(BF16)
                vall[g, pg * page:(pg + 1) * page, :] = vv.astype(BF16)
            means.append(tot * (1.0 / MOBA_BLOCK))
        kmeans.append(jnp.concatenate(means + [jnp.zeros((LANES - n_past_blocks, HEAD_DIM), F32)], axis=0))
        knew = qkv_ref[:, dq + g * HEAD_DIM:dq + (g + 1) * HEAD_DIM]
        vnew = qkv_ref[:, dq + dkv + g * HEAD_DIM:dq + dkv + (g + 1) * HEAD_DIM]
        kall[g, past:total, :] = jnp.concatenate([knew, pad], axis=0).astype(BF16)
        vall[g, past:total, :] = jnp.concatenate([vnew, pad], axis=0).astype(BF16)

    qs = jnp.concatenate([qkv_ref[:, h * HEAD_DIM:(h + 1) * HEAD_DIM] for h in range(N_HEADS)], axis=0)
    gate_all = lax.dot_general(qs, jnp.concatenate(kmeans, axis=0), _NT, precision=HIGHEST,
                               preferred_element_type=F32)
    row_g = lax.broadcasted_iota(jnp.int32, (rows, LANES), 0) // hq
    gate = gate_all[:, :LANES]
    for g in range(1, N_KV_HEADS):
        gate = jnp.where(row_g == g, gate_all[:, g * LANES:(g + 1) * LANES], gate)
    selm = _select_blocks(gate, n_past_blocks).astype(BF16)

    qb = (qs * (HEAD_DIM ** -0.5)).astype(BF16)
    s = jnp.concatenate([lax.dot_general(qb[g * hq:(g + 1) * hq], kall[g], _NT, preferred_element_type=F32)
                         for g in range(N_KV_HEADS)], axis=0)
    s = s + _dot(selm, expand[...])
    s_far = s[:, :near0]
    s_near = s[:, near0:] + tab_ref[...].reshape(rows, MOBA_BLOCK + LANES)
    m = jnp.maximum(jnp.max(s_far, axis=1, keepdims=True), jnp.max(s_near, axis=1, keepdims=True))
    p_far = jnp.exp(s_far - m)
    p_near = jnp.exp(s_near - m)
    inv_l = 1.0 / (jnp.sum(p_far, axis=1, keepdims=True) + jnp.sum(p_near, axis=1, keepdims=True))
    p_far = p_far.astype(BF16)
    p_near = p_near.astype(BF16)
    for g in range(N_KV_HEADS):
        rs = slice(g * hq, (g + 1) * hq)
        out = (_dot(p_far[rs], vall[g, :near0, :]) + _dot(p_near[rs], vall[g, near0:, :])) * inv_l[rs]
        for hh in range(HEADS_PER_KV):
            c0 = (g * HEADS_PER_KV + hh) * HEAD_DIM
            o_ref[:, c0:c0 + HEAD_DIM] = out[hh * dec_seq:(hh + 1) * dec_seq]


def _attn_sample(page_table, qkv_f32, cache_k, cache_v, tab_s, row0, dec_seq):
    dec_batch, n_pages = page_table.shape
    page = cache_k.shape[1] // N_KV_HEADS
    total = n_pages * page + LANES
    blk0 = row0 // dec_seq

    def page_spec(p):
        return pl.BlockSpec((None, page * N_KV_HEADS, HEAD_DIM), lambda b, pt, p=p: (pt[b, p], 0, 0))

    body = functools.partial(_attn_sample_body, n_pages=n_pages, page=page, dec_seq=dec_seq)
    grid_spec = pltpu.PrefetchScalarGridSpec(
        num_scalar_prefetch=1,
        grid=(dec_batch,),
        in_specs=([pl.BlockSpec((dec_seq, qkv_f32.shape[1]), lambda b, pt: (blk0 + b, 0))]
                  + [page_spec(p) for p in range(n_pages)]
                  + [page_spec(p) for p in range(n_pages)]
                  + [pl.BlockSpec(tab_s.shape, lambda b, pt: (0, 0, 0))]),
        out_specs=pl.BlockSpec((dec_seq, N_HEADS * HEAD_DIM), lambda b, pt: (b, 0)),
        scratch_shapes=[pltpu.VMEM((N_KV_HEADS, total, HEAD_DIM), BF16),
                        pltpu.VMEM((N_KV_HEADS, total, HEAD_DIM), BF16),
                        pltpu.VMEM((LANES, total), BF16)],
    )
    return pl.pallas_call(
        body,
        grid_spec=grid_spec,
        out_shape=jax.ShapeDtypeStruct((dec_batch * dec_seq, N_HEADS * HEAD_DIM), F32),
        compiler_params=_params("arbitrary"),
        name="moba_attn_sample",
    )(page_table, qkv_f32, *([cache_k] * n_pages), *([cache_v] * n_pages), tab_s)


def _proj_ln_body(*refs, n_w, n_x, n_tiles, n_first):
    a1_ref, a2_ref = refs[:2]
    w_refs = refs[2:2 + n_w]
    x_refs = refs[2 + n_w:2 + n_w + n_x]
    g_ref, b_ref, o_ref, a_scr, z_scr = refs[2 + n_w + n_x:]
    i = pl.program_id(0)
    j = pl.program_id(1)
    _load_split_bf16(a_scr, a1_ref, a2_ref, n_first)
    a = a_scr[...]
    z = _dot(a, w_refs[0][...])
    if n_w == 2:
        z = z * _sigmoid(_dot(a, w_refs[1][...]))
    z_scr[j] = z

    @pl.when(j == n_tiles - 1)
    def _():
        zfull = jnp.concatenate([z_scr[t] for t in range(n_tiles)], axis=1)
        x = x_refs[0][...] if n_x == 1 else jnp.where(i < n_first, x_refs[0][...], x_refs[1][...])
        o_ref[...] = _layer_norm(ALPHA * x + zfull, g_ref[...], b_ref[...])


def _proj_ln(a1, a2, ws, xs, g, b, tm, tn):
    k = a1.shape[1]
    m = a1.shape[0] + a2.shape[0]
    n = ws[0].shape[1]
    n_first = a1.shape[0] // tm
    assert a1.shape[0] % tm == 0 and a2.shape[0] % tm == 0
    assert len(xs) == 1 or xs[0].shape[0] == a1.shape[0]
    n_tiles = n // tn
    body = functools.partial(_proj_ln_body, n_w=len(ws), n_x=len(xs), n_tiles=n_tiles, n_first=n_first)
    row = pl.BlockSpec((tm, n), lambda i, j: (i, 0))
    vec = pl.BlockSpec((1, n), lambda i, j: (0, 0))
    return pl.pallas_call(
        body,
        grid=(m // tm, n_tiles),
        in_specs=(_split_rows((tm, k), n_first)
                  + [pl.BlockSpec((k, tn), lambda i, j: (0, j)) for _ in ws]
                  + ([row] if len(xs) == 1 else _split_rows((tm, n), n_first))
                  + [vec, vec]),
        out_specs=row,
        out_shape=jax.ShapeDtypeStruct((m, n), F32),
        scratch_shapes=[pltpu.VMEM((tm, k), BF16), pltpu.VMEM((n_tiles, tm, tn), F32)],
        compiler_params=_params("parallel", "arbitrary"),
        name="glu_res_ln" if len(ws) == 2 else "proj_res_ln",
    )(a1, a2, *ws, *xs, g.reshape(1, n), b.reshape(1, n))


def _ffn_body(te_ref, tr_ref, *refs, n_f, dense, half):
    del te_ref
    if dense:
        x_ref, wg_ref, wu_ref, wd_ref, g_ref, b_ref, o_ref, xb = refs[:8]
    else:
        x_ref, wg_ref, wu_ref, wd_ref, o_ref, xb = refs[:6]
    tm = x_ref.shape[0]
    t = pl.program_id(0)
    f = pl.program_id(1)
    n_rows = tr_ref[t]

    @pl.when(f == 0)
    def _():
        xb[...] = x_ref[...].astype(BF16)
        o_ref[...] = jnp.zeros_like(o_ref)

    def swiglu_rows(n):
        x = xb[:n, :]
        hg = _dot(x, wg_ref[...].astype(BF16))
        hu = _dot(x, wu_ref[...].astype(BF16))
        h = (hg * _sigmoid(hg)) * hu
        o_ref[:n, :] += _dot(h.astype(BF16), wd_ref[...].astype(BF16))

    if half == tm:
        pl.when(n_rows > 0)(lambda: swiglu_rows(tm))
    else:
        pl.when(n_rows > half)(lambda: swiglu_rows(tm))
        pl.when((n_rows > 0) & (n_rows <= half))(lambda: swiglu_rows(half))

    if dense:
        @pl.when(f == n_f - 1)
        def _():
            o_ref[...] = _layer_norm(ALPHA * x_ref[...] + o_ref[...], g_ref[...], b_ref[...])


def _ffn_call(tile_expert, tile_rows, x, wg, wu, wd, extra, extra_specs, tm, tf, half, dense, name):
    s, d = x.shape
    n_t = s // tm
    n_f = wg.shape[-1] // tf
    live = lambda t, tr: jnp.minimum(tr[t], 1)
    x_spec = pl.BlockSpec((tm, d), lambda t, f, te, tr: (t, 0))
    wgu = pl.BlockSpec((None, d, tf), lambda t, f, te, tr: (te[t], 0, f * live(t, tr)))
    wds = pl.BlockSpec((None, tf, d), lambda t, f, te, tr: (te[t], f * live(t, tr), 0))
    grid_spec = pltpu.PrefetchScalarGridSpec(
        num_scalar_prefetch=2, grid=(n_t, n_f),
        in_specs=[x_spec, wgu, wgu, wds] + extra_specs,
        out_specs=x_spec,
        scratch_shapes=[pltpu.VMEM((tm, d), BF16)])
    return pl.pallas_call(
        functools.partial(_ffn_body, n_f=n_f, dense=dense, half=half),
        grid_spec=grid_spec,
        out_shape=jax.ShapeDtypeStruct((s, d), F32),
        compiler_params=_params("parallel", "arbitrary"),
        name=name,
    )(tile_expert, tile_rows, x, wg, wu, wd, *extra)


def _ffn_dense(x, wg, wu, wd, g, b, tm, tf):
    m, d = x.shape
    n_t = m // tm
    vec = pl.BlockSpec((1, d), lambda t, f, te, tr: (0, 0))
    return _ffn_call(jnp.zeros((n_t,), jnp.int32), jnp.full((n_t,), tm, jnp.int32), x, wg, wu, wd,
                     [g.reshape(1, d), b.reshape(1, d)], [vec, vec], tm, tf, tm, True, "ffn_res_ln")


def _ffn_moe(tile_expert, tile_rows, xs, wg, wu, wd, tm, tf, half):
    return _ffn_call(tile_expert, tile_rows, xs, wg, wu, wd, [], [], tm, tf, half, False, "moe_ffn")


def _router_body(y_ref, w_ref, i_ref, g_ref):
    logits = jnp.dot(y_ref[...], w_ref[...], precision=HIGHEST, preferred_element_type=F32)
    lane = lax.broadcasted_iota(jnp.int32, logits.shape, 1)
    lane_f = lane.astype(F32)
    l1 = jnp.where(lane < N_EXPERTS, logits, -jnp.inf)
    m1 = jnp.max(l1, axis=1, keepdims=True)
    i1 = jnp.min(jnp.where(l1 == m1, lane_f, float(LANES)), axis=1, keepdims=True)
    l2 = jnp.where(lane_f == i1, -jnp.inf, l1)
    m2 = jnp.max(l2, axis=1, keepdims=True)
    i2 = jnp.min(jnp.where(l2 == m2, lane_f, float(LANES)), axis=1, keepdims=True)
    e = jnp.exp(m2 - m1)
    g1 = 1.0 / (1.0 + e)
    g2 = e / (1.0 + e)
    i_ref[...] = jnp.where(lane == 0, i1, jnp.where(lane == 1, i2, 0.0)).astype(jnp.int32)
    g_ref[...] = jnp.where(lane == 0, g1, jnp.where(lane == 1, g2, 0.0))


def _router(y, w_router, tm):
    m, d = y.shape
    wr = jnp.zeros((d, LANES), F32).at[:, :N_EXPERTS].set(w_router)
    row = pl.BlockSpec((tm, LANES), lambda i: (i, 0))
    return pl.pallas_call(
        _router_body,
        grid=(m // tm,),
        in_specs=[pl.BlockSpec((tm, d), lambda i: (i, 0)), pl.BlockSpec((d, LANES), lambda i: (0, 0))],
        out_specs=[row, row],
        out_shape=[jax.ShapeDtypeStruct((m, LANES), jnp.int32), jax.ShapeDtypeStruct((m, LANES), F32)],
        compiler_params=_params("parallel"),
        name="moe_router",
    )(y, wr)


def _combine_ln_body(x_ref, a_ref, b2_ref, tg_ref, g_ref, b_ref, op_ref, os_ref, *, n_first):
    i = pl.program_id(0)
    tg = tg_ref[...]
    moe = tg[:, 0:1] * a_ref[...] + tg[:, 1:2] * b2_ref[...]
    out = _layer_norm(ALPHA * x_ref[...] + moe, g_ref[...], b_ref[...])

    @pl.when(i < n_first)
    def _():
        op_ref[...] = out

    @pl.when(i >= n_first)
    def _():
        os_ref[...] = out


def _combine_ln(x, pairs, top_gate, g, b, m_first, tm):
    m, d = x.shape
    n_first = m_first // tm
    vec = pl.BlockSpec((1, d), lambda i: (0, 0))
    return pl.pallas_call(
        functools.partial(_combine_ln_body, n_first=n_first),
        grid=(m // tm,),
        in_specs=[pl.BlockSpec((tm, d), lambda i: (i, 0)),
                  pl.BlockSpec((None, tm, d), lambda i: (0, i, 0)),
                  pl.BlockSpec((None, tm, d), lambda i: (1, i, 0)),
                  pl.BlockSpec((tm, LANES), lambda i: (i, 0)),
                  vec, vec],
        out_specs=[pl.BlockSpec((tm, d), lambda i: (jnp.minimum(i, n_first - 1), 0)),
                   pl.BlockSpec((tm, d), lambda i: (jnp.maximum(i - n_first, 0), 0))],
        out_shape=[jax.ShapeDtypeStruct((m_first, d), F32), jax.ShapeDtypeStruct((m - m_first, d), F32)],
        compiler_params=_params("arbitrary"),
        name="moe_combine_ln",
    )(x, pairs, pairs, top_gate, g.reshape(1, d), b.reshape(1, d))


def _moe_routing(top_idx, tm):
    m = top_idx.shape[0]
    n_pairs = m * TOP_K
    n_tiles = n_pairs // tm + N_EXPERTS
    i32 = jnp.int32
    e_flat = jnp.concatenate([top_idx[:, k] for k in range(TOP_K)])
    order = jnp.argsort(e_flat, stable=True).astype(i32)
    inv = jnp.argsort(order).astype(i32)
    counts = jnp.sum((e_flat[None, :] == jnp.arange(N_EXPERTS, dtype=i32)[:, None]).astype(i32), axis=1)
    padded = ((counts + tm - 1) // tm) * tm
    ends_padded = jnp.cumsum(padded)
    starts_padded = ends_padded - padded
    starts = jnp.cumsum(counts) - counts
    tile_start = jnp.arange(n_tiles, dtype=i32) * tm
    tile_expert = jnp.minimum(jnp.searchsorted(ends_padded, tile_start, side="right"), N_EXPERTS - 1).astype(i32)
    tile_rows = jnp.clip(counts[tile_expert] - (tile_start - starts_padded[tile_expert]), 0, tm).astype(i32)
    slot = jnp.arange(n_tiles * tm, dtype=i32)
    slot_e = jnp.repeat(tile_expert, tm)
    rank = slot - starts_padded[slot_e]
    live = (rank >= 0) & (rank < counts[slot_e])
    src = order[jnp.clip(starts[slot_e] + rank, 0, n_pairs - 1)]
    slot_token = jnp.where(live, src % m, slot % m)
    pair_slot = starts_padded[e_flat] + inv - starts[e_flat]
    return tile_expert, tile_rows, slot_token, pair_slot


def _gelu_tanh(x):
    return 0.5 * x * (1.0 + jnp.tanh(math.sqrt(2.0 / math.pi) * (x + 0.044715 * (x * x * x))))


def _ssm_body(*refs, sequential):
    (u_ref, bd_ref, wbr_ref, wbi_ref, vcr_ref, vci_ref, a8r_ref, a8i_ref, d_ref, h0r_ref, h0i_ref,
     y_ref, hr_ref, hi_ref) = refs[:14]
    scr = refs[14:]
    n_chunks = u_ref.shape[0]
    ub = [u_ref[:, s, :].astype(BF16) for s in range(SSM_CHUNK)]
    hl_r = _dot(ub[0], wbr_ref[0])
    hl_i = _dot(ub[0], wbi_ref[0])
    for s in range(1, SSM_CHUNK):
        hl_r = hl_r + _dot(ub[s], wbr_ref[s])
        hl_i = hl_i + _dot(ub[s], wbi_ref[s])
    a8r = a8r_ref[...]
    a8i = a8i_ref[...]
    if sequential:
        hlr_scr, hli_scr, hinr_scr, hini_scr = scr
        hlr_scr[...] = hl_r
        hli_scr[...] = hl_i

        def tile_step(kb, carry):
            hr, hi = carry
            base = pl.multiple_of(kb * SUBLANES, SUBLANES)
            tr = hlr_scr[pl.ds(base, SUBLANES), :]
            ti = hli_scr[pl.ds(base, SUBLANES), :]
            rows_r, rows_i = [], []
            for r in range(SUBLANES):
                rows_r.append(hr)
                rows_i.append(hi)
                hr, hi = (a8r * hr - a8i * hi + tr[r:r + 1], a8r * hi + a8i * hr + ti[r:r + 1])
            hinr_scr[pl.ds(base, SUBLANES), :] = jnp.concatenate(rows_r, axis=0)
            hini_scr[pl.ds(base, SUBLANES), :] = jnp.concatenate(rows_i, axis=0)
            return hr, hi

        hr, hi = lax.fori_loop(0, n_chunks // SUBLANES, tile_step, (h0r_ref[...], h0i_ref[...]))
        hr_ref[...] = hr
        hi_ref[...] = hi
        hin_r = hinr_scr[...]
        hin_i = hini_scr[...]
    else:
        gpb = h0r_ref.shape[1]
        p_n = h0r_ref.shape[2]
        hin_r = jnp.concatenate([h0r_ref[:, g, :] for g in range(gpb)], axis=1)
        hin_i = jnp.concatenate([h0i_ref[:, g, :] for g in range(gpb)], axis=1)
        hr = a8r * hin_r - a8i * hin_i + hl_r
        hi = a8r * hin_i + a8i * hin_r + hl_i
        for g in range(gpb):
            hr_ref[:, g, :] = hr[:, g * p_n:(g + 1) * p_n]
            hi_ref[:, g, :] = hi[:, g * p_n:(g + 1) * p_n]
    hb_r = hin_r.astype(BF16)
    hb_i = hin_i.astype(BF16)
    d = d_ref[...]
    for t in range(SSM_CHUNK):
        y = _dot(hb_r, vcr_ref[t]) + _dot(hb_i, vci_ref[t])
        for s in range(t + 1):
            y = y + _dot(ub[s], bd_ref[t - s])
        y_ref[:, t, :] = _gelu_tanh(y + d * u_ref[:, t, :])


def _group_blockdiag(x, rows_per_group, lanes_per_group):
    t_n, g_n, a_n, b_n = x.shape
    gpb = LANES // SSM_GROUP
    nb = g_n // gpb
    assert a_n == rows_per_group and b_n == lanes_per_group
    x = x.reshape(t_n, nb, gpb * a_n, 1, b_n)
    x = jnp.broadcast_to(x, (t_n, nb, gpb * a_n, gpb, b_n)).reshape(t_n, nb, gpb * a_n, gpb * b_n)
    r_grp = np.arange(gpb * a_n)[:, None] // a_n
    l_grp = np.arange(gpb * b_n)[None, :] // b_n
    return jnp.where(jnp.asarray(r_grp == l_grp), x, 0.0).astype(BF16).transpose(1, 0, 2, 3)


def _ssm_operators(a_re, a_im, log_dt, b_re, b_im, c_re, c_im):
    g_n, p_n = a_re.shape
    lam_re = jnp.minimum(a_re, -1e-4)
    lam_im = a_im
    dt = jnp.exp(log_dt)[:, None]
    decay = jnp.exp(lam_re * dt)
    ar = decay * jnp.cos(lam_im * dt)
    ai = decay * jnp.sin(lam_im * dt)
    num_re = ar - 1.0
    den = lam_re * lam_re + lam_im * lam_im
    f_re = (num_re * lam_re + ai * lam_im) / den
    f_im = (ai * lam_re - num_re * lam_im) / den
    bt_re = jnp.swapaxes(b_re, 1, 2)
    bt_im = jnp.swapaxes(b_im, 1, 2)
    bb_re = f_re[:, None, :] * bt_re - f_im[:, None, :] * bt_im
    bb_im = f_re[:, None, :] * bt_im + f_im[:, None, :] * bt_re
    pw_re, pw_im = [jnp.ones_like(ar)], [jnp.zeros_like(ar)]
    for _ in range(SSM_CHUNK):
        pr, pi = pw_re[-1], pw_im[-1]
        pw_re.append(pr * ar - pi * ai)
        pw_im.append(pr * ai + pi * ar)
    pw_re = jnp.stack(pw_re)
    pw_im = jnp.stack(pw_im)
    pk_re = pw_re[:SSM_CHUNK, :, None, :]
    pk_im = pw_im[:SSM_CHUNK, :, None, :]
    ab_re = pk_re * bb_re - pk_im * bb_im
    ab_im = pk_re * bb_im + pk_im * bb_re
    conv = (jnp.einsum("gcp,tgdp->tgdc", c_re, ab_re, precision=HIGHEST)
            - jnp.einsum("gcp,tgdp->tgdc", c_im, ab_im, precision=HIGHEST))
    bd = _group_blockdiag(conv, SSM_GROUP, SSM_GROUP)
    wb_re = _group_blockdiag(ab_re[::-1], SSM_GROUP, p_n)
    wb_im = _group_blockdiag(ab_im[::-1], SSM_GROUP, p_n)
    pr = pw_re[1:, :, :, None]
    pi = pw_im[1:, :, :, None]
    ct_re = jnp.swapaxes(c_re, 1, 2)[None]
    ct_im = jnp.swapaxes(c_im, 1, 2)[None]
    vc_re = _group_blockdiag(ct_re * pr - ct_im * pi, p_n, SSM_GROUP)
    vc_im = _group_blockdiag(-(ct_re * pi + ct_im * pr), p_n, SSM_GROUP)
    a8r = pw_re[SSM_CHUNK].reshape(1, g_n * p_n)
    a8i = pw_im[SSM_CHUNK].reshape(1, g_n * p_n)
    return bd, wb_re, wb_im, vc_re, vc_im, a8r, a8i


def _ssm_scan(u3, ops, d_skip, h0_re, h0_im, chunk0, n_chunks, sequential):
    bd, wb_re, wb_im, vc_re, vc_im, a8r, a8i = ops
    nb = bd.shape[0]
    sw = wb_re.shape[-1]
    cblk = chunk0 // n_chunks
    op4 = lambda shp: pl.BlockSpec((None,) + shp, lambda j: (j, 0, 0, 0))
    srow = pl.BlockSpec((1, sw), lambda j: (0, j))
    if sequential:
        hspec = srow
        scratch = [pltpu.VMEM((n_chunks, sw), F32)] * 4
    else:
        hspec = pl.BlockSpec((n_chunks, sw // SSM_STATE, SSM_STATE), lambda j: (0, j, 0))
        scratch = []
    return pl.pallas_call(
        functools.partial(_ssm_body, sequential=sequential),
        grid=(nb,),
        in_specs=[pl.BlockSpec((n_chunks, SSM_CHUNK, LANES), lambda j: (cblk, 0, j)),
                  op4(bd.shape[1:]), op4(wb_re.shape[1:]), op4(wb_im.shape[1:]),
                  op4(vc_re.shape[1:]), op4(vc_im.shape[1:]),
                  srow, srow, pl.BlockSpec((1, LANES), lambda j: (0, j)), hspec, hspec],
        out_specs=[pl.BlockSpec((n_chunks, SSM_CHUNK, LANES), lambda j: (0, 0, j)), hspec, hspec],
        out_shape=[jax.ShapeDtypeStruct((n_chunks, SSM_CHUNK, u3.shape[2]), F32),
                   jax.ShapeDtypeStruct(h0_re.shape, F32), jax.ShapeDtypeStruct(h0_im.shape, F32)],
        scratch_shapes=scratch,
        compiler_params=_params("parallel"),
        name="s5_scan_seq" if sequential else "s5_scan_step",
    )(u3, bd, wb_re, wb_im, vc_re, vc_im, a8r, a8i, d_skip.reshape(1, -1), h0_re, h0_im)


def kernel(x_prompt, x_sample, cache_k, cache_v, state_ssm_re, state_ssm_im, page_table, rel_bias, ln_g, ln_b,
           w_qkv, w_o, w_ssm_in, ssm_a_re, ssm_a_im, ssm_log_dt, ssm_b_re, ssm_b_im, ssm_c_re, ssm_c_im, ssm_d,
           w_glu_v, w_glu_g, w_ff_gate, w_ff_up, w_ff_down, w_router, w_moe_gate, w_moe_up, w_moe_down):
    batch, seq, d = x_prompt.shape
    dec_batch, dec_seq, _ = x_sample.shape
    assert batch == 1 and d == D_MODEL and dec_seq == SSM_CHUNK and seq % MOBA_BLOCK == 0
    m_p = batch * seq
    m_s = dec_batch * dec_seq
    m = m_p + m_s
    dq = N_HEADS * HEAD_DIM
    dkv = N_KV_HEADS * HEAD_DIM
    n_pool, page = cache_k.shape[1], cache_k.shape[2]

    x_p = x_prompt.reshape(m_p, d)
    x_s = x_sample.reshape(m_s, d)

    qkv_f32, qkv_b = _qkv_proj(x_p, x_s, w_qkv[0].astype(BF16), tm=1024, tn=512)
    kmean = _block_means(qkv_f32, seq // MOBA_BLOCK)
    tabs, tab_s = _bias_tables(rel_bias)
    attn_p = _attn_prompt(qkv_f32, qkv_b, kmean, tabs, seq)
    ck = cache_k[0].reshape(n_pool, page * N_KV_HEADS, HEAD_DIM)
    cv = cache_v[0].reshape(n_pool, page * N_KV_HEADS, HEAD_DIM)
    attn_s = _attn_sample(page_table, qkv_f32, ck, cv, tab_s, m_p, dec_seq)
    y = _proj_ln(attn_p, attn_s, [w_o[0].astype(BF16)], [x_p, x_s], ln_g[0, 0], ln_b[0, 0], tm=512, tn=512)
    y = _ffn_dense(y, w_ff_gate, w_ff_up, w_ff_down, ln_g[0, 1], ln_b[0, 1], tm=1024, tf=256)

    k_all = qkv_f32[:, dq:dq + dkv]
    v_all = qkv_f32[:, dq + dkv:]
    k_prompt = k_all[:m_p].reshape(1, batch, seq, N_KV_HEADS, HEAD_DIM)
    v_prompt = v_all[:m_p].reshape(1, batch, seq, N_KV_HEADS, HEAD_DIM)
    k_sample = k_all[m_p:].reshape(1, dec_batch, dec_seq, N_KV_HEADS, HEAD_DIM)
    v_sample = v_all[m_p:].reshape(1, dec_batch, dec_seq, N_KV_HEADS, HEAD_DIM)

    u = _matmul(y, w_ssm_in[0].astype(BF16), tm=1024, tn=512)
    u3 = u.reshape(m // SSM_CHUNK, SSM_CHUNK, d)
    ops = _ssm_operators(ssm_a_re[0], ssm_a_im[0], ssm_log_dt[0], ssm_b_re[0], ssm_b_im[0],
                         ssm_c_re[0], ssm_c_im[0])
    n_state = N_SSM_GROUPS * SSM_STATE
    zero = jnp.zeros((batch, n_state), F32)
    yg_p, hrp, hip = _ssm_scan(u3, ops, ssm_d[0], zero, zero, 0, m_p // SSM_CHUNK, True)
    yg_s, hrs, his = _ssm_scan(u3, ops, ssm_d[0], state_ssm_re[0], state_ssm_im[0], m_p // SSM_CHUNK, dec_batch,
                               False)
    y = _proj_ln(yg_p.reshape(m_p, d), yg_s.reshape(m_s, d), [w_glu_v[0].astype(BF16), w_glu_g[0].astype(BF16)],
                 [y], ln_g[1, 0], ln_b[1, 0], tm=512, tn=512)

    tm_moe = 1024
    top_idx, top_gate = _router(y, w_router[0], tm=512)
    tile_expert, tile_rows, slot_token, pair_slot = _moe_routing(top_idx, tm_moe)
    xs = y.at[slot_token].get(mode="promise_in_bounds")
    ys = _ffn_moe(tile_expert, tile_rows, xs, w_moe_gate[0], w_moe_up[0], w_moe_down[0], tm_moe, tf=256, half=512)
    pairs = ys.at[pair_slot].get(mode="promise_in_bounds").reshape(TOP_K, m, d)
    out_p, out_s = _combine_ln(y, pairs, top_gate, ln_g[1, 1], ln_b[1, 1], m_p, tm=512)

    return (out_p.reshape(batch, seq, d), out_s.reshape(dec_batch, dec_seq, d),
            k_prompt, v_prompt, k_sample, v_sample,
            hrp.reshape(1, batch, N_SSM_GROUPS, SSM_STATE), hip.reshape(1, batch, N_SSM_GROUPS, SSM_STATE),
            hrs.reshape(1, dec_batch, N_SSM_GROUPS, SSM_STATE), his.reshape(1, dec_batch, N_SSM_GROUPS, SSM_STATE))
```

```python
import functools
import math

import numpy as np
import jax
import jax.numpy as jnp
from jax import lax
from jax.experimental import pallas as pl
from jax.experimental.pallas import tpu as pltpu

D_MODEL = 2048
N_HEADS = 16
HEAD_DIM = D_MODEL // N_HEADS
N_KV_HEADS = 4
HEADS_PER_KV = N_HEADS // N_KV_HEADS
MOBA_BLOCK = 256
MOBA_TOPK = 3
NUM_BUCKETS = 32
MAX_DISTANCE = 128
SSM_GROUP = 16
N_SSM_GROUPS = D_MODEL // SSM_GROUP
SSM_STATE = 64
SSM_CHUNK = 8
D_FF = 7 * D_MODEL // 2
N_EXPERTS = 8
TOP_K = 2
DEPTH = 2
ALPHA = (2 * DEPTH) ** 0.25
LN_EPS = 1e-5
NEG_INF = -1e30

LANES = 128
SUBLANES = 8
VMEM_LIMIT = 60 * 1024 * 1024

F32 = jnp.float32
BF16 = jnp.bfloat16
HIGHEST = lax.Precision.HIGHEST
_NT = (((1,), (1,)), ((), ()))


def _params(*sem):
    return pltpu.CompilerParams(dimension_semantics=sem, vmem_limit_bytes=VMEM_LIMIT)


def _dot(a, b):
    return jnp.dot(a, b, preferred_element_type=F32)


def _sigmoid(x):
    return 1.0 / (1.0 + jnp.exp(-x))


def _layer_norm(y, g, b):
    mean = jnp.mean(y, axis=-1, keepdims=True)
    yc = y - mean
    var = jnp.mean(yc * yc, axis=-1, keepdims=True)
    return yc * lax.rsqrt(var + LN_EPS) * g + b


def _mm_body(x_ref, w_ref, o_ref, xb_scr):
    @pl.when(pl.program_id(1) == 0)
    def _():
        xb_scr[...] = x_ref[...].astype(BF16)

    o_ref[...] = _dot(xb_scr[...], w_ref[...])


def _matmul(x, w, tm, tn):
    m, k = x.shape
    n = w.shape[1]
    return pl.pallas_call(
        _mm_body,
        grid=(m // tm, n // tn),
        in_specs=[pl.BlockSpec((tm, k), lambda i, j: (i, 0)),
                  pl.BlockSpec((k, tn), lambda i, j: (0, j))],
        out_specs=pl.BlockSpec((tm, tn), lambda i, j: (i, j)),
        out_shape=jax.ShapeDtypeStruct((m, n), F32),
        scratch_shapes=[pltpu.VMEM((tm, k), BF16)],
        compiler_params=_params("parallel", "arbitrary"),
        name="matmul",
    )(x, w)


def _split_rows(block, n_first):
    return [pl.BlockSpec(block, lambda i, j: (jnp.minimum(i, n_first - 1), 0)),
            pl.BlockSpec(block, lambda i, j: (jnp.maximum(i - n_first, 0), 0))]


def _load_split_bf16(dst, src1, src2, n_first):
    i = pl.program_id(0)
    j = pl.program_id(1)

    @pl.when((j == 0) & (i < n_first))
    def _():
        dst[...] = src1[...].astype(BF16)

    @pl.when((j == 0) & (i >= n_first))
    def _():
        dst[...] = src2[...].astype(BF16)


def _qkv_body(x1_ref, x2_ref, w_ref, of_ref, ob_ref, xb_scr, *, nq_tiles, scale, n_first):
    _load_split_bf16(xb_scr, x1_ref, x2_ref, n_first)
    acc = _dot(xb_scr[...], w_ref[...])
    of_ref[...] = acc
    s = jnp.where(pl.program_id(1) < nq_tiles, scale, 1.0).astype(F32)
    ob_ref[...] = (acc * s).astype(BF16)


def _qkv_proj(x1, x2, w, tm, tn):
    k = x1.shape[1]
    m = x1.shape[0] + x2.shape[0]
    n = w.shape[1]
    n_first = x1.shape[0] // tm
    assert x1.shape[0] % tm == 0 and x2.shape[0] % tm == 0
    body = functools.partial(_qkv_body, nq_tiles=(N_HEADS * HEAD_DIM) // tn, scale=HEAD_DIM ** -0.5,
                             n_first=n_first)
    return pl.pallas_call(
        body,
        grid=(m // tm, n // tn),
        in_specs=_split_rows((tm, k), n_first) + [pl.BlockSpec((k, tn), lambda i, j: (0, j))],
        out_specs=[pl.BlockSpec((tm, tn), lambda i, j: (i, j)),
                   pl.BlockSpec((tm, tn), lambda i, j: (i, j))],
        out_shape=[jax.ShapeDtypeStruct((m, n), F32), jax.ShapeDtypeStruct((m, n), BF16)],
        scratch_shapes=[pltpu.VMEM((tm, k), BF16)],
        compiler_params=_params("parallel", "arbitrary"),
        name="qkv_proj",
    )(x1, x2, w)


def _kmean_body(k_ref, o_ref):
    i = pl.program_id(0)

    @pl.when(i == 0)
    def _():
        o_ref[...] = jnp.zeros_like(o_ref)

    mean = jnp.sum(k_ref[...], axis=0, keepdims=True) * (1.0 / MOBA_BLOCK)
    rows = lax.broadcasted_iota(jnp.int32, o_ref.shape, 0)
    o_ref[...] = jnp.where(rows == i, mean, o_ref[...])


def _block_means(qkv_f32, n_blocks):
    dkv = N_KV_HEADS * HEAD_DIM
    kcol = (N_HEADS * HEAD_DIM) // dkv
    return pl.pallas_call(
        _kmean_body,
        grid=(n_blocks,),
        in_specs=[pl.BlockSpec((MOBA_BLOCK, dkv), lambda i: (i, kcol))],
        out_specs=pl.BlockSpec((LANES, dkv), lambda i: (0, 0)),
        out_shape=jax.ShapeDtypeStruct((LANES, dkv), F32),
        compiler_params=_params("arbitrary"),
        name="moba_block_means",
    )(qkv_f32)


def _t5_bucket_np(dist):
    n = np.maximum(dist, 0)
    max_exact = NUM_BUCKETS // 2
    nf = np.maximum(n, 1).astype(np.float32)
    large = max_exact + (np.log(nf / np.float32(max_exact)) / np.float32(math.log(MAX_DISTANCE / max_exact))
                         * np.float32(NUM_BUCKETS - max_exact)).astype(np.int32)
    large = np.minimum(large, NUM_BUCKETS - 1)
    return np.where(n < max_exact, n, large).astype(np.int32)


def _bucket_tables():
    q = np.arange(MOBA_BLOCK)[:, None]
    k = np.arange(MOBA_BLOCK)[None, :]
    own = np.where(q - k >= 0, _t5_bucket_np(q - k), -1)
    prev = _t5_bucket_np(q - k + MOBA_BLOCK)
    return np.stack([own, prev]).astype(np.int32)


def _bias_body(rb_ref, idx_ref, o_ref, os_ref):
    h = pl.program_id(0)
    far = rb_ref[h, NUM_BUCKETS - 1]
    tabs = []
    for m in range(2):
        idx = idx_ref[m]
        acc = jnp.full(idx.shape, NEG_INF, F32)
        for b in range(NUM_BUCKETS):
            acc = jnp.where(idx == b, rb_ref[h, b] - far, acc)
        o_ref[0, m] = acc
        tabs.append(acc)
    os_ref[0] = jnp.concatenate([tabs[1][:SUBLANES, :], tabs[0][:SUBLANES, :LANES]], axis=1)


def _bias_tables(rel_bias):
    idx = jnp.asarray(_bucket_tables())
    return pl.pallas_call(
        _bias_body,
        grid=(N_HEADS,),
        in_specs=[pl.BlockSpec(memory_space=pltpu.SMEM),
                  pl.BlockSpec((2, MOBA_BLOCK, MOBA_BLOCK), lambda h: (0, 0, 0))],
        out_specs=[pl.BlockSpec((1, 2, MOBA_BLOCK, MOBA_BLOCK), lambda h: (h, 0, 0, 0)),
                   pl.BlockSpec((1, SUBLANES, MOBA_BLOCK + LANES), lambda h: (h, 0, 0))],
        out_shape=[jax.ShapeDtypeStruct((N_HEADS, 2, MOBA_BLOCK, MOBA_BLOCK), F32),
                   jax.ShapeDtypeStruct((N_HEADS, SUBLANES, MOBA_BLOCK + LANES), F32)],
        compiler_params=_params("parallel"),
        name="moba_bias_tables",
    )(rel_bias, idx)


def _select_blocks(gate, n_past):
    lane = lax.broadcasted_iota(jnp.int32, gate.shape, 1)
    lane_f = lane.astype(F32)
    g = jnp.where(lane < n_past, gate, NEG_INF)
    sel = jnp.full(gate.shape, NEG_INF, F32)
    for _ in range(MOBA_TOPK):
        mx = jnp.max(g, axis=1, keepdims=True)
        idx = jnp.min(jnp.where(g == mx, lane_f, float(gate.shape[1])), axis=1, keepdims=True)
        pick = lane_f == idx
        sel = jnp.where(pick, 0.0, sel)
        g = jnp.where(pick, -jnp.inf, g)
    return jnp.where(lane < n_past, sel, 0.0)


def _attn_prompt_body(qf_ref, qb_ref, k_ref, v_ref, km_ref, tab_ref, o_ref, qa_scr, m_scr, l_scr, acc_scr):
    i = pl.program_id(1)
    rows = HEADS_PER_KV * MOBA_BLOCK
    km = km_ref[...]
    lane = lax.broadcasted_iota(jnp.int32, (MOBA_BLOCK, LANES), 1)
    dummy = LANES - 1
    for hh in range(HEADS_PER_KV):
        cs = slice(hh * HEAD_DIM, (hh + 1) * HEAD_DIM)
        rs = slice(hh * MOBA_BLOCK, (hh + 1) * MOBA_BLOCK)
        gate = lax.dot_general(qf_ref[:, cs], km, _NT, precision=HIGHEST, preferred_element_type=F32)
        mask = jnp.where(lane == dummy, NEG_INF, _select_blocks(gate, i))
        qa_scr[rs, :HEAD_DIM] = qb_ref[:, cs]
        qa_scr[rs, HEAD_DIM:] = mask.astype(BF16)
    qa = qa_scr[...]

    def block_pair(ja, lane_a, jb, lane_b):
        rhs, vals = [], []
        for j, mask_lane in ((ja, lane_a), (jb, lane_b)):
            start = pl.multiple_of(j * MOBA_BLOCK, MOBA_BLOCK)
            kj = k_ref[pl.ds(start, MOBA_BLOCK), :]
            rhs.append(jnp.concatenate([kj, (lane == mask_lane).astype(BF16)], axis=1))
            vals.append(v_ref[pl.ds(start, MOBA_BLOCK), :])
        s = lax.dot_general(qa, jnp.concatenate(rhs, axis=0), _NT, preferred_element_type=F32)
        return s, jnp.concatenate(vals, axis=0)

    def softmax_pv(s, v, m_new):
        p = jnp.exp(s - jnp.concatenate([m_new] * (s.shape[1] // LANES), axis=1))
        return jnp.sum(p, axis=1, keepdims=True), _dot(p.astype(BF16), v)

    has_prev = i >= 1
    s, v = block_pair(i, -1, jnp.maximum(i - 1, 0), jnp.where(has_prev, i - 1, dummy))
    s = s + jnp.concatenate([tab_ref[:, 0].reshape(rows, MOBA_BLOCK), tab_ref[:, 1].reshape(rows, MOBA_BLOCK)],
                            axis=1)
    m0 = jnp.broadcast_to(jnp.max(s, axis=1, keepdims=True), (rows, LANES))
    l0, acc0 = softmax_pv(s, v, m0)
    m_scr[...] = m0
    l_scr[...] = jnp.broadcast_to(l0, (rows, LANES))
    acc_scr[...] = acc0

    n_far = jnp.maximum(i - 1, 0)

    def far_pair(t, carry):
        j0 = 2 * t
        j1 = j0 + 1
        ok1 = j1 < n_far
        s, v = block_pair(j0, j0, jnp.where(ok1, j1, 0), jnp.where(ok1, j1, dummy))
        m_prev = m_scr[...]
        m_new = jnp.maximum(m_prev, jnp.max(s, axis=1, keepdims=True))
        alpha = jnp.exp(m_prev - m_new)
        l_add, acc_add = softmax_pv(s, v, m_new)
        l_scr[...] = alpha * l_scr[...] + l_add
        acc_scr[...] = alpha * acc_scr[...] + acc_add
        m_scr[...] = m_new
        return carry

    lax.fori_loop(0, (n_far + 1) // 2, far_pair, 0)

    out = acc_scr[...] / l_scr[...]
    for hh in range(HEADS_PER_KV):
        o_ref[:, hh * HEAD_DIM:(hh + 1) * HEAD_DIM] = out[hh * MOBA_BLOCK:(hh + 1) * MOBA_BLOCK]


def _attn_prompt(qkv_f32, qkv_b, kmean, tabs, seq):
    nblk = seq // MOBA_BLOCK
    assert nblk < LANES - 1
    gw = HEADS_PER_KV * HEAD_DIM
    kcol = (N_HEADS * HEAD_DIM) // HEAD_DIM
    vcol = kcol + N_KV_HEADS
    rows = HEADS_PER_KV * MOBA_BLOCK
    return pl.pallas_call(
        _attn_prompt_body,
        grid=(N_KV_HEADS, nblk),
        in_specs=[pl.BlockSpec((MOBA_BLOCK, gw), lambda g, i: (i, g)),
                  pl.BlockSpec((MOBA_BLOCK, gw), lambda g, i: (i, g)),
                  pl.BlockSpec((seq, HEAD_DIM), lambda g, i: (0, kcol + g)),
                  pl.BlockSpec((seq, HEAD_DIM), lambda g, i: (0, vcol + g)),
                  pl.BlockSpec((LANES, HEAD_DIM), lambda g, i: (0, g)),
                  pl.BlockSpec((HEADS_PER_KV, 2, MOBA_BLOCK, MOBA_BLOCK), lambda g, i: (g, 0, 0, 0))],
        out_specs=pl.BlockSpec((MOBA_BLOCK, gw), lambda g, i: (i, g)),
        out_shape=jax.ShapeDtypeStruct((seq, N_HEADS * HEAD_DIM), F32),
        scratch_shapes=[pltpu.VMEM((rows, 2 * HEAD_DIM), BF16),
                        pltpu.VMEM((rows, LANES), F32),
                        pltpu.VMEM((rows, LANES), F32),
                        pltpu.VMEM((rows, HEAD_DIM), F32)],
        compiler_params=_params("parallel", "arbitrary"),
        name="moba_attn_prompt",
    )(qkv_f32, qkv_b, qkv_b, qkv_b, kmean, tabs)


def _attn_sample_body(pt_ref, *refs, n_pages, page, dec_seq):
    del pt_ref
    qkv_ref = refs[0]
    kp = refs[1:1 + n_pages]
    vp = refs[1 + n_pages:1 + 2 * n_pages]
    tab_ref = refs[1 + 2 * n_pages]
    o_ref = refs[2 + 2 * n_pages]
    kall, vall, expand = refs[3 + 2 * n_pages:]
    past = n_pages * page
    n_past_blocks = past // MOBA_BLOCK
    pages_per_block = MOBA_BLOCK // page
    near0 = past - MOBA_BLOCK
    total = past + LANES
    dq = N_HEADS * HEAD_DIM
    dkv = N_KV_HEADS * HEAD_DIM
    hq = HEADS_PER_KV * dec_seq
    rows = N_HEADS * dec_seq
    pad = jnp.zeros((LANES - dec_seq, HEAD_DIM), F32)

    @pl.when(pl.program_id(0) == 0)
    def _():
        key_blk = lax.broadcasted_iota(jnp.int32, (LANES, total), 1) // MOBA_BLOCK
        expand[...] = (key_blk == lax.broadcasted_iota(jnp.int32, (LANES, total), 0)).astype(BF16)

    kmeans = []
    for g in range(N_KV_HEADS):
        means = []
        for blk in range(n_past_blocks):
            tot = jnp.zeros((1, HEAD_DIM), F32)
            for pp in range(pages_per_block):
                pg = blk * pages_per_block + pp
                kk = kp[pg][pl.ds(g, page, stride=N_KV_HEADS), :]
                vv = vp[pg][pl.ds(g, page, stride=N_KV_HEADS), :]
                tot = tot + jnp.sum(kk, axis=0, keepdims=True)
                kall[g, pg * page:(pg + 1) * page, :] = kk.astype(BF16)
                vall[g, pg * page:(pg + 1) * page, :] = vv.astype(BF16)
            means.append(tot * (1.0 / MOBA_BLOCK))
        kmeans.append(jnp.concatenate(means + [jnp.zeros((LANES - n_past_blocks, HEAD_DIM), F32)], axis=0))
        knew = qkv_ref[:, dq + g * HEAD_DIM:dq + (g + 1) * HEAD_DIM]
        vnew = qkv_ref[:, dq + dkv + g * HEAD_DIM:dq + dkv + (g + 1) * HEAD_DIM]
        kall[g, past:total, :] = jnp.concatenate([knew, pad], axis=0).astype(BF16)
        vall[g, past:total, :] = jnp.concatenate([vnew, pad], axis=0).astype(BF16)

    qs = jnp.concatenate([qkv_ref[:, h * HEAD_DIM:(h + 1) * HEAD_DIM] for h in range(N_HEADS)], axis=0)
    gate_all = lax.dot_general(qs, jnp.concatenate(kmeans, axis=0), _NT, precision=HIGHEST,
                               preferred_element_type=F32)
    row_g = lax.broadcasted_iota(jnp.int32, (rows, LANES), 0) // hq
    gate = gate_all[:, :LANES]
    for g in range(1, N_KV_HEADS):
        gate = jnp.where(row_g == g, gate_all[:, g * LANES:(g + 1) * LANES], gate)
    selm = _select_blocks(gate, n_past_blocks).astype(BF16)

    qb = (qs * (HEAD_DIM ** -0.5)).astype(BF16)
    s = jnp.concatenate([lax.dot_general(qb[g * hq:(g + 1) * hq], kall[g], _NT, preferred_element_type=F32)
                         for g in range(N_KV_HEADS)], axis=0)
    s = s + _dot(selm, expand[...])
    s_far = s[:, :near0]
    s_near = s[:, near0:] + tab_ref[...].reshape(rows, MOBA_BLOCK + LANES)
    m = jnp.maximum(jnp.max(s_far, axis=1, keepdims=True), jnp.max(s_near, axis=1, keepdims=True))
    p_far = jnp.exp(s_far - m)
    p_near = jnp.exp(s_near - m)
    inv_l = 1.0 / (jnp.sum(p_far, axis=1, keepdims=True) + jnp.sum(p_near, axis=1, keepdims=True))
    p_far = p_far.astype(BF16)
    p_near = p_near.astype(BF16)
    for g in range(N_KV_HEADS):
        rs = slice(g * hq, (g + 1) * hq)
        out = (_dot(p_far[rs], vall[g, :near0, :]) + _dot(p_near[rs], vall[g, near0:, :])) * inv_l[rs]
        for hh in range(HEADS_PER_KV):
            c0 = (g * HEADS_PER_KV + hh) * HEAD_DIM
            o_ref[:, c0:c0 + HEAD_DIM] = out[hh * dec_seq:(hh + 1) * dec_seq]


def _attn_sample(page_table, qkv_f32, cache_k, cache_v, tab_s, row0, dec_seq):
    dec_batch, n_pages = page_table.shape
    page = cache_k.shape[1] // N_KV_HEADS
    total = n_pages * page + LANES
    blk0 = row0 // dec_seq

    def page_spec(p):
        return pl.BlockSpec((None, page * N_KV_HEADS, HEAD_DIM), lambda b, pt, p=p: (pt[b, p], 0, 0))

    body = functools.partial(_attn_sample_body, n_pages=n_pages, page=page, dec_seq=dec_seq)
    grid_spec = pltpu.PrefetchScalarGridSpec(
        num_scalar_prefetch=1,
        grid=(dec_batch,),
        in_specs=([pl.BlockSpec((dec_seq, qkv_f32.shape[1]), lambda b, pt: (blk0 + b, 0))]
                  + [page_spec(p) for p in range(n_pages)]
                  + [page_spec(p) for p in range(n_pages)]
                  + [pl.BlockSpec(tab_s.shape, lambda b, pt: (0, 0, 0))]),
        out_specs=pl.BlockSpec((dec_seq, N_HEADS * HEAD_DIM), lambda b, pt: (b, 0)),
        scratch_shapes=[pltpu.VMEM((N_KV_HEADS, total, HEAD_DIM), BF16),
                        pltpu.VMEM((N_KV_HEADS, total, HEAD_DIM), BF16),
                        pltpu.VMEM((LANES, total), BF16)],
    )
    return pl.pallas_call(
        body,
        grid_spec=grid_spec,
        out_shape=jax.ShapeDtypeStruct((dec_batch * dec_seq, N_HEADS * HEAD_DIM), F32),
        compiler_params=_params("arbitrary"),
        name="moba_attn_sample",
    )(page_table, qkv_f32, *([cache_k] * n_pages), *([cache_v] * n_pages), tab_s)


def _proj_ln_body(*refs, n_w, n_x, n_tiles, n_first):
    a1_ref, a2_ref = refs[:2]
    w_refs = refs[2:2 + n_w]
    x_refs = refs[2 + n_w:2 + n_w + n_x]
    g_ref, b_ref, o_ref, a_scr, z_scr = refs[2 + n_w + n_x:]
    i = pl.program_id(0)
    j = pl.program_id(1)
    _load_split_bf16(a_scr, a1_ref, a2_ref, n_first)
    a = a_scr[...]
    z = _dot(a, w_refs[0][...])
    if n_w == 2:
        z = z * _sigmoid(_dot(a, w_refs[1][...]))
    z_scr[j] = z

    @pl.when(j == n_tiles - 1)
    def _():
        zfull = jnp.concatenate([z_scr[t] for t in range(n_tiles)], axis=1)
        x = x_refs[0][...] if n_x == 1 else jnp.where(i < n_first, x_refs[0][...], x_refs[1][...])
        o_ref[...] = _layer_norm(ALPHA * x + zfull, g_ref[...], b_ref[...])


def _proj_ln(a1, a2, ws, xs, g, b, tm, tn):
    k = a1.shape[1]
    m = a1.shape[0] + a2.shape[0]
    n = ws[0].shape[1]
    n_first = a1.shape[0] // tm
    assert a1.shape[0] % tm == 0 and a2.shape[0] % tm == 0
    assert len(xs) == 1 or xs[0].shape[0] == a1.shape[0]
    n_tiles = n // tn
    body = functools.partial(_proj_ln_body, n_w=len(ws), n_x=len(xs), n_tiles=n_tiles, n_first=n_first)
    row = pl.BlockSpec((tm, n), lambda i, j: (i, 0))
    vec = pl.BlockSpec((1, n), lambda i, j: (0, 0))
    return pl.pallas_call(
        body,
        grid=(m // tm, n_tiles),
        in_specs=(_split_rows((tm, k), n_first)
                  + [pl.BlockSpec((k, tn), lambda i, j: (0, j)) for _ in ws]
                  + ([row] if len(xs) == 1 else _split_rows((tm, n), n_first))
                  + [vec, vec]),
        out_specs=row,
        out_shape=jax.ShapeDtypeStruct((m, n), F32),
        scratch_shapes=[pltpu.VMEM((tm, k), BF16), pltpu.VMEM((n_tiles, tm, tn), F32)],
        compiler_params=_params("parallel", "arbitrary"),
        name="glu_res_ln" if len(ws) == 2 else "proj_res_ln",
    )(a1, a2, *ws, *xs, g.reshape(1, n), b.reshape(1, n))


def _ffn_body(te_ref, tr_ref, *refs, n_f, dense, half):
    del te_ref
    if dense:
        x_ref, wg_ref, wu_ref, wd_ref, g_ref, b_ref, o_ref, xb = refs[:8]
    else:
        x_ref, wg_ref, wu_ref, wd_ref, o_ref, xb = refs[:6]
    tm = x_ref.shape[0]
    t = pl.program_id(0)
    f = pl.program_id(1)
    n_rows = tr_ref[t]

    @pl.when(f == 0)
    def _():
        xb[...] = x_ref[...].astype(BF16)
        o_ref[...] = jnp.zeros_like(o_ref)

    def swiglu_rows(n):
        x = xb[:n, :]
        hg = _dot(x, wg_ref[...].astype(BF16))
        hu = _dot(x, wu_ref[...].astype(BF16))
        h = (hg * _sigmoid(hg)) * hu
        o_ref[:n, :] += _dot(h.astype(BF16), wd_ref[...].astype(BF16))

    for n in range(half, tm + 1, half):
        pl.when((n_rows > n - half) & (n_rows <= n))(functools.partial(swiglu_rows, n))

    if dense:
        @pl.when(f == n_f - 1)
        def _():
            o_ref[...] = _layer_norm(ALPHA * x_ref[...] + o_ref[...], g_ref[...], b_ref[...])


def _ffn_call(tile_expert, tile_rows, x, wg, wu, wd, extra, extra_specs, tm, tf, half, dense, name):
    s, d = x.shape
    n_t = s // tm
    n_f = wg.shape[-1] // tf
    live = lambda t, tr: jnp.minimum(tr[t], 1)
    x_spec = pl.BlockSpec((tm, d), lambda t, f, te, tr: (t, 0))
    wgu = pl.BlockSpec((None, d, tf), lambda t, f, te, tr: (te[t], 0, f * live(t, tr)))
    wds = pl.BlockSpec((None, tf, d), lambda t, f, te, tr: (te[t], f * live(t, tr), 0))
    grid_spec = pltpu.PrefetchScalarGridSpec(
        num_scalar_prefetch=2, grid=(n_t, n_f),
        in_specs=[x_spec, wgu, wgu, wds] + extra_specs,
        out_specs=x_spec,
        scratch_shapes=[pltpu.VMEM((tm, d), BF16)])
    return pl.pallas_call(
        functools.partial(_ffn_body, n_f=n_f, dense=dense, half=half),
        grid_spec=grid_spec,
        out_shape=jax.ShapeDtypeStruct((s, d), F32),
        compiler_params=_params("parallel", "arbitrary"),
        name=name,
    )(tile_expert, tile_rows, x, wg, wu, wd, *extra)


def _ffn_dense(x, wg, wu, wd, g, b, tm, tf):
    m, d = x.shape
    n_t = m // tm
    vec = pl.BlockSpec((1, d), lambda t, f, te, tr: (0, 0))
    return _ffn_call(jnp.zeros((n_t,), jnp.int32), jnp.full((n_t,), tm, jnp.int32), x, wg, wu, wd,
                     [g.reshape(1, d), b.reshape(1, d)], [vec, vec], tm, tf, tm, True, "ffn_res_ln")


def _ffn_moe(tile_expert, tile_rows, xs, wg, wu, wd, tm, tf, half):
    return _ffn_call(tile_expert, tile_rows, xs, wg, wu, wd, [], [], tm, tf, half, False, "moe_ffn")


def _router_body(y_ref, w_ref, i_ref, g_ref):
    logits = jnp.dot(y_ref[...], w_ref[...], precision=HIGHEST, preferred_element_type=F32)
    lane = lax.broadcasted_iota(jnp.int32, logits.shape, 1)
    lane_f = lane.astype(F32)
    l1 = jnp.where(lane < N_EXPERTS, logits, -jnp.inf)
    m1 = jnp.max(l1, axis=1, keepdims=True)
    i1 = jnp.min(jnp.where(l1 == m1, lane_f, float(LANES)), axis=1, keepdims=True)
    l2 = jnp.where(lane_f == i1, -jnp.inf, l1)
    m2 = jnp.max(l2, axis=1, keepdims=True)
    i2 = jnp.min(jnp.where(l2 == m2, lane_f, float(LANES)), axis=1, keepdims=True)
    e = jnp.exp(m2 - m1)
    g1 = 1.0 / (1.0 + e)
    g2 = e / (1.0 + e)
    i_ref[...] = jnp.where(lane == 0, i1, jnp.where(lane == 1, i2, 0.0)).astype(jnp.int32)
    g_ref[...] = jnp.where(lane == 0, g1, jnp.where(lane == 1, g2, 0.0))


def _router(y, w_router, tm):
    m, d = y.shape
    wr = jnp.zeros((d, LANES), F32).at[:, :N_EXPERTS].set(w_router)
    row = pl.BlockSpec((tm, LANES), lambda i: (i, 0))
    return pl.pallas_call(
        _router_body,
        grid=(m // tm,),
        in_specs=[pl.BlockSpec((tm, d), lambda i: (i, 0)), pl.BlockSpec((d, LANES), lambda i: (0, 0))],
        out_specs=[row, row],
        out_shape=[jax.ShapeDtypeStruct((m, LANES), jnp.int32), jax.ShapeDtypeStruct((m, LANES), F32)],
        compiler_params=_params("parallel"),
        name="moe_router",
    )(y, wr)


def _combine_ln_body(x_ref, a_ref, b2_ref, tg_ref, g_ref, b_ref, op_ref, os_ref, *, n_first):
    i = pl.program_id(0)
    tg = tg_ref[...]
    moe = tg[:, 0:1] * a_ref[...] + tg[:, 1:2] * b2_ref[...]
    out = _layer_norm(ALPHA * x_ref[...] + moe, g_ref[...], b_ref[...])

    @pl.when(i < n_first)
    def _():
        op_ref[...] = out

    @pl.when(i >= n_first)
    def _():
        os_ref[...] = out


def _combine_ln(x, pairs, top_gate, g, b, m_first, tm):
    m, d = x.shape
    n_first = m_first // tm
    vec = pl.BlockSpec((1, d), lambda i: (0, 0))
    return pl.pallas_call(
        functools.partial(_combine_ln_body, n_first=n_first),
        grid=(m // tm,),
        in_specs=[pl.BlockSpec((tm, d), lambda i: (i, 0)),
                  pl.BlockSpec((None, tm, d), lambda i: (0, i, 0)),
                  pl.BlockSpec((None, tm, d), lambda i: (1, i, 0)),
                  pl.BlockSpec((tm, LANES), lambda i: (i, 0)),
                  vec, vec],
        out_specs=[pl.BlockSpec((tm, d), lambda i: (jnp.minimum(i, n_first - 1), 0)),
                   pl.BlockSpec((tm, d), lambda i: (jnp.maximum(i - n_first, 0), 0))],
        out_shape=[jax.ShapeDtypeStruct((m_first, d), F32), jax.ShapeDtypeStruct((m - m_first, d), F32)],
        compiler_params=_params("arbitrary"),
        name="moe_combine_ln",
    )(x, pairs, pairs, top_gate, g.reshape(1, d), b.reshape(1, d))


def _moe_routing(top_idx, tm):
    m = top_idx.shape[0]
    n_pairs = m * TOP_K
    n_tiles = n_pairs // tm + N_EXPERTS
    i32 = jnp.int32
    e_flat = jnp.concatenate([top_idx[:, k] for k in range(TOP_K)])
    order = jnp.argsort(e_flat, stable=True).astype(i32)
    inv = jnp.argsort(order).astype(i32)
    counts = jnp.sum((e_flat[None, :] == jnp.arange(N_EXPERTS, dtype=i32)[:, None]).astype(i32), axis=1)
    padded = ((counts + tm - 1) // tm) * tm
    ends_padded = jnp.cumsum(padded)
    starts_padded = ends_padded - padded
    starts = jnp.cumsum(counts) - counts
    tile_start = jnp.arange(n_tiles, dtype=i32) * tm
    tile_expert = jnp.minimum(jnp.searchsorted(ends_padded, tile_start, side="right"), N_EXPERTS - 1).astype(i32)
    tile_rows = jnp.clip(counts[tile_expert] - (tile_start - starts_padded[tile_expert]), 0, tm).astype(i32)
    slot = jnp.arange(n_tiles * tm, dtype=i32)
    slot_e = jnp.repeat(tile_expert, tm)
    rank = slot - starts_padded[slot_e]
    live = (rank >= 0) & (rank < counts[slot_e])
    src = order[jnp.clip(starts[slot_e] + rank, 0, n_pairs - 1)]
    slot_token = jnp.where(live, src % m, slot % m)
    pair_slot = starts_padded[e_flat] + inv - starts[e_flat]
    return tile_expert, tile_rows, slot_token, pair_slot


def _gelu_tanh(x):
    return 0.5 * x * (1.0 + jnp.tanh(math.sqrt(2.0 / math.pi) * (x + 0.044715 * (x * x * x))))


def _ssm_body(*refs, sequential):
    (u_ref, tz_ref, wb_ref, vc_ref, a8r_ref, a8i_ref, d_ref, h0r_ref, h0i_ref, y_ref, hr_ref, hi_ref) = refs[:12]
    scr = refs[12:]
    n_chunks = u_ref.shape[0] // SSM_CHUNK
    sw = a8r_ref.shape[1]
    step_rows = lambda s: pl.ds(s, n_chunks, stride=SSM_CHUNK)
    ucat = jnp.concatenate([u_ref[step_rows(s), :].astype(BF16) for s in range(SSM_CHUNK)], axis=1)
    hl = _dot(ucat, wb_ref[...])
    hl_r = hl[:, :sw]
    hl_i = hl[:, sw:]
    a8r = a8r_ref[...]
    a8i = a8i_ref[...]
    if sequential:
        hlr_scr, hli_scr, hinr_scr, hini_scr = scr
        hlr_scr[...] = hl_r
        hli_scr[...] = hl_i

        def tile_step(kb, carry):
            hr, hi = carry
            base = pl.multiple_of(kb * SUBLANES, SUBLANES)
            tr = hlr_scr[pl.ds(base, SUBLANES), :]
            ti = hli_scr[pl.ds(base, SUBLANES), :]
            rows_r, rows_i = [], []
            for r in range(SUBLANES):
                rows_r.append(hr)
                rows_i.append(hi)
                hr, hi = (a8r * hr - a8i * hi + tr[r:r + 1], a8r * hi + a8i * hr + ti[r:r + 1])
            hinr_scr[pl.ds(base, SUBLANES), :] = jnp.concatenate(rows_r, axis=0)
            hini_scr[pl.ds(base, SUBLANES), :] = jnp.concatenate(rows_i, axis=0)
            return hr, hi

        hr, hi = lax.fori_loop(0, n_chunks // SUBLANES, tile_step, (h0r_ref[...], h0i_ref[...]))
        hr_ref[...] = hr
        hi_ref[...] = hi
        hin_r = hinr_scr[...]
        hin_i = hini_scr[...]
    else:
        gpb = h0r_ref.shape[1]
        p_n = h0r_ref.shape[2]
        hin_r = jnp.concatenate([h0r_ref[:, g, :] for g in range(gpb)], axis=1)
        hin_i = jnp.concatenate([h0i_ref[:, g, :] for g in range(gpb)], axis=1)
        hr = a8r * hin_r - a8i * hin_i + hl_r
        hi = a8r * hin_i + a8i * hin_r + hl_i
        for g in range(gpb):
            hr_ref[:, g, :] = hr[:, g * p_n:(g + 1) * p_n]
            hi_ref[:, g, :] = hi[:, g * p_n:(g + 1) * p_n]
    hcat = jnp.concatenate([hin_r.astype(BF16), hin_i.astype(BF16)], axis=1)
    d = d_ref[...]
    steps_per_dot = 2
    for t0 in range(0, SSM_CHUNK, steps_per_dot):
        cols = slice(t0 * LANES, (t0 + steps_per_dot) * LANES)
        k = (t0 + steps_per_dot) * LANES
        y2 = _dot(ucat[:, :k], tz_ref[:k, cols]) + _dot(hcat, vc_ref[:, cols])
        for t in range(t0, t0 + steps_per_dot):
            y = y2[:, (t - t0) * LANES:(t - t0 + 1) * LANES]
            y_ref[step_rows(t), :] = _gelu_tanh(y + d * u_ref[step_rows(t), :])


def _group_blockdiag(x, rows_per_group, lanes_per_group):
    t_n, g_n, a_n, b_n = x.shape
    gpb = LANES // SSM_GROUP
    nb = g_n // gpb
    assert a_n == rows_per_group and b_n == lanes_per_group
    x = x.reshape(t_n, nb, gpb * a_n, 1, b_n)
    x = jnp.broadcast_to(x, (t_n, nb, gpb * a_n, gpb, b_n)).reshape(t_n, nb, gpb * a_n, gpb * b_n)
    r_grp = np.arange(gpb * a_n)[:, None] // a_n
    l_grp = np.arange(gpb * b_n)[None, :] // b_n
    return jnp.where(jnp.asarray(r_grp == l_grp), x, 0.0).astype(BF16).transpose(1, 0, 2, 3)


def _ssm_operators(a_re, a_im, log_dt, b_re, b_im, c_re, c_im):
    g_n, p_n = a_re.shape
    lam_re = jnp.minimum(a_re, -1e-4)
    lam_im = a_im
    dt = jnp.exp(log_dt)[:, None]
    decay = jnp.exp(lam_re * dt)
    ar = decay * jnp.cos(lam_im * dt)
    ai = decay * jnp.sin(lam_im * dt)
    num_re = ar - 1.0
    den = lam_re * lam_re + lam_im * lam_im
    f_re = (num_re * lam_re + ai * lam_im) / den
    f_im = (ai * lam_re - num_re * lam_im) / den
    bt_re = jnp.swapaxes(b_re, 1, 2)
    bt_im = jnp.swapaxes(b_im, 1, 2)
    bb_re = f_re[:, None, :] * bt_re - f_im[:, None, :] * bt_im
    bb_im = f_re[:, None, :] * bt_im + f_im[:, None, :] * bt_re
    pw_re, pw_im = [jnp.ones_like(ar)], [jnp.zeros_like(ar)]
    for _ in range(SSM_CHUNK):
        pr, pi = pw_re[-1], pw_im[-1]
        pw_re.append(pr * ar - pi * ai)
        pw_im.append(pr * ai + pi * ar)
    pw_re = jnp.stack(pw_re)
    pw_im = jnp.stack(pw_im)
    pk_re = pw_re[:SSM_CHUNK, :, None, :]
    pk_im = pw_im[:SSM_CHUNK, :, None, :]
    ab_re = pk_re * bb_re - pk_im * bb_im
    ab_im = pk_re * bb_im + pk_im * bb_re
    conv = (jnp.einsum("gcp,tgdp->tgdc", c_re, ab_re, precision=HIGHEST)
            - jnp.einsum("gcp,tgdp->tgdc", c_im, ab_im, precision=HIGHEST))
    bd = _group_blockdiag(conv, SSM_GROUP, SSM_GROUP)
    nb = bd.shape[0]
    tau = np.arange(SSM_CHUNK)[None, :] - np.arange(SSM_CHUNK)[:, None]
    bd_ext = jnp.concatenate([bd, jnp.zeros_like(bd[:, :1])], axis=1)
    tz = bd_ext[:, np.where(tau >= 0, tau, SSM_CHUNK)]
    tz = tz.transpose(0, 1, 3, 2, 4).reshape(nb, SSM_CHUNK * LANES, SSM_CHUNK * LANES)
    wb = jnp.concatenate([_group_blockdiag(ab_re[::-1], SSM_GROUP, p_n),
                          _group_blockdiag(ab_im[::-1], SSM_GROUP, p_n)], axis=3)
    wb = wb.reshape(nb, SSM_CHUNK * LANES, wb.shape[3])
    pr = pw_re[1:, :, :, None]
    pi = pw_im[1:, :, :, None]
    ct_re = jnp.swapaxes(c_re, 1, 2)[None]
    ct_im = jnp.swapaxes(c_im, 1, 2)[None]
    vc = jnp.concatenate([_group_blockdiag(ct_re * pr - ct_im * pi, p_n, SSM_GROUP),
                          _group_blockdiag(-(ct_re * pi + ct_im * pr), p_n, SSM_GROUP)], axis=2)
    vc = vc.transpose(0, 2, 1, 3).reshape(nb, vc.shape[2], SSM_CHUNK * LANES)
    a8r = pw_re[SSM_CHUNK].reshape(1, g_n * p_n)
    a8i = pw_im[SSM_CHUNK].reshape(1, g_n * p_n)
    return tz, wb, vc, a8r, a8i


def _ssm_scan(u, ops, d_skip, h0_re, h0_im, chunk0, n_chunks, sequential):
    tz, wb, vc, a8r, a8i = ops
    nb = tz.shape[0]
    sw = wb.shape[-1] // 2
    rows = n_chunks * SSM_CHUNK
    cblk = chunk0 // n_chunks
    op3 = lambda a: pl.BlockSpec((None,) + a.shape[1:], lambda j: (j, 0, 0))
    srow = pl.BlockSpec((1, sw), lambda j: (0, j))
    if sequential:
        hspec = srow
        scratch = [pltpu.VMEM((n_chunks, sw), F32)] * 4
    else:
        hspec = pl.BlockSpec((n_chunks, sw // SSM_STATE, SSM_STATE), lambda j: (0, j, 0))
        scratch = []
    return pl.pallas_call(
        functools.partial(_ssm_body, sequential=sequential),
        grid=(nb,),
        in_specs=[pl.BlockSpec((rows, LANES), lambda j: (cblk, j)),
                  op3(tz), op3(wb), op3(vc),
                  srow, srow, pl.BlockSpec((1, LANES), lambda j: (0, j)), hspec, hspec],
        out_specs=[pl.BlockSpec((rows, LANES), lambda j: (0, j)), hspec, hspec],
        out_shape=[jax.ShapeDtypeStruct((rows, u.shape[1]), F32),
                   jax.ShapeDtypeStruct(h0_re.shape, F32), jax.ShapeDtypeStruct(h0_im.shape, F32)],
        scratch_shapes=scratch,
        compiler_params=_params("parallel"),
        name="s5_scan_seq" if sequential else "s5_scan_step",
    )(u, tz, wb, vc, a8r, a8i, d_skip.reshape(1, -1), h0_re, h0_im)


def kernel(x_prompt, x_sample, cache_k, cache_v, state_ssm_re, state_ssm_im, page_table, rel_bias, ln_g, ln_b,
           w_qkv, w_o, w_ssm_in, ssm_a_re, ssm_a_im, ssm_log_dt, ssm_b_re, ssm_b_im, ssm_c_re, ssm_c_im, ssm_d,
           w_glu_v, w_glu_g, w_ff_gate, w_ff_up, w_ff_down, w_router, w_moe_gate, w_moe_up, w_moe_down):
    batch, seq, d = x_prompt.shape
    dec_batch, dec_seq, _ = x_sample.shape
    assert batch == 1 and d == D_MODEL and dec_seq == SSM_CHUNK and seq % MOBA_BLOCK == 0
    m_p = batch * seq
    m_s = dec_batch * dec_seq
    m = m_p + m_s
    dq = N_HEADS * HEAD_DIM
    dkv = N_KV_HEADS * HEAD_DIM
    n_pool, page = cache_k.shape[1], cache_k.shape[2]

    x_p = x_prompt.reshape(m_p, d)
    x_s = x_sample.reshape(m_s, d)

    qkv_f32, qkv_b = _qkv_proj(x_p, x_s, w_qkv[0].astype(BF16), tm=1024, tn=512)
    kmean = _block_means(qkv_f32, seq // MOBA_BLOCK)
    tabs, tab_s = _bias_tables(rel_bias)
    attn_p = _attn_prompt(qkv_f32, qkv_b, kmean, tabs, seq)
    ck = cache_k[0].reshape(n_pool, page * N_KV_HEADS, HEAD_DIM)
    cv = cache_v[0].reshape(n_pool, page * N_KV_HEADS, HEAD_DIM)
    attn_s = _attn_sample(page_table, qkv_f32, ck, cv, tab_s, m_p, dec_seq)
    y = _proj_ln(attn_p, attn_s, [w_o[0].astype(BF16)], [x_p, x_s], ln_g[0, 0], ln_b[0, 0], tm=512, tn=512)
    y = _ffn_dense(y, w_ff_gate, w_ff_up, w_ff_down, ln_g[0, 1], ln_b[0, 1], tm=1024, tf=256)

    k_all = qkv_f32[:, dq:dq + dkv]
    v_all = qkv_f32[:, dq + dkv:]
    k_prompt = k_all[:m_p].reshape(1, batch, seq, N_KV_HEADS, HEAD_DIM)
    v_prompt = v_all[:m_p].reshape(1, batch, seq, N_KV_HEADS, HEAD_DIM)
    k_sample = k_all[m_p:].reshape(1, dec_batch, dec_seq, N_KV_HEADS, HEAD_DIM)
    v_sample = v_all[m_p:].reshape(1, dec_batch, dec_seq, N_KV_HEADS, HEAD_DIM)

    u = _matmul(y, w_ssm_in[0].astype(BF16), tm=1024, tn=1024)
    ops = _ssm_operators(ssm_a_re[0], ssm_a_im[0], ssm_log_dt[0], ssm_b_re[0], ssm_b_im[0],
                         ssm_c_re[0], ssm_c_im[0])
    n_state = N_SSM_GROUPS * SSM_STATE
    zero = jnp.zeros((batch, n_state), F32)
    yg_p, hrp, hip = _ssm_scan(u, ops, ssm_d[0], zero, zero, 0, m_p // SSM_CHUNK, True)
    yg_s, hrs, his = _ssm_scan(u, ops, ssm_d[0], state_ssm_re[0], state_ssm_im[0], m_p // SSM_CHUNK, dec_batch,
                               False)
    y = _proj_ln(yg_p, yg_s, [w_glu_v[0].astype(BF16), w_glu_g[0].astype(BF16)],
                 [y], ln_g[1, 0], ln_b[1, 0], tm=512, tn=512)

    tm_moe = 1024
    top_idx, top_gate = _router(y, w_router[0], tm=512)
    tile_expert, tile_rows, slot_token, pair_slot = _moe_routing(top_idx, tm_moe)
    xs = y.at[slot_token].get(mode="promise_in_bounds")
    ys = _ffn_moe(tile_expert, tile_rows, xs, w_moe_gate[0], w_moe_up[0], w_moe_down[0], tm_moe, tf=256, half=256)
    pairs = ys.at[pair_slot].get(mode="promise_in_bounds").reshape(TOP_K, m, d)
    out_p, out_s = _combine_ln(y, pairs, top_gate, ln_g[1, 1], ln_b[1, 1], m_p, tm=512)

    return (out_p.reshape(batch, seq, d), out_s.reshape(dec_batch, dec_seq, d),
            k_prompt, v_prompt, k_sample, v_sample,
            hrp.reshape(1, batch, N_SSM_GROUPS, SSM_STATE), hip.reshape(1, batch, N_SSM_GROUPS, SSM_STATE),
            hrs.reshape(1, dec_batch, N_SSM_GROUPS, SSM_STATE), his.reshape(1, dec_batch, N_SSM_GROUPS, SSM_STATE))
```

```python
import functools
import math

import numpy as np
import jax
import jax.numpy as jnp
from jax import lax
from jax.experimental import pallas as pl
from jax.experimental.pallas import tpu as pltpu

D_MODEL = 2048
N_HEADS = 16
HEAD_DIM = D_MODEL // N_HEADS
N_KV_HEADS = 4
HEADS_PER_KV = N_HEADS // N_KV_HEADS
MOBA_BLOCK = 256
MOBA_TOPK = 3
NUM_BUCKETS = 32
MAX_DISTANCE = 128
SSM_GROUP = 16
N_SSM_GROUPS = D_MODEL // SSM_GROUP
SSM_STATE = 64
SSM_CHUNK = 8
D_FF = 7 * D_MODEL // 2
N_EXPERTS = 8
TOP_K = 2
DEPTH = 2
ALPHA = (2 * DEPTH) ** 0.25
LN_EPS = 1e-5
NEG_INF = -1e30

LANES = 128
SUBLANES = 8
VMEM_LIMIT = 60 * 1024 * 1024

F32 = jnp.float32
BF16 = jnp.bfloat16
HIGHEST = lax.Precision.HIGHEST
_NT = (((1,), (1,)), ((), ()))


def _params(*sem):
    return pltpu.CompilerParams(dimension_semantics=sem, vmem_limit_bytes=VMEM_LIMIT)


def _dot(a, b):
    return jnp.dot(a, b, preferred_element_type=F32)


def _sigmoid(x):
    return 1.0 / (1.0 + jnp.exp(-x))


def _layer_norm(y, g, b):
    mean = jnp.mean(y, axis=-1, keepdims=True)
    yc = y - mean
    var = jnp.mean(yc * yc, axis=-1, keepdims=True)
    return yc * lax.rsqrt(var + LN_EPS) * g + b


def _mm_body(x_ref, w_ref, o_ref, xb_scr):
    @pl.when(pl.program_id(1) == 0)
    def _():
        xb_scr[...] = x_ref[...].astype(BF16)

    o_ref[...] = _dot(xb_scr[...], w_ref[...])


def _matmul(x, w, tm, tn):
    m, k = x.shape
    n = w.shape[1]
    return pl.pallas_call(
        _mm_body,
        grid=(m // tm, n // tn),
        in_specs=[pl.BlockSpec((tm, k), lambda i, j: (i, 0)),
                  pl.BlockSpec((k, tn), lambda i, j: (0, j))],
        out_specs=pl.BlockSpec((tm, tn), lambda i, j: (i, j)),
        out_shape=jax.ShapeDtypeStruct((m, n), F32),
        scratch_shapes=[pltpu.VMEM((tm, k), BF16)],
        compiler_params=_params("parallel", "arbitrary"),
        name="matmul",
    )(x, w)


def _split_rows(block, n_first):
    return [pl.BlockSpec(block, lambda i, j: (jnp.minimum(i, n_first - 1), 0)),
            pl.BlockSpec(block, lambda i, j: (jnp.maximum(i - n_first, 0), 0))]


def _load_split_bf16(dst, src1, src2, n_first):
    i = pl.program_id(0)
    j = pl.program_id(1)

    @pl.when((j == 0) & (i < n_first))
    def _():
        dst[...] = src1[...].astype(BF16)

    @pl.when((j == 0) & (i >= n_first))
    def _():
        dst[...] = src2[...].astype(BF16)


def _qkv_body(x1_ref, x2_ref, w_ref, of_ref, ob_ref, xb_scr, *, nq_tiles, scale, n_first):
    _load_split_bf16(xb_scr, x1_ref, x2_ref, n_first)
    acc = _dot(xb_scr[...], w_ref[...])
    of_ref[...] = acc
    s = jnp.where(pl.program_id(1) < nq_tiles, scale, 1.0).astype(F32)
    ob_ref[...] = (acc * s).astype(BF16)


def _qkv_proj(x1, x2, w, tm, tn):
    k = x1.shape[1]
    m = x1.shape[0] + x2.shape[0]
    n = w.shape[1]
    n_first = x1.shape[0] // tm
    assert x1.shape[0] % tm == 0 and x2.shape[0] % tm == 0
    body = functools.partial(_qkv_body, nq_tiles=(N_HEADS * HEAD_DIM) // tn, scale=HEAD_DIM ** -0.5,
                             n_first=n_first)
    return pl.pallas_call(
        body,
        grid=(m // tm, n // tn),
        in_specs=_split_rows((tm, k), n_first) + [pl.BlockSpec((k, tn), lambda i, j: (0, j))],
        out_specs=[pl.BlockSpec((tm, tn), lambda i, j: (i, j)),
                   pl.BlockSpec((tm, tn), lambda i, j: (i, j))],
        out_shape=[jax.ShapeDtypeStruct((m, n), F32), jax.ShapeDtypeStruct((m, n), BF16)],
        scratch_shapes=[pltpu.VMEM((tm, k), BF16)],
        compiler_params=_params("parallel", "arbitrary"),
        name="qkv_proj",
    )(x1, x2, w)


def _kmean_body(k_ref, o_ref):
    i = pl.program_id(0)

    @pl.when(i == 0)
    def _():
        o_ref[...] = jnp.zeros_like(o_ref)

    mean = jnp.sum(k_ref[...], axis=0, keepdims=True) * (1.0 / MOBA_BLOCK)
    rows = lax.broadcasted_iota(jnp.int32, o_ref.shape, 0)
    o_ref[...] = jnp.where(rows == i, mean, o_ref[...])


def _block_means(qkv_f32, n_blocks):
    dkv = N_KV_HEADS * HEAD_DIM
    kcol = (N_HEADS * HEAD_DIM) // dkv
    return pl.pallas_call(
        _kmean_body,
        grid=(n_blocks,),
        in_specs=[pl.BlockSpec((MOBA_BLOCK, dkv), lambda i: (i, kcol))],
        out_specs=pl.BlockSpec((LANES, dkv), lambda i: (0, 0)),
        out_shape=jax.ShapeDtypeStruct((LANES, dkv), F32),
        compiler_params=_params("arbitrary"),
        name="moba_block_means",
    )(qkv_f32)


def _t5_bucket_np(dist):
    n = np.maximum(dist, 0)
    max_exact = NUM_BUCKETS // 2
    nf = np.maximum(n, 1).astype(np.float32)
    large = max_exact + (np.log(nf / np.float32(max_exact)) / np.float32(math.log(MAX_DISTANCE / max_exact))
                         * np.float32(NUM_BUCKETS - max_exact)).astype(np.int32)
    large = np.minimum(large, NUM_BUCKETS - 1)
    return np.where(n < max_exact, n, large).astype(np.int32)


def _bucket_tables():
    q = np.arange(MOBA_BLOCK)[:, None]
    k = np.arange(MOBA_BLOCK)[None, :]
    own = np.where(q - k >= 0, _t5_bucket_np(q - k), -1)
    prev = _t5_bucket_np(q - k + MOBA_BLOCK)
    return np.stack([own, prev]).astype(np.int32)


def _bias_body(rb_ref, idx_ref, o_ref, os_ref):
    h = pl.program_id(0)
    far = rb_ref[h, NUM_BUCKETS - 1]
    tabs = []
    for m in range(2):
        idx = idx_ref[m]
        acc = jnp.full(idx.shape, NEG_INF, F32)
        for b in range(NUM_BUCKETS):
            acc = jnp.where(idx == b, rb_ref[h, b] - far, acc)
        o_ref[0, m] = acc
        tabs.append(acc)
    os_ref[0] = jnp.concatenate([tabs[1][:SUBLANES, :], tabs[0][:SUBLANES, :LANES]], axis=1)


def _bias_tables(rel_bias):
    idx = jnp.asarray(_bucket_tables())
    return pl.pallas_call(
        _bias_body,
        grid=(N_HEADS,),
        in_specs=[pl.BlockSpec(memory_space=pltpu.SMEM),
                  pl.BlockSpec((2, MOBA_BLOCK, MOBA_BLOCK), lambda h: (0, 0, 0))],
        out_specs=[pl.BlockSpec((1, 2, MOBA_BLOCK, MOBA_BLOCK), lambda h: (h, 0, 0, 0)),
                   pl.BlockSpec((1, SUBLANES, MOBA_BLOCK + LANES), lambda h: (h, 0, 0))],
        out_shape=[jax.ShapeDtypeStruct((N_HEADS, 2, MOBA_BLOCK, MOBA_BLOCK), F32),
                   jax.ShapeDtypeStruct((N_HEADS, SUBLANES, MOBA_BLOCK + LANES), F32)],
        compiler_params=_params("parallel"),
        name="moba_bias_tables",
    )(rel_bias, idx)


def _select_blocks(gate, n_past):
    lane = lax.broadcasted_iota(jnp.int32, gate.shape, 1)
    lane_f = lane.astype(F32)
    g = jnp.where(lane < n_past, gate, NEG_INF)
    sel = jnp.full(gate.shape, NEG_INF, F32)
    for _ in range(MOBA_TOPK):
        mx = jnp.max(g, axis=1, keepdims=True)
        idx = jnp.min(jnp.where(g == mx, lane_f, float(gate.shape[1])), axis=1, keepdims=True)
        pick = lane_f == idx
        sel = jnp.where(pick, 0.0, sel)
        g = jnp.where(pick, -jnp.inf, g)
    return jnp.where(lane < n_past, sel, 0.0)


def _attn_prompt_body(qf_ref, qb_ref, k_ref, v_ref, km_ref, tab_ref, o_ref, qa_scr, s_scr, m_scr, l_scr, acc_scr):
    i = pl.program_id(1)
    rows = HEADS_PER_KV * MOBA_BLOCK
    km = km_ref[...]
    lane = lax.broadcasted_iota(jnp.int32, (MOBA_BLOCK, LANES), 1)
    dummy = LANES - 1
    for hh in range(HEADS_PER_KV):
        cs = slice(hh * HEAD_DIM, (hh + 1) * HEAD_DIM)
        rs = slice(hh * MOBA_BLOCK, (hh + 1) * MOBA_BLOCK)
        gate = lax.dot_general(qf_ref[:, cs], km, _NT, precision=HIGHEST, preferred_element_type=F32)
        mask = jnp.where(lane == dummy, NEG_INF, _select_blocks(gate, i))
        qa_scr[rs, :HEAD_DIM] = qb_ref[:, cs]
        qa_scr[rs, HEAD_DIM:] = mask.astype(BF16)
    qa = qa_scr[...]

    def block_rows(j):
        return pl.ds(pl.multiple_of(j * MOBA_BLOCK, MOBA_BLOCK), MOBA_BLOCK)

    def pair_scores(ja, lane_a, jb, lane_b):
        rhs = [jnp.concatenate([k_ref[block_rows(j), :], (lane == mask_lane).astype(BF16)], axis=1)
               for j, mask_lane in ((ja, lane_a), (jb, lane_b))]
        return lax.dot_general(qa, jnp.concatenate(rhs, axis=0), _NT, preferred_element_type=F32)

    n_far = jnp.maximum(i - 1, 0)
    n_pairs = 1 + (n_far + 1) // 2

    def far_blocks(t):
        out = []
        for j in (2 * t - 2, 2 * t - 1):
            ok = (j >= 0) & (j < n_far)
            out += [jnp.where(ok, j, 0), jnp.where(ok, j, dummy)]
        return out

    s0 = pair_scores(i, -1, jnp.maximum(i - 1, 0), jnp.where(i >= 1, i - 1, dummy))
    s_scr[...] = s0 + jnp.concatenate([tab_ref[:, 0].reshape(rows, MOBA_BLOCK),
                                       tab_ref[:, 1].reshape(rows, MOBA_BLOCK)], axis=1)
    m_scr[...] = jnp.full(m_scr.shape, NEG_INF, F32)
    l_scr[...] = jnp.zeros_like(l_scr)
    acc_scr[...] = jnp.zeros_like(acc_scr)

    def pair_step(t, carry):
        na, la, nb_, lb = far_blocks(t + 1)
        rhs_next = jnp.concatenate(
            [jnp.concatenate([k_ref[block_rows(j), :], (lane == mask_lane).astype(BF16)], axis=1)
             for j, mask_lane in ((na, la), (nb_, lb))], axis=0)
        ja, _, jb, _ = far_blocks(t)
        ja = jnp.where(t == 0, i, ja)
        jb = jnp.where(t == 0, jnp.maximum(i - 1, 0), jb)
        v = jnp.concatenate([v_ref[block_rows(ja), :], v_ref[block_rows(jb), :]], axis=0)
        for hh in range(HEADS_PER_KV):
            rs = slice(hh * MOBA_BLOCK, (hh + 1) * MOBA_BLOCK)
            s = s_scr[rs, :]
            s_scr[rs, :] = lax.dot_general(qa_scr[rs, :], rhs_next, _NT, preferred_element_type=F32)
            m_prev = m_scr[rs, :]
            m_new = jnp.maximum(m_prev, jnp.max(s, axis=1, keepdims=True))
            alpha = jnp.exp(m_prev - m_new)
            p = jnp.exp(s - jnp.concatenate([m_new] * (s.shape[1] // LANES), axis=1))
            l_scr[rs, :] = alpha * l_scr[rs, :] + jnp.sum(p, axis=1, keepdims=True)
            acc_scr[rs, :] = alpha * acc_scr[rs, :] + _dot(p.astype(BF16), v)
            m_scr[rs, :] = m_new
        return carry

    lax.fori_loop(0, n_pairs, pair_step, 0)

    out = acc_scr[...] / l_scr[...]
    for hh in range(HEADS_PER_KV):
        o_ref[:, hh * HEAD_DIM:(hh + 1) * HEAD_DIM] = out[hh * MOBA_BLOCK:(hh + 1) * MOBA_BLOCK]


def _attn_prompt(qkv_f32, qkv_b, kmean, tabs, seq):
    nblk = seq // MOBA_BLOCK
    assert nblk < LANES - 1
    gw = HEADS_PER_KV * HEAD_DIM
    kcol = (N_HEADS * HEAD_DIM) // HEAD_DIM
    vcol = kcol + N_KV_HEADS
    rows = HEADS_PER_KV * MOBA_BLOCK
    return pl.pallas_call(
        _attn_prompt_body,
        grid=(N_KV_HEADS, nblk),
        in_specs=[pl.BlockSpec((MOBA_BLOCK, gw), lambda g, i: (i, g)),
                  pl.BlockSpec((MOBA_BLOCK, gw), lambda g, i: (i, g)),
                  pl.BlockSpec((seq, HEAD_DIM), lambda g, i: (0, kcol + g)),
                  pl.BlockSpec((seq, HEAD_DIM), lambda g, i: (0, vcol + g)),
                  pl.BlockSpec((LANES, HEAD_DIM), lambda g, i: (0, g)),
                  pl.BlockSpec((HEADS_PER_KV, 2, MOBA_BLOCK, MOBA_BLOCK), lambda g, i: (g, 0, 0, 0))],
        out_specs=pl.BlockSpec((MOBA_BLOCK, gw), lambda g, i: (i, g)),
        out_shape=jax.ShapeDtypeStruct((seq, N_HEADS * HEAD_DIM), F32),
        scratch_shapes=[pltpu.VMEM((rows, 2 * HEAD_DIM), BF16),
                        pltpu.VMEM((rows, 2 * MOBA_BLOCK), F32),
                        pltpu.VMEM((rows, LANES), F32),
                        pltpu.VMEM((rows, LANES), F32),
                        pltpu.VMEM((rows, HEAD_DIM), F32)],
        compiler_params=_params("parallel", "arbitrary"),
        name="moba_attn_prompt",
    )(qkv_f32, qkv_b, qkv_b, qkv_b, kmean, tabs)


def _attn_sample_body(pt_ref, *refs, n_pages, page, dec_seq):
    del pt_ref
    qkv_ref = refs[0]
    kp = refs[1:1 + n_pages]
    vp = refs[1 + n_pages:1 + 2 * n_pages]
    tab_ref = refs[1 + 2 * n_pages]
    o_ref = refs[2 + 2 * n_pages]
    kall, vall, expand = refs[3 + 2 * n_pages:]
    past = n_pages * page
    n_past_blocks = past // MOBA_BLOCK
    pages_per_block = MOBA_BLOCK // page
    near0 = past - MOBA_BLOCK
    total = past + LANES
    dq = N_HEADS * HEAD_DIM
    dkv = N_KV_HEADS * HEAD_DIM
    hq = HEADS_PER_KV * dec_seq
    rows = N_HEADS * dec_seq
    pad = jnp.zeros((LANES - dec_seq, HEAD_DIM), F32)

    @pl.when(pl.program_id(0) == 0)
    def _():
        key_blk = lax.broadcasted_iota(jnp.int32, (LANES, total), 1) // MOBA_BLOCK
        expand[...] = (key_blk == lax.broadcasted_iota(jnp.int32, (LANES, total), 0)).astype(BF16)

    kmeans = []
    for g in range(N_KV_HEADS):
        means = []
        for blk in range(n_past_blocks):
            tot = jnp.zeros((1, HEAD_DIM), F32)
            for pp in range(pages_per_block):
                pg = blk * pages_per_block + pp
                kk = kp[pg][pl.ds(g, page, stride=N_KV_HEADS), :]
                vv = vp[pg][pl.ds(g, page, stride=N_KV_HEADS), :]
                tot = tot + jnp.sum(kk, axis=0, keepdims=True)
                kall[g, pg * page:(pg + 1) * page, :] = kk.astype(BF16)
                vall[g, pg * page:(pg + 1) * page, :] = vv.astype(BF16)
            means.append(tot * (1.0 / MOBA_BLOCK))
        kmeans.append(jnp.concatenate(means + [jnp.zeros((LANES - n_past_blocks, HEAD_DIM), F32)], axis=0))
        knew = qkv_ref[:, dq + g * HEAD_DIM:dq + (g + 1) * HEAD_DIM]
        vnew = qkv_ref[:, dq + dkv + g * HEAD_DIM:dq + dkv + (g + 1) * HEAD_DIM]
        kall[g, past:total, :] = jnp.concatenate([knew, pad], axis=0).astype(BF16)
        vall[g, past:total, :] = jnp.concatenate([vnew, pad], axis=0).astype(BF16)

    qs = jnp.concatenate([qkv_ref[:, h * HEAD_DIM:(h + 1) * HEAD_DIM] for h in range(N_HEADS)], axis=0)
    gate_all = lax.dot_general(qs, jnp.concatenate(kmeans, axis=0), _NT, precision=HIGHEST,
                               preferred_element_type=F32)
    row_g = lax.broadcasted_iota(jnp.int32, (rows, LANES), 0) // hq
    gate = gate_all[:, :LANES]
    for g in range(1, N_KV_HEADS):
        gate = jnp.where(row_g == g, gate_all[:, g * LANES:(g + 1) * LANES], gate)
    selm = _select_blocks(gate, n_past_blocks).astype(BF16)

    qb = (qs * (HEAD_DIM ** -0.5)).astype(BF16)
    s = jnp.concatenate([lax.dot_general(qb[g * hq:(g + 1) * hq], kall[g], _NT, preferred_element_type=F32)
                         for g in range(N_KV_HEADS)], axis=0)
    s = s + _dot(selm, expand[...])
    s_far = s[:, :near0]
    s_near = s[:, near0:] + tab_ref[...].reshape(rows, MOBA_BLOCK + LANES)
    m = jnp.maximum(jnp.max(s_far, axis=1, keepdims=True), jnp.max(s_near, axis=1, keepdims=True))
    p_far = jnp.exp(s_far - m)
    p_near = jnp.exp(s_near - m)
    inv_l = 1.0 / (jnp.sum(p_far, axis=1, keepdims=True) + jnp.sum(p_near, axis=1, keepdims=True))
    p_far = p_far.astype(BF16)
    p_near = p_near.astype(BF16)
    for g in range(N_KV_HEADS):
        rs = slice(g * hq, (g + 1) * hq)
        out = (_dot(p_far[rs], vall[g, :near0, :]) + _dot(p_near[rs], vall[g, near0:, :])) * inv_l[rs]
        for hh in range(HEADS_PER_KV):
            c0 = (g * HEADS_PER_KV + hh) * HEAD_DIM
            o_ref[:, c0:c0 + HEAD_DIM] = out[hh * dec_seq:(hh + 1) * dec_seq]


def _attn_sample(page_table, qkv_f32, cache_k, cache_v, tab_s, row0, dec_seq):
    dec_batch, n_pages = page_table.shape
    page = cache_k.shape[1] // N_KV_HEADS
    total = n_pages * page + LANES
    blk0 = row0 // dec_seq

    def page_spec(p):
        return pl.BlockSpec((None, page * N_KV_HEADS, HEAD_DIM), lambda b, pt, p=p: (pt[b, p], 0, 0))

    body = functools.partial(_attn_sample_body, n_pages=n_pages, page=page, dec_seq=dec_seq)
    grid_spec = pltpu.PrefetchScalarGridSpec(
        num_scalar_prefetch=1,
        grid=(dec_batch,),
        in_specs=([pl.BlockSpec((dec_seq, qkv_f32.shape[1]), lambda b, pt: (blk0 + b, 0))]
                  + [page_spec(p) for p in range(n_pages)]
                  + [page_spec(p) for p in range(n_pages)]
                  + [pl.BlockSpec(tab_s.shape, lambda b, pt: (0, 0, 0))]),
        out_specs=pl.BlockSpec((dec_seq, N_HEADS * HEAD_DIM), lambda b, pt: (b, 0)),
        scratch_shapes=[pltpu.VMEM((N_KV_HEADS, total, HEAD_DIM), BF16),
                        pltpu.VMEM((N_KV_HEADS, total, HEAD_DIM), BF16),
                        pltpu.VMEM((LANES, total), BF16)],
    )
    return pl.pallas_call(
        body,
        grid_spec=grid_spec,
        out_shape=jax.ShapeDtypeStruct((dec_batch * dec_seq, N_HEADS * HEAD_DIM), F32),
        compiler_params=_params("arbitrary"),
        name="moba_attn_sample",
    )(page_table, qkv_f32, *([cache_k] * n_pages), *([cache_v] * n_pages), tab_s)


def _proj_ln_body(*refs, n_w, n_x, n_tiles, n_first):
    a1_ref, a2_ref = refs[:2]
    w_refs = refs[2:2 + n_w]
    x_refs = refs[2 + n_w:2 + n_w + n_x]
    g_ref, b_ref, o_ref, a_scr, z_scr = refs[2 + n_w + n_x:]
    i = pl.program_id(0)
    j = pl.program_id(1)
    _load_split_bf16(a_scr, a1_ref, a2_ref, n_first)
    a = a_scr[...]
    z = _dot(a, w_refs[0][...])
    if n_w == 2:
        z = z * _sigmoid(_dot(a, w_refs[1][...]))
    z_scr[j] = z

    @pl.when(j == n_tiles - 1)
    def _():
        zfull = jnp.concatenate([z_scr[t] for t in range(n_tiles)], axis=1)
        x = x_refs[0][...] if n_x == 1 else jnp.where(i < n_first, x_refs[0][...], x_refs[1][...])
        o_ref[...] = _layer_norm(ALPHA * x + zfull, g_ref[...], b_ref[...])


def _proj_ln(a1, a2, ws, xs, g, b, tm, tn):
    k = a1.shape[1]
    m = a1.shape[0] + a2.shape[0]
    n = ws[0].shape[1]
    n_first = a1.shape[0] // tm
    assert a1.shape[0] % tm == 0 and a2.shape[0] % tm == 0
    assert len(xs) == 1 or xs[0].shape[0] == a1.shape[0]
    n_tiles = n // tn
    body = functools.partial(_proj_ln_body, n_w=len(ws), n_x=len(xs), n_tiles=n_tiles, n_first=n_first)
    row = pl.BlockSpec((tm, n), lambda i, j: (i, 0))
    vec = pl.BlockSpec((1, n), lambda i, j: (0, 0))
    return pl.pallas_call(
        body,
        grid=(m // tm, n_tiles),
        in_specs=(_split_rows((tm, k), n_first)
                  + [pl.BlockSpec((k, tn), lambda i, j: (0, j)) for _ in ws]
                  + ([row] if len(xs) == 1 else _split_rows((tm, n), n_first))
                  + [vec, vec]),
        out_specs=row,
        out_shape=jax.ShapeDtypeStruct((m, n), F32),
        scratch_shapes=[pltpu.VMEM((tm, k), BF16), pltpu.VMEM((n_tiles, tm, tn), F32)],
        compiler_params=_params("parallel", "arbitrary"),
        name="glu_res_ln" if len(ws) == 2 else "proj_res_ln",
    )(a1, a2, *ws, *xs, g.reshape(1, n), b.reshape(1, n))


def _ffn_body(te_ref, tr_ref, *refs, n_f, dense, half):
    del te_ref
    if dense:
        x_ref, wg_ref, wu_ref, wd_ref, g_ref, b_ref, o_ref, xb = refs[:8]
    else:
        x_ref, wg_ref, wu_ref, wd_ref, o_ref, xb = refs[:6]
    tm = x_ref.shape[0]
    t = pl.program_id(0)
    f = pl.program_id(1)
    n_rows = tr_ref[t]

    @pl.when(f == 0)
    def _():
        xb[...] = x_ref[...].astype(BF16)
        o_ref[...] = jnp.zeros_like(o_ref)

    def swiglu_rows(n):
        x = xb[:n, :]
        hg = _dot(x, wg_ref[...].astype(BF16))
        hu = _dot(x, wu_ref[...].astype(BF16))
        h = (hg * _sigmoid(hg)) * hu
        o_ref[:n, :] += _dot(h.astype(BF16), wd_ref[...].astype(BF16))

    for n in range(half, tm + 1, half):
        pl.when((n_rows > n - half) & (n_rows <= n))(functools.partial(swiglu_rows, n))

    if dense:
        @pl.when(f == n_f - 1)
        def _():
            o_ref[...] = _layer_norm(ALPHA * x_ref[...] + o_ref[...], g_ref[...], b_ref[...])


def _ffn_call(tile_expert, tile_rows, x, wg, wu, wd, extra, extra_specs, tm, tf, half, dense, name):
    s, d = x.shape
    n_t = s // tm
    n_f = wg.shape[-1] // tf
    live = lambda t, tr: jnp.minimum(tr[t], 1)
    x_spec = pl.BlockSpec((tm, d), lambda t, f, te, tr: (t, 0))
    wgu = pl.BlockSpec((None, d, tf), lambda t, f, te, tr: (te[t], 0, f * live(t, tr)))
    wds = pl.BlockSpec((None, tf, d), lambda t, f, te, tr: (te[t], f * live(t, tr), 0))
    grid_spec = pltpu.PrefetchScalarGridSpec(
        num_scalar_prefetch=2, grid=(n_t, n_f),
        in_specs=[x_spec, wgu, wgu, wds] + extra_specs,
        out_specs=x_spec,
        scratch_shapes=[pltpu.VMEM((tm, d), BF16)])
    return pl.pallas_call(
        functools.partial(_ffn_body, n_f=n_f, dense=dense, half=half),
        grid_spec=grid_spec,
        out_shape=jax.ShapeDtypeStruct((s, d), F32),
        compiler_params=_params("parallel", "arbitrary"),
        name=name,
    )(tile_expert, tile_rows, x, wg, wu, wd, *extra)


def _ffn_dense(x, wg, wu, wd, g, b, tm, tf):
    m, d = x.shape
    n_t = m // tm
    vec = pl.BlockSpec((1, d), lambda t, f, te, tr: (0, 0))
    return _ffn_call(jnp.zeros((n_t,), jnp.int32), jnp.full((n_t,), tm, jnp.int32), x, wg, wu, wd,
                     [g.reshape(1, d), b.reshape(1, d)], [vec, vec], tm, tf, tm, True, "ffn_res_ln")


def _ffn_moe(tile_expert, tile_rows, xs, wg, wu, wd, tm, tf, half):
    return _ffn_call(tile_expert, tile_rows, xs, wg, wu, wd, [], [], tm, tf, half, False, "moe_ffn")


def _router_body(y_ref, w_ref, i_ref, g_ref):
    logits = jnp.dot(y_ref[...], w_ref[...], precision=HIGHEST, preferred_element_type=F32)
    lane = lax.broadcasted_iota(jnp.int32, logits.shape, 1)
    lane_f = lane.astype(F32)
    l1 = jnp.where(lane < N_EXPERTS, logits, -jnp.inf)
    m1 = jnp.max(l1, axis=1, keepdims=True)
    i1 = jnp.min(jnp.where(l1 == m1, lane_f, float(LANES)), axis=1, keepdims=True)
    l2 = jnp.where(lane_f == i1, -jnp.inf, l1)
    m2 = jnp.max(l2, axis=1, keepdims=True)
    i2 = jnp.min(jnp.where(l2 == m2, lane_f, float(LANES)), axis=1, keepdims=True)
    e = jnp.exp(m2 - m1)
    g1 = 1.0 / (1.0 + e)
    g2 = e / (1.0 + e)
    i_ref[...] = jnp.where(lane == 0, i1, jnp.where(lane == 1, i2, 0.0)).astype(jnp.int32)
    g_ref[...] = jnp.where(lane == 0, g1, jnp.where(lane == 1, g2, 0.0))


def _router(y, w_router, tm):
    m, d = y.shape
    wr = jnp.zeros((d, LANES), F32).at[:, :N_EXPERTS].set(w_router)
    row = pl.BlockSpec((tm, LANES), lambda i: (i, 0))
    return pl.pallas_call(
        _router_body,
        grid=(m // tm,),
        in_specs=[pl.BlockSpec((tm, d), lambda i: (i, 0)), pl.BlockSpec((d, LANES), lambda i: (0, 0))],
        out_specs=[row, row],
        out_shape=[jax.ShapeDtypeStruct((m, LANES), jnp.int32), jax.ShapeDtypeStruct((m, LANES), F32)],
        compiler_params=_params("parallel"),
        name="moe_router",
    )(y, wr)


def _combine_ln_body(x_ref, a_ref, b2_ref, tg_ref, g_ref, b_ref, op_ref, os_ref, *, n_first):
    i = pl.program_id(0)
    tg = tg_ref[...]
    moe = tg[:, 0:1] * a_ref[...] + tg[:, 1:2] * b2_ref[...]
    out = _layer_norm(ALPHA * x_ref[...] + moe, g_ref[...], b_ref[...])

    @pl.when(i < n_first)
    def _():
        op_ref[...] = out

    @pl.when(i >= n_first)
    def _():
        os_ref[...] = out


def _combine_ln(x, pairs, top_gate, g, b, m_first, tm):
    m, d = x.shape
    n_first = m_first // tm
    vec = pl.BlockSpec((1, d), lambda i: (0, 0))
    return pl.pallas_call(
        functools.partial(_combine_ln_body, n_first=n_first),
        grid=(m // tm,),
        in_specs=[pl.BlockSpec((tm, d), lambda i: (i, 0)),
                  pl.BlockSpec((None, tm, d), lambda i: (0, i, 0)),
                  pl.BlockSpec((None, tm, d), lambda i: (1, i, 0)),
                  pl.BlockSpec((tm, LANES), lambda i: (i, 0)),
                  vec, vec],
        out_specs=[pl.BlockSpec((tm, d), lambda i: (jnp.minimum(i, n_first - 1), 0)),
                   pl.BlockSpec((tm, d), lambda i: (jnp.maximum(i - n_first, 0), 0))],
        out_shape=[jax.ShapeDtypeStruct((m_first, d), F32), jax.ShapeDtypeStruct((m - m_first, d), F32)],
        compiler_params=_params("arbitrary"),
        name="moe_combine_ln",
    )(x, pairs, pairs, top_gate, g.reshape(1, d), b.reshape(1, d))


def _moe_routing(top_idx, tm):
    m = top_idx.shape[0]
    n_pairs = m * TOP_K
    n_tiles = n_pairs // tm + N_EXPERTS
    i32 = jnp.int32
    e_flat = jnp.concatenate([top_idx[:, k] for k in range(TOP_K)])
    order = jnp.argsort(e_flat, stable=True).astype(i32)
    inv = jnp.argsort(order).astype(i32)
    counts = jnp.sum((e_flat[None, :] == jnp.arange(N_EXPERTS, dtype=i32)[:, None]).astype(i32), axis=1)
    padded = ((counts + tm - 1) // tm) * tm
    ends_padded = jnp.cumsum(padded)
    starts_padded = ends_padded - padded
    starts = jnp.cumsum(counts) - counts
    tile_start = jnp.arange(n_tiles, dtype=i32) * tm
    tile_expert = jnp.minimum(jnp.searchsorted(ends_padded, tile_start, side="right"), N_EXPERTS - 1).astype(i32)
    tile_rows = jnp.clip(counts[tile_expert] - (tile_start - starts_padded[tile_expert]), 0, tm).astype(i32)
    slot = jnp.arange(n_tiles * tm, dtype=i32)
    slot_e = jnp.repeat(tile_expert, tm)
    rank = slot - starts_padded[slot_e]
    live = (rank >= 0) & (rank < counts[slot_e])
    src = order[jnp.clip(starts[slot_e] + rank, 0, n_pairs - 1)]
    slot_token = jnp.where(live, src % m, slot % m)
    pair_slot = starts_padded[e_flat] + inv - starts[e_flat]
    return tile_expert, tile_rows, slot_token, pair_slot


def _gelu_tanh(x):
    return 0.5 * x * (1.0 + jnp.tanh(math.sqrt(2.0 / math.pi) * (x + 0.044715 * (x * x * x))))


def _ssm_body(*refs, sequential):
    (u_ref, tz_ref, wb_ref, vc_ref, a8r_ref, a8i_ref, d_ref, h0r_ref, h0i_ref, y_ref, hr_ref, hi_ref) = refs[:12]
    scr = refs[12:]
    n_chunks = u_ref.shape[0] // SSM_CHUNK
    sw = a8r_ref.shape[1]
    step_rows = lambda s: pl.ds(s, n_chunks, stride=SSM_CHUNK)
    ucat = jnp.concatenate([u_ref[step_rows(s), :].astype(BF16) for s in range(SSM_CHUNK)], axis=1)
    hl = _dot(ucat, wb_ref[...].reshape(SSM_CHUNK * LANES, 2 * sw))
    hl_r = hl[:, :sw]
    hl_i = hl[:, sw:]
    a8r = a8r_ref[...]
    a8i = a8i_ref[...]
    if sequential:
        hlr_scr, hli_scr, hinr_scr, hini_scr = scr
        hlr_scr[...] = hl_r
        hli_scr[...] = hl_i

        def tile_step(kb, carry):
            hr, hi = carry
            base = pl.multiple_of(kb * SUBLANES, SUBLANES)
            tr = hlr_scr[pl.ds(base, SUBLANES), :]
            ti = hli_scr[pl.ds(base, SUBLANES), :]
            rows_r, rows_i = [], []
            for r in range(SUBLANES):
                rows_r.append(hr)
                rows_i.append(hi)
                hr, hi = (a8r * hr - a8i * hi + tr[r:r + 1], a8r * hi + a8i * hr + ti[r:r + 1])
            hinr_scr[pl.ds(base, SUBLANES), :] = jnp.concatenate(rows_r, axis=0)
            hini_scr[pl.ds(base, SUBLANES), :] = jnp.concatenate(rows_i, axis=0)
            return hr, hi

        hr, hi = lax.fori_loop(0, n_chunks // SUBLANES, tile_step, (h0r_ref[...], h0i_ref[...]))
        hr_ref[...] = hr
        hi_ref[...] = hi
        hin_r = hinr_scr[...]
        hin_i = hini_scr[...]
    else:
        gpb = h0r_ref.shape[1]
        p_n = h0r_ref.shape[2]
        hin_r = jnp.concatenate([h0r_ref[:, g, :] for g in range(gpb)], axis=1)
        hin_i = jnp.concatenate([h0i_ref[:, g, :] for g in range(gpb)], axis=1)
        hr = a8r * hin_r - a8i * hin_i + hl_r
        hi = a8r * hin_i + a8i * hin_r + hl_i
        for g in range(gpb):
            hr_ref[:, g, :] = hr[:, g * p_n:(g + 1) * p_n]
            hi_ref[:, g, :] = hi[:, g * p_n:(g + 1) * p_n]
    hcat = jnp.concatenate([hin_r.astype(BF16), hin_i.astype(BF16)], axis=1)
    d = d_ref[...]
    steps_per_dot = 2
    for t0 in range(0, SSM_CHUNK, steps_per_dot):
        cols = slice(t0 * LANES, (t0 + steps_per_dot) * LANES)
        ks = t0 + steps_per_dot
        k = ks * LANES
        y2 = (_dot(ucat[:, :k], tz_ref[:ks, :, cols].reshape(k, steps_per_dot * LANES))
              + _dot(hcat, vc_ref[:, :, cols].reshape(2 * sw, steps_per_dot * LANES)))
        for t in range(t0, t0 + steps_per_dot):
            y = y2[:, (t - t0) * LANES:(t - t0 + 1) * LANES]
            y_ref[step_rows(t), :] = _gelu_tanh(y + d * u_ref[step_rows(t), :])


GROUPS_PER_BLOCK = LANES // SSM_GROUP


def _spread_groups(x, g_axis):
    x = jnp.expand_dims(x, -2)
    shape = [1] * x.ndim
    shape[g_axis] = GROUPS_PER_BLOCK
    shape[-2] = GROUPS_PER_BLOCK
    eye = np.eye(GROUPS_PER_BLOCK, dtype=bool).reshape(shape)
    return jnp.where(jnp.asarray(eye), x, 0.0).astype(BF16)


def _ssm_operators(a_re, a_im, log_dt, b_re, b_im, c_re, c_im):
    g_n, p_n = a_re.shape
    lam_re = jnp.minimum(a_re, -1e-4)
    lam_im = a_im
    dt = jnp.exp(log_dt)[:, None]
    decay = jnp.exp(lam_re * dt)
    ar = decay * jnp.cos(lam_im * dt)
    ai = decay * jnp.sin(lam_im * dt)
    num_re = ar - 1.0
    den = lam_re * lam_re + lam_im * lam_im
    f_re = (num_re * lam_re + ai * lam_im) / den
    f_im = (ai * lam_re - num_re * lam_im) / den
    bt_re = jnp.swapaxes(b_re, 1, 2)
    bt_im = jnp.swapaxes(b_im, 1, 2)
    bb_re = f_re[:, None, :] * bt_re - f_im[:, None, :] * bt_im
    bb_im = f_re[:, None, :] * bt_im + f_im[:, None, :] * bt_re
    pw_re, pw_im = [jnp.ones_like(ar)], [jnp.zeros_like(ar)]
    for _ in range(SSM_CHUNK):
        pr, pi = pw_re[-1], pw_im[-1]
        pw_re.append(pr * ar - pi * ai)
        pw_im.append(pr * ai + pi * ar)
    pw_re = jnp.stack(pw_re)
    pw_im = jnp.stack(pw_im)
    pk_re = pw_re[:SSM_CHUNK, :, None, :]
    pk_im = pw_im[:SSM_CHUNK, :, None, :]
    ab_re = pk_re * bb_re - pk_im * bb_im
    ab_im = pk_re * bb_im + pk_im * bb_re
    conv = (jnp.einsum("gcp,tgdp->tgdc", c_re, ab_re, precision=HIGHEST)
            - jnp.einsum("gcp,tgdp->tgdc", c_im, ab_im, precision=HIGHEST))
    gpb = GROUPS_PER_BLOCK
    nb = g_n // gpb
    t_n = SSM_CHUNK
    tau = np.arange(t_n)[None, :] - np.arange(t_n)[:, None]
    conv_ext = jnp.concatenate([conv, jnp.zeros_like(conv[:1])], axis=0)
    tz = conv_ext[np.where(tau >= 0, tau, t_n)]
    tz = tz.reshape(t_n, t_n, nb, gpb, SSM_GROUP, SSM_GROUP).transpose(0, 2, 3, 4, 1, 5)
    tz = _spread_groups(tz, 2).reshape(t_n, nb, LANES, t_n * LANES)
    wb = jnp.concatenate(
        [_spread_groups(ab[::-1].reshape(t_n, nb, gpb, SSM_GROUP, p_n), 2).reshape(t_n, nb, LANES, gpb * p_n)
         for ab in (ab_re, ab_im)], axis=3)
    pr = pw_re[1:, :, :, None]
    pi = pw_im[1:, :, :, None]
    ct_re = jnp.swapaxes(c_re, 1, 2)[None]
    ct_im = jnp.swapaxes(c_im, 1, 2)[None]
    vc = jnp.stack(
        [_spread_groups(x.reshape(t_n, nb, gpb, p_n, SSM_GROUP).transpose(1, 2, 3, 0, 4), 1)
         .reshape(nb, gpb * p_n, t_n * LANES)
         for x in (ct_re * pr - ct_im * pi, -(ct_re * pi + ct_im * pr))], axis=0)
    a8r = pw_re[SSM_CHUNK].reshape(1, g_n * p_n)
    a8i = pw_im[SSM_CHUNK].reshape(1, g_n * p_n)
    return tz, wb, vc, a8r, a8i


def _ssm_scan(u, ops, d_skip, h0_re, h0_im, chunk0, n_chunks, sequential):
    tz, wb, vc, a8r, a8i = ops
    nb = tz.shape[1]
    sw = wb.shape[-1] // 2
    rows = n_chunks * SSM_CHUNK
    cblk = chunk0 // n_chunks
    op3 = lambda a: pl.BlockSpec((a.shape[0], None) + a.shape[2:], lambda j: (0, j, 0, 0))
    srow = pl.BlockSpec((1, sw), lambda j: (0, j))
    if sequential:
        hspec = srow
        scratch = [pltpu.VMEM((n_chunks, sw), F32)] * 4
    else:
        hspec = pl.BlockSpec((n_chunks, sw // SSM_STATE, SSM_STATE), lambda j: (0, j, 0))
        scratch = []
    return pl.pallas_call(
        functools.partial(_ssm_body, sequential=sequential),
        grid=(nb,),
        in_specs=[pl.BlockSpec((rows, LANES), lambda j: (cblk, j)),
                  op3(tz), op3(wb), op3(vc),
                  srow, srow, pl.BlockSpec((1, LANES), lambda j: (0, j)), hspec, hspec],
        out_specs=[pl.BlockSpec((rows, LANES), lambda j: (0, j)), hspec, hspec],
        out_shape=[jax.ShapeDtypeStruct((rows, u.shape[1]), F32),
                   jax.ShapeDtypeStruct(h0_re.shape, F32), jax.ShapeDtypeStruct(h0_im.shape, F32)],
        scratch_shapes=scratch,
        compiler_params=_params("parallel"),
        name="s5_scan_seq" if sequential else "s5_scan_step",
    )(u, tz, wb, vc, a8r, a8i, d_skip.reshape(1, -1), h0_re, h0_im)


def kernel(x_prompt, x_sample, cache_k, cache_v, state_ssm_re, state_ssm_im, page_table, rel_bias, ln_g, ln_b,
           w_qkv, w_o, w_ssm_in, ssm_a_re, ssm_a_im, ssm_log_dt, ssm_b_re, ssm_b_im, ssm_c_re, ssm_c_im, ssm_d,
           w_glu_v, w_glu_g, w_ff_gate, w_ff_up, w_ff_down, w_router, w_moe_gate, w_moe_up, w_moe_down):
    batch, seq, d = x_prompt.shape
    dec_batch, dec_seq, _ = x_sample.shape
    assert batch == 1 and d == D_MODEL and dec_seq == SSM_CHUNK and seq % MOBA_BLOCK == 0
    m_p = batch * seq
    m_s = dec_batch * dec_seq
    m = m_p + m_s
    dq = N_HEADS * HEAD_DIM
    dkv = N_KV_HEADS * HEAD_DIM
    n_pool, page = cache_k.shape[1], cache_k.shape[2]

    x_p = x_prompt.reshape(m_p, d)
    x_s = x_sample.reshape(m_s, d)

    qkv_f32, qkv_b = _qkv_proj(x_p, x_s, w_qkv[0].astype(BF16), tm=1024, tn=512)
    kmean = _block_means(qkv_f32, seq // MOBA_BLOCK)
    tabs, tab_s = _bias_tables(rel_bias)
    attn_p = _attn_prompt(qkv_f32, qkv_b, kmean, tabs, seq)
    ck = cache_k[0].reshape(n_pool, page * N_KV_HEADS, HEAD_DIM)
    cv = cache_v[0].reshape(n_pool, page * N_KV_HEADS, HEAD_DIM)
    attn_s = _attn_sample(page_table, qkv_f32, ck, cv, tab_s, m_p, dec_seq)
    y = _proj_ln(attn_p, attn_s, [w_o[0].astype(BF16)], [x_p, x_s], ln_g[0, 0], ln_b[0, 0], tm=512, tn=512)
    y = _ffn_dense(y, w_ff_gate, w_ff_up, w_ff_down, ln_g[0, 1], ln_b[0, 1], tm=1024, tf=256)

    k_all = qkv_f32[:, dq:dq + dkv]
    v_all = qkv_f32[:, dq + dkv:]
    k_prompt = k_all[:m_p].reshape(1, batch, seq, N_KV_HEADS, HEAD_DIM)
    v_prompt = v_all[:m_p].reshape(1, batch, seq, N_KV_HEADS, HEAD_DIM)
    k_sample = k_all[m_p:].reshape(1, dec_batch, dec_seq, N_KV_HEADS, HEAD_DIM)
    v_sample = v_all[m_p:].reshape(1, dec_batch, dec_seq, N_KV_HEADS, HEAD_DIM)

    u = _matmul(y, w_ssm_in[0].astype(BF16), tm=1024, tn=1024)
    ops = _ssm_operators(ssm_a_re[0], ssm_a_im[0], ssm_log_dt[0], ssm_b_re[0], ssm_b_im[0],
                         ssm_c_re[0], ssm_c_im[0])
    n_state = N_SSM_GROUPS * SSM_STATE
    zero = jnp.zeros((batch, n_state), F32)
    yg_p, hrp, hip = _ssm_scan(u, ops, ssm_d[0], zero, zero, 0, m_p // SSM_CHUNK, True)
    yg_s, hrs, his = _ssm_scan(u, ops, ssm_d[0], state_ssm_re[0], state_ssm_im[0], m_p // SSM_CHUNK, dec_batch,
                               False)
    y = _proj_ln(yg_p, yg_s, [w_glu_v[0].astype(BF16), w_glu_g[0].astype(BF16)],
                 [y], ln_g[1, 0], ln_b[1, 0], tm=512, tn=512)

    tm_moe = 1024
    top_idx, top_gate = _router(y, w_router[0], tm=512)
    tile_expert, tile_rows, slot_token, pair_slot = _moe_routing(top_idx, tm_moe)
    xs = y.at[slot_token].get(mode="promise_in_bounds")
    ys = _ffn_moe(tile_expert, tile_rows, xs, w_moe_gate[0], w_moe_up[0], w_moe_down[0], tm_moe, tf=256, half=256)
    pairs = ys.at[pair_slot].get(mode="promise_in_bounds").reshape(TOP_K, m, d)
    out_p, out_s = _combine_ln(y, pairs, top_gate, ln_g[1, 1], ln_b[1, 1], m_p, tm=512)

    return (out_p.reshape(batch, seq, d), out_s.reshape(dec_batch, dec_seq, d),
            k_prompt, v_prompt, k_sample, v_sample,
            hrp.reshape(1, batch, N_SSM_GROUPS, SSM_STATE), hip.reshape(1, batch, N_SSM_GROUPS, SSM_STATE),
            hrs.reshape(1, dec_batch, N_SSM_GROUPS, SSM_STATE), his.reshape(1, dec_batch, N_SSM_GROUPS, SSM_STATE))
```

```python
import functools
import math

import numpy as np
import jax
import jax.numpy as jnp
from jax import lax
from jax.experimental import pallas as pl
from jax.experimental.pallas import tpu as pltpu

D_MODEL = 2048
N_HEADS = 16
HEAD_DIM = D_MODEL // N_HEADS
N_KV_HEADS = 4
HEADS_PER_KV = N_HEADS // N_KV_HEADS
MOBA_BLOCK = 256
MOBA_TOPK = 3
NUM_BUCKETS = 32
MAX_DISTANCE = 128
SSM_GROUP = 16
N_SSM_GROUPS = D_MODEL // SSM_GROUP
SSM_STATE = 64
SSM_CHUNK = 8
D_FF = 7 * D_MODEL // 2
N_EXPERTS = 8
TOP_K = 2
DEPTH = 2
ALPHA = (2 * DEPTH) ** 0.25
LN_EPS = 1e-5
NEG_INF = -1e30

LANES = 128
SUBLANES = 8
VMEM_LIMIT = 60 * 1024 * 1024

F32 = jnp.float32
BF16 = jnp.bfloat16
HIGHEST = lax.Precision.HIGHEST
_NT = (((1,), (1,)), ((), ()))


def _params(*sem):
    return pltpu.CompilerParams(dimension_semantics=sem, vmem_limit_bytes=VMEM_LIMIT)


def _dot(a, b):
    return jnp.dot(a, b, preferred_element_type=F32)


def _sigmoid(x):
    return 1.0 / (1.0 + jnp.exp(-x))


def _layer_norm(y, g, b):
    mean = jnp.mean(y, axis=-1, keepdims=True)
    yc = y - mean
    var = jnp.mean(yc * yc, axis=-1, keepdims=True)
    return yc * lax.rsqrt(var + LN_EPS) * g + b


def _mm_body(x_ref, w_ref, o_ref, xb_scr):
    @pl.when(pl.program_id(1) == 0)
    def _():
        xb_scr[...] = x_ref[...].astype(BF16)

    o_ref[...] = _dot(xb_scr[...], w_ref[...])


def _matmul(x, w, tm, tn):
    m, k = x.shape
    n = w.shape[1]
    return pl.pallas_call(
        _mm_body,
        grid=(m // tm, n // tn),
        in_specs=[pl.BlockSpec((tm, k), lambda i, j: (i, 0)),
                  pl.BlockSpec((k, tn), lambda i, j: (0, j))],
        out_specs=pl.BlockSpec((tm, tn), lambda i, j: (i, j)),
        out_shape=jax.ShapeDtypeStruct((m, n), F32),
        scratch_shapes=[pltpu.VMEM((tm, k), BF16)],
        compiler_params=_params("parallel", "arbitrary"),
        name="matmul",
    )(x, w)


def _split_rows(block, n_first):
    return [pl.BlockSpec(block, lambda i, j: (jnp.minimum(i, n_first - 1), 0)),
            pl.BlockSpec(block, lambda i, j: (jnp.maximum(i - n_first, 0), 0))]


def _load_split_bf16(dst, src1, src2, n_first):
    i = pl.program_id(0)
    j = pl.program_id(1)

    @pl.when((j == 0) & (i < n_first))
    def _():
        dst[...] = src1[...].astype(BF16)

    @pl.when((j == 0) & (i >= n_first))
    def _():
        dst[...] = src2[...].astype(BF16)


def _qkv_body(x1_ref, x2_ref, w_ref, of_ref, ob_ref, xb_scr, *, nq_tiles, scale, n_first):
    _load_split_bf16(xb_scr, x1_ref, x2_ref, n_first)
    acc = _dot(xb_scr[...], w_ref[...])
    of_ref[...] = acc
    s = jnp.where(pl.program_id(1) < nq_tiles, scale, 1.0).astype(F32)
    ob_ref[...] = (acc * s).astype(BF16)


def _qkv_proj(x1, x2, w, tm, tn):
    k = x1.shape[1]
    m = x1.shape[0] + x2.shape[0]
    n = w.shape[1]
    n_first = x1.shape[0] // tm
    assert x1.shape[0] % tm == 0 and x2.shape[0] % tm == 0
    body = functools.partial(_qkv_body, nq_tiles=(N_HEADS * HEAD_DIM) // tn, scale=HEAD_DIM ** -0.5,
                             n_first=n_first)
    return pl.pallas_call(
        body,
        grid=(m // tm, n // tn),
        in_specs=_split_rows((tm, k), n_first) + [pl.BlockSpec((k, tn), lambda i, j: (0, j))],
        out_specs=[pl.BlockSpec((tm, tn), lambda i, j: (i, j)),
                   pl.BlockSpec((tm, tn), lambda i, j: (i, j))],
        out_shape=[jax.ShapeDtypeStruct((m, n), F32), jax.ShapeDtypeStruct((m, n), BF16)],
        scratch_shapes=[pltpu.VMEM((tm, k), BF16)],
        compiler_params=_params("parallel", "arbitrary"),
        name="qkv_proj",
    )(x1, x2, w)


def _kmean_body(k_ref, o_ref):
    i = pl.program_id(0)

    @pl.when(i == 0)
    def _():
        o_ref[...] = jnp.zeros_like(o_ref)

    mean = jnp.sum(k_ref[...], axis=0, keepdims=True) * (1.0 / MOBA_BLOCK)
    rows = lax.broadcasted_iota(jnp.int32, o_ref.shape, 0)
    o_ref[...] = jnp.where(rows == i, mean, o_ref[...])


def _block_means(qkv_f32, n_blocks):
    dkv = N_KV_HEADS * HEAD_DIM
    kcol = (N_HEADS * HEAD_DIM) // dkv
    return pl.pallas_call(
        _kmean_body,
        grid=(n_blocks,),
        in_specs=[pl.BlockSpec((MOBA_BLOCK, dkv), lambda i: (i, kcol))],
        out_specs=pl.BlockSpec((LANES, dkv), lambda i: (0, 0)),
        out_shape=jax.ShapeDtypeStruct((LANES, dkv), F32),
        compiler_params=_params("arbitrary"),
        name="moba_block_means",
    )(qkv_f32)


def _t5_bucket_np(dist):
    n = np.maximum(dist, 0)
    max_exact = NUM_BUCKETS // 2
    nf = np.maximum(n, 1).astype(np.float32)
    large = max_exact + (np.log(nf / np.float32(max_exact)) / np.float32(math.log(MAX_DISTANCE / max_exact))
                         * np.float32(NUM_BUCKETS - max_exact)).astype(np.int32)
    large = np.minimum(large, NUM_BUCKETS - 1)
    return np.where(n < max_exact, n, large).astype(np.int32)


def _bucket_tables():
    q = np.arange(MOBA_BLOCK)[:, None]
    k = np.arange(MOBA_BLOCK)[None, :]
    own = np.where(q - k >= 0, _t5_bucket_np(q - k), -1)
    prev = _t5_bucket_np(q - k + MOBA_BLOCK)
    return np.stack([own, prev]).astype(np.int32)


def _bias_body(rb_ref, idx_ref, o_ref, os_ref):
    h = pl.program_id(0)
    far = rb_ref[h, NUM_BUCKETS - 1]
    tabs = []
    for m in range(2):
        idx = idx_ref[m]
        acc = jnp.full(idx.shape, NEG_INF, F32)
        for b in range(NUM_BUCKETS):
            acc = jnp.where(idx == b, rb_ref[h, b] - far, acc)
        o_ref[0, m] = acc
        tabs.append(acc)
    os_ref[0] = jnp.concatenate([tabs[1][:SUBLANES, :], tabs[0][:SUBLANES, :LANES]], axis=1)


def _bias_tables(rel_bias):
    idx = jnp.asarray(_bucket_tables())
    return pl.pallas_call(
        _bias_body,
        grid=(N_HEADS,),
        in_specs=[pl.BlockSpec(memory_space=pltpu.SMEM),
                  pl.BlockSpec((2, MOBA_BLOCK, MOBA_BLOCK), lambda h: (0, 0, 0))],
        out_specs=[pl.BlockSpec((1, 2, MOBA_BLOCK, MOBA_BLOCK), lambda h: (h, 0, 0, 0)),
                   pl.BlockSpec((1, SUBLANES, MOBA_BLOCK + LANES), lambda h: (h, 0, 0))],
        out_shape=[jax.ShapeDtypeStruct((N_HEADS, 2, MOBA_BLOCK, MOBA_BLOCK), F32),
                   jax.ShapeDtypeStruct((N_HEADS, SUBLANES, MOBA_BLOCK + LANES), F32)],
        compiler_params=_params("parallel"),
        name="moba_bias_tables",
    )(rel_bias, idx)


def _select_blocks(gate, n_past):
    lane = lax.broadcasted_iota(jnp.int32, gate.shape, 1)
    lane_f = lane.astype(F32)
    g = jnp.where(lane < n_past, gate, NEG_INF)
    sel = jnp.full(gate.shape, NEG_INF, F32)
    for _ in range(MOBA_TOPK):
        mx = jnp.max(g, axis=1, keepdims=True)
        idx = jnp.min(jnp.where(g == mx, lane_f, float(gate.shape[1])), axis=1, keepdims=True)
        pick = lane_f == idx
        sel = jnp.where(pick, 0.0, sel)
        g = jnp.where(pick, -jnp.inf, g)
    return jnp.where(lane < n_past, sel, 0.0)


def _attn_prompt_body(qf_ref, qb_ref, k_ref, v_ref, km_ref, tab_ref, o_ref, qa_scr, m_scr, l_scr, acc_scr):
    i = pl.program_id(1)
    rows = HEADS_PER_KV * MOBA_BLOCK
    km = km_ref[...]
    lane = lax.broadcasted_iota(jnp.int32, (MOBA_BLOCK, LANES), 1)
    dummy = LANES - 1
    for hh in range(HEADS_PER_KV):
        cs = slice(hh * HEAD_DIM, (hh + 1) * HEAD_DIM)
        rs = slice(hh * MOBA_BLOCK, (hh + 1) * MOBA_BLOCK)
        gate = lax.dot_general(qf_ref[:, cs], km, _NT, precision=HIGHEST, preferred_element_type=F32)
        mask = jnp.where(lane == dummy, NEG_INF, _select_blocks(gate, i))
        qa_scr[rs, :HEAD_DIM] = qb_ref[:, cs]
        qa_scr[rs, HEAD_DIM:] = mask.astype(BF16)
    qa = qa_scr[...]

    def scores(j, mask_lane):
        start = pl.multiple_of(j * MOBA_BLOCK, MOBA_BLOCK)
        kj = k_ref[pl.ds(start, MOBA_BLOCK), :]
        vj = v_ref[pl.ds(start, MOBA_BLOCK), :]
        rhs = jnp.concatenate([kj, (lane == mask_lane).astype(BF16)], axis=1)
        return lax.dot_general(qa, rhs, _NT, preferred_element_type=F32), vj

    def softmax_parts(parts, m_new):
        m2 = jnp.concatenate([m_new, m_new], axis=1)
        l_add = None
        acc_add = None
        for s, vj in parts:
            p = jnp.exp(s - m2)
            ls = jnp.sum(p, axis=1, keepdims=True)
            pv = _dot(p.astype(BF16), vj)
            l_add = ls if l_add is None else l_add + ls
            acc_add = pv if acc_add is None else acc_add + pv
        return l_add, acc_add

    has_prev = i >= 1
    s_own, v_own = scores(i, -1)
    s_own = s_own + tab_ref[:, 0].reshape(rows, MOBA_BLOCK)
    s_prev, v_prev = scores(jnp.maximum(i - 1, 0), jnp.where(has_prev, i - 1, dummy))
    s_prev = s_prev + tab_ref[:, 1].reshape(rows, MOBA_BLOCK)
    m0 = jnp.maximum(jnp.max(s_own, axis=1, keepdims=True), jnp.max(s_prev, axis=1, keepdims=True))
    m0 = jnp.broadcast_to(m0, (rows, LANES))
    l0, acc0 = softmax_parts([(s_own, v_own), (s_prev, v_prev)], m0)
    m_scr[...] = m0
    l_scr[...] = jnp.broadcast_to(l0, (rows, LANES))
    acc_scr[...] = acc0

    n_far = jnp.maximum(i - 1, 0)

    def far_pair(t, carry):
        j0 = 2 * t
        j1 = j0 + 1
        ok1 = j1 < n_far
        parts = [scores(j0, j0), scores(jnp.where(ok1, j1, 0), jnp.where(ok1, j1, dummy))]
        m_prev = m_scr[...]
        m_new = m_prev
        for s, _ in parts:
            m_new = jnp.maximum(m_new, jnp.max(s, axis=1, keepdims=True))
        alpha = jnp.exp(m_prev - m_new)
        l_add, acc_add = softmax_parts(parts, m_new)
        l_scr[...] = alpha * l_scr[...] + l_add
        acc_scr[...] = alpha * acc_scr[...] + acc_add
        m_scr[...] = m_new
        return carry

    lax.fori_loop(0, (n_far + 1) // 2, far_pair, 0)

    out = acc_scr[...] / l_scr[...]
    for hh in range(HEADS_PER_KV):
        o_ref[:, hh * HEAD_DIM:(hh + 1) * HEAD_DIM] = out[hh * MOBA_BLOCK:(hh + 1) * MOBA_BLOCK]


def _attn_prompt(qkv_f32, qkv_b, kmean, tabs, seq):
    nblk = seq // MOBA_BLOCK
    assert nblk < LANES - 1
    gw = HEADS_PER_KV * HEAD_DIM
    kcol = (N_HEADS * HEAD_DIM) // HEAD_DIM
    vcol = kcol + N_KV_HEADS
    rows = HEADS_PER_KV * MOBA_BLOCK
    return pl.pallas_call(
        _attn_prompt_body,
        grid=(N_KV_HEADS, nblk),
        in_specs=[pl.BlockSpec((MOBA_BLOCK, gw), lambda g, i: (i, g)),
                  pl.BlockSpec((MOBA_BLOCK, gw), lambda g, i: (i, g)),
                  pl.BlockSpec((seq, HEAD_DIM), lambda g, i: (0, kcol + g)),
                  pl.BlockSpec((seq, HEAD_DIM), lambda g, i: (0, vcol + g)),
                  pl.BlockSpec((LANES, HEAD_DIM), lambda g, i: (0, g)),
                  pl.BlockSpec((HEADS_PER_KV, 2, MOBA_BLOCK, MOBA_BLOCK), lambda g, i: (g, 0, 0, 0))],
        out_specs=pl.BlockSpec((MOBA_BLOCK, gw), lambda g, i: (i, g)),
        out_shape=jax.ShapeDtypeStruct((seq, N_HEADS * HEAD_DIM), F32),
        scratch_shapes=[pltpu.VMEM((rows, 2 * HEAD_DIM), BF16),
                        pltpu.VMEM((rows, LANES), F32),
                        pltpu.VMEM((rows, LANES), F32),
                        pltpu.VMEM((rows, HEAD_DIM), F32)],
        compiler_params=_params("parallel", "arbitrary"),
        name="moba_attn_prompt",
    )(qkv_f32, qkv_b, qkv_b, qkv_b, kmean, tabs)


def _attn_sample_body(pt_ref, *refs, n_pages, page, dec_seq):
    del pt_ref
    qkv_ref = refs[0]
    kp = refs[1:1 + n_pages]
    vp = refs[1 + n_pages:1 + 2 * n_pages]
    tab_ref = refs[1 + 2 * n_pages]
    o_ref = refs[2 + 2 * n_pages]
    kall, vall, expand = refs[3 + 2 * n_pages:]
    past = n_pages * page
    n_past_blocks = past // MOBA_BLOCK
    pages_per_block = MOBA_BLOCK // page
    near0 = past - MOBA_BLOCK
    total = past + LANES
    dq = N_HEADS * HEAD_DIM
    dkv = N_KV_HEADS * HEAD_DIM
    hq = HEADS_PER_KV * dec_seq
    rows = N_HEADS * dec_seq
    pad = jnp.zeros((LANES - dec_seq, HEAD_DIM), F32)

    @pl.when(pl.program_id(0) == 0)
    def _():
        key_blk = lax.broadcasted_iota(jnp.int32, (LANES, total), 1) // MOBA_BLOCK
        expand[...] = (key_blk == lax.broadcasted_iota(jnp.int32, (LANES, total), 0)).astype(BF16)

    kmeans = []
    for g in range(N_KV_HEADS):
        means = []
        for blk in range(n_past_blocks):
            tot = jnp.zeros((1, HEAD_DIM), F32)
            for pp in range(pages_per_block):
                pg = blk * pages_per_block + pp
                kk = kp[pg][pl.ds(g, page, stride=N_KV_HEADS), :]
                vv = vp[pg][pl.ds(g, page, stride=N_KV_HEADS), :]
                tot = tot + jnp.sum(kk, axis=0, keepdims=True)
                kall[g, pg * page:(pg + 1) * page, :] = kk.astype(BF16)
                vall[g, pg * page:(pg + 1) * page, :] = vv.astype(BF16)
            means.append(tot * (1.0 / MOBA_BLOCK))
        kmeans.append(jnp.concatenate(means + [jnp.zeros((LANES - n_past_blocks, HEAD_DIM), F32)], axis=0))
        knew = qkv_ref[:, dq + g * HEAD_DIM:dq + (g + 1) * HEAD_DIM]
        vnew = qkv_ref[:, dq + dkv + g * HEAD_DIM:dq + dkv + (g + 1) * HEAD_DIM]
        kall[g, past:total, :] = jnp.concatenate([knew, pad], axis=0).astype(BF16)
        vall[g, past:total, :] = jnp.concatenate([vnew, pad], axis=0).astype(BF16)

    qs = jnp.concatenate([qkv_ref[:, h * HEAD_DIM:(h + 1) * HEAD_DIM] for h in range(N_HEADS)], axis=0)
    gate_all = lax.dot_general(qs, jnp.concatenate(kmeans, axis=0), _NT, precision=HIGHEST,
                               preferred_element_type=F32)
    row_g = lax.broadcasted_iota(jnp.int32, (rows, LANES), 0) // hq
    gate = gate_all[:, :LANES]
    for g in range(1, N_KV_HEADS):
        gate = jnp.where(row_g == g, gate_all[:, g * LANES:(g + 1) * LANES], gate)
    selm = _select_blocks(gate, n_past_blocks).astype(BF16)

    qb = (qs * (HEAD_DIM ** -0.5)).astype(BF16)
    s = jnp.concatenate([lax.dot_general(qb[g * hq:(g + 1) * hq], kall[g], _NT, preferred_element_type=F32)
                         for g in range(N_KV_HEADS)], axis=0)
    s = s + _dot(selm, expand[...])
    s_far = s[:, :near0]
    s_near = s[:, near0:] + tab_ref[...].reshape(rows, MOBA_BLOCK + LANES)
    m = jnp.maximum(jnp.max(s_far, axis=1, keepdims=True), jnp.max(s_near, axis=1, keepdims=True))
    p_far = jnp.exp(s_far - m)
    p_near = jnp.exp(s_near - m)
    inv_l = 1.0 / (jnp.sum(p_far, axis=1, keepdims=True) + jnp.sum(p_near, axis=1, keepdims=True))
    p_far = p_far.astype(BF16)
    p_near = p_near.astype(BF16)
    for g in range(N_KV_HEADS):
        rs = slice(g * hq, (g + 1) * hq)
        out = (_dot(p_far[rs], vall[g, :near0, :]) + _dot(p_near[rs], vall[g, near0:, :])) * inv_l[rs]
        for hh in range(HEADS_PER_KV):
            c0 = (g * HEADS_PER_KV + hh) * HEAD_DIM
            o_ref[:, c0:c0 + HEAD_DIM] = out[hh * dec_seq:(hh + 1) * dec_seq]


def _attn_sample(page_table, qkv_f32, cache_k, cache_v, tab_s, row0, dec_seq):
    dec_batch, n_pages = page_table.shape
    page = cache_k.shape[1] // N_KV_HEADS
    total = n_pages * page + LANES
    blk0 = row0 // dec_seq

    def page_spec(p):
        return pl.BlockSpec((None, page * N_KV_HEADS, HEAD_DIM), lambda b, pt, p=p: (pt[b, p], 0, 0))

    body = functools.partial(_attn_sample_body, n_pages=n_pages, page=page, dec_seq=dec_seq)
    grid_spec = pltpu.PrefetchScalarGridSpec(
        num_scalar_prefetch=1,
        grid=(dec_batch,),
        in_specs=([pl.BlockSpec((dec_seq, qkv_f32.shape[1]), lambda b, pt: (blk0 + b, 0))]
                  + [page_spec(p) for p in range(n_pages)]
                  + [page_spec(p) for p in range(n_pages)]
                  + [pl.BlockSpec(tab_s.shape, lambda b, pt: (0, 0, 0))]),
        out_specs=pl.BlockSpec((dec_seq, N_HEADS * HEAD_DIM), lambda b, pt: (b, 0)),
        scratch_shapes=[pltpu.VMEM((N_KV_HEADS, total, HEAD_DIM), BF16),
                        pltpu.VMEM((N_KV_HEADS, total, HEAD_DIM), BF16),
                        pltpu.VMEM((LANES, total), BF16)],
    )
    return pl.pallas_call(
        body,
        grid_spec=grid_spec,
        out_shape=jax.ShapeDtypeStruct((dec_batch * dec_seq, N_HEADS * HEAD_DIM), F32),
        compiler_params=_params("arbitrary"),
        name="moba_attn_sample",
    )(page_table, qkv_f32, *([cache_k] * n_pages), *([cache_v] * n_pages), tab_s)


def _proj_ln_body(*refs, n_w, n_x, n_tiles, n_first):
    a1_ref, a2_ref = refs[:2]
    w_refs = refs[2:2 + n_w]
    x_refs = refs[2 + n_w:2 + n_w + n_x]
    g_ref, b_ref, o_ref, a_scr, z_scr = refs[2 + n_w + n_x:]
    i = pl.program_id(0)
    j = pl.program_id(1)
    _load_split_bf16(a_scr, a1_ref, a2_ref, n_first)
    a = a_scr[...]
    z = _dot(a, w_refs[0][...])
    if n_w == 2:
        z = z * _sigmoid(_dot(a, w_refs[1][...]))
    z_scr[j] = z

    @pl.when(j == n_tiles - 1)
    def _():
        zfull = jnp.concatenate([z_scr[t] for t in range(n_tiles)], axis=1)
        x = x_refs[0][...] if n_x == 1 else jnp.where(i < n_first, x_refs[0][...], x_refs[1][...])
        o_ref[...] = _layer_norm(ALPHA * x + zfull, g_ref[...], b_ref[...])


def _proj_ln(a1, a2, ws, xs, g, b, tm, tn):
    k = a1.shape[1]
    m = a1.shape[0] + a2.shape[0]
    n = ws[0].shape[1]
    n_first = a1.shape[0] // tm
    assert a1.shape[0] % tm == 0 and a2.shape[0] % tm == 0
    assert len(xs) == 1 or xs[0].shape[0] == a1.shape[0]
    n_tiles = n // tn
    body = functools.partial(_proj_ln_body, n_w=len(ws), n_x=len(xs), n_tiles=n_tiles, n_first=n_first)
    row = pl.BlockSpec((tm, n), lambda i, j: (i, 0))
    vec = pl.BlockSpec((1, n), lambda i, j: (0, 0))
    return pl.pallas_call(
        body,
        grid=(m // tm, n_tiles),
        in_specs=(_split_rows((tm, k), n_first)
                  + [pl.BlockSpec((k, tn), lambda i, j: (0, j)) for _ in ws]
                  + ([row] if len(xs) == 1 else _split_rows((tm, n), n_first))
                  + [vec, vec]),
        out_specs=row,
        out_shape=jax.ShapeDtypeStruct((m, n), F32),
        scratch_shapes=[pltpu.VMEM((tm, k), BF16), pltpu.VMEM((n_tiles, tm, tn), F32)],
        compiler_params=_params("parallel", "arbitrary"),
        name="glu_res_ln" if len(ws) == 2 else "proj_res_ln",
    )(a1, a2, *ws, *xs, g.reshape(1, n), b.reshape(1, n))


def _ffn_body(te_ref, tr_ref, *refs, n_f, dense, half):
    del te_ref
    if dense:
        x_ref, wg_ref, wu_ref, wd_ref, g_ref, b_ref, o_ref, xb = refs[:8]
    else:
        x_ref, wg_ref, wu_ref, wd_ref, o_ref, xb = refs[:6]
    tm = x_ref.shape[0]
    t = pl.program_id(0)
    f = pl.program_id(1)
    n_rows = tr_ref[t]

    @pl.when(f == 0)
    def _():
        xb[...] = x_ref[...].astype(BF16)
        o_ref[...] = jnp.zeros_like(o_ref)

    def swiglu_rows(n):
        x = xb[:n, :]
        hg = _dot(x, wg_ref[...].astype(BF16))
        hu = _dot(x, wu_ref[...].astype(BF16))
        h = (hg * _sigmoid(hg)) * hu
        o_ref[:n, :] += _dot(h.astype(BF16), wd_ref[...].astype(BF16))

    for n in range(half, tm + 1, half):
        pl.when((n_rows > n - half) & (n_rows <= n))(functools.partial(swiglu_rows, n))

    if dense:
        @pl.when(f == n_f - 1)
        def _():
            o_ref[...] = _layer_norm(ALPHA * x_ref[...] + o_ref[...], g_ref[...], b_ref[...])


def _ffn_call(tile_expert, tile_rows, x, wg, wu, wd, extra, extra_specs, tm, tf, half, dense, name):
    s, d = x.shape
    n_t = s // tm
    n_f = wg.shape[-1] // tf
    live = lambda t, tr: jnp.minimum(tr[t], 1)
    x_spec = pl.BlockSpec((tm, d), lambda t, f, te, tr: (t, 0))
    wgu = pl.BlockSpec((None, d, tf), lambda t, f, te, tr: (te[t], 0, f * live(t, tr)))
    wds = pl.BlockSpec((None, tf, d), lambda t, f, te, tr: (te[t], f * live(t, tr), 0))
    grid_spec = pltpu.PrefetchScalarGridSpec(
        num_scalar_prefetch=2, grid=(n_t, n_f),
        in_specs=[x_spec, wgu, wgu, wds] + extra_specs,
        out_specs=x_spec,
        scratch_shapes=[pltpu.VMEM((tm, d), BF16)])
    return pl.pallas_call(
        functools.partial(_ffn_body, n_f=n_f, dense=dense, half=half),
        grid_spec=grid_spec,
        out_shape=jax.ShapeDtypeStruct((s, d), F32),
        compiler_params=_params("parallel", "arbitrary"),
        name=name,
    )(tile_expert, tile_rows, x, wg, wu, wd, *extra)


def _ffn_dense(x, wg, wu, wd, g, b, tm, tf):
    m, d = x.shape
    n_t = m // tm
    vec = pl.BlockSpec((1, d), lambda t, f, te, tr: (0, 0))
    return _ffn_call(jnp.zeros((n_t,), jnp.int32), jnp.full((n_t,), tm, jnp.int32), x, wg, wu, wd,
                     [g.reshape(1, d), b.reshape(1, d)], [vec, vec], tm, tf, tm, True, "ffn_res_ln")


def _ffn_moe(tile_expert, tile_rows, xs, wg, wu, wd, tm, tf, half):
    return _ffn_call(tile_expert, tile_rows, xs, wg, wu, wd, [], [], tm, tf, half, False, "moe_ffn")


def _router_body(y_ref, w_ref, i_ref, g_ref):
    logits = jnp.dot(y_ref[...], w_ref[...], precision=HIGHEST, preferred_element_type=F32)
    lane = lax.broadcasted_iota(jnp.int32, logits.shape, 1)
    lane_f = lane.astype(F32)
    l1 = jnp.where(lane < N_EXPERTS, logits, -jnp.inf)
    m1 = jnp.max(l1, axis=1, keepdims=True)
    i1 = jnp.min(jnp.where(l1 == m1, lane_f, float(LANES)), axis=1, keepdims=True)
    l2 = jnp.where(lane_f == i1, -jnp.inf, l1)
    m2 = jnp.max(l2, axis=1, keepdims=True)
    i2 = jnp.min(jnp.where(l2 == m2, lane_f, float(LANES)), axis=1, keepdims=True)
    e = jnp.exp(m2 - m1)
    g1 = 1.0 / (1.0 + e)
    g2 = e / (1.0 + e)
    i_ref[...] = jnp.where(lane == 0, i1, jnp.where(lane == 1, i2, 0.0)).astype(jnp.int32)
    g_ref[...] = jnp.where(lane == 0, g1, jnp.where(lane == 1, g2, 0.0))


def _router(y, w_router, tm):
    m, d = y.shape
    wr = jnp.zeros((d, LANES), F32).at[:, :N_EXPERTS].set(w_router)
    row = pl.BlockSpec((tm, LANES), lambda i: (i, 0))
    return pl.pallas_call(
        _router_body,
        grid=(m // tm,),
        in_specs=[pl.BlockSpec((tm, d), lambda i: (i, 0)), pl.BlockSpec((d, LANES), lambda i: (0, 0))],
        out_specs=[row, row],
        out_shape=[jax.ShapeDtypeStruct((m, LANES), jnp.int32), jax.ShapeDtypeStruct((m, LANES), F32)],
        compiler_params=_params("parallel"),
        name="moe_router",
    )(y, wr)


def _combine_ln_body(x_ref, a_ref, b2_ref, tg_ref, g_ref, b_ref, op_ref, os_ref, *, n_first):
    i = pl.program_id(0)
    tg = tg_ref[...]
    moe = tg[:, 0:1] * a_ref[...] + tg[:, 1:2] * b2_ref[...]
    out = _layer_norm(ALPHA * x_ref[...] + moe, g_ref[...], b_ref[...])

    @pl.when(i < n_first)
    def _():
        op_ref[...] = out

    @pl.when(i >= n_first)
    def _():
        os_ref[...] = out


def _combine_ln(x, pairs, top_gate, g, b, m_first, tm):
    m, d = x.shape
    n_first = m_first // tm
    vec = pl.BlockSpec((1, d), lambda i: (0, 0))
    return pl.pallas_call(
        functools.partial(_combine_ln_body, n_first=n_first),
        grid=(m // tm,),
        in_specs=[pl.BlockSpec((tm, d), lambda i: (i, 0)),
                  pl.BlockSpec((None, tm, d), lambda i: (0, i, 0)),
                  pl.BlockSpec((None, tm, d), lambda i: (1, i, 0)),
                  pl.BlockSpec((tm, LANES), lambda i: (i, 0)),
                  vec, vec],
        out_specs=[pl.BlockSpec((tm, d), lambda i: (jnp.minimum(i, n_first - 1), 0)),
                   pl.BlockSpec((tm, d), lambda i: (jnp.maximum(i - n_first, 0), 0))],
        out_shape=[jax.ShapeDtypeStruct((m_first, d), F32), jax.ShapeDtypeStruct((m - m_first, d), F32)],
        compiler_params=_params("arbitrary"),
        name="moe_combine_ln",
    )(x, pairs, pairs, top_gate, g.reshape(1, d), b.reshape(1, d))


def _moe_routing(top_idx, tm):
    m = top_idx.shape[0]
    n_pairs = m * TOP_K
    n_tiles = n_pairs // tm + N_EXPERTS
    i32 = jnp.int32
    e_flat = jnp.concatenate([top_idx[:, k] for k in range(TOP_K)])
    order = jnp.argsort(e_flat, stable=True).astype(i32)
    inv = jnp.argsort(order).astype(i32)
    counts = jnp.sum((e_flat[None, :] == jnp.arange(N_EXPERTS, dtype=i32)[:, None]).astype(i32), axis=1)
    padded = ((counts + tm - 1) // tm) * tm
    ends_padded = jnp.cumsum(padded)
    starts_padded = ends_padded - padded
    starts = jnp.cumsum(counts) - counts
    tile_start = jnp.arange(n_tiles, dtype=i32) * tm
    tile_expert = jnp.minimum(jnp.searchsorted(ends_padded, tile_start, side="right"), N_EXPERTS - 1).astype(i32)
    tile_rows = jnp.clip(counts[tile_expert] - (tile_start - starts_padded[tile_expert]), 0, tm).astype(i32)
    slot = jnp.arange(n_tiles * tm, dtype=i32)
    slot_e = jnp.repeat(tile_expert, tm)
    rank = slot - starts_padded[slot_e]
    live = (rank >= 0) & (rank < counts[slot_e])
    src = order[jnp.clip(starts[slot_e] + rank, 0, n_pairs - 1)]
    slot_token = jnp.where(live, src % m, slot % m)
    pair_slot = starts_padded[e_flat] + inv - starts[e_flat]
    return tile_expert, tile_rows, slot_token, pair_slot


def _gelu_tanh(x):
    return 0.5 * x * (1.0 + jnp.tanh(math.sqrt(2.0 / math.pi) * (x + 0.044715 * (x * x * x))))


def _ssm_body(*refs, sequential):
    (u_ref, tz_ref, wb_ref, vc_ref, a8r_ref, a8i_ref, d_ref, h0r_ref, h0i_ref, y_ref, hr_ref, hi_ref) = refs[:12]
    scr = refs[12:]
    n_chunks = u_ref.shape[0] // SSM_CHUNK
    sw = a8r_ref.shape[1]
    step_rows = lambda s: pl.ds(s, n_chunks, stride=SSM_CHUNK)
    ucat = jnp.concatenate([u_ref[step_rows(s), :].astype(BF16) for s in range(SSM_CHUNK)], axis=1)
    hl = _dot(ucat, wb_ref[...])
    hl_r = hl[:, :sw]
    hl_i = hl[:, sw:]
    a8r = a8r_ref[...]
    a8i = a8i_ref[...]
    if sequential:
        hlr_scr, hli_scr, hinr_scr, hini_scr = scr
        hlr_scr[...] = hl_r
        hli_scr[...] = hl_i

        def tile_step(kb, carry):
            hr, hi = carry
            base = pl.multiple_of(kb * SUBLANES, SUBLANES)
            tr = hlr_scr[pl.ds(base, SUBLANES), :]
            ti = hli_scr[pl.ds(base, SUBLANES), :]
            rows_r, rows_i = [], []
            for r in range(SUBLANES):
                rows_r.append(hr)
                rows_i.append(hi)
                hr, hi = (a8r * hr - a8i * hi + tr[r:r + 1], a8r * hi + a8i * hr + ti[r:r + 1])
            hinr_scr[pl.ds(base, SUBLANES), :] = jnp.concatenate(rows_r, axis=0)
            hini_scr[pl.ds(base, SUBLANES), :] = jnp.concatenate(rows_i, axis=0)
            return hr, hi

        hr, hi = lax.fori_loop(0, n_chunks // SUBLANES, tile_step, (h0r_ref[...], h0i_ref[...]))
        hr_ref[...] = hr
        hi_ref[...] = hi
        hin_r = hinr_scr[...]
        hin_i = hini_scr[...]
    else:
        gpb = h0r_ref.shape[1]
        p_n = h0r_ref.shape[2]
        hin_r = jnp.concatenate([h0r_ref[:, g, :] for g in range(gpb)], axis=1)
        hin_i = jnp.concatenate([h0i_ref[:, g, :] for g in range(gpb)], axis=1)
        hr = a8r * hin_r - a8i * hin_i + hl_r
        hi = a8r * hin_i + a8i * hin_r + hl_i
        for g in range(gpb):
            hr_ref[:, g, :] = hr[:, g * p_n:(g + 1) * p_n]
            hi_ref[:, g, :] = hi[:, g * p_n:(g + 1) * p_n]
    hcat = jnp.concatenate([hin_r.astype(BF16), hin_i.astype(BF16)], axis=1)
    d = d_ref[...]
    steps_per_dot = 2
    for t0 in range(0, SSM_CHUNK, steps_per_dot):
        cols = slice(t0 * LANES, (t0 + steps_per_dot) * LANES)
        ks = t0 + steps_per_dot
        k = ks * LANES
        y2 = _dot(ucat[:, :k], tz_ref[:k, cols]) + _dot(hcat, vc_ref[:, cols])
        for t in range(t0, t0 + steps_per_dot):
            y = y2[:, (t - t0) * LANES:(t - t0 + 1) * LANES]
            y_ref[step_rows(t), :] = _gelu_tanh(y + d * u_ref[step_rows(t), :])


GROUPS_PER_BLOCK = LANES // SSM_GROUP


def _spread_groups(x, rows_per_group):
    r, w = x.shape
    wl = GROUPS_PER_BLOCK * w
    src = lax.broadcasted_iota(jnp.int32, (w, wl), 0)
    dst = lax.broadcasted_iota(jnp.int32, (w, wl), 1)
    tiled = _dot(x.astype(BF16), (dst % w == src).astype(BF16))
    row_g = (lax.broadcasted_iota(jnp.int32, (r, wl), 0) // rows_per_group) % GROUPS_PER_BLOCK
    lane_g = lax.broadcasted_iota(jnp.int32, (r, wl), 1) // w
    return jnp.where(row_g == lane_g, tiled, 0.0).astype(BF16)


def _ssm_ops_body(conv_ref, abr_ref, abi_ref, vcr_ref, vci_ref, tz_ref, wb_ref, vc_ref):
    t_n = conv_ref.shape[0]
    sw = vcr_ref.shape[1]
    bd = [_spread_groups(conv_ref[tau], SSM_GROUP) for tau in range(t_n)]
    zero = jnp.zeros_like(bd[0])
    for s in range(t_n):
        rs = slice(s * LANES, (s + 1) * LANES)
        tz_ref[rs, :] = jnp.concatenate([bd[t - s] if t >= s else zero for t in range(t_n)], axis=1)
        wb_ref[rs, :] = jnp.concatenate([_spread_groups(abr_ref[t_n - 1 - s], SSM_GROUP),
                                         _spread_groups(abi_ref[t_n - 1 - s], SSM_GROUP)], axis=1)
    for t in range(t_n):
        cs = slice(t * LANES, (t + 1) * LANES)
        vc_ref[:sw, cs] = _spread_groups(vcr_ref[t], SSM_STATE)
        vc_ref[sw:, cs] = _spread_groups(vci_ref[t], SSM_STATE)


def _ssm_operators(a_re, a_im, log_dt, b_re, b_im, c_re, c_im):
    g_n, p_n = a_re.shape
    lam_re = jnp.minimum(a_re, -1e-4)
    lam_im = a_im
    dt = jnp.exp(log_dt)[:, None]
    decay = jnp.exp(lam_re * dt)
    ar = decay * jnp.cos(lam_im * dt)
    ai = decay * jnp.sin(lam_im * dt)
    num_re = ar - 1.0
    den = lam_re * lam_re + lam_im * lam_im
    f_re = (num_re * lam_re + ai * lam_im) / den
    f_im = (ai * lam_re - num_re * lam_im) / den
    bt_re = jnp.swapaxes(b_re, 1, 2)
    bt_im = jnp.swapaxes(b_im, 1, 2)
    bb_re = f_re[:, None, :] * bt_re - f_im[:, None, :] * bt_im
    bb_im = f_re[:, None, :] * bt_im + f_im[:, None, :] * bt_re
    pw_re, pw_im = [jnp.ones_like(ar)], [jnp.zeros_like(ar)]
    for _ in range(SSM_CHUNK):
        pr, pi = pw_re[-1], pw_im[-1]
        pw_re.append(pr * ar - pi * ai)
        pw_im.append(pr * ai + pi * ar)
    pw_re = jnp.stack(pw_re)
    pw_im = jnp.stack(pw_im)
    pk_re = pw_re[:SSM_CHUNK, :, None, :]
    pk_im = pw_im[:SSM_CHUNK, :, None, :]
    ab_re = pk_re * bb_re - pk_im * bb_im
    ab_im = pk_re * bb_im + pk_im * bb_re
    conv = (jnp.einsum("gcp,tgdp->tgdc", c_re, ab_re, precision=HIGHEST)
            - jnp.einsum("gcp,tgdp->tgdc", c_im, ab_im, precision=HIGHEST))
    gpb = GROUPS_PER_BLOCK
    nb = g_n // gpb
    t_n = SSM_CHUNK
    pr = pw_re[1:, :, :, None]
    pi = pw_im[1:, :, :, None]
    ct_re = jnp.swapaxes(c_re, 1, 2)[None]
    ct_im = jnp.swapaxes(c_im, 1, 2)[None]
    tables = [conv.reshape(t_n, nb, LANES, SSM_GROUP),
              ab_re.reshape(t_n, nb, LANES, p_n), ab_im.reshape(t_n, nb, LANES, p_n),
              (ct_re * pr - ct_im * pi).reshape(t_n, nb, gpb * p_n, SSM_GROUP),
              (-(ct_re * pi + ct_im * pr)).reshape(t_n, nb, gpb * p_n, SSM_GROUP)]
    side = t_n * LANES
    op_spec = pl.BlockSpec((None, side, side), lambda j: (j, 0, 0))
    tz, wb, vc = pl.pallas_call(
        _ssm_ops_body,
        grid=(nb,),
        in_specs=[pl.BlockSpec((t_n, None) + t.shape[2:], lambda j: (0, j, 0, 0)) for t in tables],
        out_specs=[op_spec] * 3,
        out_shape=[jax.ShapeDtypeStruct((nb, side, side), BF16)] * 3,
        compiler_params=_params("parallel"),
        name="s5_chunk_operators",
    )(*tables)
    a8r = pw_re[SSM_CHUNK].reshape(1, g_n * p_n)
    a8i = pw_im[SSM_CHUNK].reshape(1, g_n * p_n)
    return tz, wb, vc, a8r, a8i


def _ssm_scan(u, ops, d_skip, h0_re, h0_im, chunk0, n_chunks, sequential):
    tz, wb, vc, a8r, a8i = ops
    nb = tz.shape[0]
    sw = wb.shape[-1] // 2
    rows = n_chunks * SSM_CHUNK
    cblk = chunk0 // n_chunks
    op3 = lambda a: pl.BlockSpec((None,) + a.shape[1:], lambda j: (j, 0, 0))
    srow = pl.BlockSpec((1, sw), lambda j: (0, j))
    if sequential:
        hspec = srow
        scratch = [pltpu.VMEM((n_chunks, sw), F32)] * 4
    else:
        hspec = pl.BlockSpec((n_chunks, sw // SSM_STATE, SSM_STATE), lambda j: (0, j, 0))
        scratch = []
    return pl.pallas_call(
        functools.partial(_ssm_body, sequential=sequential),
        grid=(nb,),
        in_specs=[pl.BlockSpec((rows, LANES), lambda j: (cblk, j)),
                  op3(tz), op3(wb), op3(vc),
                  srow, srow, pl.BlockSpec((1, LANES), lambda j: (0, j)), hspec, hspec],
        out_specs=[pl.BlockSpec((rows, LANES), lambda j: (0, j)), hspec, hspec],
        out_shape=[jax.ShapeDtypeStruct((rows, u.shape[1]), F32),
                   jax.ShapeDtypeStruct(h0_re.shape, F32), jax.ShapeDtypeStruct(h0_im.shape, F32)],
        scratch_shapes=scratch,
        compiler_params=_params("parallel"),
        name="s5_scan_seq" if sequential else "s5_scan_step",
    )(u, tz, wb, vc, a8r, a8i, d_skip.reshape(1, -1), h0_re, h0_im)


def kernel(x_prompt, x_sample, cache_k, cache_v, state_ssm_re, state_ssm_im, page_table, rel_bias, ln_g, ln_b,
           w_qkv, w_o, w_ssm_in, ssm_a_re, ssm_a_im, ssm_log_dt, ssm_b_re, ssm_b_im, ssm_c_re, ssm_c_im, ssm_d,
           w_glu_v, w_glu_g, w_ff_gate, w_ff_up, w_ff_down, w_router, w_moe_gate, w_moe_up, w_moe_down):
    batch, seq, d = x_prompt.shape
    dec_batch, dec_seq, _ = x_sample.shape
    assert batch == 1 and d == D_MODEL and dec_seq == SSM_CHUNK and seq % MOBA_BLOCK == 0
    m_p = batch * seq
    m_s = dec_batch * dec_seq
    m = m_p + m_s
    dq = N_HEADS * HEAD_DIM
    dkv = N_KV_HEADS * HEAD_DIM
    n_pool, page = cache_k.shape[1], cache_k.shape[2]

    x_p = x_prompt.reshape(m_p, d)
    x_s = x_sample.reshape(m_s, d)

    qkv_f32, qkv_b = _qkv_proj(x_p, x_s, w_qkv[0].astype(BF16), tm=1024, tn=512)
    kmean = _block_means(qkv_f32, seq // MOBA_BLOCK)
    tabs, tab_s = _bias_tables(rel_bias)
    attn_p = _attn_prompt(qkv_f32, qkv_b, kmean, tabs, seq)
    ck = cache_k[0].reshape(n_pool, page * N_KV_HEADS, HEAD_DIM)
    cv = cache_v[0].reshape(n_pool, page * N_KV_HEADS, HEAD_DIM)
    attn_s = _attn_sample(page_table, qkv_f32, ck, cv, tab_s, m_p, dec_seq)
    y = _proj_ln(attn_p, attn_s, [w_o[0].astype(BF16)], [x_p, x_s], ln_g[0, 0], ln_b[0, 0], tm=512, tn=512)
    y = _ffn_dense(y, w_ff_gate, w_ff_up, w_ff_down, ln_g[0, 1], ln_b[0, 1], tm=1024, tf=256)

    k_all = qkv_f32[:, dq:dq + dkv]
    v_all = qkv_f32[:, dq + dkv:]
    k_prompt = k_all[:m_p].reshape(1, batch, seq, N_KV_HEADS, HEAD_DIM)
    v_prompt = v_all[:m_p].reshape(1, batch, seq, N_KV_HEADS, HEAD_DIM)
    k_sample = k_all[m_p:].reshape(1, dec_batch, dec_seq, N_KV_HEADS, HEAD_DIM)
    v_sample = v_all[m_p:].reshape(1, dec_batch, dec_seq, N_KV_HEADS, HEAD_DIM)

    u = _matmul(y, w_ssm_in[0].astype(BF16), tm=1024, tn=1024)
    ops = _ssm_operators(ssm_a_re[0], ssm_a_im[0], ssm_log_dt[0], ssm_b_re[0], ssm_b_im[0],
                         ssm_c_re[0], ssm_c_im[0])
    n_state = N_SSM_GROUPS * SSM_STATE
    zero = jnp.zeros((batch, n_state), F32)
    yg_p, hrp, hip = _ssm_scan(u, ops, ssm_d[0], zero, zero, 0, m_p // SSM_CHUNK, True)
    yg_s, hrs, his = _ssm_scan(u, ops, ssm_d[0], state_ssm_re[0], state_ssm_im[0], m_p // SSM_CHUNK, dec_batch,
                               False)
    y = _proj_ln(yg_p, yg_s, [w_glu_v[0].astype(BF16), w_glu_g[0].astype(BF16)],
                 [y], ln_g[1, 0], ln_b[1, 0], tm=512, tn=512)

    tm_moe = 1024
    top_idx, top_gate = _router(y, w_router[0], tm=512)
    tile_expert, tile_rows, slot_token, pair_slot = _moe_routing(top_idx, tm_moe)
    xs = y.at[slot_token].get(mode="promise_in_bounds")
    ys = _ffn_moe(tile_expert, tile_rows, xs, w_moe_gate[0], w_moe_up[0], w_moe_down[0], tm_moe, tf=256, half=256)
    pairs = ys.at[pair_slot].get(mode="promise_in_bounds").reshape(TOP_K, m, d)
    out_p, out_s = _combine_ln(y, pairs, top_gate, ln_g[1, 1], ln_b[1, 1], m_p, tm=512)

    return (out_p.reshape(batch, seq, d), out_s.reshape(dec_batch, dec_seq, d),
            k_prompt, v_prompt, k_sample, v_sample,
            hrp.reshape(1, batch, N_SSM_GROUPS, SSM_STATE), hip.reshape(1, batch, N_SSM_GROUPS, SSM_STATE),
            hrs.reshape(1, dec_batch, N_SSM_GROUPS, SSM_STATE), his.reshape(1, dec_batch, N_SSM_GROUPS, SSM_STATE))
```

```python
import functools
import math

import numpy as np
import jax
import jax.numpy as jnp
from jax import lax
from jax.experimental import pallas as pl
from jax.experimental.pallas import tpu as pltpu

D_MODEL = 2048
N_HEADS = 16
HEAD_DIM = D_MODEL // N_HEADS
N_KV_HEADS = 4
HEADS_PER_KV = N_HEADS // N_KV_HEADS
MOBA_BLOCK = 256
MOBA_TOPK = 3
NUM_BUCKETS = 32
MAX_DISTANCE = 128
SSM_GROUP = 16
N_SSM_GROUPS = D_MODEL // SSM_GROUP
SSM_STATE = 64
SSM_CHUNK = 8
D_FF = 7 * D_MODEL // 2
N_EXPERTS = 8
TOP_K = 2
DEPTH = 2
ALPHA = (2 * DEPTH) ** 0.25
LN_EPS = 1e-5
NEG_INF = -1e30

LANES = 128
SUBLANES = 8
VMEM_LIMIT = 60 * 1024 * 1024

F32 = jnp.float32
BF16 = jnp.bfloat16
HIGHEST = lax.Precision.HIGHEST
_NT = (((1,), (1,)), ((), ()))


def _params(*sem):
    return pltpu.CompilerParams(dimension_semantics=sem, vmem_limit_bytes=VMEM_LIMIT)


def _dot(a, b):
    return jnp.dot(a, b, preferred_element_type=F32)


def _sigmoid(x):
    return 1.0 / (1.0 + jnp.exp(-x))


def _layer_norm(y, g, b):
    mean = jnp.mean(y, axis=-1, keepdims=True)
    yc = y - mean
    var = jnp.mean(yc * yc, axis=-1, keepdims=True)
    return yc * lax.rsqrt(var + LN_EPS) * g + b


def _mm_body(x_ref, w_ref, o_ref, xb_scr):
    @pl.when(pl.program_id(1) == 0)
    def _():
        xb_scr[...] = x_ref[...].astype(BF16)

    o_ref[...] = _dot(xb_scr[...], w_ref[...])


def _matmul(x, w, tm, tn):
    m, k = x.shape
    n = w.shape[1]
    return pl.pallas_call(
        _mm_body,
        grid=(m // tm, n // tn),
        in_specs=[pl.BlockSpec((tm, k), lambda i, j: (i, 0)),
                  pl.BlockSpec((k, tn), lambda i, j: (0, j))],
        out_specs=pl.BlockSpec((tm, tn), lambda i, j: (i, j)),
        out_shape=jax.ShapeDtypeStruct((m, n), F32),
        scratch_shapes=[pltpu.VMEM((tm, k), BF16)],
        compiler_params=_params("parallel", "arbitrary"),
        name="matmul",
    )(x, w)


def _split_rows(block, n_first):
    return [pl.BlockSpec(block, lambda i, j: (jnp.minimum(i, n_first - 1), 0)),
            pl.BlockSpec(block, lambda i, j: (jnp.maximum(i - n_first, 0), 0))]


def _load_split_bf16(dst, src1, src2, n_first):
    i = pl.program_id(0)
    j = pl.program_id(1)

    @pl.when((j == 0) & (i < n_first))
    def _():
        dst[...] = src1[...].astype(BF16)

    @pl.when((j == 0) & (i >= n_first))
    def _():
        dst[...] = src2[...].astype(BF16)


def _qkv_body(x1_ref, x2_ref, w_ref, of_ref, ob_ref, xb_scr, *, nq_tiles, scale, n_first):
    _load_split_bf16(xb_scr, x1_ref, x2_ref, n_first)
    acc = _dot(xb_scr[...], w_ref[...])
    of_ref[...] = acc
    s = jnp.where(pl.program_id(1) < nq_tiles, scale, 1.0).astype(F32)
    ob_ref[...] = (acc * s).astype(BF16)


def _qkv_proj(x1, x2, w, tm, tn):
    k = x1.shape[1]
    m = x1.shape[0] + x2.shape[0]
    n = w.shape[1]
    n_first = x1.shape[0] // tm
    assert x1.shape[0] % tm == 0 and x2.shape[0] % tm == 0
    body = functools.partial(_qkv_body, nq_tiles=(N_HEADS * HEAD_DIM) // tn, scale=HEAD_DIM ** -0.5,
                             n_first=n_first)
    return pl.pallas_call(
        body,
        grid=(m // tm, n // tn),
        in_specs=_split_rows((tm, k), n_first) + [pl.BlockSpec((k, tn), lambda i, j: (0, j))],
        out_specs=[pl.BlockSpec((tm, tn), lambda i, j: (i, j)),
                   pl.BlockSpec((tm, tn), lambda i, j: (i, j))],
        out_shape=[jax.ShapeDtypeStruct((m, n), F32), jax.ShapeDtypeStruct((m, n), BF16)],
        scratch_shapes=[pltpu.VMEM((tm, k), BF16)],
        compiler_params=_params("parallel", "arbitrary"),
        name="qkv_proj",
    )(x1, x2, w)


def _kmean_body(k_ref, o_ref):
    i = pl.program_id(0)

    @pl.when(i == 0)
    def _():
        o_ref[...] = jnp.zeros_like(o_ref)

    mean = jnp.sum(k_ref[...], axis=0, keepdims=True) * (1.0 / MOBA_BLOCK)
    rows = lax.broadcasted_iota(jnp.int32, o_ref.shape, 0)
    o_ref[...] = jnp.where(rows == i, mean, o_ref[...])


def _block_means(qkv_f32, n_blocks):
    dkv = N_KV_HEADS * HEAD_DIM
    kcol = (N_HEADS * HEAD_DIM) // dkv
    return pl.pallas_call(
        _kmean_body,
        grid=(n_blocks,),
        in_specs=[pl.BlockSpec((MOBA_BLOCK, dkv), lambda i: (i, kcol))],
        out_specs=pl.BlockSpec((LANES, dkv), lambda i: (0, 0)),
        out_shape=jax.ShapeDtypeStruct((LANES, dkv), F32),
        compiler_params=_params("arbitrary"),
        name="moba_block_means",
    )(qkv_f32)


def _t5_bucket_np(dist):
    n = np.maximum(dist, 0)
    max_exact = NUM_BUCKETS // 2
    nf = np.maximum(n, 1).astype(np.float32)
    large = max_exact + (np.log(nf / np.float32(max_exact)) / np.float32(math.log(MAX_DISTANCE / max_exact))
                         * np.float32(NUM_BUCKETS - max_exact)).astype(np.int32)
    large = np.minimum(large, NUM_BUCKETS - 1)
    return np.where(n < max_exact, n, large).astype(np.int32)


def _bucket_tables():
    q = np.arange(MOBA_BLOCK)[:, None]
    k = np.arange(MOBA_BLOCK)[None, :]
    own = np.where(q - k >= 0, _t5_bucket_np(q - k), -1)
    prev = _t5_bucket_np(q - k + MOBA_BLOCK)
    return np.stack([own, prev]).astype(np.int32)


def _bias_body(rb_ref, idx_ref, o_ref, os_ref):
    h = pl.program_id(0)
    far = rb_ref[h, NUM_BUCKETS - 1]
    tabs = []
    for m in range(2):
        idx = idx_ref[m]
        acc = jnp.full(idx.shape, NEG_INF, F32)
        for b in range(NUM_BUCKETS):
            acc = jnp.where(idx == b, rb_ref[h, b] - far, acc)
        o_ref[0, m] = acc
        tabs.append(acc)
    os_ref[0] = jnp.concatenate([tabs[1][:SUBLANES, :], tabs[0][:SUBLANES, :LANES]], axis=1)


def _bias_tables(rel_bias):
    idx = jnp.asarray(_bucket_tables())
    return pl.pallas_call(
        _bias_body,
        grid=(N_HEADS,),
        in_specs=[pl.BlockSpec(memory_space=pltpu.SMEM),
                  pl.BlockSpec((2, MOBA_BLOCK, MOBA_BLOCK), lambda h: (0, 0, 0))],
        out_specs=[pl.BlockSpec((1, 2, MOBA_BLOCK, MOBA_BLOCK), lambda h: (h, 0, 0, 0)),
                   pl.BlockSpec((1, SUBLANES, MOBA_BLOCK + LANES), lambda h: (h, 0, 0))],
        out_shape=[jax.ShapeDtypeStruct((N_HEADS, 2, MOBA_BLOCK, MOBA_BLOCK), F32),
                   jax.ShapeDtypeStruct((N_HEADS, SUBLANES, MOBA_BLOCK + LANES), F32)],
        compiler_params=_params("parallel"),
        name="moba_bias_tables",
    )(rel_bias, idx)


def _select_blocks(gate, n_past):
    lane = lax.broadcasted_iota(jnp.int32, gate.shape, 1)
    lane_f = lane.astype(F32)
    g = jnp.where(lane < n_past, gate, NEG_INF)
    sel = jnp.full(gate.shape, NEG_INF, F32)
    for _ in range(MOBA_TOPK):
        mx = jnp.max(g, axis=1, keepdims=True)
        idx = jnp.min(jnp.where(g == mx, lane_f, float(gate.shape[1])), axis=1, keepdims=True)
        pick = lane_f == idx
        sel = jnp.where(pick, 0.0, sel)
        g = jnp.where(pick, -jnp.inf, g)
    return jnp.where(lane < n_past, sel, 0.0)


def _attn_prompt_body(qf_ref, qb_ref, k_ref, v_ref, km_ref, tab_ref, o_ref, qa_scr, m_scr, l_scr, acc_scr):
    i = pl.program_id(1)
    rows = HEADS_PER_KV * MOBA_BLOCK
    km = km_ref[...]
    lane = lax.broadcasted_iota(jnp.int32, (MOBA_BLOCK, LANES), 1)
    dummy = LANES - 1
    for hh in range(HEADS_PER_KV):
        cs = slice(hh * HEAD_DIM, (hh + 1) * HEAD_DIM)
        rs = slice(hh * MOBA_BLOCK, (hh + 1) * MOBA_BLOCK)
        gate = lax.dot_general(qf_ref[:, cs], km, _NT, precision=HIGHEST, preferred_element_type=F32)
        mask = jnp.where(lane == dummy, NEG_INF, _select_blocks(gate, i))
        qa_scr[rs, :HEAD_DIM] = qb_ref[:, cs]
        qa_scr[rs, HEAD_DIM:] = mask.astype(BF16)
    qa = qa_scr[...]

    def scores(j, mask_lane):
        start = pl.multiple_of(j * MOBA_BLOCK, MOBA_BLOCK)
        kj = k_ref[pl.ds(start, MOBA_BLOCK), :]
        vj = v_ref[pl.ds(start, MOBA_BLOCK), :]
        rhs = jnp.concatenate([kj, (lane == mask_lane).astype(BF16)], axis=1)
        return lax.dot_general(qa, rhs, _NT, preferred_element_type=F32), vj

    def softmax_parts(parts, m_new):
        m2 = jnp.concatenate([m_new, m_new], axis=1)
        l_add = None
        acc_add = None
        for s, vj in parts:
            p = jnp.exp(s - m2)
            ls = jnp.sum(p, axis=1, keepdims=True)
            pv = _dot(p.astype(BF16), vj)
            l_add = ls if l_add is None else l_add + ls
            acc_add = pv if acc_add is None else acc_add + pv
        return l_add, acc_add

    has_prev = i >= 1
    s_own, v_own = scores(i, -1)
    s_own = s_own + tab_ref[:, 0].reshape(rows, MOBA_BLOCK)
    s_prev, v_prev = scores(jnp.maximum(i - 1, 0), jnp.where(has_prev, i - 1, dummy))
    s_prev = s_prev + tab_ref[:, 1].reshape(rows, MOBA_BLOCK)
    m0 = jnp.maximum(jnp.max(s_own, axis=1, keepdims=True), jnp.max(s_prev, axis=1, keepdims=True))
    m0 = jnp.broadcast_to(m0, (rows, LANES))
    l0, acc0 = softmax_parts([(s_own, v_own), (s_prev, v_prev)], m0)
    m_scr[...] = m0
    l_scr[...] = jnp.broadcast_to(l0, (rows, LANES))
    acc_scr[...] = acc0

    n_far = jnp.maximum(i - 1, 0)

    def far_pair(t, carry):
        j0 = 2 * t
        j1 = j0 + 1
        ok1 = j1 < n_far
        parts = [scores(j0, j0), scores(jnp.where(ok1, j1, 0), jnp.where(ok1, j1, dummy))]
        m_prev = m_scr[...]
        m_new = m_prev
        for s, _ in parts:
            m_new = jnp.maximum(m_new, jnp.max(s, axis=1, keepdims=True))
        alpha = jnp.exp(m_prev - m_new)
        l_add, acc_add = softmax_parts(parts, m_new)
        l_scr[...] = alpha * l_scr[...] + l_add
        acc_scr[...] = alpha * acc_scr[...] + acc_add
        m_scr[...] = m_new
        return carry

    lax.fori_loop(0, (n_far + 1) // 2, far_pair, 0)

    out = acc_scr[...] / l_scr[...]
    for hh in range(HEADS_PER_KV):
        o_ref[:, hh * HEAD_DIM:(hh + 1) * HEAD_DIM] = out[hh * MOBA_BLOCK:(hh + 1) * MOBA_BLOCK]


def _attn_prompt(qkv_f32, qkv_b, kmean, tabs, seq):
    nblk = seq // MOBA_BLOCK
    assert nblk < LANES - 1
    gw = HEADS_PER_KV * HEAD_DIM
    kcol = (N_HEADS * HEAD_DIM) // HEAD_DIM
    vcol = kcol + N_KV_HEADS
    rows = HEADS_PER_KV * MOBA_BLOCK
    return pl.pallas_call(
        _attn_prompt_body,
        grid=(N_KV_HEADS, nblk),
        in_specs=[pl.BlockSpec((MOBA_BLOCK, gw), lambda g, i: (i, g)),
                  pl.BlockSpec((MOBA_BLOCK, gw), lambda g, i: (i, g)),
                  pl.BlockSpec((seq, HEAD_DIM), lambda g, i: (0, kcol + g)),
                  pl.BlockSpec((seq, HEAD_DIM), lambda g, i: (0, vcol + g)),
                  pl.BlockSpec((LANES, HEAD_DIM), lambda g, i: (0, g)),
                  pl.BlockSpec((HEADS_PER_KV, 2, MOBA_BLOCK, MOBA_BLOCK), lambda g, i: (g, 0, 0, 0))],
        out_specs=pl.BlockSpec((MOBA_BLOCK, gw), lambda g, i: (i, g)),
        out_shape=jax.ShapeDtypeStruct((seq, N_HEADS * HEAD_DIM), F32),
        scratch_shapes=[pltpu.VMEM((rows, 2 * HEAD_DIM), BF16),
                        pltpu.VMEM((rows, LANES), F32),
                        pltpu.VMEM((rows, LANES), F32),
                        pltpu.VMEM((rows, HEAD_DIM), F32)],
        compiler_params=_params("parallel", "arbitrary"),
        name="moba_attn_prompt",
    )(qkv_f32, qkv_b, qkv_b, qkv_b, kmean, tabs)


def _attn_sample_body(pt_ref, *refs, n_pages, page, dec_seq):
    del pt_ref
    qkv_ref = refs[0]
    kp = refs[1:1 + n_pages]
    vp = refs[1 + n_pages:1 + 2 * n_pages]
    tab_ref = refs[1 + 2 * n_pages]
    o_ref = refs[2 + 2 * n_pages]
    kall, vall, expand = refs[3 + 2 * n_pages:]
    past = n_pages * page
    n_past_blocks = past // MOBA_BLOCK
    pages_per_block = MOBA_BLOCK // page
    near0 = past - MOBA_BLOCK
    total = past + LANES
    dq = N_HEADS * HEAD_DIM
    dkv = N_KV_HEADS * HEAD_DIM
    hq = HEADS_PER_KV * dec_seq
    rows = N_HEADS * dec_seq
    pad = jnp.zeros((LANES - dec_seq, HEAD_DIM), F32)

    @pl.when(pl.program_id(0) == 0)
    def _():
        key_blk = lax.broadcasted_iota(jnp.int32, (LANES, total), 1) // MOBA_BLOCK
        expand[...] = (key_blk == lax.broadcasted_iota(jnp.int32, (LANES, total), 0)).astype(BF16)

    kmeans = []
    for g in range(N_KV_HEADS):
        means = []
        for blk in range(n_past_blocks):
            tot = jnp.zeros((1, HEAD_DIM), F32)
            for pp in range(pages_per_block):
                pg = blk * pages_per_block + pp
                kk = kp[pg][pl.ds(g, page, stride=N_KV_HEADS), :]
                vv = vp[pg][pl.ds(g, page, stride=N_KV_HEADS), :]
                tot = tot + jnp.sum(kk, axis=0, keepdims=True)
                kall[g, pg * page:(pg + 1) * page, :] = kk.astype(BF16)
                vall[g, pg * page:(pg + 1) * page, :] = vv.astype(BF16)
            means.append(tot * (1.0 / MOBA_BLOCK))
        kmeans.append(jnp.concatenate(means + [jnp.zeros((LANES - n_past_blocks, HEAD_DIM), F32)], axis=0))
        knew = qkv_ref[:, dq + g * HEAD_DIM:dq + (g + 1) * HEAD_DIM]
        vnew = qkv_ref[:, dq + dkv + g * HEAD_DIM:dq + dkv + (g + 1) * HEAD_DIM]
        kall[g, past:total, :] = jnp.concatenate([knew, pad], axis=0).astype(BF16)
        vall[g, past:total, :] = jnp.concatenate([vnew, pad], axis=0).astype(BF16)

    qs = jnp.concatenate([qkv_ref[:, h * HEAD_DIM:(h + 1) * HEAD_DIM] for h in range(N_HEADS)], axis=0)
    gate_all = lax.dot_general(qs, jnp.concatenate(kmeans, axis=0), _NT, precision=HIGHEST,
                               preferred_element_type=F32)
    row_g = lax.broadcasted_iota(jnp.int32, (rows, LANES), 0) // hq
    gate = gate_all[:, :LANES]
    for g in range(1, N_KV_HEADS):
        gate = jnp.where(row_g == g, gate_all[:, g * LANES:(g + 1) * LANES], gate)
    selm = _select_blocks(gate, n_past_blocks).astype(BF16)

    qb = (qs * (HEAD_DIM ** -0.5)).astype(BF16)
    s = jnp.concatenate([lax.dot_general(qb[g * hq:(g + 1) * hq], kall[g], _NT, preferred_element_type=F32)
                         for g in range(N_KV_HEADS)], axis=0)
    s = s + _dot(selm, expand[...])
    s_far = s[:, :near0]
    s_near = s[:, near0:] + tab_ref[...].reshape(rows, MOBA_BLOCK + LANES)
    m = jnp.maximum(jnp.max(s_far, axis=1, keepdims=True), jnp.max(s_near, axis=1, keepdims=True))
    p_far = jnp.exp(s_far - m)
    p_near = jnp.exp(s_near - m)
    inv_l = 1.0 / (jnp.sum(p_far, axis=1, keepdims=True) + jnp.sum(p_near, axis=1, keepdims=True))
    p_far = p_far.astype(BF16)
    p_near = p_near.astype(BF16)
    for g in range(N_KV_HEADS):
        rs = slice(g * hq, (g + 1) * hq)
        out = (_dot(p_far[rs], vall[g, :near0, :]) + _dot(p_near[rs], vall[g, near0:, :])) * inv_l[rs]
        for hh in range(HEADS_PER_KV):
            c0 = (g * HEADS_PER_KV + hh) * HEAD_DIM
            o_ref[:, c0:c0 + HEAD_DIM] = out[hh * dec_seq:(hh + 1) * dec_seq]


def _attn_sample(page_table, qkv_f32, cache_k, cache_v, tab_s, row0, dec_seq):
    dec_batch, n_pages = page_table.shape
    page = cache_k.shape[1] // N_KV_HEADS
    total = n_pages * page + LANES
    blk0 = row0 // dec_seq

    def page_spec(p):
        return pl.BlockSpec((None, page * N_KV_HEADS, HEAD_DIM), lambda b, pt, p=p: (pt[b, p], 0, 0))

    body = functools.partial(_attn_sample_body, n_pages=n_pages, page=page, dec_seq=dec_seq)
    grid_spec = pltpu.PrefetchScalarGridSpec(
        num_scalar_prefetch=1,
        grid=(dec_batch,),
        in_specs=([pl.BlockSpec((dec_seq, qkv_f32.shape[1]), lambda b, pt: (blk0 + b, 0))]
                  + [page_spec(p) for p in range(n_pages)]
                  + [page_spec(p) for p in range(n_pages)]
                  + [pl.BlockSpec(tab_s.shape, lambda b, pt: (0, 0, 0))]),
        out_specs=pl.BlockSpec((dec_seq, N_HEADS * HEAD_DIM), lambda b, pt: (b, 0)),
        scratch_shapes=[pltpu.VMEM((N_KV_HEADS, total, HEAD_DIM), BF16),
                        pltpu.VMEM((N_KV_HEADS, total, HEAD_DIM), BF16),
                        pltpu.VMEM((LANES, total), BF16)],
    )
    return pl.pallas_call(
        body,
        grid_spec=grid_spec,
        out_shape=jax.ShapeDtypeStruct((dec_batch * dec_seq, N_HEADS * HEAD_DIM), F32),
        compiler_params=_params("arbitrary"),
        name="moba_attn_sample",
    )(page_table, qkv_f32, *([cache_k] * n_pages), *([cache_v] * n_pages), tab_s)


def _pack_bf16_halves(x):
    c = x.shape[1] // 2
    lo = lax.bitcast_convert_type(x[:, :c].astype(BF16).astype(F32), jnp.uint32)
    hi = lax.bitcast_convert_type(x[:, c:].astype(BF16).astype(F32), jnp.uint32)
    return hi | (lo >> 16)


def _unpack_bf16_halves(u):
    lo = lax.bitcast_convert_type(u << 16, F32)
    hi = lax.bitcast_convert_type(u & jnp.uint32(0xFFFF0000), F32)
    return jnp.concatenate([lo, hi], axis=1)


def _proj_ln_body(*refs, n_w, n_x, n_tiles, n_first, packed):
    a1_ref, a2_ref = refs[:2]
    w_refs = refs[2:2 + n_w]
    x_refs = refs[2 + n_w:2 + n_w + n_x]
    if packed:
        g_ref, b_ref, o_ref, op_ref, a_scr, z_scr = refs[2 + n_w + n_x:]
    else:
        g_ref, b_ref, o_ref, a_scr, z_scr = refs[2 + n_w + n_x:]
    i = pl.program_id(0)
    j = pl.program_id(1)
    _load_split_bf16(a_scr, a1_ref, a2_ref, n_first)
    a = a_scr[...]
    z = _dot(a, w_refs[0][...])
    if n_w == 2:
        z = z * _sigmoid(_dot(a, w_refs[1][...]))
    z_scr[j] = z

    @pl.when(j == n_tiles - 1)
    def _():
        zfull = jnp.concatenate([z_scr[t] for t in range(n_tiles)], axis=1)
        x = x_refs[0][...] if n_x == 1 else jnp.where(i < n_first, x_refs[0][...], x_refs[1][...])
        out = _layer_norm(ALPHA * x + zfull, g_ref[...], b_ref[...])
        o_ref[...] = out
        if packed:
            op_ref[...] = _pack_bf16_halves(out)


def _proj_ln(a1, a2, ws, xs, g, b, tm, tn, packed=False):
    k = a1.shape[1]
    m = a1.shape[0] + a2.shape[0]
    n = ws[0].shape[1]
    n_first = a1.shape[0] // tm
    assert a1.shape[0] % tm == 0 and a2.shape[0] % tm == 0
    assert len(xs) == 1 or xs[0].shape[0] == a1.shape[0]
    n_tiles = n // tn
    body = functools.partial(_proj_ln_body, n_w=len(ws), n_x=len(xs), n_tiles=n_tiles, n_first=n_first,
                             packed=packed)
    row = pl.BlockSpec((tm, n), lambda i, j: (i, 0))
    vec = pl.BlockSpec((1, n), lambda i, j: (0, 0))
    out_specs, out_shape = row, jax.ShapeDtypeStruct((m, n), F32)
    if packed:
        out_specs = [row, pl.BlockSpec((tm, n // 2), lambda i, j: (i, 0))]
        out_shape = [out_shape, jax.ShapeDtypeStruct((m, n // 2), jnp.uint32)]
    return pl.pallas_call(
        body,
        grid=(m // tm, n_tiles),
        in_specs=(_split_rows((tm, k), n_first)
                  + [pl.BlockSpec((k, tn), lambda i, j: (0, j)) for _ in ws]
                  + ([row] if len(xs) == 1 else _split_rows((tm, n), n_first))
                  + [vec, vec]),
        out_specs=out_specs,
        out_shape=out_shape,
        scratch_shapes=[pltpu.VMEM((tm, k), BF16), pltpu.VMEM((n_tiles, tm, tn), F32)],
        compiler_params=_params("parallel", "arbitrary"),
        name="glu_res_ln" if len(ws) == 2 else "proj_res_ln",
    )(a1, a2, *ws, *xs, g.reshape(1, n), b.reshape(1, n))


def _ffn_body(te_ref, tr_ref, *refs, n_f, dense, half):
    del te_ref
    if dense:
        x_ref, wg_ref, wu_ref, wd_ref, g_ref, b_ref, o_ref, xb = refs
        acc = o_ref
    else:
        x_ref, wg_ref, wu_ref, wd_ref, o_ref, xb, acc = refs
    tm = x_ref.shape[0]
    t = pl.program_id(0)
    f = pl.program_id(1)
    n_rows = tr_ref[t]

    @pl.when(f == 0)
    def _():
        x = x_ref[...] if dense else _unpack_bf16_halves(x_ref[...])
        xb[...] = x.astype(BF16)
        acc[...] = jnp.zeros_like(acc)

    def swiglu_rows(n):
        x = xb[:n, :]
        hg = _dot(x, wg_ref[...].astype(BF16))
        hu = _dot(x, wu_ref[...].astype(BF16))
        h = (hg * _sigmoid(hg)) * hu
        acc[:n, :] += _dot(h.astype(BF16), wd_ref[...].astype(BF16))

    for n in range(half, tm + 1, half):
        pl.when((n_rows > n - half) & (n_rows <= n))(functools.partial(swiglu_rows, n))

    @pl.when(f == n_f - 1)
    def _():
        if dense:
            o_ref[...] = _layer_norm(ALPHA * x_ref[...] + acc[...], g_ref[...], b_ref[...])
        else:
            o_ref[...] = _pack_bf16_halves(acc[...])


def _ffn_call(tile_expert, tile_rows, x, wg, wu, wd, extra, extra_specs, tm, tf, half, dense, name):
    s = x.shape[0]
    d = wg.shape[-2]
    n_t = s // tm
    n_f = wg.shape[-1] // tf
    live = lambda t, tr: jnp.minimum(tr[t], 1)
    x_spec = pl.BlockSpec((tm, x.shape[1]), lambda t, f, te, tr: (t, 0))
    wgu = pl.BlockSpec((None, d, tf), lambda t, f, te, tr: (te[t], 0, f * live(t, tr)))
    wds = pl.BlockSpec((None, tf, d), lambda t, f, te, tr: (te[t], f * live(t, tr), 0))
    grid_spec = pltpu.PrefetchScalarGridSpec(
        num_scalar_prefetch=2, grid=(n_t, n_f),
        in_specs=[x_spec, wgu, wgu, wds] + extra_specs,
        out_specs=x_spec,
        scratch_shapes=[pltpu.VMEM((tm, d), BF16)] + ([] if dense else [pltpu.VMEM((tm, d), F32)]))
    return pl.pallas_call(
        functools.partial(_ffn_body, n_f=n_f, dense=dense, half=half),
        grid_spec=grid_spec,
        out_shape=jax.ShapeDtypeStruct(x.shape, x.dtype),
        compiler_params=_params("parallel", "arbitrary"),
        name=name,
    )(tile_expert, tile_rows, x, wg, wu, wd, *extra)


def _ffn_dense(x, wg, wu, wd, g, b, tm, tf):
    m, d = x.shape
    n_t = m // tm
    vec = pl.BlockSpec((1, d), lambda t, f, te, tr: (0, 0))
    return _ffn_call(jnp.zeros((n_t,), jnp.int32), jnp.full((n_t,), tm, jnp.int32), x, wg, wu, wd,
                     [g.reshape(1, d), b.reshape(1, d)], [vec, vec], tm, tf, tm, True, "ffn_res_ln")


def _ffn_moe(tile_expert, tile_rows, xs, wg, wu, wd, tm, tf, half):
    return _ffn_call(tile_expert, tile_rows, xs, wg, wu, wd, [], [], tm, tf, half, False, "moe_ffn")


def _router_body(y_ref, w_ref, i_ref, g_ref):
    logits = jnp.dot(y_ref[...], w_ref[...], precision=HIGHEST, preferred_element_type=F32)
    lane = lax.broadcasted_iota(jnp.int32, logits.shape, 1)
    lane_f = lane.astype(F32)
    l1 = jnp.where(lane < N_EXPERTS, logits, -jnp.inf)
    m1 = jnp.max(l1, axis=1, keepdims=True)
    i1 = jnp.min(jnp.where(l1 == m1, lane_f, float(LANES)), axis=1, keepdims=True)
    l2 = jnp.where(lane_f == i1, -jnp.inf, l1)
    m2 = jnp.max(l2, axis=1, keepdims=True)
    i2 = jnp.min(jnp.where(l2 == m2, lane_f, float(LANES)), axis=1, keepdims=True)
    e = jnp.exp(m2 - m1)
    g1 = 1.0 / (1.0 + e)
    g2 = e / (1.0 + e)
    i_ref[...] = jnp.where(lane == 0, i1, jnp.where(lane == 1, i2, 0.0)).astype(jnp.int32)
    g_ref[...] = jnp.where(lane == 0, g1, jnp.where(lane == 1, g2, 0.0))


def _router(y, w_router, tm):
    m, d = y.shape
    wr = jnp.zeros((d, LANES), F32).at[:, :N_EXPERTS].set(w_router)
    row = pl.BlockSpec((tm, LANES), lambda i: (i, 0))
    return pl.pallas_call(
        _router_body,
        grid=(m // tm,),
        in_specs=[pl.BlockSpec((tm, d), lambda i: (i, 0)), pl.BlockSpec((d, LANES), lambda i: (0, 0))],
        out_specs=[row, row],
        out_shape=[jax.ShapeDtypeStruct((m, LANES), jnp.int32), jax.ShapeDtypeStruct((m, LANES), F32)],
        compiler_params=_params("parallel"),
        name="moe_router",
    )(y, wr)


def _combine_ln_body(x_ref, a_ref, b2_ref, tg_ref, g_ref, b_ref, op_ref, os_ref, *, n_first):
    i = pl.program_id(0)
    tg = tg_ref[...]
    moe = tg[:, 0:1] * _unpack_bf16_halves(a_ref[...]) + tg[:, 1:2] * _unpack_bf16_halves(b2_ref[...])
    out = _layer_norm(ALPHA * x_ref[...] + moe, g_ref[...], b_ref[...])

    @pl.when(i < n_first)
    def _():
        op_ref[...] = out

    @pl.when(i >= n_first)
    def _():
        os_ref[...] = out


def _combine_ln(x, pairs, top_gate, g, b, m_first, tm):
    m, d = x.shape
    dp = pairs.shape[2]
    n_first = m_first // tm
    vec = pl.BlockSpec((1, d), lambda i: (0, 0))
    return pl.pallas_call(
        functools.partial(_combine_ln_body, n_first=n_first),
        grid=(m // tm,),
        in_specs=[pl.BlockSpec((tm, d), lambda i: (i, 0)),
                  pl.BlockSpec((None, tm, dp), lambda i: (0, i, 0)),
                  pl.BlockSpec((None, tm, dp), lambda i: (1, i, 0)),
                  pl.BlockSpec((tm, LANES), lambda i: (i, 0)),
                  vec, vec],
        out_specs=[pl.BlockSpec((tm, d), lambda i: (jnp.minimum(i, n_first - 1), 0)),
                   pl.BlockSpec((tm, d), lambda i: (jnp.maximum(i - n_first, 0), 0))],
        out_shape=[jax.ShapeDtypeStruct((m_first, d), F32), jax.ShapeDtypeStruct((m - m_first, d), F32)],
        compiler_params=_params("arbitrary"),
        name="moe_combine_ln",
    )(x, pairs, pairs, top_gate, g.reshape(1, d), b.reshape(1, d))


def _moe_routing(top_idx, tm):
    m = top_idx.shape[0]
    n_pairs = m * TOP_K
    n_tiles = n_pairs // tm + N_EXPERTS
    i32 = jnp.int32
    e_flat = jnp.concatenate([top_idx[:, k] for k in range(TOP_K)])
    order = jnp.argsort(e_flat, stable=True).astype(i32)
    inv = jnp.argsort(order).astype(i32)
    counts = jnp.sum((e_flat[None, :] == jnp.arange(N_EXPERTS, dtype=i32)[:, None]).astype(i32), axis=1)
    padded = ((counts + tm - 1) // tm) * tm
    ends_padded = jnp.cumsum(padded)
    starts_padded = ends_padded - padded
    starts = jnp.cumsum(counts) - counts
    tile_start = jnp.arange(n_tiles, dtype=i32) * tm
    tile_expert = jnp.minimum(jnp.searchsorted(ends_padded, tile_start, side="right"), N_EXPERTS - 1).astype(i32)
    tile_rows = jnp.clip(counts[tile_expert] - (tile_start - starts_padded[tile_expert]), 0, tm).astype(i32)
    slot = jnp.arange(n_tiles * tm, dtype=i32)
    slot_e = jnp.repeat(tile_expert, tm)
    rank = slot - starts_padded[slot_e]
    live = (rank >= 0) & (rank < counts[slot_e])
    src = order[jnp.clip(starts[slot_e] + rank, 0, n_pairs - 1)]
    slot_token = jnp.where(live, src % m, slot % m)
    pair_slot = starts_padded[e_flat] + inv - starts[e_flat]
    return tile_expert, tile_rows, slot_token, pair_slot


def _gelu_tanh(x):
    return 0.5 * x * (1.0 + jnp.tanh(math.sqrt(2.0 / math.pi) * (x + 0.044715 * (x * x * x))))


def _ssm_body(*refs, sequential):
    (u_ref, tz_ref, wb_ref, vc_ref, a8r_ref, a8i_ref, d_ref, h0r_ref, h0i_ref, y_ref, hr_ref, hi_ref) = refs[:12]
    scr = refs[12:]
    n_chunks = u_ref.shape[0] // SSM_CHUNK
    sw = a8r_ref.shape[1]
    step_rows = lambda s: pl.ds(s, n_chunks, stride=SSM_CHUNK)
    ucat = jnp.concatenate([u_ref[step_rows(s), :].astype(BF16) for s in range(SSM_CHUNK)], axis=1)
    hl = _dot(ucat, wb_ref[...])
    hl_r = hl[:, :sw]
    hl_i = hl[:, sw:]
    a8r = a8r_ref[...]
    a8i = a8i_ref[...]
    if sequential:
        hlr_scr, hli_scr, hinr_scr, hini_scr = scr
        hlr_scr[...] = hl_r
        hli_scr[...] = hl_i

        def tile_step(kb, carry):
            hr, hi = carry
            base = pl.multiple_of(kb * SUBLANES, SUBLANES)
            tr = hlr_scr[pl.ds(base, SUBLANES), :]
            ti = hli_scr[pl.ds(base, SUBLANES), :]
            rows_r, rows_i = [], []
            for r in range(SUBLANES):
                rows_r.append(hr)
                rows_i.append(hi)
                hr, hi = (a8r * hr - a8i * hi + tr[r:r + 1], a8r * hi + a8i * hr + ti[r:r + 1])
            hinr_scr[pl.ds(base, SUBLANES), :] = jnp.concatenate(rows_r, axis=0)
            hini_scr[pl.ds(base, SUBLANES), :] = jnp.concatenate(rows_i, axis=0)
            return hr, hi

        hr, hi = lax.fori_loop(0, n_chunks // SUBLANES, tile_step, (h0r_ref[...], h0i_ref[...]))
        hr_ref[...] = hr
        hi_ref[...] = hi
        hin_r = hinr_scr[...]
        hin_i = hini_scr[...]
    else:
        gpb = h0r_ref.shape[1]
        p_n = h0r_ref.shape[2]
        hin_r = jnp.concatenate([h0r_ref[:, g, :] for g in range(gpb)], axis=1)
        hin_i = jnp.concatenate([h0i_ref[:, g, :] for g in range(gpb)], axis=1)
        hr = a8r * hin_r - a8i * hin_i + hl_r
        hi = a8r * hin_i + a8i * hin_r + hl_i
        for g in range(gpb):
            hr_ref[:, g, :] = hr[:, g * p_n:(g + 1) * p_n]
            hi_ref[:, g, :] = hi[:, g * p_n:(g + 1) * p_n]
    hcat = jnp.concatenate([hin_r.astype(BF16), hin_i.astype(BF16)], axis=1)
    d = d_ref[...]
    steps_per_dot = 2
    for t0 in range(0, SSM_CHUNK, steps_per_dot):
        cols = slice(t0 * LANES, (t0 + steps_per_dot) * LANES)
        ks = t0 + steps_per_dot
        k = ks * LANES
        y2 = _dot(ucat[:, :k], tz_ref[:k, cols]) + _dot(hcat, vc_ref[:, cols])
        for t in range(t0, t0 + steps_per_dot):
            y = y2[:, (t - t0) * LANES:(t - t0 + 1) * LANES]
            y_ref[step_rows(t), :] = _gelu_tanh(y + d * u_ref[step_rows(t), :])


GROUPS_PER_BLOCK = LANES // SSM_GROUP


def _spread_groups(x, rows_per_group):
    r, w = x.shape
    wl = GROUPS_PER_BLOCK * w
    src = lax.broadcasted_iota(jnp.int32, (w, wl), 0)
    dst = lax.broadcasted_iota(jnp.int32, (w, wl), 1)
    tiled = _dot(x.astype(BF16), (dst % w == src).astype(BF16))
    row_g = (lax.broadcasted_iota(jnp.int32, (r, wl), 0) // rows_per_group) % GROUPS_PER_BLOCK
    lane_g = lax.broadcasted_iota(jnp.int32, (r, wl), 1) // w
    return jnp.where(row_g == lane_g, tiled, 0.0).astype(BF16)


def _ssm_ops_body(conv_ref, abr_ref, abi_ref, vcr_ref, vci_ref, tz_ref, wb_ref, vc_ref):
    t_n = conv_ref.shape[0]
    sw = vcr_ref.shape[1]
    bd = [_spread_groups(conv_ref[tau], SSM_GROUP) for tau in range(t_n)]
    zero = jnp.zeros_like(bd[0])
    for s in range(t_n):
        rs = slice(s * LANES, (s + 1) * LANES)
        tz_ref[rs, :] = jnp.concatenate([bd[t - s] if t >= s else zero for t in range(t_n)], axis=1)
        wb_ref[rs, :] = jnp.concatenate([_spread_groups(abr_ref[t_n - 1 - s], SSM_GROUP),
                                         _spread_groups(abi_ref[t_n - 1 - s], SSM_GROUP)], axis=1)
    for t in range(t_n):
        cs = slice(t * LANES, (t + 1) * LANES)
        vc_ref[:sw, cs] = _spread_groups(vcr_ref[t], SSM_STATE)
        vc_ref[sw:, cs] = _spread_groups(vci_ref[t], SSM_STATE)


def _ssm_operators(a_re, a_im, log_dt, b_re, b_im, c_re, c_im):
    g_n, p_n = a_re.shape
    lam_re = jnp.minimum(a_re, -1e-4)
    lam_im = a_im
    dt = jnp.exp(log_dt)[:, None]
    decay = jnp.exp(lam_re * dt)
    ar = decay * jnp.cos(lam_im * dt)
    ai = decay * jnp.sin(lam_im * dt)
    num_re = ar - 1.0
    den = lam_re * lam_re + lam_im * lam_im
    f_re = (num_re * lam_re + ai * lam_im) / den
    f_im = (ai * lam_re - num_re * lam_im) / den
    bt_re = jnp.swapaxes(b_re, 1, 2)
    bt_im = jnp.swapaxes(b_im, 1, 2)
    bb_re = f_re[:, None, :] * bt_re - f_im[:, None, :] * bt_im
    bb_im = f_re[:, None, :] * bt_im + f_im[:, None, :] * bt_re
    pw_re, pw_im = [jnp.ones_like(ar)], [jnp.zeros_like(ar)]
    for _ in range(SSM_CHUNK):
        pr, pi = pw_re[-1], pw_im[-1]
        pw_re.append(pr * ar - pi * ai)
        pw_im.append(pr * ai + pi * ar)
    pw_re = jnp.stack(pw_re)
    pw_im = jnp.stack(pw_im)
    pk_re = pw_re[:SSM_CHUNK, :, None, :]
    pk_im = pw_im[:SSM_CHUNK, :, None, :]
    ab_re = pk_re * bb_re - pk_im * bb_im
    ab_im = pk_re * bb_im + pk_im * bb_re
    conv = (jnp.einsum("gcp,tgdp->tgdc", c_re, ab_re, precision=HIGHEST)
            - jnp.einsum("gcp,tgdp->tgdc", c_im, ab_im, precision=HIGHEST))
    gpb = GROUPS_PER_BLOCK
    nb = g_n // gpb
    t_n = SSM_CHUNK
    pr = pw_re[1:, :, :, None]
    pi = pw_im[1:, :, :, None]
    ct_re = jnp.swapaxes(c_re, 1, 2)[None]
    ct_im = jnp.swapaxes(c_im, 1, 2)[None]
    tables = [conv.reshape(t_n, nb, LANES, SSM_GROUP),
              ab_re.reshape(t_n, nb, LANES, p_n), ab_im.reshape(t_n, nb, LANES, p_n),
              (ct_re * pr - ct_im * pi).reshape(t_n, nb, gpb * p_n, SSM_GROUP),
              (-(ct_re * pi + ct_im * pr)).reshape(t_n, nb, gpb * p_n, SSM_GROUP)]
    side = t_n * LANES
    op_spec = pl.BlockSpec((None, side, side), lambda j: (j, 0, 0))
    tz, wb, vc = pl.pallas_call(
        _ssm_ops_body,
        grid=(nb,),
        in_specs=[pl.BlockSpec((t_n, None) + t.shape[2:], lambda j: (0, j, 0, 0)) for t in tables],
        out_specs=[op_spec] * 3,
        out_shape=[jax.ShapeDtypeStruct((nb, side, side), BF16)] * 3,
        compiler_params=_params("parallel"),
        name="s5_chunk_operators",
    )(*tables)
    a8r = pw_re[SSM_CHUNK].reshape(1, g_n * p_n)
    a8i = pw_im[SSM_CHUNK].reshape(1, g_n * p_n)
    return tz, wb, vc, a8r, a8i


def _ssm_scan(u, ops, d_skip, h0_re, h0_im, chunk0, n_chunks, sequential):
    tz, wb, vc, a8r, a8i = ops
    nb = tz.shape[0]
    sw = wb.shape[-1] // 2
    rows = n_chunks * SSM_CHUNK
    cblk = chunk0 // n_chunks
    op3 = lambda a: pl.BlockSpec((None,) + a.shape[1:], lambda j: (j, 0, 0))
    srow = pl.BlockSpec((1, sw), lambda j: (0, j))
    if sequential:
        hspec = srow
        scratch = [pltpu.VMEM((n_chunks, sw), F32)] * 4
    else:
        hspec = pl.BlockSpec((n_chunks, sw // SSM_STATE, SSM_STATE), lambda j: (0, j, 0))
        scratch = []
    return pl.pallas_call(
        functools.partial(_ssm_body, sequential=sequential),
        grid=(nb,),
        in_specs=[pl.BlockSpec((rows, LANES), lambda j: (cblk, j)),
                  op3(tz), op3(wb), op3(vc),
                  srow, srow, pl.BlockSpec((1, LANES), lambda j: (0, j)), hspec, hspec],
        out_specs=[pl.BlockSpec((rows, LANES), lambda j: (0, j)), hspec, hspec],
        out_shape=[jax.ShapeDtypeStruct((rows, u.shape[1]), F32),
                   jax.ShapeDtypeStruct(h0_re.shape, F32), jax.ShapeDtypeStruct(h0_im.shape, F32)],
        scratch_shapes=scratch,
        compiler_params=_params("parallel"),
        name="s5_scan_seq" if sequential else "s5_scan_step",
    )(u, tz, wb, vc, a8r, a8i, d_skip.reshape(1, -1), h0_re, h0_im)


def kernel(x_prompt, x_sample, cache_k, cache_v, state_ssm_re, state_ssm_im, page_table, rel_bias, ln_g, ln_b,
           w_qkv, w_o, w_ssm_in, ssm_a_re, ssm_a_im, ssm_log_dt, ssm_b_re, ssm_b_im, ssm_c_re, ssm_c_im, ssm_d,
           w_glu_v, w_glu_g, w_ff_gate, w_ff_up, w_ff_down, w_router, w_moe_gate, w_moe_up, w_moe_down):
    batch, seq, d = x_prompt.shape
    dec_batch, dec_seq, _ = x_sample.shape
    assert batch == 1 and d == D_MODEL and dec_seq == SSM_CHUNK and seq % MOBA_BLOCK == 0
    m_p = batch * seq
    m_s = dec_batch * dec_seq
    m = m_p + m_s
    dq = N_HEADS * HEAD_DIM
    dkv = N_KV_HEADS * HEAD_DIM
    n_pool, page = cache_k.shape[1], cache_k.shape[2]

    x_p = x_prompt.reshape(m_p, d)
    x_s = x_sample.reshape(m_s, d)

    qkv_f32, qkv_b = _qkv_proj(x_p, x_s, w_qkv[0].astype(BF16), tm=1024, tn=512)
    kmean = _block_means(qkv_f32, seq // MOBA_BLOCK)
    tabs, tab_s = _bias_tables(rel_bias)
    attn_p = _attn_prompt(qkv_f32, qkv_b, kmean, tabs, seq)
    ck = cache_k[0].reshape(n_pool, page * N_KV_HEADS, HEAD_DIM)
    cv = cache_v[0].reshape(n_pool, page * N_KV_HEADS, HEAD_DIM)
    attn_s = _attn_sample(page_table, qkv_f32, ck, cv, tab_s, m_p, dec_seq)
    y = _proj_ln(attn_p, attn_s, [w_o[0].astype(BF16)], [x_p, x_s], ln_g[0, 0], ln_b[0, 0], tm=512, tn=512)
    y = _ffn_dense(y, w_ff_gate, w_ff_up, w_ff_down, ln_g[0, 1], ln_b[0, 1], tm=1024, tf=256)

    k_all = qkv_f32[:, dq:dq + dkv]
    v_all = qkv_f32[:, dq + dkv:]
    k_prompt = k_all[:m_p].reshape(1, batch, seq, N_KV_HEADS, HEAD_DIM)
    v_prompt = v_all[:m_p].reshape(1, batch, seq, N_KV_HEADS, HEAD_DIM)
    k_sample = k_all[m_p:].reshape(1, dec_batch, dec_seq, N_KV_HEADS, HEAD_DIM)
    v_sample = v_all[m_p:].reshape(1, dec_batch, dec_seq, N_KV_HEADS, HEAD_DIM)

    u = _matmul(y, w_ssm_in[0].astype(BF16), tm=1024, tn=1024)
    ops = _ssm_operators(ssm_a_re[0], ssm_a_im[0], ssm_log_dt[0], ssm_b_re[0], ssm_b_im[0],
                         ssm_c_re[0], ssm_c_im[0])
    n_state = N_SSM_GROUPS * SSM_STATE
    zero = jnp.zeros((batch, n_state), F32)
    yg_p, hrp, hip = _ssm_scan(u, ops, ssm_d[0], zero, zero, 0, m_p // SSM_CHUNK, True)
    yg_s, hrs, his = _ssm_scan(u, ops, ssm_d[0], state_ssm_re[0], state_ssm_im[0], m_p // SSM_CHUNK, dec_batch,
                               False)
    y, y_packed = _proj_ln(yg_p, yg_s, [w_glu_v[0].astype(BF16), w_glu_g[0].astype(BF16)],
                           [y], ln_g[1, 0], ln_b[1, 0], tm=512, tn=512, packed=True)

    tm_moe = 1024
    top_idx, top_gate = _router(y, w_router[0], tm=512)
    tile_expert, tile_rows, slot_token, pair_slot = _moe_routing(top_idx, tm_moe)
    xs = y_packed.at[slot_token].get(mode="promise_in_bounds")
    ys = _ffn_moe(tile_expert, tile_rows, xs, w_moe_gate[0], w_moe_up[0], w_moe_down[0], tm_moe, tf=512, half=256)
    pairs = ys.at[pair_slot].get(mode="promise_in_bounds").reshape(TOP_K, m, d // 2)
    out_p, out_s = _combine_ln(y, pairs, top_gate, ln_g[1, 1], ln_b[1, 1], m_p, tm=512)

    return (out_p.reshape(batch, seq, d), out_s.reshape(dec_batch, dec_seq, d),
            k_prompt, v_prompt, k_sample, v_sample,
            hrp.reshape(1, batch, N_SSM_GROUPS, SSM_STATE), hip.reshape(1, batch, N_SSM_GROUPS, SSM_STATE),
            hrs.reshape(1, dec_batch, N_SSM_GROUPS, SSM_STATE), his.reshape(1, dec_batch, N_SSM_GROUPS, SSM_STATE))
```

```python
import functools
import math

import numpy as np
import jax
import jax.numpy as jnp
from jax import lax
from jax.experimental import pallas as pl
from jax.experimental.pallas import tpu as pltpu

D_MODEL = 2048
N_HEADS = 16
HEAD_DIM = D_MODEL // N_HEADS
N_KV_HEADS = 4
HEADS_PER_KV = N_HEADS // N_KV_HEADS
MOBA_BLOCK = 256
MOBA_TOPK = 3
NUM_BUCKETS = 32
MAX_DISTANCE = 128
SSM_GROUP = 16
N_SSM_GROUPS = D_MODEL // SSM_GROUP
SSM_STATE = 64
SSM_CHUNK = 8
D_FF = 7 * D_MODEL // 2
N_EXPERTS = 8
TOP_K = 2
DEPTH = 2
ALPHA = (2 * DEPTH) ** 0.25
LN_EPS = 1e-5
NEG_INF = -1e30

LANES = 128
SUBLANES = 8
VMEM_LIMIT = 60 * 1024 * 1024

F32 = jnp.float32
BF16 = jnp.bfloat16
HIGHEST = lax.Precision.HIGHEST
_NT = (((1,), (1,)), ((), ()))


def _params(*sem):
    return pltpu.CompilerParams(dimension_semantics=sem, vmem_limit_bytes=VMEM_LIMIT)


def _dot(a, b):
    return jnp.dot(a, b, preferred_element_type=F32)


def _sigmoid(x):
    return 1.0 / (1.0 + jnp.exp(-x))


def _layer_norm(y, g, b):
    mean = jnp.mean(y, axis=-1, keepdims=True)
    yc = y - mean
    var = jnp.mean(yc * yc, axis=-1, keepdims=True)
    return yc * lax.rsqrt(var + LN_EPS) * g + b


def _mm_body(x_ref, w_ref, o_ref, xb_scr):
    @pl.when(pl.program_id(1) == 0)
    def _():
        xb_scr[...] = x_ref[...].astype(BF16)

    o_ref[...] = _dot(xb_scr[...], w_ref[...])


def _matmul(x, w, tm, tn):
    m, k = x.shape
    n = w.shape[1]
    return pl.pallas_call(
        _mm_body,
        grid=(m // tm, n // tn),
        in_specs=[pl.BlockSpec((tm, k), lambda i, j: (i, 0)),
                  pl.BlockSpec((k, tn), lambda i, j: (0, j))],
        out_specs=pl.BlockSpec((tm, tn), lambda i, j: (i, j)),
        out_shape=jax.ShapeDtypeStruct((m, n), F32),
        scratch_shapes=[pltpu.VMEM((tm, k), BF16)],
        compiler_params=_params("parallel", "arbitrary"),
        name="matmul",
    )(x, w)


def _split_rows(block, n_first):
    return [pl.BlockSpec(block, lambda i, j: (jnp.minimum(i, n_first - 1), 0)),
            pl.BlockSpec(block, lambda i, j: (jnp.maximum(i - n_first, 0), 0))]


def _load_split_bf16(dst, src1, src2, n_first):
    i = pl.program_id(0)
    j = pl.program_id(1)

    @pl.when((j == 0) & (i < n_first))
    def _():
        dst[...] = src1[...].astype(BF16)

    @pl.when((j == 0) & (i >= n_first))
    def _():
        dst[...] = src2[...].astype(BF16)


def _qkv_body(x1_ref, x2_ref, w_ref, of_ref, ob_ref, xb_scr, *, nq_tiles, scale, n_first):
    _load_split_bf16(xb_scr, x1_ref, x2_ref, n_first)
    acc = _dot(xb_scr[...], w_ref[...])
    of_ref[...] = acc
    s = jnp.where(pl.program_id(1) < nq_tiles, scale, 1.0).astype(F32)
    ob_ref[...] = (acc * s).astype(BF16)


def _qkv_proj(x1, x2, w, tm, tn):
    k = x1.shape[1]
    m = x1.shape[0] + x2.shape[0]
    n = w.shape[1]
    n_first = x1.shape[0] // tm
    assert x1.shape[0] % tm == 0 and x2.shape[0] % tm == 0
    body = functools.partial(_qkv_body, nq_tiles=(N_HEADS * HEAD_DIM) // tn, scale=HEAD_DIM ** -0.5,
                             n_first=n_first)
    return pl.pallas_call(
        body,
        grid=(m // tm, n // tn),
        in_specs=_split_rows((tm, k), n_first) + [pl.BlockSpec((k, tn), lambda i, j: (0, j))],
        out_specs=[pl.BlockSpec((tm, tn), lambda i, j: (i, j)),
                   pl.BlockSpec((tm, tn), lambda i, j: (i, j))],
        out_shape=[jax.ShapeDtypeStruct((m, n), F32), jax.ShapeDtypeStruct((m, n), BF16)],
        scratch_shapes=[pltpu.VMEM((tm, k), BF16)],
        compiler_params=_params("parallel", "arbitrary"),
        name="qkv_proj",
    )(x1, x2, w)


def _kmean_body(k_ref, o_ref):
    i = pl.program_id(0)

    @pl.when(i == 0)
    def _():
        o_ref[...] = jnp.zeros_like(o_ref)

    mean = jnp.sum(k_ref[...], axis=0, keepdims=True) * (1.0 / MOBA_BLOCK)
    rows = lax.broadcasted_iota(jnp.int32, o_ref.shape, 0)
    o_ref[...] = jnp.where(rows == i, mean, o_ref[...])


def _block_means(qkv_f32, n_blocks):
    dkv = N_KV_HEADS * HEAD_DIM
    kcol = (N_HEADS * HEAD_DIM) // dkv
    return pl.pallas_call(
        _kmean_body,
        grid=(n_blocks,),
        in_specs=[pl.BlockSpec((MOBA_BLOCK, dkv), lambda i: (i, kcol))],
        out_specs=pl.BlockSpec((LANES, dkv), lambda i: (0, 0)),
        out_shape=jax.ShapeDtypeStruct((LANES, dkv), F32),
        compiler_params=_params("arbitrary"),
        name="moba_block_means",
    )(qkv_f32)


def _t5_bucket_np(dist):
    n = np.maximum(dist, 0)
    max_exact = NUM_BUCKETS // 2
    nf = np.maximum(n, 1).astype(np.float32)
    large = max_exact + (np.log(nf / np.float32(max_exact)) / np.float32(math.log(MAX_DISTANCE / max_exact))
                         * np.float32(NUM_BUCKETS - max_exact)).astype(np.int32)
    large = np.minimum(large, NUM_BUCKETS - 1)
    return np.where(n < max_exact, n, large).astype(np.int32)


def _bucket_tables():
    q = np.arange(MOBA_BLOCK)[:, None]
    k = np.arange(MOBA_BLOCK)[None, :]
    own = np.where(q - k >= 0, _t5_bucket_np(q - k), -1)
    prev = _t5_bucket_np(q - k + MOBA_BLOCK)
    return np.stack([own, prev]).astype(np.int32)


def _bias_body(rb_ref, idx_ref, o_ref, os_ref):
    h = pl.program_id(0)
    far = rb_ref[h, NUM_BUCKETS - 1]
    tabs = []
    for m in range(2):
        idx = idx_ref[m]
        acc = jnp.full(idx.shape, NEG_INF, F32)
        for b in range(NUM_BUCKETS):
            acc = jnp.where(idx == b, rb_ref[h, b] - far, acc)
        o_ref[0, m] = acc
        tabs.append(acc)
    os_ref[0] = jnp.concatenate([tabs[1][:SUBLANES, :], tabs[0][:SUBLANES, :LANES]], axis=1)


def _bias_tables(rel_bias):
    idx = jnp.asarray(_bucket_tables())
    return pl.pallas_call(
        _bias_body,
        grid=(N_HEADS,),
        in_specs=[pl.BlockSpec(memory_space=pltpu.SMEM),
                  pl.BlockSpec((2, MOBA_BLOCK, MOBA_BLOCK), lambda h: (0, 0, 0))],
        out_specs=[pl.BlockSpec((1, 2, MOBA_BLOCK, MOBA_BLOCK), lambda h: (h, 0, 0, 0)),
                   pl.BlockSpec((1, SUBLANES, MOBA_BLOCK + LANES), lambda h: (h, 0, 0))],
        out_shape=[jax.ShapeDtypeStruct((N_HEADS, 2, MOBA_BLOCK, MOBA_BLOCK), F32),
                   jax.ShapeDtypeStruct((N_HEADS, SUBLANES, MOBA_BLOCK + LANES), F32)],
        compiler_params=_params("parallel"),
        name="moba_bias_tables",
    )(rel_bias, idx)


def _select_blocks(gate, n_past):
    lane = lax.broadcasted_iota(jnp.int32, gate.shape, 1)
    lane_f = lane.astype(F32)
    g = jnp.where(lane < n_past, gate, NEG_INF)
    sel = jnp.full(gate.shape, NEG_INF, F32)
    for _ in range(MOBA_TOPK):
        mx = jnp.max(g, axis=1, keepdims=True)
        idx = jnp.min(jnp.where(g == mx, lane_f, float(gate.shape[1])), axis=1, keepdims=True)
        pick = lane_f == idx
        sel = jnp.where(pick, 0.0, sel)
        g = jnp.where(pick, -jnp.inf, g)
    return jnp.where(lane < n_past, sel, 0.0)


def _attn_prompt_body(qf_ref, qb_ref, k_ref, v_ref, km_ref, tab_ref, o_ref, qa_scr, m_scr, l_scr, acc_scr):
    i = pl.program_id(1)
    rows = HEADS_PER_KV * MOBA_BLOCK
    km = km_ref[...]
    lane = lax.broadcasted_iota(jnp.int32, (MOBA_BLOCK, LANES), 1)
    dummy = LANES - 1
    for hh in range(HEADS_PER_KV):
        cs = slice(hh * HEAD_DIM, (hh + 1) * HEAD_DIM)
        rs = slice(hh * MOBA_BLOCK, (hh + 1) * MOBA_BLOCK)
        gate = lax.dot_general(qf_ref[:, cs], km, _NT, precision=HIGHEST, preferred_element_type=F32)
        mask = jnp.where(lane == dummy, NEG_INF, _select_blocks(gate, i))
        qa_scr[rs, :HEAD_DIM] = qb_ref[:, cs]
        qa_scr[rs, HEAD_DIM:] = mask.astype(BF16)
    qa = qa_scr[...]

    def scores(j, mask_lane):
        start = pl.multiple_of(j * MOBA_BLOCK, MOBA_BLOCK)
        kj = k_ref[pl.ds(start, MOBA_BLOCK), :]
        vj = v_ref[pl.ds(start, MOBA_BLOCK), :]
        rhs = jnp.concatenate([kj, (lane == mask_lane).astype(BF16)], axis=1)
        return lax.dot_general(qa, rhs, _NT, preferred_element_type=F32), vj

    def softmax_parts(parts, m_new):
        m2 = jnp.concatenate([m_new, m_new], axis=1)
        l_add = None
        acc_add = None
        for s, vj in parts:
            p = jnp.exp(s - m2)
            ls = jnp.sum(p, axis=1, keepdims=True)
            pv = _dot(p.astype(BF16), vj)
            l_add = ls if l_add is None else l_add + ls
            acc_add = pv if acc_add is None else acc_add + pv
        return l_add, acc_add

    has_prev = i >= 1
    s_own, v_own = scores(i, -1)
    s_own = s_own + tab_ref[:, 0].reshape(rows, MOBA_BLOCK)
    s_prev, v_prev = scores(jnp.maximum(i - 1, 0), jnp.where(has_prev, i - 1, dummy))
    s_prev = s_prev + tab_ref[:, 1].reshape(rows, MOBA_BLOCK)
    m0 = jnp.maximum(jnp.max(s_own, axis=1, keepdims=True), jnp.max(s_prev, axis=1, keepdims=True))
    m0 = jnp.broadcast_to(m0, (rows, LANES))
    l0, acc0 = softmax_parts([(s_own, v_own), (s_prev, v_prev)], m0)
    m_scr[...] = m0
    l_scr[...] = jnp.broadcast_to(l0, (rows, LANES))
    acc_scr[...] = acc0

    n_far = jnp.maximum(i - 1, 0)

    def far_pair(t, carry):
        j0 = 2 * t
        j1 = j0 + 1
        ok1 = j1 < n_far
        parts = [scores(j0, j0), scores(jnp.where(ok1, j1, 0), jnp.where(ok1, j1, dummy))]
        m_prev = m_scr[...]
        m_new = m_prev
        for s, _ in parts:
            m_new = jnp.maximum(m_new, jnp.max(s, axis=1, keepdims=True))
        alpha = jnp.exp(m_prev - m_new)
        l_add, acc_add = softmax_parts(parts, m_new)
        l_scr[...] = alpha * l_scr[...] + l_add
        acc_scr[...] = alpha * acc_scr[...] + acc_add
        m_scr[...] = m_new
        return carry

    lax.fori_loop(0, (n_far + 1) // 2, far_pair, 0)

    out = acc_scr[...] / l_scr[...]
    for hh in range(HEADS_PER_KV):
        o_ref[:, hh * HEAD_DIM:(hh + 1) * HEAD_DIM] = out[hh * MOBA_BLOCK:(hh + 1) * MOBA_BLOCK]


def _attn_sample_core(first_step, qkv_ref, kp, vp, tab_ref, o_ref, kall, vall, expand, page, dec_seq):
    n_pages = len(kp)
    past = n_pages * page
    n_past_blocks = past // MOBA_BLOCK
    pages_per_block = MOBA_BLOCK // page
    near0 = past - MOBA_BLOCK
    total = past + LANES
    dq = N_HEADS * HEAD_DIM
    dkv = N_KV_HEADS * HEAD_DIM
    hq = HEADS_PER_KV * dec_seq
    rows = N_HEADS * dec_seq
    pad = jnp.zeros((LANES - dec_seq, HEAD_DIM), F32)

    @pl.when(first_step)
    def _():
        key_blk = lax.broadcasted_iota(jnp.int32, (LANES, total), 1) // MOBA_BLOCK
        expand[...] = (key_blk == lax.broadcasted_iota(jnp.int32, (LANES, total), 0)).astype(BF16)

    kmeans = []
    for g in range(N_KV_HEADS):
        means = []
        for blk in range(n_past_blocks):
            tot = jnp.zeros((1, HEAD_DIM), F32)
            for pp in range(pages_per_block):
                pg = blk * pages_per_block + pp
                kk = kp[pg][pl.ds(g, page, stride=N_KV_HEADS), :]
                vv = vp[pg][pl.ds(g, page, stride=N_KV_HEADS), :]
                tot = tot + jnp.sum(kk, axis=0, keepdims=True)
                kall[g, pg * page:(pg + 1) * page, :] = kk.astype(BF16)
                vall[g, pg * page:(pg + 1) * page, :] = vv.astype(BF16)
            means.append(tot * (1.0 / MOBA_BLOCK))
        kmeans.append(jnp.concatenate(means + [jnp.zeros((LANES - n_past_blocks, HEAD_DIM), F32)], axis=0))
        knew = qkv_ref[:, dq + g * HEAD_DIM:dq + (g + 1) * HEAD_DIM]
        vnew = qkv_ref[:, dq + dkv + g * HEAD_DIM:dq + dkv + (g + 1) * HEAD_DIM]
        kall[g, past:total, :] = jnp.concatenate([knew, pad], axis=0).astype(BF16)
        vall[g, past:total, :] = jnp.concatenate([vnew, pad], axis=0).astype(BF16)

    qs = jnp.concatenate([qkv_ref[:, h * HEAD_DIM:(h + 1) * HEAD_DIM] for h in range(N_HEADS)], axis=0)
    gate_all = lax.dot_general(qs, jnp.concatenate(kmeans, axis=0), _NT, precision=HIGHEST,
                               preferred_element_type=F32)
    row_g = lax.broadcasted_iota(jnp.int32, (rows, LANES), 0) // hq
    gate = gate_all[:, :LANES]
    for g in range(1, N_KV_HEADS):
        gate = jnp.where(row_g == g, gate_all[:, g * LANES:(g + 1) * LANES], gate)
    selm = _select_blocks(gate, n_past_blocks).astype(BF16)

    qb = (qs * (HEAD_DIM ** -0.5)).astype(BF16)
    s = jnp.concatenate([lax.dot_general(qb[g * hq:(g + 1) * hq], kall[g], _NT, preferred_element_type=F32)
                         for g in range(N_KV_HEADS)], axis=0)
    s = s + _dot(selm, expand[...])
    s_far = s[:, :near0]
    s_near = s[:, near0:] + tab_ref[...].reshape(rows, MOBA_BLOCK + LANES)
    m = jnp.maximum(jnp.max(s_far, axis=1, keepdims=True), jnp.max(s_near, axis=1, keepdims=True))
    p_far = jnp.exp(s_far - m)
    p_near = jnp.exp(s_near - m)
    inv_l = 1.0 / (jnp.sum(p_far, axis=1, keepdims=True) + jnp.sum(p_near, axis=1, keepdims=True))
    p_far = p_far.astype(BF16)
    p_near = p_near.astype(BF16)
    for g in range(N_KV_HEADS):
        rs = slice(g * hq, (g + 1) * hq)
        out = (_dot(p_far[rs], vall[g, :near0, :]) + _dot(p_near[rs], vall[g, near0:, :])) * inv_l[rs]
        for hh in range(HEADS_PER_KV):
            c0 = (g * HEADS_PER_KV + hh) * HEAD_DIM
            o_ref[:, c0:c0 + HEAD_DIM] = out[hh * dec_seq:(hh + 1) * dec_seq]


def _attn_body(pt_ref, *refs, n_pages, page, dec_seq):
    del pt_ref
    n_prompt_in = 6
    n_sample_in = 2 + 2 * n_pages
    prompt_in = refs[:n_prompt_in]
    qkv_ref, *pages, tab_ref = refs[n_prompt_in:n_prompt_in + n_sample_in]
    op_ref, os_ref = refs[n_prompt_in + n_sample_in:n_prompt_in + n_sample_in + 2]
    scr = refs[n_prompt_in + n_sample_in + 2:]
    _attn_prompt_body(*prompt_in, op_ref, *scr[:4])
    first_step = (pl.program_id(0) == 0) & (pl.program_id(1) == 0)
    _attn_sample_core(first_step, qkv_ref, pages[:n_pages], pages[n_pages:], tab_ref, os_ref, *scr[4:],
                      page, dec_seq)


def _attention(page_table, qkv_f32, qkv_b, kmean, tabs, tab_s, cache_k, cache_v, seq, dec_seq):
    nblk = seq // MOBA_BLOCK
    assert nblk < LANES - 1
    dec_batch, n_pages = page_table.shape
    assert dec_batch == N_KV_HEADS * nblk
    page = cache_k.shape[1] // N_KV_HEADS
    total = n_pages * page + LANES
    blk0 = seq // dec_seq
    gw = HEADS_PER_KV * HEAD_DIM
    kcol = (N_HEADS * HEAD_DIM) // HEAD_DIM
    vcol = kcol + N_KV_HEADS
    rows = HEADS_PER_KV * MOBA_BLOCK

    def page_spec(p):
        return pl.BlockSpec((None, page * N_KV_HEADS, HEAD_DIM),
                            lambda g, i, pt, p=p: (pt[g * nblk + i, p], 0, 0))

    grid_spec = pltpu.PrefetchScalarGridSpec(
        num_scalar_prefetch=1,
        grid=(N_KV_HEADS, nblk),
        in_specs=([pl.BlockSpec((MOBA_BLOCK, gw), lambda g, i, pt: (i, g)),
                   pl.BlockSpec((MOBA_BLOCK, gw), lambda g, i, pt: (i, g)),
                   pl.BlockSpec((seq, HEAD_DIM), lambda g, i, pt: (0, kcol + g)),
                   pl.BlockSpec((seq, HEAD_DIM), lambda g, i, pt: (0, vcol + g)),
                   pl.BlockSpec((LANES, HEAD_DIM), lambda g, i, pt: (0, g)),
                   pl.BlockSpec((HEADS_PER_KV, 2, MOBA_BLOCK, MOBA_BLOCK), lambda g, i, pt: (g, 0, 0, 0)),
                   pl.BlockSpec((dec_seq, qkv_f32.shape[1]), lambda g, i, pt: (blk0 + g * nblk + i, 0))]
                  + [page_spec(p) for p in range(n_pages)]
                  + [page_spec(p) for p in range(n_pages)]
                  + [pl.BlockSpec(tab_s.shape, lambda g, i, pt: (0, 0, 0))]),
        out_specs=[pl.BlockSpec((MOBA_BLOCK, gw), lambda g, i, pt: (i, g)),
                   pl.BlockSpec((dec_seq, N_HEADS * HEAD_DIM), lambda g, i, pt: (g * nblk + i, 0))],
        scratch_shapes=[pltpu.VMEM((rows, 2 * HEAD_DIM), BF16),
                        pltpu.VMEM((rows, LANES), F32),
                        pltpu.VMEM((rows, LANES), F32),
                        pltpu.VMEM((rows, HEAD_DIM), F32),
                        pltpu.VMEM((N_KV_HEADS, total, HEAD_DIM), BF16),
                        pltpu.VMEM((N_KV_HEADS, total, HEAD_DIM), BF16),
                        pltpu.VMEM((LANES, total), BF16)],
    )
    return pl.pallas_call(
        functools.partial(_attn_body, n_pages=n_pages, page=page, dec_seq=dec_seq),
        grid_spec=grid_spec,
        out_shape=[jax.ShapeDtypeStruct((seq, N_HEADS * HEAD_DIM), F32),
                   jax.ShapeDtypeStruct((dec_batch * dec_seq, N_HEADS * HEAD_DIM), F32)],
        compiler_params=_params("arbitrary", "arbitrary"),
        name="moba_attention",
    )(page_table, qkv_f32, qkv_b, qkv_b, qkv_b, kmean, tabs, qkv_f32,
      *([cache_k] * n_pages), *([cache_v] * n_pages), tab_s)


def _pack_bf16_halves(x):
    c = x.shape[1] // 2
    lo = lax.bitcast_convert_type(x[:, :c].astype(BF16).astype(F32), jnp.uint32)
    hi = lax.bitcast_convert_type(x[:, c:].astype(BF16).astype(F32), jnp.uint32)
    return hi | (lo >> 16)


def _unpack_bf16_halves(u):
    lo = lax.bitcast_convert_type(u << 16, F32)
    hi = lax.bitcast_convert_type(u & jnp.uint32(0xFFFF0000), F32)
    return jnp.concatenate([lo, hi], axis=1)


def _proj_ln_body(*refs, n_w, n_x, n_tiles, n_first, packed):
    a1_ref, a2_ref = refs[:2]
    w_refs = refs[2:2 + n_w]
    x_refs = refs[2 + n_w:2 + n_w + n_x]
    if packed:
        g_ref, b_ref, o_ref, op_ref, a_scr, z_scr = refs[2 + n_w + n_x:]
    else:
        g_ref, b_ref, o_ref, a_scr, z_scr = refs[2 + n_w + n_x:]
    i = pl.program_id(0)
    j = pl.program_id(1)
    _load_split_bf16(a_scr, a1_ref, a2_ref, n_first)
    a = a_scr[...]
    z = _dot(a, w_refs[0][...])
    if n_w == 2:
        z = z * _sigmoid(_dot(a, w_refs[1][...]))
    z_scr[j] = z

    @pl.when(j == n_tiles - 1)
    def _():
        zfull = jnp.concatenate([z_scr[t] for t in range(n_tiles)], axis=1)
        x = x_refs[0][...] if n_x == 1 else jnp.where(i < n_first, x_refs[0][...], x_refs[1][...])
        out = _layer_norm(ALPHA * x + zfull, g_ref[...], b_ref[...])
        o_ref[...] = out
        if packed:
            op_ref[...] = _pack_bf16_halves(out)


def _proj_ln(a1, a2, ws, xs, g, b, tm, tn, packed=False):
    k = a1.shape[1]
    m = a1.shape[0] + a2.shape[0]
    n = ws[0].shape[1]
    n_first = a1.shape[0] // tm
    assert a1.shape[0] % tm == 0 and a2.shape[0] % tm == 0
    assert len(xs) == 1 or xs[0].shape[0] == a1.shape[0]
    n_tiles = n // tn
    body = functools.partial(_proj_ln_body, n_w=len(ws), n_x=len(xs), n_tiles=n_tiles, n_first=n_first,
                             packed=packed)
    row = pl.BlockSpec((tm, n), lambda i, j: (i, 0))
    vec = pl.BlockSpec((1, n), lambda i, j: (0, 0))
    out_specs, out_shape = row, jax.ShapeDtypeStruct((m, n), F32)
    if packed:
        out_specs = [row, pl.BlockSpec((tm, n // 2), lambda i, j: (i, 0))]
        out_shape = [out_shape, jax.ShapeDtypeStruct((m, n // 2), jnp.uint32)]
    return pl.pallas_call(
        body,
        grid=(m // tm, n_tiles),
        in_specs=(_split_rows((tm, k), n_first)
                  + [pl.BlockSpec((k, tn), lambda i, j: (0, j)) for _ in ws]
                  + ([row] if len(xs) == 1 else _split_rows((tm, n), n_first))
                  + [vec, vec]),
        out_specs=out_specs,
        out_shape=out_shape,
        scratch_shapes=[pltpu.VMEM((tm, k), BF16), pltpu.VMEM((n_tiles, tm, tn), F32)],
        compiler_params=_params("parallel", "arbitrary"),
        name="glu_res_ln" if len(ws) == 2 else "proj_res_ln",
    )(a1, a2, *ws, *xs, g.reshape(1, n), b.reshape(1, n))


def _ffn_body(te_ref, tr_ref, *refs, n_f, dense, half):
    del te_ref
    if dense:
        x_ref, wg_ref, wu_ref, wd_ref, g_ref, b_ref, o_ref, xb = refs
        acc = o_ref
    else:
        x_ref, wg_ref, wu_ref, wd_ref, o_ref, xb, acc = refs
    tm = x_ref.shape[0]
    t = pl.program_id(0)
    f = pl.program_id(1)
    n_rows = tr_ref[t]

    @pl.when(f == 0)
    def _():
        x = x_ref[...] if dense else _unpack_bf16_halves(x_ref[...])
        xb[...] = x.astype(BF16)
        acc[...] = jnp.zeros_like(acc)

    def swiglu_rows(n):
        x = xb[:n, :]
        hg = _dot(x, wg_ref[...].astype(BF16))
        hu = _dot(x, wu_ref[...].astype(BF16))
        h = (hg * _sigmoid(hg)) * hu
        acc[:n, :] += _dot(h.astype(BF16), wd_ref[...].astype(BF16))

    for n in range(half, tm + 1, half):
        pl.when((n_rows > n - half) & (n_rows <= n))(functools.partial(swiglu_rows, n))

    @pl.when(f == n_f - 1)
    def _():
        if dense:
            o_ref[...] = _layer_norm(ALPHA * x_ref[...] + acc[...], g_ref[...], b_ref[...])
        else:
            o_ref[...] = _pack_bf16_halves(acc[...])


def _ffn_call(tile_expert, tile_rows, x, wg, wu, wd, extra, extra_specs, tm, tf, half, dense, name):
    s = x.shape[0]
    d = wg.shape[-2]
    n_t = s // tm
    n_f = wg.shape[-1] // tf
    live = lambda t, tr: jnp.minimum(tr[t], 1)
    x_spec = pl.BlockSpec((tm, x.shape[1]), lambda t, f, te, tr: (t, 0))
    wgu = pl.BlockSpec((None, d, tf), lambda t, f, te, tr: (te[t], 0, f * live(t, tr)))
    wds = pl.BlockSpec((None, tf, d), lambda t, f, te, tr: (te[t], f * live(t, tr), 0))
    grid_spec = pltpu.PrefetchScalarGridSpec(
        num_scalar_prefetch=2, grid=(n_t, n_f),
        in_specs=[x_spec, wgu, wgu, wds] + extra_specs,
        out_specs=x_spec,
        scratch_shapes=[pltpu.VMEM((tm, d), BF16)] + ([] if dense else [pltpu.VMEM((tm, d), F32)]))
    return pl.pallas_call(
        functools.partial(_ffn_body, n_f=n_f, dense=dense, half=half),
        grid_spec=grid_spec,
        out_shape=jax.ShapeDtypeStruct(x.shape, x.dtype),
        compiler_params=_params("parallel", "arbitrary"),
        name=name,
    )(tile_expert, tile_rows, x, wg, wu, wd, *extra)


def _ffn_dense(x, wg, wu, wd, g, b, tm, tf):
    m, d = x.shape
    n_t = m // tm
    vec = pl.BlockSpec((1, d), lambda t, f, te, tr: (0, 0))
    return _ffn_call(jnp.zeros((n_t,), jnp.int32), jnp.full((n_t,), tm, jnp.int32), x, wg, wu, wd,
                     [g.reshape(1, d), b.reshape(1, d)], [vec, vec], tm, tf, tm, True, "ffn_res_ln")


def _ffn_moe(tile_expert, tile_rows, xs, wg, wu, wd, tm, tf, half):
    return _ffn_call(tile_expert, tile_rows, xs, wg, wu, wd, [], [], tm, tf, half, False, "moe_ffn")


def _router_body(y_ref, w_ref, i_ref, g_ref):
    logits = jnp.dot(y_ref[...], w_ref[...], precision=HIGHEST, preferred_element_type=F32)
    lane = lax.broadcasted_iota(jnp.int32, logits.shape, 1)
    lane_f = lane.astype(F32)
    l1 = jnp.where(lane < N_EXPERTS, logits, -jnp.inf)
    m1 = jnp.max(l1, axis=1, keepdims=True)
    i1 = jnp.min(jnp.where(l1 == m1, lane_f, float(LANES)), axis=1, keepdims=True)
    l2 = jnp.where(lane_f == i1, -jnp.inf, l1)
    m2 = jnp.max(l2, axis=1, keepdims=True)
    i2 = jnp.min(jnp.where(l2 == m2, lane_f, float(LANES)), axis=1, keepdims=True)
    e = jnp.exp(m2 - m1)
    g1 = 1.0 / (1.0 + e)
    g2 = e / (1.0 + e)
    i_ref[...] = jnp.where(lane == 0, i1, jnp.where(lane == 1, i2, 0.0)).astype(jnp.int32)
    g_ref[...] = jnp.where(lane == 0, g1, jnp.where(lane == 1, g2, 0.0))


def _router(y, w_router, tm):
    m, d = y.shape
    wr = jnp.zeros((d, LANES), F32).at[:, :N_EXPERTS].set(w_router)
    row = pl.BlockSpec((tm, LANES), lambda i: (i, 0))
    return pl.pallas_call(
        _router_body,
        grid=(m // tm,),
        in_specs=[pl.BlockSpec((tm, d), lambda i: (i, 0)), pl.BlockSpec((d, LANES), lambda i: (0, 0))],
        out_specs=[row, row],
        out_shape=[jax.ShapeDtypeStruct((m, LANES), jnp.int32), jax.ShapeDtypeStruct((m, LANES), F32)],
        compiler_params=_params("parallel"),
        name="moe_router",
    )(y, wr)


def _combine_ln_body(x_ref, a_ref, b2_ref, tg_ref, g_ref, b_ref, op_ref, os_ref, *, n_first):
    i = pl.program_id(0)
    tg = tg_ref[...]
    moe = tg[:, 0:1] * _unpack_bf16_halves(a_ref[...]) + tg[:, 1:2] * _unpack_bf16_halves(b2_ref[...])
    out = _layer_norm(ALPHA * x_ref[...] + moe, g_ref[...], b_ref[...])

    @pl.when(i < n_first)
    def _():
        op_ref[...] = out

    @pl.when(i >= n_first)
    def _():
        os_ref[...] = out


def _combine_ln(x, pairs, top_gate, g, b, m_first, tm):
    m, d = x.shape
    dp = pairs.shape[2]
    n_first = m_first // tm
    vec = pl.BlockSpec((1, d), lambda i: (0, 0))
    return pl.pallas_call(
        functools.partial(_combine_ln_body, n_first=n_first),
        grid=(m // tm,),
        in_specs=[pl.BlockSpec((tm, d), lambda i: (i, 0)),
                  pl.BlockSpec((None, tm, dp), lambda i: (0, i, 0)),
                  pl.BlockSpec((None, tm, dp), lambda i: (1, i, 0)),
                  pl.BlockSpec((tm, LANES), lambda i: (i, 0)),
                  vec, vec],
        out_specs=[pl.BlockSpec((tm, d), lambda i: (jnp.minimum(i, n_first - 1), 0)),
                   pl.BlockSpec((tm, d), lambda i: (jnp.maximum(i - n_first, 0), 0))],
        out_shape=[jax.ShapeDtypeStruct((m_first, d), F32), jax.ShapeDtypeStruct((m - m_first, d), F32)],
        compiler_params=_params("arbitrary"),
        name="moe_combine_ln",
    )(x, pairs, pairs, top_gate, g.reshape(1, d), b.reshape(1, d))


def _moe_routing(top_idx, tm):
    m = top_idx.shape[0]
    n_pairs = m * TOP_K
    n_tiles = n_pairs // tm + N_EXPERTS
    i32 = jnp.int32
    e_flat = jnp.concatenate([top_idx[:, k] for k in range(TOP_K)])
    order = jnp.argsort(e_flat, stable=True).astype(i32)
    inv = jnp.argsort(order).astype(i32)
    counts = jnp.sum((e_flat[None, :] == jnp.arange(N_EXPERTS, dtype=i32)[:, None]).astype(i32), axis=1)
    padded = ((counts + tm - 1) // tm) * tm
    ends_padded = jnp.cumsum(padded)
    starts_padded = ends_padded - padded
    starts = jnp.cumsum(counts) - counts
    tile_start = jnp.arange(n_tiles, dtype=i32) * tm
    tile_expert = jnp.minimum(jnp.searchsorted(ends_padded, tile_start, side="right"), N_EXPERTS - 1).astype(i32)
    tile_rows = jnp.clip(counts[tile_expert] - (tile_start - starts_padded[tile_expert]), 0, tm).astype(i32)
    slot = jnp.arange(n_tiles * tm, dtype=i32)
    slot_e = jnp.repeat(tile_expert, tm)
    rank = slot - starts_padded[slot_e]
    live = (rank >= 0) & (rank < counts[slot_e])
    src = order[jnp.clip(starts[slot_e] + rank, 0, n_pairs - 1)]
    slot_token = jnp.where(live, src % m, slot % m)
    pair_slot = starts_padded[e_flat] + inv - starts[e_flat]
    return tile_expert, tile_rows, slot_token, pair_slot


def _gelu_tanh(x):
    return 0.5 * x * (1.0 + jnp.tanh(math.sqrt(2.0 / math.pi) * (x + 0.044715 * (x * x * x))))


def _ssm_body(*refs, sequential):
    (u_ref, tz_ref, wb_ref, vc_ref, a8r_ref, a8i_ref, d_ref, h0r_ref, h0i_ref, y_ref, hr_ref, hi_ref) = refs[:12]
    scr = refs[12:]
    n_chunks = u_ref.shape[0] // SSM_CHUNK
    sw = a8r_ref.shape[1]
    step_rows = lambda s: pl.ds(s, n_chunks, stride=SSM_CHUNK)
    ucat = jnp.concatenate([u_ref[step_rows(s), :].astype(BF16) for s in range(SSM_CHUNK)], axis=1)
    hl = _dot(ucat, wb_ref[...])
    hl_r = hl[:, :sw]
    hl_i = hl[:, sw:]
    a8r = a8r_ref[...]
    a8i = a8i_ref[...]
    if sequential:
        hlr_scr, hli_scr, hinr_scr, hini_scr = scr
        hlr_scr[...] = hl_r
        hli_scr[...] = hl_i

        def tile_step(kb, carry):
            hr, hi = carry
            base = pl.multiple_of(kb * SUBLANES, SUBLANES)
            tr = hlr_scr[pl.ds(base, SUBLANES), :]
            ti = hli_scr[pl.ds(base, SUBLANES), :]
            rows_r, rows_i = [], []
            for r in range(SUBLANES):
                rows_r.append(hr)
                rows_i.append(hi)
                hr, hi = (a8r * hr - a8i * hi + tr[r:r + 1], a8r * hi + a8i * hr + ti[r:r + 1])
            hinr_scr[pl.ds(base, SUBLANES), :] = jnp.concatenate(rows_r, axis=0)
            hini_scr[pl.ds(base, SUBLANES), :] = jnp.concatenate(rows_i, axis=0)
            return hr, hi

        hr, hi = lax.fori_loop(0, n_chunks // SUBLANES, tile_step, (h0r_ref[...], h0i_ref[...]))
        hr_ref[...] = hr
        hi_ref[...] = hi
        hin_r = hinr_scr[...]
        hin_i = hini_scr[...]
    else:
        gpb = h0r_ref.shape[1]
        p_n = h0r_ref.shape[2]
        hin_r = jnp.concatenate([h0r_ref[:, g, :] for g in range(gpb)], axis=1)
        hin_i = jnp.concatenate([h0i_ref[:, g, :] for g in range(gpb)], axis=1)
        hr = a8r * hin_r - a8i * hin_i + hl_r
        hi = a8r * hin_i + a8i * hin_r + hl_i
        for g in range(gpb):
            hr_ref[:, g, :] = hr[:, g * p_n:(g + 1) * p_n]
            hi_ref[:, g, :] = hi[:, g * p_n:(g + 1) * p_n]
    hcat = jnp.concatenate([hin_r.astype(BF16), hin_i.astype(BF16)], axis=1)
    d = d_ref[...]
    steps_per_dot = 2
    for t0 in range(0, SSM_CHUNK, steps_per_dot):
        cols = slice(t0 * LANES, (t0 + steps_per_dot) * LANES)
        ks = t0 + steps_per_dot
        k = ks * LANES
        y2 = _dot(ucat[:, :k], tz_ref[:k, cols]) + _dot(hcat, vc_ref[:, cols])
        for t in range(t0, t0 + steps_per_dot):
            y = y2[:, (t - t0) * LANES:(t - t0 + 1) * LANES]
            y_ref[step_rows(t), :] = _gelu_tanh(y + d * u_ref[step_rows(t), :])


GROUPS_PER_BLOCK = LANES // SSM_GROUP


def _spread_groups(x, rows_per_group):
    r, w = x.shape
    wl = GROUPS_PER_BLOCK * w
    src = lax.broadcasted_iota(jnp.int32, (w, wl), 0)
    dst = lax.broadcasted_iota(jnp.int32, (w, wl), 1)
    tiled = _dot(x.astype(BF16), (dst % w == src).astype(BF16))
    row_g = (lax.broadcasted_iota(jnp.int32, (r, wl), 0) // rows_per_group) % GROUPS_PER_BLOCK
    lane_g = lax.broadcasted_iota(jnp.int32, (r, wl), 1) // w
    return jnp.where(row_g == lane_g, tiled, 0.0).astype(BF16)


def _ssm_ops_body(conv_ref, abr_ref, abi_ref, vcr_ref, vci_ref, tz_ref, wb_ref, vc_ref):
    t_n = conv_ref.shape[0]
    sw = vcr_ref.shape[1]
    bd = [_spread_groups(conv_ref[tau], SSM_GROUP) for tau in range(t_n)]
    zero = jnp.zeros_like(bd[0])
    for s in range(t_n):
        rs = slice(s * LANES, (s + 1) * LANES)
        tz_ref[rs, :] = jnp.concatenate([bd[t - s] if t >= s else zero for t in range(t_n)], axis=1)
        wb_ref[rs, :] = jnp.concatenate([_spread_groups(abr_ref[t_n - 1 - s], SSM_GROUP),
                                         _spread_groups(abi_ref[t_n - 1 - s], SSM_GROUP)], axis=1)
    for t in range(t_n):
        cs = slice(t * LANES, (t + 1) * LANES)
        vc_ref[:sw, cs] = _spread_groups(vcr_ref[t], SSM_STATE)
        vc_ref[sw:, cs] = _spread_groups(vci_ref[t], SSM_STATE)


def _ssm_operators(a_re, a_im, log_dt, b_re, b_im, c_re, c_im):
    g_n, p_n = a_re.shape
    lam_re = jnp.minimum(a_re, -1e-4)
    lam_im = a_im
    dt = jnp.exp(log_dt)[:, None]
    decay = jnp.exp(lam_re * dt)
    ar = decay * jnp.cos(lam_im * dt)
    ai = decay * jnp.sin(lam_im * dt)
    num_re = ar - 1.0
    den = lam_re * lam_re + lam_im * lam_im
    f_re = (num_re * lam_re + ai * lam_im) / den
    f_im = (ai * lam_re - num_re * lam_im) / den
    bt_re = jnp.swapaxes(b_re, 1, 2)
    bt_im = jnp.swapaxes(b_im, 1, 2)
    bb_re = f_re[:, None, :] * bt_re - f_im[:, None, :] * bt_im
    bb_im = f_re[:, None, :] * bt_im + f_im[:, None, :] * bt_re
    pw_re, pw_im = [jnp.ones_like(ar)], [jnp.zeros_like(ar)]
    for _ in range(SSM_CHUNK):
        pr, pi = pw_re[-1], pw_im[-1]
        pw_re.append(pr * ar - pi * ai)
        pw_im.append(pr * ai + pi * ar)
    pw_re = jnp.stack(pw_re)
    pw_im = jnp.stack(pw_im)
    pk_re = pw_re[:SSM_CHUNK, :, None, :]
    pk_im = pw_im[:SSM_CHUNK, :, None, :]
    ab_re = pk_re * bb_re - pk_im * bb_im
    ab_im = pk_re * bb_im + pk_im * bb_re
    conv = (jnp.einsum("gcp,tgdp->tgdc", c_re, ab_re, precision=HIGHEST)
            - jnp.einsum("gcp,tgdp->tgdc", c_im, ab_im, precision=HIGHEST))
    gpb = GROUPS_PER_BLOCK
    nb = g_n // gpb
    t_n = SSM_CHUNK
    pr = pw_re[1:, :, :, None]
    pi = pw_im[1:, :, :, None]
    ct_re = jnp.swapaxes(c_re, 1, 2)[None]
    ct_im = jnp.swapaxes(c_im, 1, 2)[None]
    tables = [conv.reshape(t_n, nb, LANES, SSM_GROUP),
              ab_re.reshape(t_n, nb, LANES, p_n), ab_im.reshape(t_n, nb, LANES, p_n),
              (ct_re * pr - ct_im * pi).reshape(t_n, nb, gpb * p_n, SSM_GROUP),
              (-(ct_re * pi + ct_im * pr)).reshape(t_n, nb, gpb * p_n, SSM_GROUP)]
    side = t_n * LANES
    op_spec = pl.BlockSpec((None, side, side), lambda j: (j, 0, 0))
    tz, wb, vc = pl.pallas_call(
        _ssm_ops_body,
        grid=(nb,),
        in_specs=[pl.BlockSpec((t_n, None) + t.shape[2:], lambda j: (0, j, 0, 0)) for t in tables],
        out_specs=[op_spec] * 3,
        out_shape=[jax.ShapeDtypeStruct((nb, side, side), BF16)] * 3,
        compiler_params=_params("parallel"),
        name="s5_chunk_operators",
    )(*tables)
    a8r = pw_re[SSM_CHUNK].reshape(1, g_n * p_n)
    a8i = pw_im[SSM_CHUNK].reshape(1, g_n * p_n)
    return tz, wb, vc, a8r, a8i


def _ssm_scan(u, ops, d_skip, h0_re, h0_im, chunk0, n_chunks, sequential):
    tz, wb, vc, a8r, a8i = ops
    nb = tz.shape[0]
    sw = wb.shape[-1] // 2
    rows = n_chunks * SSM_CHUNK
    cblk = chunk0 // n_chunks
    op3 = lambda a: pl.BlockSpec((None,) + a.shape[1:], lambda j: (j, 0, 0))
    srow = pl.BlockSpec((1, sw), lambda j: (0, j))
    if sequential:
        hspec = srow
        scratch = [pltpu.VMEM((n_chunks, sw), F32)] * 4
    else:
        hspec = pl.BlockSpec((n_chunks, sw // SSM_STATE, SSM_STATE), lambda j: (0, j, 0))
        scratch = []
    return pl.pallas_call(
        functools.partial(_ssm_body, sequential=sequential),
        grid=(nb,),
        in_specs=[pl.BlockSpec((rows, LANES), lambda j: (cblk, j)),
                  op3(tz), op3(wb), op3(vc),
                  srow, srow, pl.BlockSpec((1, LANES), lambda j: (0, j)), hspec, hspec],
        out_specs=[pl.BlockSpec((rows, LANES), lambda j: (0, j)), hspec, hspec],
        out_shape=[jax.ShapeDtypeStruct((rows, u.shape[1]), F32),
                   jax.ShapeDtypeStruct(h0_re.shape, F32), jax.ShapeDtypeStruct(h0_im.shape, F32)],
        scratch_shapes=scratch,
        compiler_params=_params("parallel"),
        name="s5_scan_seq" if sequential else "s5_scan_step",
    )(u, tz, wb, vc, a8r, a8i, d_skip.reshape(1, -1), h0_re, h0_im)


def kernel(x_prompt, x_sample, cache_k, cache_v, state_ssm_re, state_ssm_im, page_table, rel_bias, ln_g, ln_b,
           w_qkv, w_o, w_ssm_in, ssm_a_re, ssm_a_im, ssm_log_dt, ssm_b_re, ssm_b_im, ssm_c_re, ssm_c_im, ssm_d,
           w_glu_v, w_glu_g, w_ff_gate, w_ff_up, w_ff_down, w_router, w_moe_gate, w_moe_up, w_moe_down):
    batch, seq, d = x_prompt.shape
    dec_batch, dec_seq, _ = x_sample.shape
    assert batch == 1 and d == D_MODEL and dec_seq == SSM_CHUNK and seq % MOBA_BLOCK == 0
    m_p = batch * seq
    m_s = dec_batch * dec_seq
    m = m_p + m_s
    dq = N_HEADS * HEAD_DIM
    dkv = N_KV_HEADS * HEAD_DIM
    n_pool, page = cache_k.shape[1], cache_k.shape[2]

    x_p = x_prompt.reshape(m_p, d)
    x_s = x_sample.reshape(m_s, d)

    qkv_f32, qkv_b = _qkv_proj(x_p, x_s, w_qkv[0].astype(BF16), tm=1024, tn=512)
    kmean = _block_means(qkv_f32, seq // MOBA_BLOCK)
    tabs, tab_s = _bias_tables(rel_bias)
    ck = cache_k[0].reshape(n_pool, page * N_KV_HEADS, HEAD_DIM)
    cv = cache_v[0].reshape(n_pool, page * N_KV_HEADS, HEAD_DIM)
    attn_p, attn_s = _attention(page_table, qkv_f32, qkv_b, kmean, tabs, tab_s, ck, cv, seq, dec_seq)
    y = _proj_ln(attn_p, attn_s, [w_o[0].astype(BF16)], [x_p, x_s], ln_g[0, 0], ln_b[0, 0], tm=512, tn=512)
    y = _ffn_dense(y, w_ff_gate, w_ff_up, w_ff_down, ln_g[0, 1], ln_b[0, 1], tm=1024, tf=256)

    k_all = qkv_f32[:, dq:dq + dkv]
    v_all = qkv_f32[:, dq + dkv:]
    k_prompt = k_all[:m_p].reshape(1, batch, seq, N_KV_HEADS, HEAD_DIM)
    v_prompt = v_all[:m_p].reshape(1, batch, seq, N_KV_HEADS, HEAD_DIM)
    k_sample = k_all[m_p:].reshape(1, dec_batch, dec_seq, N_KV_HEADS, HEAD_DIM)
    v_sample = v_all[m_p:].reshape(1, dec_batch, dec_seq, N_KV_HEADS, HEAD_DIM)

    u = _matmul(y, w_ssm_in[0].astype(BF16), tm=1024, tn=1024)
    ops = _ssm_operators(ssm_a_re[0], ssm_a_im[0], ssm_log_dt[0], ssm_b_re[0], ssm_b_im[0],
                         ssm_c_re[0], ssm_c_im[0])
    n_state = N_SSM_GROUPS * SSM_STATE
    zero = jnp.zeros((batch, n_state), F32)
    yg_p, hrp, hip = _ssm_scan(u, ops, ssm_d[0], zero, zero, 0, m_p // SSM_CHUNK, True)
    yg_s, hrs, his = _ssm_scan(u, ops, ssm_d[0], state_ssm_re[0], state_ssm_im[0], m_p // SSM_CHUNK, dec_batch,
                               False)
    y, y_packed = _proj_ln(yg_p, yg_s, [w_glu_v[0].astype(BF16), w_glu_g[0].astype(BF16)],
                           [y], ln_g[1, 0], ln_b[1, 0], tm=512, tn=512, packed=True)

    tm_moe = 1024
    top_idx, top_gate = _router(y, w_router[0], tm=512)
    tile_expert, tile_rows, slot_token, pair_slot = _moe_routing(top_idx, tm_moe)
    xs = y_packed.at[lax.optimization_barrier(slot_token)].get(mode="promise_in_bounds")
    ys = _ffn_moe(tile_expert, tile_rows, xs, w_moe_gate[0], w_moe_up[0], w_moe_down[0], tm_moe, tf=512, half=256)
    pairs = ys.at[pair_slot].get(mode="promise_in_bounds").reshape(TOP_K, m, d // 2)
    out_p, out_s = _combine_ln(y, pairs, top_gate, ln_g[1, 1], ln_b[1, 1], m_p, tm=512)

    return (out_p.reshape(batch, seq, d), out_s.reshape(dec_batch, dec_seq, d),
            k_prompt, v_prompt, k_sample, v_sample,
            hrp.reshape(1, batch, N_SSM_GROUPS, SSM_STATE), hip.reshape(1, batch, N_SSM_GROUPS, SSM_STATE),
            hrs.reshape(1, dec_batch, N_SSM_GROUPS, SSM_STATE), his.reshape(1, dec_batch, N_SSM_GROUPS, SSM_STATE))
```

```python
import functools
import math

import numpy as np
import jax
import jax.numpy as jnp
from jax import lax
from jax.experimental import pallas as pl
from jax.experimental.pallas import tpu as pltpu

D_MODEL = 2048
N_HEADS = 16
HEAD_DIM = D_MODEL // N_HEADS
N_KV_HEADS = 4
HEADS_PER_KV = N_HEADS // N_KV_HEADS
MOBA_BLOCK = 256
MOBA_TOPK = 3
NUM_BUCKETS = 32
MAX_DISTANCE = 128
SSM_GROUP = 16
N_SSM_GROUPS = D_MODEL // SSM_GROUP
SSM_STATE = 64
SSM_CHUNK = 8
D_FF = 7 * D_MODEL // 2
N_EXPERTS = 8
TOP_K = 2
DEPTH = 2
ALPHA = (2 * DEPTH) ** 0.25
LN_EPS = 1e-5
NEG_INF = -1e30

LANES = 128
SUBLANES = 8
VMEM_LIMIT = 60 * 1024 * 1024

F32 = jnp.float32
BF16 = jnp.bfloat16
HIGHEST = lax.Precision.HIGHEST
_NT = (((1,), (1,)), ((), ()))


def _params(*sem):
    return pltpu.CompilerParams(dimension_semantics=sem, vmem_limit_bytes=VMEM_LIMIT)


def _dot(a, b):
    return jnp.dot(a, b, preferred_element_type=F32)


def _sigmoid(x):
    return 1.0 / (1.0 + jnp.exp(-x))


def _layer_norm(y, g, b):
    mean = jnp.mean(y, axis=-1, keepdims=True)
    yc = y - mean
    var = jnp.mean(yc * yc, axis=-1, keepdims=True)
    return yc * lax.rsqrt(var + LN_EPS) * g + b


def _mm_body(x_ref, w_ref, o_ref, xb_scr):
    @pl.when(pl.program_id(1) == 0)
    def _():
        xb_scr[...] = x_ref[...].astype(BF16)

    o_ref[...] = _dot(xb_scr[...], w_ref[...])


def _matmul(x, w, tm, tn):
    m, k = x.shape
    n = w.shape[1]
    return pl.pallas_call(
        _mm_body,
        grid=(m // tm, n // tn),
        in_specs=[pl.BlockSpec((tm, k), lambda i, j: (i, 0)),
                  pl.BlockSpec((k, tn), lambda i, j: (0, j))],
        out_specs=pl.BlockSpec((tm, tn), lambda i, j: (i, j)),
        out_shape=jax.ShapeDtypeStruct((m, n), F32),
        scratch_shapes=[pltpu.VMEM((tm, k), BF16)],
        compiler_params=_params("parallel", "arbitrary"),
        name="matmul",
    )(x, w)


def _split_rows(block, n_first):
    return [pl.BlockSpec(block, lambda i, j: (jnp.minimum(i, n_first - 1), 0)),
            pl.BlockSpec(block, lambda i, j: (jnp.maximum(i - n_first, 0), 0))]


def _load_split_bf16(dst, src1, src2, n_first):
    i = pl.program_id(0)
    j = pl.program_id(1)

    @pl.when((j == 0) & (i < n_first))
    def _():
        dst[...] = src1[...].astype(BF16)

    @pl.when((j == 0) & (i >= n_first))
    def _():
        dst[...] = src2[...].astype(BF16)


def _qkv_body(x1_ref, x2_ref, w_ref, of_ref, ob_ref, xb_scr, *, nq_tiles, scale, n_first):
    _load_split_bf16(xb_scr, x1_ref, x2_ref, n_first)
    acc = _dot(xb_scr[...], w_ref[...])
    of_ref[...] = acc
    s = jnp.where(pl.program_id(1) < nq_tiles, scale, 1.0).astype(F32)
    ob_ref[...] = (acc * s).astype(BF16)


def _qkv_proj(x1, x2, w, tm, tn):
    k = x1.shape[1]
    m = x1.shape[0] + x2.shape[0]
    n = w.shape[1]
    n_first = x1.shape[0] // tm
    assert x1.shape[0] % tm == 0 and x2.shape[0] % tm == 0
    body = functools.partial(_qkv_body, nq_tiles=(N_HEADS * HEAD_DIM) // tn, scale=HEAD_DIM ** -0.5,
                             n_first=n_first)
    return pl.pallas_call(
        body,
        grid=(m // tm, n // tn),
        in_specs=_split_rows((tm, k), n_first) + [pl.BlockSpec((k, tn), lambda i, j: (0, j))],
        out_specs=[pl.BlockSpec((tm, tn), lambda i, j: (i, j)),
                   pl.BlockSpec((tm, tn), lambda i, j: (i, j))],
        out_shape=[jax.ShapeDtypeStruct((m, n), F32), jax.ShapeDtypeStruct((m, n), BF16)],
        scratch_shapes=[pltpu.VMEM((tm, k), BF16)],
        compiler_params=_params("parallel", "arbitrary"),
        name="qkv_proj",
    )(x1, x2, w)


def _kmean_body(k_ref, o_ref):
    i = pl.program_id(0)

    @pl.when(i == 0)
    def _():
        o_ref[...] = jnp.zeros_like(o_ref)

    mean = jnp.sum(k_ref[...], axis=0, keepdims=True) * (1.0 / MOBA_BLOCK)
    rows = lax.broadcasted_iota(jnp.int32, o_ref.shape, 0)
    o_ref[...] = jnp.where(rows == i, mean, o_ref[...])


def _block_means(qkv_f32, n_blocks):
    dkv = N_KV_HEADS * HEAD_DIM
    kcol = (N_HEADS * HEAD_DIM) // dkv
    return pl.pallas_call(
        _kmean_body,
        grid=(n_blocks,),
        in_specs=[pl.BlockSpec((MOBA_BLOCK, dkv), lambda i: (i, kcol))],
        out_specs=pl.BlockSpec((LANES, dkv), lambda i: (0, 0)),
        out_shape=jax.ShapeDtypeStruct((LANES, dkv), F32),
        compiler_params=_params("arbitrary"),
        name="moba_block_means",
    )(qkv_f32)


def _t5_bucket_np(dist):
    n = np.maximum(dist, 0)
    max_exact = NUM_BUCKETS // 2
    nf = np.maximum(n, 1).astype(np.float32)
    large = max_exact + (np.log(nf / np.float32(max_exact)) / np.float32(math.log(MAX_DISTANCE / max_exact))
                         * np.float32(NUM_BUCKETS - max_exact)).astype(np.int32)
    large = np.minimum(large, NUM_BUCKETS - 1)
    return np.where(n < max_exact, n, large).astype(np.int32)


def _bucket_tables():
    q = np.arange(MOBA_BLOCK)[:, None]
    k = np.arange(MOBA_BLOCK)[None, :]
    own = np.where(q - k >= 0, _t5_bucket_np(q - k), -1)
    prev = _t5_bucket_np(q - k + MOBA_BLOCK)
    return np.stack([own, prev]).astype(np.int32)


def _bias_body(rb_ref, idx_ref, o_ref, os_ref):
    h = pl.program_id(0)
    far = rb_ref[h, NUM_BUCKETS - 1]
    tabs = []
    for m in range(2):
        idx = idx_ref[m]
        acc = jnp.full(idx.shape, NEG_INF, F32)
        for b in range(NUM_BUCKETS):
            acc = jnp.where(idx == b, rb_ref[h, b] - far, acc)
        o_ref[0, m] = acc
        tabs.append(acc)
    os_ref[0] = jnp.concatenate([tabs[1][:SUBLANES, :], tabs[0][:SUBLANES, :LANES]], axis=1)


def _bias_tables(rel_bias):
    idx = jnp.asarray(_bucket_tables())
    return pl.pallas_call(
        _bias_body,
        grid=(N_HEADS,),
        in_specs=[pl.BlockSpec(memory_space=pltpu.SMEM),
                  pl.BlockSpec((2, MOBA_BLOCK, MOBA_BLOCK), lambda h: (0, 0, 0))],
        out_specs=[pl.BlockSpec((1, 2, MOBA_BLOCK, MOBA_BLOCK), lambda h: (h, 0, 0, 0)),
                   pl.BlockSpec((1, SUBLANES, MOBA_BLOCK + LANES), lambda h: (h, 0, 0))],
        out_shape=[jax.ShapeDtypeStruct((N_HEADS, 2, MOBA_BLOCK, MOBA_BLOCK), F32),
                   jax.ShapeDtypeStruct((N_HEADS, SUBLANES, MOBA_BLOCK + LANES), F32)],
        compiler_params=_params("parallel"),
        name="moba_bias_tables",
    )(rel_bias, idx)


def _select_blocks(gate, n_past):
    lane = lax.broadcasted_iota(jnp.int32, gate.shape, 1)
    lane_f = lane.astype(F32)
    g = jnp.where(lane < n_past, gate, NEG_INF)
    sel = jnp.full(gate.shape, NEG_INF, F32)
    for _ in range(MOBA_TOPK):
        mx = jnp.max(g, axis=1, keepdims=True)
        idx = jnp.min(jnp.where(g == mx, lane_f, float(gate.shape[1])), axis=1, keepdims=True)
        pick = lane_f == idx
        sel = jnp.where(pick, 0.0, sel)
        g = jnp.where(pick, -jnp.inf, g)
    return jnp.where(lane < n_past, sel, 0.0)


def _attn_prompt_body(qf_ref, qb_ref, k_ref, v_ref, km_ref, tab_ref, o_ref, qa_scr, m_scr, l_scr, acc_scr):
    i = pl.program_id(1)
    rows = HEADS_PER_KV * MOBA_BLOCK
    km = km_ref[...]
    lane = lax.broadcasted_iota(jnp.int32, (MOBA_BLOCK, LANES), 1)
    dummy = LANES - 1
    for hh in range(HEADS_PER_KV):
        cs = slice(hh * HEAD_DIM, (hh + 1) * HEAD_DIM)
        rs = slice(hh * MOBA_BLOCK, (hh + 1) * MOBA_BLOCK)
        gate = lax.dot_general(qf_ref[:, cs], km, _NT, precision=HIGHEST, preferred_element_type=F32)
        mask = jnp.where(lane == dummy, NEG_INF, _select_blocks(gate, i))
        qa_scr[rs, :HEAD_DIM] = qb_ref[:, cs]
        qa_scr[rs, HEAD_DIM:] = mask.astype(BF16)
    qa = qa_scr[...]

    def scores(j, mask_lane):
        start = pl.multiple_of(j * MOBA_BLOCK, MOBA_BLOCK)
        kj = k_ref[pl.ds(start, MOBA_BLOCK), :]
        vj = v_ref[pl.ds(start, MOBA_BLOCK), :]
        rhs = jnp.concatenate([kj, (lane == mask_lane).astype(BF16)], axis=1)
        return lax.dot_general(qa, rhs, _NT, preferred_element_type=F32), vj

    def softmax_parts(parts, m_new):
        m2 = jnp.concatenate([m_new, m_new], axis=1)
        l_add = None
        acc_add = None
        for s, vj in parts:
            p = jnp.exp(s - m2)
            ls = jnp.sum(p, axis=1, keepdims=True)
            pv = _dot(p.astype(BF16), vj)
            l_add = ls if l_add is None else l_add + ls
            acc_add = pv if acc_add is None else acc_add + pv
        return l_add, acc_add

    has_prev = i >= 1
    s_own, v_own = scores(i, -1)
    s_own = s_own + tab_ref[:, 0].reshape(rows, MOBA_BLOCK)
    s_prev, v_prev = scores(jnp.maximum(i - 1, 0), jnp.where(has_prev, i - 1, dummy))
    s_prev = s_prev + tab_ref[:, 1].reshape(rows, MOBA_BLOCK)
    m0 = jnp.maximum(jnp.max(s_own, axis=1, keepdims=True), jnp.max(s_prev, axis=1, keepdims=True))
    m0 = jnp.broadcast_to(m0, (rows, LANES))
    l0, acc0 = softmax_parts([(s_own, v_own), (s_prev, v_prev)], m0)
    m_scr[...] = m0
    l_scr[...] = jnp.broadcast_to(l0, (rows, LANES))
    acc_scr[...] = acc0

    n_far = jnp.maximum(i - 1, 0)

    def far_pair(t, carry):
        j0 = 2 * t
        j1 = j0 + 1
        ok1 = j1 < n_far
        parts = [scores(j0, j0), scores(jnp.where(ok1, j1, 0), jnp.where(ok1, j1, dummy))]
        m_prev = m_scr[...]
        m_new = m_prev
        for s, _ in parts:
            m_new = jnp.maximum(m_new, jnp.max(s, axis=1, keepdims=True))
        alpha = jnp.exp(m_prev - m_new)
        l_add, acc_add = softmax_parts(parts, m_new)
        l_scr[...] = alpha * l_scr[...] + l_add
        acc_scr[...] = alpha * acc_scr[...] + acc_add
        m_scr[...] = m_new
        return carry

    lax.fori_loop(0, (n_far + 1) // 2, far_pair, 0)

    out = acc_scr[...] / l_scr[...]
    for hh in range(HEADS_PER_KV):
        o_ref[:, hh * HEAD_DIM:(hh + 1) * HEAD_DIM] = out[hh * MOBA_BLOCK:(hh + 1) * MOBA_BLOCK]


def _attn_sample_core(first_step, qkv_ref, kp, vp, tab_ref, o_ref, kall, vall, expand, page, dec_seq):
    n_pages = len(kp)
    past = n_pages * page
    n_past_blocks = past // MOBA_BLOCK
    pages_per_block = MOBA_BLOCK // page
    near0 = past - MOBA_BLOCK
    total = past + LANES
    dq = N_HEADS * HEAD_DIM
    dkv = N_KV_HEADS * HEAD_DIM
    hq = HEADS_PER_KV * dec_seq
    rows = N_HEADS * dec_seq
    pad = jnp.zeros((LANES - dec_seq, HEAD_DIM), F32)

    @pl.when(first_step)
    def _():
        key_blk = lax.broadcasted_iota(jnp.int32, (LANES, total), 1) // MOBA_BLOCK
        expand[...] = (key_blk == lax.broadcasted_iota(jnp.int32, (LANES, total), 0)).astype(BF16)

    kmeans = []
    for g in range(N_KV_HEADS):
        means = []
        for blk in range(n_past_blocks):
            tot = jnp.zeros((1, HEAD_DIM), F32)
            for pp in range(pages_per_block):
                pg = blk * pages_per_block + pp
                kk = kp[pg][pl.ds(g, page, stride=N_KV_HEADS), :]
                vv = vp[pg][pl.ds(g, page, stride=N_KV_HEADS), :]
                tot = tot + jnp.sum(kk, axis=0, keepdims=True)
                kall[g, pg * page:(pg + 1) * page, :] = kk.astype(BF16)
                vall[g, pg * page:(pg + 1) * page, :] = vv.astype(BF16)
            means.append(tot * (1.0 / MOBA_BLOCK))
        kmeans.append(jnp.concatenate(means + [jnp.zeros((LANES - n_past_blocks, HEAD_DIM), F32)], axis=0))
        knew = qkv_ref[:, dq + g * HEAD_DIM:dq + (g + 1) * HEAD_DIM]
        vnew = qkv_ref[:, dq + dkv + g * HEAD_DIM:dq + dkv + (g + 1) * HEAD_DIM]
        kall[g, past:total, :] = jnp.concatenate([knew, pad], axis=0).astype(BF16)
        vall[g, past:total, :] = jnp.concatenate([vnew, pad], axis=0).astype(BF16)

    qs = jnp.concatenate([qkv_ref[:, h * HEAD_DIM:(h + 1) * HEAD_DIM] for h in range(N_HEADS)], axis=0)
    gate_all = lax.dot_general(qs, jnp.concatenate(kmeans, axis=0), _NT, precision=HIGHEST,
                               preferred_element_type=F32)
    row_g = lax.broadcasted_iota(jnp.int32, (rows, LANES), 0) // hq
    gate = gate_all[:, :LANES]
    for g in range(1, N_KV_HEADS):
        gate = jnp.where(row_g == g, gate_all[:, g * LANES:(g + 1) * LANES], gate)
    selm = _select_blocks(gate, n_past_blocks).astype(BF16)

    qb = (qs * (HEAD_DIM ** -0.5)).astype(BF16)
    s = jnp.concatenate([lax.dot_general(qb[g * hq:(g + 1) * hq], kall[g], _NT, preferred_element_type=F32)
                         for g in range(N_KV_HEADS)], axis=0)
    s = s + _dot(selm, expand[...])
    s_far = s[:, :near0]
    s_near = s[:, near0:] + tab_ref[...].reshape(rows, MOBA_BLOCK + LANES)
    m = jnp.maximum(jnp.max(s_far, axis=1, keepdims=True), jnp.max(s_near, axis=1, keepdims=True))
    p_far = jnp.exp(s_far - m)
    p_near = jnp.exp(s_near - m)
    inv_l = 1.0 / (jnp.sum(p_far, axis=1, keepdims=True) + jnp.sum(p_near, axis=1, keepdims=True))
    p_far = p_far.astype(BF16)
    p_near = p_near.astype(BF16)
    for g in range(N_KV_HEADS):
        rs = slice(g * hq, (g + 1) * hq)
        out = (_dot(p_far[rs], vall[g, :near0, :]) + _dot(p_near[rs], vall[g, near0:, :])) * inv_l[rs]
        for hh in range(HEADS_PER_KV):
            c0 = (g * HEADS_PER_KV + hh) * HEAD_DIM
            o_ref[:, c0:c0 + HEAD_DIM] = out[hh * dec_seq:(hh + 1) * dec_seq]


def _attn_body(pt_ref, *refs, n_pages, page, dec_seq):
    del pt_ref
    n_prompt_in = 6
    n_sample_in = 2 + 2 * n_pages
    prompt_in = refs[:n_prompt_in]
    qkv_ref, *pages, tab_ref = refs[n_prompt_in:n_prompt_in + n_sample_in]
    op_ref, os_ref = refs[n_prompt_in + n_sample_in:n_prompt_in + n_sample_in + 2]
    scr = refs[n_prompt_in + n_sample_in + 2:]
    _attn_prompt_body(*prompt_in, op_ref, *scr[:4])
    first_step = (pl.program_id(0) == 0) & (pl.program_id(1) == 0)
    _attn_sample_core(first_step, qkv_ref, pages[:n_pages], pages[n_pages:], tab_ref, os_ref, *scr[4:],
                      page, dec_seq)


def _attention(page_table, qkv_f32, qkv_b, kmean, tabs, tab_s, cache_k, cache_v, seq, dec_seq):
    nblk = seq // MOBA_BLOCK
    assert nblk < LANES - 1
    dec_batch, n_pages = page_table.shape
    assert dec_batch == N_KV_HEADS * nblk
    page = cache_k.shape[1] // N_KV_HEADS
    total = n_pages * page + LANES
    blk0 = seq // dec_seq
    gw = HEADS_PER_KV * HEAD_DIM
    kcol = (N_HEADS * HEAD_DIM) // HEAD_DIM
    vcol = kcol + N_KV_HEADS
    rows = HEADS_PER_KV * MOBA_BLOCK

    def page_spec(p):
        return pl.BlockSpec((None, page * N_KV_HEADS, HEAD_DIM),
                            lambda g, i, pt, p=p: (pt[g * nblk + i, p], 0, 0))

    grid_spec = pltpu.PrefetchScalarGridSpec(
        num_scalar_prefetch=1,
        grid=(N_KV_HEADS, nblk),
        in_specs=([pl.BlockSpec((MOBA_BLOCK, gw), lambda g, i, pt: (i, g)),
                   pl.BlockSpec((MOBA_BLOCK, gw), lambda g, i, pt: (i, g)),
                   pl.BlockSpec((seq, HEAD_DIM), lambda g, i, pt: (0, kcol + g)),
                   pl.BlockSpec((seq, HEAD_DIM), lambda g, i, pt: (0, vcol + g)),
                   pl.BlockSpec((LANES, HEAD_DIM), lambda g, i, pt: (0, g)),
                   pl.BlockSpec((HEADS_PER_KV, 2, MOBA_BLOCK, MOBA_BLOCK), lambda g, i, pt: (g, 0, 0, 0)),
                   pl.BlockSpec((dec_seq, qkv_f32.shape[1]), lambda g, i, pt: (blk0 + g * nblk + i, 0))]
                  + [page_spec(p) for p in range(n_pages)]
                  + [page_spec(p) for p in range(n_pages)]
                  + [pl.BlockSpec(tab_s.shape, lambda g, i, pt: (0, 0, 0))]),
        out_specs=[pl.BlockSpec((MOBA_BLOCK, gw), lambda g, i, pt: (i, g)),
                   pl.BlockSpec((dec_seq, N_HEADS * HEAD_DIM), lambda g, i, pt: (g * nblk + i, 0))],
        scratch_shapes=[pltpu.VMEM((rows, 2 * HEAD_DIM), BF16),
                        pltpu.VMEM((rows, LANES), F32),
                        pltpu.VMEM((rows, LANES), F32),
                        pltpu.VMEM((rows, HEAD_DIM), F32),
                        pltpu.VMEM((N_KV_HEADS, total, HEAD_DIM), BF16),
                        pltpu.VMEM((N_KV_HEADS, total, HEAD_DIM), BF16),
                        pltpu.VMEM((LANES, total), BF16)],
    )
    return pl.pallas_call(
        functools.partial(_attn_body, n_pages=n_pages, page=page, dec_seq=dec_seq),
        grid_spec=grid_spec,
        out_shape=[jax.ShapeDtypeStruct((seq, N_HEADS * HEAD_DIM), F32),
                   jax.ShapeDtypeStruct((dec_batch * dec_seq, N_HEADS * HEAD_DIM), F32)],
        compiler_params=_params("arbitrary", "arbitrary"),
        name="moba_attention",
    )(page_table, qkv_f32, qkv_b, qkv_b, qkv_b, kmean, tabs, qkv_f32,
      *([cache_k] * n_pages), *([cache_v] * n_pages), tab_s)


def _pack_bf16_halves(x):
    c = x.shape[1] // 2
    lo = lax.bitcast_convert_type(x[:, :c].astype(BF16).astype(F32), jnp.uint32)
    hi = lax.bitcast_convert_type(x[:, c:].astype(BF16).astype(F32), jnp.uint32)
    return lax.bitcast_convert_type(hi | (lo >> 16), F32)


def _unpack_bf16_halves(w):
    u = lax.bitcast_convert_type(w, jnp.uint32)
    lo = lax.bitcast_convert_type(u << 16, F32)
    hi = lax.bitcast_convert_type(u & jnp.uint32(0xFFFF0000), F32)
    return jnp.concatenate([lo, hi], axis=1)


def _proj_ln_body(*refs, n_w, n_x, n_tiles, n_first, packed):
    a1_ref, a2_ref = refs[:2]
    w_refs = refs[2:2 + n_w]
    x_refs = refs[2 + n_w:2 + n_w + n_x]
    if packed:
        g_ref, b_ref, o_ref, op_ref, a_scr, z_scr = refs[2 + n_w + n_x:]
    else:
        g_ref, b_ref, o_ref, a_scr, z_scr = refs[2 + n_w + n_x:]
    i = pl.program_id(0)
    j = pl.program_id(1)
    _load_split_bf16(a_scr, a1_ref, a2_ref, n_first)
    a = a_scr[...]
    z = _dot(a, w_refs[0][...])
    if n_w == 2:
        z = z * _sigmoid(_dot(a, w_refs[1][...]))
    z_scr[j] = z

    @pl.when(j == n_tiles - 1)
    def _():
        zfull = jnp.concatenate([z_scr[t] for t in range(n_tiles)], axis=1)
        x = x_refs[0][...] if n_x == 1 else jnp.where(i < n_first, x_refs[0][...], x_refs[1][...])
        out = _layer_norm(ALPHA * x + zfull, g_ref[...], b_ref[...])
        o_ref[...] = out
        if packed:
            op_ref[...] = _pack_bf16_halves(out)


def _proj_ln(a1, a2, ws, xs, g, b, tm, tn, packed=False):
    k = a1.shape[1]
    m = a1.shape[0] + a2.shape[0]
    n = ws[0].shape[1]
    n_first = a1.shape[0] // tm
    assert a1.shape[0] % tm == 0 and a2.shape[0] % tm == 0
    assert len(xs) == 1 or xs[0].shape[0] == a1.shape[0]
    n_tiles = n // tn
    body = functools.partial(_proj_ln_body, n_w=len(ws), n_x=len(xs), n_tiles=n_tiles, n_first=n_first,
                             packed=packed)
    row = pl.BlockSpec((tm, n), lambda i, j: (i, 0))
    vec = pl.BlockSpec((1, n), lambda i, j: (0, 0))
    out_specs, out_shape = row, jax.ShapeDtypeStruct((m, n), F32)
    if packed:
        out_specs = [row, pl.BlockSpec((tm, n // 2), lambda i, j: (i, 0))]
        out_shape = [out_shape, jax.ShapeDtypeStruct((m, n // 2), F32)]
    return pl.pallas_call(
        body,
        grid=(m // tm, n_tiles),
        in_specs=(_split_rows((tm, k), n_first)
                  + [pl.BlockSpec((k, tn), lambda i, j: (0, j)) for _ in ws]
                  + ([row] if len(xs) == 1 else _split_rows((tm, n), n_first))
                  + [vec, vec]),
        out_specs=out_specs,
        out_shape=out_shape,
        scratch_shapes=[pltpu.VMEM((tm, k), BF16), pltpu.VMEM((n_tiles, tm, tn), F32)],
        compiler_params=_params("parallel", "arbitrary"),
        name="glu_res_ln" if len(ws) == 2 else "proj_res_ln",
    )(a1, a2, *ws, *xs, g.reshape(1, n), b.reshape(1, n))


def _ffn_body(te_ref, tr_ref, *refs, n_f, dense, half):
    del te_ref
    if dense:
        x_ref, wg_ref, wu_ref, wd_ref, g_ref, b_ref, o_ref, xb = refs
        acc = o_ref
    else:
        x_ref, wg_ref, wu_ref, wd_ref, o_ref, xb, acc = refs
    tm = x_ref.shape[0]
    t = pl.program_id(0)
    f = pl.program_id(1)
    n_rows = tr_ref[t]

    @pl.when(f == 0)
    def _():
        x = x_ref[...] if dense else _unpack_bf16_halves(x_ref[...])
        xb[...] = x.astype(BF16)
        acc[...] = jnp.zeros_like(acc)

    def swiglu_rows(n):
        x = xb[:n, :]
        hg = _dot(x, wg_ref[...].astype(BF16))
        hu = _dot(x, wu_ref[...].astype(BF16))
        h = (hg * _sigmoid(hg)) * hu
        acc[:n, :] += _dot(h.astype(BF16), wd_ref[...].astype(BF16))

    for n in range(half, tm + 1, half):
        pl.when((n_rows > n - half) & (n_rows <= n))(functools.partial(swiglu_rows, n))

    @pl.when(f == n_f - 1)
    def _():
        if dense:
            o_ref[...] = _layer_norm(ALPHA * x_ref[...] + acc[...], g_ref[...], b_ref[...])
        else:
            o_ref[...] = _pack_bf16_halves(acc[...])


def _ffn_call(tile_expert, tile_rows, x, wg, wu, wd, extra, extra_specs, tm, tf, half, dense, name):
    s = x.shape[0]
    d = wg.shape[-2]
    n_t = s // tm
    n_f = wg.shape[-1] // tf
    live = lambda t, tr: jnp.minimum(tr[t], 1)
    x_spec = pl.BlockSpec((tm, x.shape[1]), lambda t, f, te, tr: (t, 0))
    wgu = pl.BlockSpec((None, d, tf), lambda t, f, te, tr: (te[t], 0, f * live(t, tr)))
    wds = pl.BlockSpec((None, tf, d), lambda t, f, te, tr: (te[t], f * live(t, tr), 0))
    grid_spec = pltpu.PrefetchScalarGridSpec(
        num_scalar_prefetch=2, grid=(n_t, n_f),
        in_specs=[x_spec, wgu, wgu, wds] + extra_specs,
        out_specs=x_spec,
        scratch_shapes=[pltpu.VMEM((tm, d), BF16)] + ([] if dense else [pltpu.VMEM((tm, d), F32)]))
    return pl.pallas_call(
        functools.partial(_ffn_body, n_f=n_f, dense=dense, half=half),
        grid_spec=grid_spec,
        out_shape=jax.ShapeDtypeStruct(x.shape, x.dtype),
        compiler_params=_params("parallel", "arbitrary"),
        name=name,
    )(tile_expert, tile_rows, x, wg, wu, wd, *extra)


def _ffn_dense(x, wg, wu, wd, g, b, tm, tf):
    m, d = x.shape
    n_t = m // tm
    vec = pl.BlockSpec((1, d), lambda t, f, te, tr: (0, 0))
    return _ffn_call(jnp.zeros((n_t,), jnp.int32), jnp.full((n_t,), tm, jnp.int32), x, wg, wu, wd,
                     [g.reshape(1, d), b.reshape(1, d)], [vec, vec], tm, tf, tm, True, "ffn_res_ln")


def _ffn_moe(tile_expert, tile_rows, xs, wg, wu, wd, tm, tf, half):
    return _ffn_call(tile_expert, tile_rows, xs, wg, wu, wd, [], [], tm, tf, half, False, "moe_ffn")


def _router_body(y_ref, w_ref, i_ref, g_ref):
    logits = jnp.dot(y_ref[...], w_ref[...], precision=HIGHEST, preferred_element_type=F32)
    lane = lax.broadcasted_iota(jnp.int32, logits.shape, 1)
    lane_f = lane.astype(F32)
    l1 = jnp.where(lane < N_EXPERTS, logits, -jnp.inf)
    m1 = jnp.max(l1, axis=1, keepdims=True)
    i1 = jnp.min(jnp.where(l1 == m1, lane_f, float(LANES)), axis=1, keepdims=True)
    l2 = jnp.where(lane_f == i1, -jnp.inf, l1)
    m2 = jnp.max(l2, axis=1, keepdims=True)
    i2 = jnp.min(jnp.where(l2 == m2, lane_f, float(LANES)), axis=1, keepdims=True)
    e = jnp.exp(m2 - m1)
    g1 = 1.0 / (1.0 + e)
    g2 = e / (1.0 + e)
    i_ref[...] = jnp.where(lane == 0, i1, jnp.where(lane == 1, i2, 0.0)).astype(jnp.int32)
    g_ref[...] = jnp.where(lane == 0, g1, jnp.where(lane == 1, g2, 0.0))


def _router(y, w_router, tm):
    m, d = y.shape
    wr = jnp.zeros((d, LANES), F32).at[:, :N_EXPERTS].set(w_router)
    row = pl.BlockSpec((tm, LANES), lambda i: (i, 0))
    return pl.pallas_call(
        _router_body,
        grid=(m // tm,),
        in_specs=[pl.BlockSpec((tm, d), lambda i: (i, 0)), pl.BlockSpec((d, LANES), lambda i: (0, 0))],
        out_specs=[row, row],
        out_shape=[jax.ShapeDtypeStruct((m, LANES), jnp.int32), jax.ShapeDtypeStruct((m, LANES), F32)],
        compiler_params=_params("parallel"),
        name="moe_router",
    )(y, wr)


def _combine_ln_body(x_ref, a_ref, b2_ref, tg_ref, g_ref, b_ref, op_ref, os_ref, *, n_first):
    i = pl.program_id(0)
    tg = tg_ref[...]
    moe = tg[:, 0:1] * _unpack_bf16_halves(a_ref[...]) + tg[:, 1:2] * _unpack_bf16_halves(b2_ref[...])
    out = _layer_norm(ALPHA * x_ref[...] + moe, g_ref[...], b_ref[...])

    @pl.when(i < n_first)
    def _():
        op_ref[...] = out

    @pl.when(i >= n_first)
    def _():
        os_ref[...] = out


def _combine_ln(x, pairs, top_gate, g, b, m_first, tm):
    m, d = x.shape
    dp = pairs.shape[2]
    n_first = m_first // tm
    vec = pl.BlockSpec((1, d), lambda i: (0, 0))
    return pl.pallas_call(
        functools.partial(_combine_ln_body, n_first=n_first),
        grid=(m // tm,),
        in_specs=[pl.BlockSpec((tm, d), lambda i: (i, 0)),
                  pl.BlockSpec((None, tm, dp), lambda i: (0, i, 0)),
                  pl.BlockSpec((None, tm, dp), lambda i: (1, i, 0)),
                  pl.BlockSpec((tm, LANES), lambda i: (i, 0)),
                  vec, vec],
        out_specs=[pl.BlockSpec((tm, d), lambda i: (jnp.minimum(i, n_first - 1), 0)),
                   pl.BlockSpec((tm, d), lambda i: (jnp.maximum(i - n_first, 0), 0))],
        out_shape=[jax.ShapeDtypeStruct((m_first, d), F32), jax.ShapeDtypeStruct((m - m_first, d), F32)],
        compiler_params=_params("arbitrary"),
        name="moe_combine_ln",
    )(x, pairs, pairs, top_gate, g.reshape(1, d), b.reshape(1, d))


def _moe_routing(top_idx, tm):
    m = top_idx.shape[0]
    n_pairs = m * TOP_K
    n_tiles = n_pairs // tm + N_EXPERTS
    i32 = jnp.int32
    e_flat = jnp.concatenate([top_idx[:, k] for k in range(TOP_K)])
    order = jnp.argsort(e_flat, stable=True).astype(i32)
    inv = jnp.argsort(order).astype(i32)
    counts = jnp.sum((e_flat[None, :] == jnp.arange(N_EXPERTS, dtype=i32)[:, None]).astype(i32), axis=1)
    starts = jnp.cumsum(counts) - counts
    k_tiles = (counts + tm - 1) // tm
    n_full = jnp.maximum(k_tiles - 2, 0)
    rest = counts - n_full * tm
    rows_a = jnp.where(k_tiles >= 2, (rest + 1) // 2, rest)
    rows_b = rest - rows_a
    tile_ends = jnp.cumsum(k_tiles)
    tile_first = tile_ends - k_tiles
    tile = jnp.arange(n_tiles, dtype=i32)
    tile_expert = jnp.minimum(jnp.searchsorted(tile_ends, tile, side="right"), N_EXPERTS - 1).astype(i32)
    tile_j = tile - tile_first[tile_expert]
    nf_t, ra_t, rb_t = n_full[tile_expert], rows_a[tile_expert], rows_b[tile_expert]
    tile_rows = jnp.where(tile_j < nf_t, tm, jnp.where(tile_j == nf_t, ra_t, rb_t))
    tile_rows = jnp.where(tile < tile_ends[-1], tile_rows, 0).astype(i32)
    tile_rank0 = jnp.where(tile_j <= nf_t, tile_j * tm, nf_t * tm + ra_t)
    slot = jnp.arange(n_tiles * tm, dtype=i32)
    slot_u = slot % tm
    slot_e = jnp.repeat(tile_expert, tm)
    live = slot_u < jnp.repeat(tile_rows, tm)
    src = order[jnp.clip(starts[slot_e] + jnp.repeat(tile_rank0, tm) + slot_u, 0, n_pairs - 1)]
    slot_token = jnp.where(live, src % m, slot % m)
    pair_rank = inv - starts[e_flat]
    nf_p, ra_p = n_full[e_flat], rows_a[e_flat]
    tail = pair_rank - nf_p * tm
    in_b = tail >= ra_p
    pair_j = jnp.where(tail < 0, pair_rank // tm, nf_p + in_b.astype(i32))
    pair_u = jnp.where(tail < 0, pair_rank % tm, tail - jnp.where(in_b, ra_p, 0))
    pair_slot = (tile_first[e_flat] + pair_j) * tm + pair_u
    return tile_expert, tile_rows, slot_token, pair_slot


def _gelu_tanh(x):
    return 0.5 * x * (1.0 + jnp.tanh(math.sqrt(2.0 / math.pi) * (x + 0.044715 * (x * x * x))))


def _ssm_body(*refs, sequential):
    (u_ref, tz_ref, wb_ref, vc_ref, a8r_ref, a8i_ref, d_ref, h0r_ref, h0i_ref, y_ref, hr_ref, hi_ref) = refs[:12]
    scr = refs[12:]
    n_chunks = u_ref.shape[0] // SSM_CHUNK
    sw = a8r_ref.shape[1]
    step_rows = lambda s: pl.ds(s, n_chunks, stride=SSM_CHUNK)
    ucat = jnp.concatenate([u_ref[step_rows(s), :].astype(BF16) for s in range(SSM_CHUNK)], axis=1)
    hl = _dot(ucat, wb_ref[...])
    hl_r = hl[:, :sw]
    hl_i = hl[:, sw:]
    a8r = a8r_ref[...]
    a8i = a8i_ref[...]
    if sequential:
        hlr_scr, hli_scr, hinr_scr, hini_scr = scr
        hlr_scr[...] = hl_r
        hli_scr[...] = hl_i

        def tile_step(kb, carry):
            hr, hi = carry
            base = pl.multiple_of(kb * SUBLANES, SUBLANES)
            tr = hlr_scr[pl.ds(base, SUBLANES), :]
            ti = hli_scr[pl.ds(base, SUBLANES), :]
            rows_r, rows_i = [], []
            for r in range(SUBLANES):
                rows_r.append(hr)
                rows_i.append(hi)
                hr, hi = (a8r * hr - a8i * hi + tr[r:r + 1], a8r * hi + a8i * hr + ti[r:r + 1])
            hinr_scr[pl.ds(base, SUBLANES), :] = jnp.concatenate(rows_r, axis=0)
            hini_scr[pl.ds(base, SUBLANES), :] = jnp.concatenate(rows_i, axis=0)
            return hr, hi

        hr, hi = lax.fori_loop(0, n_chunks // SUBLANES, tile_step, (h0r_ref[...], h0i_ref[...]))
        hr_ref[...] = hr
        hi_ref[...] = hi
        hin_r = hinr_scr[...]
        hin_i = hini_scr[...]
    else:
        gpb = h0r_ref.shape[1]
        p_n = h0r_ref.shape[2]
        hin_r = jnp.concatenate([h0r_ref[:, g, :] for g in range(gpb)], axis=1)
        hin_i = jnp.concatenate([h0i_ref[:, g, :] for g in range(gpb)], axis=1)
        hr = a8r * hin_r - a8i * hin_i + hl_r
        hi = a8r * hin_i + a8i * hin_r + hl_i
        for g in range(gpb):
            hr_ref[:, g, :] = hr[:, g * p_n:(g + 1) * p_n]
            hi_ref[:, g, :] = hi[:, g * p_n:(g + 1) * p_n]
    hcat = jnp.concatenate([hin_r.astype(BF16), hin_i.astype(BF16)], axis=1)
    d = d_ref[...]
    steps_per_dot = 2
    for t0 in range(0, SSM_CHUNK, steps_per_dot):
        cols = slice(t0 * LANES, (t0 + steps_per_dot) * LANES)
        ks = t0 + steps_per_dot
        k = ks * LANES
        y2 = _dot(ucat[:, :k], tz_ref[:k, cols]) + _dot(hcat, vc_ref[:, cols])
        for t in range(t0, t0 + steps_per_dot):
            y = y2[:, (t - t0) * LANES:(t - t0 + 1) * LANES]
            y_ref[step_rows(t), :] = _gelu_tanh(y + d * u_ref[step_rows(t), :])


GROUPS_PER_BLOCK = LANES // SSM_GROUP


def _spread_groups(x, rows_per_group):
    r, w = x.shape
    wl = GROUPS_PER_BLOCK * w
    src = lax.broadcasted_iota(jnp.int32, (w, wl), 0)
    dst = lax.broadcasted_iota(jnp.int32, (w, wl), 1)
    tiled = _dot(x.astype(BF16), (dst % w == src).astype(BF16))
    row_g = (lax.broadcasted_iota(jnp.int32, (r, wl), 0) // rows_per_group) % GROUPS_PER_BLOCK
    lane_g = lax.broadcasted_iota(jnp.int32, (r, wl), 1) // w
    return jnp.where(row_g == lane_g, tiled, 0.0).astype(BF16)


def _ssm_ops_body(conv_ref, abr_ref, abi_ref, vcr_ref, vci_ref, tz_ref, wb_ref, vc_ref):
    t_n = conv_ref.shape[0]
    sw = vcr_ref.shape[1]
    bd = [_spread_groups(conv_ref[tau], SSM_GROUP) for tau in range(t_n)]
    zero = jnp.zeros_like(bd[0])
    for s in range(t_n):
        rs = slice(s * LANES, (s + 1) * LANES)
        tz_ref[rs, :] = jnp.concatenate([bd[t - s] if t >= s else zero for t in range(t_n)], axis=1)
        wb_ref[rs, :] = jnp.concatenate([_spread_groups(abr_ref[t_n - 1 - s], SSM_GROUP),
                                         _spread_groups(abi_ref[t_n - 1 - s], SSM_GROUP)], axis=1)
    for t in range(t_n):
        cs = slice(t * LANES, (t + 1) * LANES)
        vc_ref[:sw, cs] = _spread_groups(vcr_ref[t], SSM_STATE)
        vc_ref[sw:, cs] = _spread_groups(vci_ref[t], SSM_STATE)


def _ssm_operators(a_re, a_im, log_dt, b_re, b_im, c_re, c_im):
    g_n, p_n = a_re.shape
    lam_re = jnp.minimum(a_re, -1e-4)
    lam_im = a_im
    dt = jnp.exp(log_dt)[:, None]
    decay = jnp.exp(lam_re * dt)
    ar = decay * jnp.cos(lam_im * dt)
    ai = decay * jnp.sin(lam_im * dt)
    num_re = ar - 1.0
    den = lam_re * lam_re + lam_im * lam_im
    f_re = (num_re * lam_re + ai * lam_im) / den
    f_im = (ai * lam_re - num_re * lam_im) / den
    bt_re = jnp.swapaxes(b_re, 1, 2)
    bt_im = jnp.swapaxes(b_im, 1, 2)
    bb_re = f_re[:, None, :] * bt_re - f_im[:, None, :] * bt_im
    bb_im = f_re[:, None, :] * bt_im + f_im[:, None, :] * bt_re
    pw_re, pw_im = [jnp.ones_like(ar)], [jnp.zeros_like(ar)]
    for _ in range(SSM_CHUNK):
        pr, pi = pw_re[-1], pw_im[-1]
        pw_re.append(pr * ar - pi * ai)
        pw_im.append(pr * ai + pi * ar)
    pw_re = jnp.stack(pw_re)
    pw_im = jnp.stack(pw_im)
    pk_re = pw_re[:SSM_CHUNK, :, None, :]
    pk_im = pw_im[:SSM_CHUNK, :, None, :]
    ab_re = pk_re * bb_re - pk_im * bb_im
    ab_im = pk_re * bb_im + pk_im * bb_re
    conv = (jnp.einsum("gcp,tgdp->tgdc", c_re, ab_re, precision=HIGHEST)
            - jnp.einsum("gcp,tgdp->tgdc", c_im, ab_im, precision=HIGHEST))
    gpb = GROUPS_PER_BLOCK
    nb = g_n // gpb
    t_n = SSM_CHUNK
    pr = pw_re[1:, :, :, None]
    pi = pw_im[1:, :, :, None]
    ct_re = jnp.swapaxes(c_re, 1, 2)[None]
    ct_im = jnp.swapaxes(c_im, 1, 2)[None]
    tables = [conv.reshape(t_n, nb, LANES, SSM_GROUP),
              ab_re.reshape(t_n, nb, LANES, p_n), ab_im.reshape(t_n, nb, LANES, p_n),
              (ct_re * pr - ct_im * pi).reshape(t_n, nb, gpb * p_n, SSM_GROUP),
              (-(ct_re * pi + ct_im * pr)).reshape(t_n, nb, gpb * p_n, SSM_GROUP)]
    side = t_n * LANES
    op_spec = pl.BlockSpec((None, side, side), lambda j: (j, 0, 0))
    tz, wb, vc = pl.pallas_call(
        _ssm_ops_body,
        grid=(nb,),
        in_specs=[pl.BlockSpec((t_n, None) + t.shape[2:], lambda j: (0, j, 0, 0)) for t in tables],
        out_specs=[op_spec] * 3,
        out_shape=[jax.ShapeDtypeStruct((nb, side, side), BF16)] * 3,
        compiler_params=_params("parallel"),
        name="s5_chunk_operators",
    )(*tables)
    a8r = pw_re[SSM_CHUNK].reshape(1, g_n * p_n)
    a8i = pw_im[SSM_CHUNK].reshape(1, g_n * p_n)
    return tz, wb, vc, a8r, a8i


def _ssm_scan(u, ops, d_skip, h0_re, h0_im, chunk0, n_chunks, sequential):
    tz, wb, vc, a8r, a8i = ops
    nb = tz.shape[0]
    sw = wb.shape[-1] // 2
    rows = n_chunks * SSM_CHUNK
    cblk = chunk0 // n_chunks
    op3 = lambda a: pl.BlockSpec((None,) + a.shape[1:], lambda j: (j, 0, 0))
    srow = pl.BlockSpec((1, sw), lambda j: (0, j))
    if sequential:
        hspec = srow
        scratch = [pltpu.VMEM((n_chunks, sw), F32)] * 4
    else:
        hspec = pl.BlockSpec((n_chunks, sw // SSM_STATE, SSM_STATE), lambda j: (0, j, 0))
        scratch = []
    return pl.pallas_call(
        functools.partial(_ssm_body, sequential=sequential),
        grid=(nb,),
        in_specs=[pl.BlockSpec((rows, LANES), lambda j: (cblk, j)),
                  op3(tz), op3(wb), op3(vc),
                  srow, srow, pl.BlockSpec((1, LANES), lambda j: (0, j)), hspec, hspec],
        out_specs=[pl.BlockSpec((rows, LANES), lambda j: (0, j)), hspec, hspec],
        out_shape=[jax.ShapeDtypeStruct((rows, u.shape[1]), F32),
                   jax.ShapeDtypeStruct(h0_re.shape, F32), jax.ShapeDtypeStruct(h0_im.shape, F32)],
        scratch_shapes=scratch,
        compiler_params=_params("parallel"),
        name="s5_scan_seq" if sequential else "s5_scan_step",
    )(u, tz, wb, vc, a8r, a8i, d_skip.reshape(1, -1), h0_re, h0_im)


def kernel(x_prompt, x_sample, cache_k, cache_v, state_ssm_re, state_ssm_im, page_table, rel_bias, ln_g, ln_b,
           w_qkv, w_o, w_ssm_in, ssm_a_re, ssm_a_im, ssm_log_dt, ssm_b_re, ssm_b_im, ssm_c_re, ssm_c_im, ssm_d,
           w_glu_v, w_glu_g, w_ff_gate, w_ff_up, w_ff_down, w_router, w_moe_gate, w_moe_up, w_moe_down):
    batch, seq, d = x_prompt.shape
    dec_batch, dec_seq, _ = x_sample.shape
    assert batch == 1 and d == D_MODEL and dec_seq == SSM_CHUNK and seq % MOBA_BLOCK == 0
    m_p = batch * seq
    m_s = dec_batch * dec_seq
    m = m_p + m_s
    dq = N_HEADS * HEAD_DIM
    dkv = N_KV_HEADS * HEAD_DIM
    n_pool, page = cache_k.shape[1], cache_k.shape[2]

    x_p = x_prompt.reshape(m_p, d)
    x_s = x_sample.reshape(m_s, d)

    qkv_f32, qkv_b = _qkv_proj(x_p, x_s, w_qkv[0].astype(BF16), tm=1024, tn=512)
    kmean = _block_means(qkv_f32, seq // MOBA_BLOCK)
    tabs, tab_s = _bias_tables(rel_bias)
    ck = cache_k[0].reshape(n_pool, page * N_KV_HEADS, HEAD_DIM)
    cv = cache_v[0].reshape(n_pool, page * N_KV_HEADS, HEAD_DIM)
    attn_p, attn_s = _attention(page_table, qkv_f32, qkv_b, kmean, tabs, tab_s, ck, cv, seq, dec_seq)
    y = _proj_ln(attn_p, attn_s, [w_o[0].astype(BF16)], [x_p, x_s], ln_g[0, 0], ln_b[0, 0], tm=512, tn=512)
    y = _ffn_dense(y, w_ff_gate, w_ff_up, w_ff_down, ln_g[0, 1], ln_b[0, 1], tm=1024, tf=256)

    k_all = qkv_f32[:, dq:dq + dkv]
    v_all = qkv_f32[:, dq + dkv:]
    k_prompt = k_all[:m_p].reshape(1, batch, seq, N_KV_HEADS, HEAD_DIM)
    v_prompt = v_all[:m_p].reshape(1, batch, seq, N_KV_HEADS, HEAD_DIM)
    k_sample = k_all[m_p:].reshape(1, dec_batch, dec_seq, N_KV_HEADS, HEAD_DIM)
    v_sample = v_all[m_p:].reshape(1, dec_batch, dec_seq, N_KV_HEADS, HEAD_DIM)

    u = _matmul(y, w_ssm_in[0].astype(BF16), tm=1024, tn=1024)
    ops = _ssm_operators(ssm_a_re[0], ssm_a_im[0], ssm_log_dt[0], ssm_b_re[0], ssm_b_im[0],
                         ssm_c_re[0], ssm_c_im[0])
    n_state = N_SSM_GROUPS * SSM_STATE
    zero = jnp.zeros((batch, n_state), F32)
    yg_p, hrp, hip = _ssm_scan(u, ops, ssm_d[0], zero, zero, 0, m_p // SSM_CHUNK, True)
    yg_s, hrs, his = _ssm_scan(u, ops, ssm_d[0], state_ssm_re[0], state_ssm_im[0], m_p // SSM_CHUNK, dec_batch,
                               False)
    y, y_packed = _proj_ln(yg_p, yg_s, [w_glu_v[0].astype(BF16), w_glu_g[0].astype(BF16)],
                           [y], ln_g[1, 0], ln_b[1, 0], tm=512, tn=512, packed=True)

    tm_moe = 1024
    top_idx, top_gate = _router(y, w_router[0], tm=512)
    tile_expert, tile_rows, slot_token, pair_slot = _moe_routing(top_idx, tm_moe)
    xs = y_packed.at[lax.optimization_barrier(slot_token)].get(mode="promise_in_bounds")
    ys = _ffn_moe(tile_expert, tile_rows, xs, w_moe_gate[0], w_moe_up[0], w_moe_down[0], tm_moe, tf=512, half=128)
    pairs = ys.at[pair_slot].get(mode="promise_in_bounds").reshape(TOP_K, m, d // 2)
    out_p, out_s = _combine_ln(y, pairs, top_gate, ln_g[1, 1], ln_b[1, 1], m_p, tm=512)

    return (out_p.reshape(batch, seq, d), out_s.reshape(dec_batch, dec_seq, d),
            k_prompt, v_prompt, k_sample, v_sample,
            hrp.reshape(1, batch, N_SSM_GROUPS, SSM_STATE), hip.reshape(1, batch, N_SSM_GROUPS, SSM_STATE),
            hrs.reshape(1, dec_batch, N_SSM_GROUPS, SSM_STATE), his.reshape(1, dec_batch, N_SSM_GROUPS, SSM_STATE))
```

```python
import functools
import math

import numpy as np
import jax
import jax.numpy as jnp
from jax import lax
from jax.experimental import pallas as pl
from jax.experimental.pallas import tpu as pltpu

D_MODEL = 2048
N_HEADS = 16
HEAD_DIM = D_MODEL // N_HEADS
N_KV_HEADS = 4
HEADS_PER_KV = N_HEADS // N_KV_HEADS
MOBA_BLOCK = 256
MOBA_TOPK = 3
NUM_BUCKETS = 32
MAX_DISTANCE = 128
SSM_GROUP = 16
N_SSM_GROUPS = D_MODEL // SSM_GROUP
SSM_STATE = 64
SSM_CHUNK = 8
D_FF = 7 * D_MODEL // 2
N_EXPERTS = 8
TOP_K = 2
DEPTH = 2
ALPHA = (2 * DEPTH) ** 0.25
LN_EPS = 1e-5
NEG_INF = -1e30

LANES = 128
SUBLANES = 8
VMEM_LIMIT = 62 * 1024 * 1024

F32 = jnp.float32
BF16 = jnp.bfloat16
HIGHEST = lax.Precision.HIGHEST
_NT = (((1,), (1,)), ((), ()))


def _params(*sem):
    return pltpu.CompilerParams(dimension_semantics=sem, vmem_limit_bytes=VMEM_LIMIT)


def _dot(a, b):
    return jnp.dot(a, b, preferred_element_type=F32)


def _sigmoid(x):
    return 1.0 / (1.0 + jnp.exp(-x))


def _layer_norm(y, g, b):
    mean = jnp.mean(y, axis=-1, keepdims=True)
    yc = y - mean
    var = jnp.mean(yc * yc, axis=-1, keepdims=True)
    return yc * lax.rsqrt(var + LN_EPS) * g + b


def _mm_body(x_ref, w_ref, o_ref, xb_scr):
    @pl.when(pl.program_id(1) == 0)
    def _():
        xb_scr[...] = x_ref[...].astype(BF16)

    o_ref[...] = _dot(xb_scr[...], w_ref[...])


def _matmul(x, w, tm, tn):
    m, k = x.shape
    n = w.shape[1]
    return pl.pallas_call(
        _mm_body,
        grid=(m // tm, n // tn),
        in_specs=[pl.BlockSpec((tm, k), lambda i, j: (i, 0)),
                  pl.BlockSpec((k, tn), lambda i, j: (0, j))],
        out_specs=pl.BlockSpec((tm, tn), lambda i, j: (i, j)),
        out_shape=jax.ShapeDtypeStruct((m, n), F32),
        scratch_shapes=[pltpu.VMEM((tm, k), BF16)],
        compiler_params=_params("parallel", "arbitrary"),
        name="matmul",
    )(x, w)


def _split_rows(block, n_first):
    return [pl.BlockSpec(block, lambda i, j: (jnp.minimum(i, n_first - 1), 0)),
            pl.BlockSpec(block, lambda i, j: (jnp.maximum(i - n_first, 0), 0))]


def _load_split_bf16(dst, src1, src2, n_first):
    i = pl.program_id(0)
    j = pl.program_id(1)

    @pl.when((j == 0) & (i < n_first))
    def _():
        dst[...] = src1[...].astype(BF16)

    @pl.when((j == 0) & (i >= n_first))
    def _():
        dst[...] = src2[...].astype(BF16)


def _qkv_body(x1_ref, x2_ref, w_ref, of_ref, ob_ref, xb_scr, *, nq_tiles, scale, n_first):
    _load_split_bf16(xb_scr, x1_ref, x2_ref, n_first)
    acc = _dot(xb_scr[...], w_ref[...])
    of_ref[...] = acc
    s = jnp.where(pl.program_id(1) < nq_tiles, scale, 1.0).astype(F32)
    ob_ref[...] = (acc * s).astype(BF16)


def _qkv_proj(x1, x2, w, tm, tn):
    k = x1.shape[1]
    m = x1.shape[0] + x2.shape[0]
    n = w.shape[1]
    n_first = x1.shape[0] // tm
    assert x1.shape[0] % tm == 0 and x2.shape[0] % tm == 0
    body = functools.partial(_qkv_body, nq_tiles=(N_HEADS * HEAD_DIM) // tn, scale=HEAD_DIM ** -0.5,
                             n_first=n_first)
    return pl.pallas_call(
        body,
        grid=(m // tm, n // tn),
        in_specs=_split_rows((tm, k), n_first) + [pl.BlockSpec((k, tn), lambda i, j: (0, j))],
        out_specs=[pl.BlockSpec((tm, tn), lambda i, j: (i, j)),
                   pl.BlockSpec((tm, tn), lambda i, j: (i, j))],
        out_shape=[jax.ShapeDtypeStruct((m, n), F32), jax.ShapeDtypeStruct((m, n), BF16)],
        scratch_shapes=[pltpu.VMEM((tm, k), BF16)],
        compiler_params=_params("parallel", "arbitrary"),
        name="qkv_proj",
    )(x1, x2, w)


def _kmean_body(k_ref, o_ref):
    i = pl.program_id(0)

    @pl.when(i == 0)
    def _():
        o_ref[...] = jnp.zeros_like(o_ref)

    mean = jnp.sum(k_ref[...], axis=0, keepdims=True) * (1.0 / MOBA_BLOCK)
    rows = lax.broadcasted_iota(jnp.int32, o_ref.shape, 0)
    o_ref[...] = jnp.where(rows == i, mean, o_ref[...])


def _block_means(qkv_f32, n_blocks):
    dkv = N_KV_HEADS * HEAD_DIM
    kcol = (N_HEADS * HEAD_DIM) // dkv
    return pl.pallas_call(
        _kmean_body,
        grid=(n_blocks,),
        in_specs=[pl.BlockSpec((MOBA_BLOCK, dkv), lambda i: (i, kcol))],
        out_specs=pl.BlockSpec((LANES, dkv), lambda i: (0, 0)),
        out_shape=jax.ShapeDtypeStruct((LANES, dkv), F32),
        compiler_params=_params("arbitrary"),
        name="moba_block_means",
    )(qkv_f32)


def _t5_bucket_np(dist):
    n = np.maximum(dist, 0)
    max_exact = NUM_BUCKETS // 2
    nf = np.maximum(n, 1).astype(np.float32)
    large = max_exact + (np.log(nf / np.float32(max_exact)) / np.float32(math.log(MAX_DISTANCE / max_exact))
                         * np.float32(NUM_BUCKETS - max_exact)).astype(np.int32)
    large = np.minimum(large, NUM_BUCKETS - 1)
    return np.where(n < max_exact, n, large).astype(np.int32)


def _bucket_tables():
    q = np.arange(MOBA_BLOCK)[:, None]
    k = np.arange(MOBA_BLOCK)[None, :]
    own = np.where(q - k >= 0, _t5_bucket_np(q - k), -1)
    prev = _t5_bucket_np(q - k + MOBA_BLOCK)
    return np.stack([own, prev]).astype(np.int32)


def _bias_body(rb_ref, idx_ref, o_ref, os_ref):
    h = pl.program_id(0)
    far = rb_ref[h, NUM_BUCKETS - 1]
    tabs = []
    for m in range(2):
        idx = idx_ref[m]
        acc = jnp.full(idx.shape, NEG_INF, F32)
        for b in range(NUM_BUCKETS):
            acc = jnp.where(idx == b, rb_ref[h, b] - far, acc)
        o_ref[0, m] = acc
        tabs.append(acc)
    os_ref[0] = jnp.concatenate([tabs[1][:SUBLANES, :], tabs[0][:SUBLANES, :LANES]], axis=1)


def _bias_tables(rel_bias):
    idx = jnp.asarray(_bucket_tables())
    return pl.pallas_call(
        _bias_body,
        grid=(N_HEADS,),
        in_specs=[pl.BlockSpec(memory_space=pltpu.SMEM),
                  pl.BlockSpec((2, MOBA_BLOCK, MOBA_BLOCK), lambda h: (0, 0, 0))],
        out_specs=[pl.BlockSpec((1, 2, MOBA_BLOCK, MOBA_BLOCK), lambda h: (h, 0, 0, 0)),
                   pl.BlockSpec((1, SUBLANES, MOBA_BLOCK + LANES), lambda h: (h, 0, 0))],
        out_shape=[jax.ShapeDtypeStruct((N_HEADS, 2, MOBA_BLOCK, MOBA_BLOCK), F32),
                   jax.ShapeDtypeStruct((N_HEADS, SUBLANES, MOBA_BLOCK + LANES), F32)],
        compiler_params=_params("parallel"),
        name="moba_bias_tables",
    )(rel_bias, idx)


def _select_blocks(gate, n_past):
    lane = lax.broadcasted_iota(jnp.int32, gate.shape, 1)
    lane_f = lane.astype(F32)
    g = jnp.where(lane < n_past, gate, NEG_INF)
    sel = jnp.full(gate.shape, NEG_INF, F32)
    for _ in range(MOBA_TOPK):
        mx = jnp.max(g, axis=1, keepdims=True)
        idx = jnp.min(jnp.where(g == mx, lane_f, float(gate.shape[1])), axis=1, keepdims=True)
        pick = lane_f == idx
        sel = jnp.where(pick, 0.0, sel)
        g = jnp.where(pick, -jnp.inf, g)
    return jnp.where(lane < n_past, sel, 0.0)


def _attn_prompt_body(qf_ref, qb_ref, k_ref, v_ref, km_ref, tab_ref, o_ref, qa_scr, m_scr, l_scr, acc_scr):
    i = pl.program_id(1)
    rows = HEADS_PER_KV * MOBA_BLOCK
    km = km_ref[...]
    lane = lax.broadcasted_iota(jnp.int32, (MOBA_BLOCK, LANES), 1)
    dummy = LANES - 1
    for hh in range(HEADS_PER_KV):
        cs = slice(hh * HEAD_DIM, (hh + 1) * HEAD_DIM)
        rs = slice(hh * MOBA_BLOCK, (hh + 1) * MOBA_BLOCK)
        gate = lax.dot_general(qf_ref[:, cs], km, _NT, precision=HIGHEST, preferred_element_type=F32)
        mask = jnp.where(lane == dummy, NEG_INF, _select_blocks(gate, i))
        qa_scr[rs, :HEAD_DIM] = qb_ref[:, cs]
        qa_scr[rs, HEAD_DIM:] = mask.astype(BF16)
    qa = qa_scr[...]

    def scores(j, mask_lane):
        start = pl.multiple_of(j * MOBA_BLOCK, MOBA_BLOCK)
        kj = k_ref[pl.ds(start, MOBA_BLOCK), :]
        vj = v_ref[pl.ds(start, MOBA_BLOCK), :]
        rhs = jnp.concatenate([kj, (lane == mask_lane).astype(BF16)], axis=1)
        return lax.dot_general(qa, rhs, _NT, preferred_element_type=F32), vj

    def softmax_parts(parts, m_new):
        m2 = jnp.concatenate([m_new, m_new], axis=1)
        l_add = None
        acc_add = None
        for s, vj in parts:
            p = jnp.exp(s - m2)
            ls = jnp.sum(p, axis=1, keepdims=True)
            pv = _dot(p.astype(BF16), vj)
            l_add = ls if l_add is None else l_add + ls
            acc_add = pv if acc_add is None else acc_add + pv
        return l_add, acc_add

    has_prev = i >= 1
    s_own, v_own = scores(i, -1)
    s_own = s_own + tab_ref[:, 0].reshape(rows, MOBA_BLOCK)
    s_prev, v_prev = scores(jnp.maximum(i - 1, 0), jnp.where(has_prev, i - 1, dummy))
    s_prev = s_prev + tab_ref[:, 1].reshape(rows, MOBA_BLOCK)
    m0 = jnp.maximum(jnp.max(s_own, axis=1, keepdims=True), jnp.max(s_prev, axis=1, keepdims=True))
    m0 = jnp.broadcast_to(m0, (rows, LANES))
    l0, acc0 = softmax_parts([(s_own, v_own), (s_prev, v_prev)], m0)
    m_scr[...] = m0
    l_scr[...] = jnp.broadcast_to(l0, (rows, LANES))
    acc_scr[...] = acc0

    n_far = jnp.maximum(i - 1, 0)

    def far_pair(t, carry):
        j0 = 2 * t
        j1 = j0 + 1
        ok1 = j1 < n_far
        parts = [scores(j0, j0), scores(jnp.where(ok1, j1, 0), jnp.where(ok1, j1, dummy))]
        m_prev = m_scr[...]
        m_new = m_prev
        for s, _ in parts:
            m_new = jnp.maximum(m_new, jnp.max(s, axis=1, keepdims=True))
        alpha = jnp.exp(m_prev - m_new)
        l_add, acc_add = softmax_parts(parts, m_new)
        l_scr[...] = alpha * l_scr[...] + l_add
        acc_scr[...] = alpha * acc_scr[...] + acc_add
        m_scr[...] = m_new
        return carry

    lax.fori_loop(0, (n_far + 1) // 2, far_pair, 0)

    out = acc_scr[...] / l_scr[...]
    for hh in range(HEADS_PER_KV):
        o_ref[:, hh * HEAD_DIM:(hh + 1) * HEAD_DIM] = out[hh * MOBA_BLOCK:(hh + 1) * MOBA_BLOCK]


def _attn_sample_core(first_step, qkv_ref, kp, vp, tab_ref, o_ref, kall, vall, expand, page, dec_seq):
    n_pages = len(kp)
    past = n_pages * page
    n_past_blocks = past // MOBA_BLOCK
    pages_per_block = MOBA_BLOCK // page
    near0 = past - MOBA_BLOCK
    total = past + LANES
    dq = N_HEADS * HEAD_DIM
    dkv = N_KV_HEADS * HEAD_DIM
    hq = HEADS_PER_KV * dec_seq
    rows = N_HEADS * dec_seq
    pad = jnp.zeros((LANES - dec_seq, HEAD_DIM), F32)

    @pl.when(first_step)
    def _():
        key_blk = lax.broadcasted_iota(jnp.int32, (LANES, total), 1) // MOBA_BLOCK
        expand[...] = (key_blk == lax.broadcasted_iota(jnp.int32, (LANES, total), 0)).astype(BF16)

    kmeans = []
    for g in range(N_KV_HEADS):
        means = []
        for blk in range(n_past_blocks):
            tot = jnp.zeros((1, HEAD_DIM), F32)
            for pp in range(pages_per_block):
                pg = blk * pages_per_block + pp
                kk = kp[pg][pl.ds(g, page, stride=N_KV_HEADS), :]
                vv = vp[pg][pl.ds(g, page, stride=N_KV_HEADS), :]
                tot = tot + jnp.sum(kk, axis=0, keepdims=True)
                kall[g, pg * page:(pg + 1) * page, :] = kk.astype(BF16)
                vall[g, pg * page:(pg + 1) * page, :] = vv.astype(BF16)
            means.append(tot * (1.0 / MOBA_BLOCK))
        kmeans.append(jnp.concatenate(means + [jnp.zeros((LANES - n_past_blocks, HEAD_DIM), F32)], axis=0))
        knew = qkv_ref[:, dq + g * HEAD_DIM:dq + (g + 1) * HEAD_DIM]
        vnew = qkv_ref[:, dq + dkv + g * HEAD_DIM:dq + dkv + (g + 1) * HEAD_DIM]
        kall[g, past:total, :] = jnp.concatenate([knew, pad], axis=0).astype(BF16)
        vall[g, past:total, :] = jnp.concatenate([vnew, pad], axis=0).astype(BF16)

    qs = jnp.concatenate([qkv_ref[:, h * HEAD_DIM:(h + 1) * HEAD_DIM] for h in range(N_HEADS)], axis=0)
    gate_all = lax.dot_general(qs, jnp.concatenate(kmeans, axis=0), _NT, precision=HIGHEST,
                               preferred_element_type=F32)
    row_g = lax.broadcasted_iota(jnp.int32, (rows, LANES), 0) // hq
    gate = gate_all[:, :LANES]
    for g in range(1, N_KV_HEADS):
        gate = jnp.where(row_g == g, gate_all[:, g * LANES:(g + 1) * LANES], gate)
    selm = _select_blocks(gate, n_past_blocks).astype(BF16)

    qb = (qs * (HEAD_DIM ** -0.5)).astype(BF16)
    s = jnp.concatenate([lax.dot_general(qb[g * hq:(g + 1) * hq], kall[g], _NT, preferred_element_type=F32)
                         for g in range(N_KV_HEADS)], axis=0)
    s = s + _dot(selm, expand[...])
    s_far = s[:, :near0]
    s_near = s[:, near0:] + tab_ref[...].reshape(rows, MOBA_BLOCK + LANES)
    m = jnp.maximum(jnp.max(s_far, axis=1, keepdims=True), jnp.max(s_near, axis=1, keepdims=True))
    p_far = jnp.exp(s_far - m)
    p_near = jnp.exp(s_near - m)
    inv_l = 1.0 / (jnp.sum(p_far, axis=1, keepdims=True) + jnp.sum(p_near, axis=1, keepdims=True))
    p_far = p_far.astype(BF16)
    p_near = p_near.astype(BF16)
    for g in range(N_KV_HEADS):
        rs = slice(g * hq, (g + 1) * hq)
        out = (_dot(p_far[rs], vall[g, :near0, :]) + _dot(p_near[rs], vall[g, near0:, :])) * inv_l[rs]
        for hh in range(HEADS_PER_KV):
            c0 = (g * HEADS_PER_KV + hh) * HEAD_DIM
            o_ref[:, c0:c0 + HEAD_DIM] = out[hh * dec_seq:(hh + 1) * dec_seq]


def _attn_body(pt_ref, *refs, n_pages, page, dec_seq):
    del pt_ref
    n_prompt_in = 6
    n_sample_in = 2 + 2 * n_pages
    prompt_in = refs[:n_prompt_in]
    qkv_ref, *pages, tab_ref = refs[n_prompt_in:n_prompt_in + n_sample_in]
    op_ref, os_ref = refs[n_prompt_in + n_sample_in:n_prompt_in + n_sample_in + 2]
    scr = refs[n_prompt_in + n_sample_in + 2:]
    _attn_prompt_body(*prompt_in, op_ref, *scr[:4])
    first_step = (pl.program_id(0) == 0) & (pl.program_id(1) == 0)
    _attn_sample_core(first_step, qkv_ref, pages[:n_pages], pages[n_pages:], tab_ref, os_ref, *scr[4:],
                      page, dec_seq)


def _attention(page_table, qkv_f32, qkv_b, kmean, tabs, tab_s, cache_k, cache_v, seq, dec_seq):
    nblk = seq // MOBA_BLOCK
    assert nblk < LANES - 1
    dec_batch, n_pages = page_table.shape
    assert dec_batch == N_KV_HEADS * nblk
    page = cache_k.shape[1] // N_KV_HEADS
    total = n_pages * page + LANES
    blk0 = seq // dec_seq
    gw = HEADS_PER_KV * HEAD_DIM
    kcol = (N_HEADS * HEAD_DIM) // HEAD_DIM
    vcol = kcol + N_KV_HEADS
    rows = HEADS_PER_KV * MOBA_BLOCK

    def page_spec(p):
        return pl.BlockSpec((None, page * N_KV_HEADS, HEAD_DIM),
                            lambda g, i, pt, p=p: (pt[g * nblk + i, p], 0, 0))

    grid_spec = pltpu.PrefetchScalarGridSpec(
        num_scalar_prefetch=1,
        grid=(N_KV_HEADS, nblk),
        in_specs=([pl.BlockSpec((MOBA_BLOCK, gw), lambda g, i, pt: (i, g)),
                   pl.BlockSpec((MOBA_BLOCK, gw), lambda g, i, pt: (i, g)),
                   pl.BlockSpec((seq, HEAD_DIM), lambda g, i, pt: (0, kcol + g)),
                   pl.BlockSpec((seq, HEAD_DIM), lambda g, i, pt: (0, vcol + g)),
                   pl.BlockSpec((LANES, HEAD_DIM), lambda g, i, pt: (0, g)),
                   pl.BlockSpec((HEADS_PER_KV, 2, MOBA_BLOCK, MOBA_BLOCK), lambda g, i, pt: (g, 0, 0, 0)),
                   pl.BlockSpec((dec_seq, qkv_f32.shape[1]), lambda g, i, pt: (blk0 + g * nblk + i, 0))]
                  + [page_spec(p) for p in range(n_pages)]
                  + [page_spec(p) for p in range(n_pages)]
                  + [pl.BlockSpec(tab_s.shape, lambda g, i, pt: (0, 0, 0))]),
        out_specs=[pl.BlockSpec((MOBA_BLOCK, gw), lambda g, i, pt: (i, g)),
                   pl.BlockSpec((dec_seq, N_HEADS * HEAD_DIM), lambda g, i, pt: (g * nblk + i, 0))],
        scratch_shapes=[pltpu.VMEM((rows, 2 * HEAD_DIM), BF16),
                        pltpu.VMEM((rows, LANES), F32),
                        pltpu.VMEM((rows, LANES), F32),
                        pltpu.VMEM((rows, HEAD_DIM), F32),
                        pltpu.VMEM((N_KV_HEADS, total, HEAD_DIM), BF16),
                        pltpu.VMEM((N_KV_HEADS, total, HEAD_DIM), BF16),
                        pltpu.VMEM((LANES, total), BF16)],
    )
    return pl.pallas_call(
        functools.partial(_attn_body, n_pages=n_pages, page=page, dec_seq=dec_seq),
        grid_spec=grid_spec,
        out_shape=[jax.ShapeDtypeStruct((seq, N_HEADS * HEAD_DIM), F32),
                   jax.ShapeDtypeStruct((dec_batch * dec_seq, N_HEADS * HEAD_DIM), F32)],
        compiler_params=_params("arbitrary", "arbitrary"),
        name="moba_attention",
    )(page_table, qkv_f32, qkv_b, qkv_b, qkv_b, kmean, tabs, qkv_f32,
      *([cache_k] * n_pages), *([cache_v] * n_pages), tab_s)


def _pack_bf16_halves(x):
    c = x.shape[1] // 2
    lo = lax.bitcast_convert_type(x[:, :c].astype(BF16).astype(F32), jnp.uint32)
    hi = lax.bitcast_convert_type(x[:, c:].astype(BF16).astype(F32), jnp.uint32)
    return lax.bitcast_convert_type(hi | (lo >> 16), F32)


def _unpack_bf16_halves(w):
    u = lax.bitcast_convert_type(w, jnp.uint32)
    lo = lax.bitcast_convert_type(u << 16, F32)
    hi = lax.bitcast_convert_type(u & jnp.uint32(0xFFFF0000), F32)
    return jnp.concatenate([lo, hi], axis=1)


def _proj_ln_body(*refs, n_w, n_x, n_tiles, n_first, packed):
    a1_ref, a2_ref = refs[:2]
    w_refs = refs[2:2 + n_w]
    x_refs = refs[2 + n_w:2 + n_w + n_x]
    if packed:
        g_ref, b_ref, o_ref, op_ref, a_scr, z_scr = refs[2 + n_w + n_x:]
    else:
        g_ref, b_ref, o_ref, a_scr, z_scr = refs[2 + n_w + n_x:]
    i = pl.program_id(0)
    j = pl.program_id(1)
    _load_split_bf16(a_scr, a1_ref, a2_ref, n_first)
    a = a_scr[...]
    z = _dot(a, w_refs[0][...])
    if n_w == 2:
        z = z * _sigmoid(_dot(a, w_refs[1][...]))
    z_scr[j] = z

    @pl.when(j == n_tiles - 1)
    def _():
        zfull = jnp.concatenate([z_scr[t] for t in range(n_tiles)], axis=1)
        x = x_refs[0][...] if n_x == 1 else jnp.where(i < n_first, x_refs[0][...], x_refs[1][...])
        out = _layer_norm(ALPHA * x + zfull, g_ref[...], b_ref[...])
        o_ref[...] = out
        if packed:
            op_ref[...] = _pack_bf16_halves(out)


def _proj_ln(a1, a2, ws, xs, g, b, tm, tn, packed=False):
    k = a1.shape[1]
    m = a1.shape[0] + a2.shape[0]
    n = ws[0].shape[1]
    n_first = a1.shape[0] // tm
    assert a1.shape[0] % tm == 0 and a2.shape[0] % tm == 0
    assert len(xs) == 1 or xs[0].shape[0] == a1.shape[0]
    n_tiles = n // tn
    body = functools.partial(_proj_ln_body, n_w=len(ws), n_x=len(xs), n_tiles=n_tiles, n_first=n_first,
                             packed=packed)
    row = pl.BlockSpec((tm, n), lambda i, j: (i, 0))
    vec = pl.BlockSpec((1, n), lambda i, j: (0, 0))
    out_specs, out_shape = row, jax.ShapeDtypeStruct((m, n), F32)
    if packed:
        out_specs = [row, pl.BlockSpec((tm, n // 2), lambda i, j: (i, 0))]
        out_shape = [out_shape, jax.ShapeDtypeStruct((m, n // 2), F32)]
    return pl.pallas_call(
        body,
        grid=(m // tm, n_tiles),
        in_specs=(_split_rows((tm, k), n_first)
                  + [pl.BlockSpec((k, tn), lambda i, j: (0, j)) for _ in ws]
                  + ([row] if len(xs) == 1 else _split_rows((tm, n), n_first))
                  + [vec, vec]),
        out_specs=out_specs,
        out_shape=out_shape,
        scratch_shapes=[pltpu.VMEM((tm, k), BF16), pltpu.VMEM((n_tiles, tm, tn), F32)],
        compiler_params=_params("parallel", "arbitrary"),
        name="glu_res_ln" if len(ws) == 2 else "proj_res_ln",
    )(a1, a2, *ws, *xs, g.reshape(1, n), b.reshape(1, n))


def _ffn_body(te_ref, tr_ref, *refs, n_f, dense, half):
    del te_ref
    if dense:
        x_ref, wg_ref, wu_ref, wd_ref, g_ref, b_ref, o_ref, xb = refs
        acc = o_ref
    else:
        x_ref, wg_ref, wu_ref, wd_ref, o_ref, xb, acc = refs
    tm = x_ref.shape[0]
    t = pl.program_id(0)
    f = pl.program_id(1)
    n_rows = tr_ref[t]

    @pl.when(f == 0)
    def _():
        x = x_ref[...] if dense else _unpack_bf16_halves(x_ref[...])
        xb[...] = x.astype(BF16)
        acc[...] = jnp.zeros_like(acc)

    def swiglu_rows(n):
        x = xb[:n, :]
        hg = _dot(x, wg_ref[...].astype(BF16))
        hu = _dot(x, wu_ref[...].astype(BF16))
        h = (hg * _sigmoid(hg)) * hu
        acc[:n, :] += _dot(h.astype(BF16), wd_ref[...].astype(BF16))

    for n in range(half, tm + 1, half):
        pl.when((n_rows > n - half) & (n_rows <= n))(functools.partial(swiglu_rows, n))

    @pl.when(f == n_f - 1)
    def _():
        if dense:
            o_ref[...] = _layer_norm(ALPHA * x_ref[...] + acc[...], g_ref[...], b_ref[...])
        else:
            o_ref[...] = _pack_bf16_halves(acc[...])


def _ffn_call(tile_expert, tile_rows, x, wg, wu, wd, extra, extra_specs, tm, tf, half, dense, name):
    s = x.shape[0]
    d = wg.shape[-2]
    n_t = s // tm
    n_f = wg.shape[-1] // tf
    live = lambda t, tr: jnp.minimum(tr[t], 1)
    x_spec = pl.BlockSpec((tm, x.shape[1]), lambda t, f, te, tr: (t, 0))
    x_in = pl.BlockSpec((tm, x.shape[1]), lambda t, f, te, tr: (t, 0), pipeline_mode=pl.Buffered(1)) if dense else x_spec
    wgu = pl.BlockSpec((None, d, tf), lambda t, f, te, tr: (te[t], 0, f * live(t, tr)))
    wds = pl.BlockSpec((None, tf, d), lambda t, f, te, tr: (te[t], f * live(t, tr), 0))
    grid_spec = pltpu.PrefetchScalarGridSpec(
        num_scalar_prefetch=2, grid=(n_t, n_f),
        in_specs=[x_in, wgu, wgu, wds] + extra_specs,
        out_specs=x_spec,
        scratch_shapes=[pltpu.VMEM((tm, d), BF16)] + ([] if dense else [pltpu.VMEM((tm, d), F32)]))
    return pl.pallas_call(
        functools.partial(_ffn_body, n_f=n_f, dense=dense, half=half),
        grid_spec=grid_spec,
        out_shape=jax.ShapeDtypeStruct(x.shape, x.dtype),
        compiler_params=_params("parallel", "arbitrary"),
        name=name,
    )(tile_expert, tile_rows, x, wg, wu, wd, *extra)


def _ffn_dense(x, wg, wu, wd, g, b, tm, tf):
    m, d = x.shape
    n_t = m // tm
    vec = pl.BlockSpec((1, d), lambda t, f, te, tr: (0, 0))
    return _ffn_call(jnp.zeros((n_t,), jnp.int32), jnp.full((n_t,), tm, jnp.int32), x, wg, wu, wd,
                     [g.reshape(1, d), b.reshape(1, d)], [vec, vec], tm, tf, tm, True, "ffn_res_ln")


def _ffn_moe(tile_expert, tile_rows, xs, wg, wu, wd, tm, tf, half):
    return _ffn_call(tile_expert, tile_rows, xs, wg, wu, wd, [], [], tm, tf, half, False, "moe_ffn")


def _router_body(y_ref, w_ref, i_ref, g_ref):
    y = y_ref[...]
    w = w_ref[...]
    y_hi = y.astype(BF16)
    w_hi = w.astype(BF16)
    y_lo = (y - y_hi.astype(F32)).astype(BF16)
    w_lo = (w - w_hi.astype(F32)).astype(BF16)
    logits = _dot(y_hi, w_hi) + (_dot(y_hi, w_lo) + _dot(y_lo, w_hi))
    lane = lax.broadcasted_iota(jnp.int32, logits.shape, 1)
    lane_f = lane.astype(F32)
    l1 = jnp.where(lane < N_EXPERTS, logits, -jnp.inf)
    m1 = jnp.max(l1, axis=1, keepdims=True)
    i1 = jnp.min(jnp.where(l1 == m1, lane_f, float(LANES)), axis=1, keepdims=True)
    l2 = jnp.where(lane_f == i1, -jnp.inf, l1)
    m2 = jnp.max(l2, axis=1, keepdims=True)
    i2 = jnp.min(jnp.where(l2 == m2, lane_f, float(LANES)), axis=1, keepdims=True)
    e = jnp.exp(m2 - m1)
    g1 = 1.0 / (1.0 + e)
    g2 = e / (1.0 + e)
    i_ref[...] = jnp.where(lane == 0, i1, jnp.where(lane == 1, i2, 0.0)).astype(jnp.int32)
    g_ref[...] = jnp.where(lane == 0, g1, jnp.where(lane == 1, g2, 0.0))


def _router(y, w_router, tm):
    m, d = y.shape
    wr = jnp.zeros((d, LANES), F32).at[:, :N_EXPERTS].set(w_router)
    row = pl.BlockSpec((tm, LANES), lambda i: (i, 0))
    return pl.pallas_call(
        _router_body,
        grid=(m // tm,),
        in_specs=[pl.BlockSpec((tm, d), lambda i: (i, 0)), pl.BlockSpec((d, LANES), lambda i: (0, 0))],
        out_specs=[row, row],
        out_shape=[jax.ShapeDtypeStruct((m, LANES), jnp.int32), jax.ShapeDtypeStruct((m, LANES), F32)],
        compiler_params=_params("parallel"),
        name="moe_router",
    )(y, wr)


def _combine_ln_body(x_ref, a_ref, b2_ref, tg_ref, g_ref, b_ref, op_ref, os_ref, *, n_first):
    i = pl.program_id(0)
    tg = tg_ref[...]
    moe = tg[:, 0:1] * _unpack_bf16_halves(a_ref[...]) + tg[:, 1:2] * _unpack_bf16_halves(b2_ref[...])
    out = _layer_norm(ALPHA * x_ref[...] + moe, g_ref[...], b_ref[...])

    @pl.when(i < n_first)
    def _():
        op_ref[...] = out

    @pl.when(i >= n_first)
    def _():
        os_ref[...] = out


def _combine_ln(x, pairs, top_gate, g, b, m_first, tm):
    m, d = x.shape
    dp = pairs.shape[2]
    n_first = m_first // tm
    vec = pl.BlockSpec((1, d), lambda i: (0, 0))
    return pl.pallas_call(
        functools.partial(_combine_ln_body, n_first=n_first),
        grid=(m // tm,),
        in_specs=[pl.BlockSpec((tm, d), lambda i: (i, 0)),
                  pl.BlockSpec((None, tm, dp), lambda i: (0, i, 0)),
                  pl.BlockSpec((None, tm, dp), lambda i: (1, i, 0)),
                  pl.BlockSpec((tm, LANES), lambda i: (i, 0)),
                  vec, vec],
        out_specs=[pl.BlockSpec((tm, d), lambda i: (jnp.minimum(i, n_first - 1), 0)),
                   pl.BlockSpec((tm, d), lambda i: (jnp.maximum(i - n_first, 0), 0))],
        out_shape=[jax.ShapeDtypeStruct((m_first, d), F32), jax.ShapeDtypeStruct((m - m_first, d), F32)],
        compiler_params=_params("arbitrary"),
        name="moe_combine_ln",
    )(x, pairs, pairs, top_gate, g.reshape(1, d), b.reshape(1, d))


def _moe_routing(top_idx, tm):
    m = top_idx.shape[0]
    n_pairs = m * TOP_K
    n_tiles = n_pairs // tm + N_EXPERTS
    i32 = jnp.int32
    e_flat = jnp.concatenate([top_idx[:, k] for k in range(TOP_K)])
    order = jnp.argsort(e_flat, stable=True).astype(i32)
    inv = jnp.argsort(order).astype(i32)
    counts = jnp.sum((e_flat[None, :] == jnp.arange(N_EXPERTS, dtype=i32)[:, None]).astype(i32), axis=1)
    starts = jnp.cumsum(counts) - counts
    k_tiles = (counts + tm - 1) // tm
    n_full = jnp.maximum(k_tiles - 2, 0)
    rest = counts - n_full * tm
    rows_a = jnp.where(k_tiles >= 2, (rest + 1) // 2, rest)
    rows_b = rest - rows_a
    tile_ends = jnp.cumsum(k_tiles)
    tile_first = tile_ends - k_tiles
    tile = jnp.arange(n_tiles, dtype=i32)
    tile_expert = jnp.minimum(jnp.searchsorted(tile_ends, tile, side="right"), N_EXPERTS - 1).astype(i32)
    tile_j = tile - tile_first[tile_expert]
    nf_t, ra_t, rb_t = n_full[tile_expert], rows_a[tile_expert], rows_b[tile_expert]
    tile_rows = jnp.where(tile_j < nf_t, tm, jnp.where(tile_j == nf_t, ra_t, rb_t))
    tile_rows = jnp.where(tile < tile_ends[-1], tile_rows, 0).astype(i32)
    tile_rank0 = jnp.where(tile_j <= nf_t, tile_j * tm, nf_t * tm + ra_t)
    slot = jnp.arange(n_tiles * tm, dtype=i32)
    slot_u = slot % tm
    slot_e = jnp.repeat(tile_expert, tm)
    live = slot_u < jnp.repeat(tile_rows, tm)
    src = order[jnp.clip(starts[slot_e] + jnp.repeat(tile_rank0, tm) + slot_u, 0, n_pairs - 1)]
    slot_token = jnp.where(live, src % m, slot % m)
    pair_rank = inv - starts[e_flat]
    nf_p, ra_p = n_full[e_flat], rows_a[e_flat]
    tail = pair_rank - nf_p * tm
    in_b = tail >= ra_p
    pair_j = jnp.where(tail < 0, pair_rank // tm, nf_p + in_b.astype(i32))
    pair_u = jnp.where(tail < 0, pair_rank % tm, tail - jnp.where(in_b, ra_p, 0))
    pair_slot = (tile_first[e_flat] + pair_j) * tm + pair_u
    return tile_expert, tile_rows, slot_token, pair_slot


def _gelu_tanh(x):
    return 0.5 * x * (1.0 + jnp.tanh(math.sqrt(2.0 / math.pi) * (x + 0.044715 * (x * x * x))))


def _ssm_body(*refs, sequential):
    (u_ref, tz_ref, wb_ref, vc_ref, a8r_ref, a8i_ref, d_ref, h0r_ref, h0i_ref, y_ref, hr_ref, hi_ref) = refs[:12]
    scr = refs[12:]
    n_chunks = u_ref.shape[0] // SSM_CHUNK
    sw = a8r_ref.shape[1]
    step_rows = lambda s: pl.ds(s, n_chunks, stride=SSM_CHUNK)
    ucat = jnp.concatenate([u_ref[step_rows(s), :].astype(BF16) for s in range(SSM_CHUNK)], axis=1)
    hl = _dot(ucat, wb_ref[...])
    hl_r = hl[:, :sw]
    hl_i = hl[:, sw:]
    a8r = a8r_ref[...]
    a8i = a8i_ref[...]
    if sequential:
        hlr_scr, hli_scr, hinr_scr, hini_scr = scr
        hlr_scr[...] = hl_r
        hli_scr[...] = hl_i

        def tile_step(kb, carry):
            hr, hi = carry
            base = pl.multiple_of(kb * SUBLANES, SUBLANES)
            tr = hlr_scr[pl.ds(base, SUBLANES), :]
            ti = hli_scr[pl.ds(base, SUBLANES), :]
            rows_r, rows_i = [], []
            for r in range(SUBLANES):
                rows_r.append(hr)
                rows_i.append(hi)
                hr, hi = (a8r * hr - a8i * hi + tr[r:r + 1], a8r * hi + a8i * hr + ti[r:r + 1])
            hinr_scr[pl.ds(base, SUBLANES), :] = jnp.concatenate(rows_r, axis=0)
            hini_scr[pl.ds(base, SUBLANES), :] = jnp.concatenate(rows_i, axis=0)
            return hr, hi

        hr, hi = lax.fori_loop(0, n_chunks // SUBLANES, tile_step, (h0r_ref[...], h0i_ref[...]))
        hr_ref[...] = hr
        hi_ref[...] = hi
        hin_r = hinr_scr[...]
        hin_i = hini_scr[...]
    else:
        gpb = h0r_ref.shape[1]
        p_n = h0r_ref.shape[2]
        hin_r = jnp.concatenate([h0r_ref[:, g, :] for g in range(gpb)], axis=1)
        hin_i = jnp.concatenate([h0i_ref[:, g, :] for g in range(gpb)], axis=1)
        hr = a8r * hin_r - a8i * hin_i + hl_r
        hi = a8r * hin_i + a8i * hin_r + hl_i
        for g in range(gpb):
            hr_ref[:, g, :] = hr[:, g * p_n:(g + 1) * p_n]
            hi_ref[:, g, :] = hi[:, g * p_n:(g + 1) * p_n]
    hcat = jnp.concatenate([hin_r.astype(BF16), hin_i.astype(BF16)], axis=1)
    d = d_ref[...]
    steps_per_dot = 2
    for t0 in range(0, SSM_CHUNK, steps_per_dot):
        cols = slice(t0 * LANES, (t0 + steps_per_dot) * LANES)
        ks = t0 + steps_per_dot
        k = ks * LANES
        y2 = _dot(ucat[:, :k], tz_ref[:k, cols]) + _dot(hcat, vc_ref[:, cols])
        for t in range(t0, t0 + steps_per_dot):
            y = y2[:, (t - t0) * LANES:(t - t0 + 1) * LANES]
            y_ref[step_rows(t), :] = _gelu_tanh(y + d * u_ref[step_rows(t), :])


GROUPS_PER_BLOCK = LANES // SSM_GROUP


def _spread_groups(x, rows_per_group):
    r, w = x.shape
    wl = GROUPS_PER_BLOCK * w
    src = lax.broadcasted_iota(jnp.int32, (w, wl), 0)
    dst = lax.broadcasted_iota(jnp.int32, (w, wl), 1)
    tiled = _dot(x.astype(BF16), (dst % w == src).astype(BF16))
    row_g = (lax.broadcasted_iota(jnp.int32, (r, wl), 0) // rows_per_group) % GROUPS_PER_BLOCK
    lane_g = lax.broadcasted_iota(jnp.int32, (r, wl), 1) // w
    return jnp.where(row_g == lane_g, tiled, 0.0).astype(BF16)


def _ssm_ops_body(conv_ref, abr_ref, abi_ref, vcr_ref, vci_ref, tz_ref, wb_ref, vc_ref):
    t_n = conv_ref.shape[0]
    sw = vcr_ref.shape[1]
    bd = [_spread_groups(conv_ref[tau], SSM_GROUP) for tau in range(t_n)]
    zero = jnp.zeros_like(bd[0])
    for s in range(t_n):
        rs = slice(s * LANES, (s + 1) * LANES)
        tz_ref[rs, :] = jnp.concatenate([bd[t - s] if t >= s else zero for t in range(t_n)], axis=1)
        wb_ref[rs, :] = jnp.concatenate([_spread_groups(abr_ref[t_n - 1 - s], SSM_GROUP),
                                         _spread_groups(abi_ref[t_n - 1 - s], SSM_GROUP)], axis=1)
    for t in range(t_n):
        cs = slice(t * LANES, (t + 1) * LANES)
        vc_ref[:sw, cs] = _spread_groups(vcr_ref[t], SSM_STATE)
        vc_ref[sw:, cs] = _spread_groups(vci_ref[t], SSM_STATE)


def _ssm_operators(a_re, a_im, log_dt, b_re, b_im, c_re, c_im):
    g_n, p_n = a_re.shape
    lam_re = jnp.minimum(a_re, -1e-4)
    lam_im = a_im
    dt = jnp.exp(log_dt)[:, None]
    decay = jnp.exp(lam_re * dt)
    ar = decay * jnp.cos(lam_im * dt)
    ai = decay * jnp.sin(lam_im * dt)
    num_re = ar - 1.0
    den = lam_re * lam_re + lam_im * lam_im
    f_re = (num_re * lam_re + ai * lam_im) / den
    f_im = (ai * lam_re - num_re * lam_im) / den
    bt_re = jnp.swapaxes(b_re, 1, 2)
    bt_im = jnp.swapaxes(b_im, 1, 2)
    bb_re = f_re[:, None, :] * bt_re - f_im[:, None, :] * bt_im
    bb_im = f_re[:, None, :] * bt_im + f_im[:, None, :] * bt_re
    pw_re, pw_im = [jnp.ones_like(ar)], [jnp.zeros_like(ar)]
    for _ in range(SSM_CHUNK):
        pr, pi = pw_re[-1], pw_im[-1]
        pw_re.append(pr * ar - pi * ai)
        pw_im.append(pr * ai + pi * ar)
    pw_re = jnp.stack(pw_re)
    pw_im = jnp.stack(pw_im)
    pk_re = pw_re[:SSM_CHUNK, :, None, :]
    pk_im = pw_im[:SSM_CHUNK, :, None, :]
    ab_re = pk_re * bb_re - pk_im * bb_im
    ab_im = pk_re * bb_im + pk_im * bb_re
    conv = (jnp.einsum("gcp,tgdp->tgdc", c_re, ab_re, precision=HIGHEST)
            - jnp.einsum("gcp,tgdp->tgdc", c_im, ab_im, precision=HIGHEST))
    gpb = GROUPS_PER_BLOCK
    nb = g_n // gpb
    t_n = SSM_CHUNK
    pr = pw_re[1:, :, :, None]
    pi = pw_im[1:, :, :, None]
    ct_re = jnp.swapaxes(c_re, 1, 2)[None]
    ct_im = jnp.swapaxes(c_im, 1, 2)[None]
    tables = [conv.reshape(t_n, nb, LANES, SSM_GROUP),
              ab_re.reshape(t_n, nb, LANES, p_n), ab_im.reshape(t_n, nb, LANES, p_n),
              (ct_re * pr - ct_im * pi).reshape(t_n, nb, gpb * p_n, SSM_GROUP),
              (-(ct_re * pi + ct_im * pr)).reshape(t_n, nb, gpb * p_n, SSM_GROUP)]
    side = t_n * LANES
    op_spec = pl.BlockSpec((None, side, side), lambda j: (j, 0, 0))
    tz, wb, vc = pl.pallas_call(
        _ssm_ops_body,
        grid=(nb,),
        in_specs=[pl.BlockSpec((t_n, None) + t.shape[2:], lambda j: (0, j, 0, 0)) for t in tables],
        out_specs=[op_spec] * 3,
        out_shape=[jax.ShapeDtypeStruct((nb, side, side), BF16)] * 3,
        compiler_params=_params("parallel"),
        name="s5_chunk_operators",
    )(*tables)
    a8r = pw_re[SSM_CHUNK].reshape(1, g_n * p_n)
    a8i = pw_im[SSM_CHUNK].reshape(1, g_n * p_n)
    return tz, wb, vc, a8r, a8i


def _ssm_scan(u, ops, d_skip, h0_re, h0_im, chunk0, n_chunks, sequential):
    tz, wb, vc, a8r, a8i = ops
    nb = tz.shape[0]
    sw = wb.shape[-1] // 2
    rows = n_chunks * SSM_CHUNK
    cblk = chunk0 // n_chunks
    op3 = lambda a: pl.BlockSpec((None,) + a.shape[1:], lambda j: (j, 0, 0))
    srow = pl.BlockSpec((1, sw), lambda j: (0, j))
    if sequential:
        hspec = srow
        scratch = [pltpu.VMEM((n_chunks, sw), F32)] * 4
    else:
        hspec = pl.BlockSpec((n_chunks, sw // SSM_STATE, SSM_STATE), lambda j: (0, j, 0))
        scratch = []
    return pl.pallas_call(
        functools.partial(_ssm_body, sequential=sequential),
        grid=(nb,),
        in_specs=[pl.BlockSpec((rows, LANES), lambda j: (cblk, j)),
                  op3(tz), op3(wb), op3(vc),
                  srow, srow, pl.BlockSpec((1, LANES), lambda j: (0, j)), hspec, hspec],
        out_specs=[pl.BlockSpec((rows, LANES), lambda j: (0, j)), hspec, hspec],
        out_shape=[jax.ShapeDtypeStruct((rows, u.shape[1]), F32),
                   jax.ShapeDtypeStruct(h0_re.shape, F32), jax.ShapeDtypeStruct(h0_im.shape, F32)],
        scratch_shapes=scratch,
        compiler_params=_params("parallel"),
        name="s5_scan_seq" if sequential else "s5_scan_step",
    )(u, tz, wb, vc, a8r, a8i, d_skip.reshape(1, -1), h0_re, h0_im)


def kernel(x_prompt, x_sample, cache_k, cache_v, state_ssm_re, state_ssm_im, page_table, rel_bias, ln_g, ln_b,
           w_qkv, w_o, w_ssm_in, ssm_a_re, ssm_a_im, ssm_log_dt, ssm_b_re, ssm_b_im, ssm_c_re, ssm_c_im, ssm_d,
           w_glu_v, w_glu_g, w_ff_gate, w_ff_up, w_ff_down, w_router, w_moe_gate, w_moe_up, w_moe_down):
    batch, seq, d = x_prompt.shape
    dec_batch, dec_seq, _ = x_sample.shape
    assert batch == 1 and d == D_MODEL and dec_seq == SSM_CHUNK and seq % MOBA_BLOCK == 0
    m_p = batch * seq
    m_s = dec_batch * dec_seq
    m = m_p + m_s
    dq = N_HEADS * HEAD_DIM
    dkv = N_KV_HEADS * HEAD_DIM
    n_pool, page = cache_k.shape[1], cache_k.shape[2]

    x_p = x_prompt.reshape(m_p, d)
    x_s = x_sample.reshape(m_s, d)

    qkv_f32, qkv_b = _qkv_proj(x_p, x_s, w_qkv[0].astype(BF16), tm=1024, tn=512)
    kmean = _block_means(qkv_f32, seq // MOBA_BLOCK)
    tabs, tab_s = _bias_tables(rel_bias)
    ck = cache_k[0].reshape(n_pool, page * N_KV_HEADS, HEAD_DIM)
    cv = cache_v[0].reshape(n_pool, page * N_KV_HEADS, HEAD_DIM)
    attn_p, attn_s = _attention(page_table, qkv_f32, qkv_b, kmean, tabs, tab_s, ck, cv, seq, dec_seq)
    y = _proj_ln(attn_p, attn_s, [w_o[0].astype(BF16)], [x_p, x_s], ln_g[0, 0], ln_b[0, 0], tm=512, tn=512)
    y = _ffn_dense(y, w_ff_gate, w_ff_up, w_ff_down, ln_g[0, 1], ln_b[0, 1], tm=1024, tf=512)

    k_all = qkv_f32[:, dq:dq + dkv]
    v_all = qkv_f32[:, dq + dkv:]
    k_prompt = k_all[:m_p].reshape(1, batch, seq, N_KV_HEADS, HEAD_DIM)
    v_prompt = v_all[:m_p].reshape(1, batch, seq, N_KV_HEADS, HEAD_DIM)
    k_sample = k_all[m_p:].reshape(1, dec_batch, dec_seq, N_KV_HEADS, HEAD_DIM)
    v_sample = v_all[m_p:].reshape(1, dec_batch, dec_seq, N_KV_HEADS, HEAD_DIM)

    u = _matmul(y, w_ssm_in[0].astype(BF16), tm=1024, tn=1024)
    ops = _ssm_operators(ssm_a_re[0], ssm_a_im[0], ssm_log_dt[0], ssm_b_re[0], ssm_b_im[0],
                         ssm_c_re[0], ssm_c_im[0])
    n_state = N_SSM_GROUPS * SSM_STATE
    zero = jnp.zeros((batch, n_state), F32)
    yg_p, hrp, hip = _ssm_scan(u, ops, ssm_d[0], zero, zero, 0, m_p // SSM_CHUNK, True)
    yg_s, hrs, his = _ssm_scan(u, ops, ssm_d[0], state_ssm_re[0], state_ssm_im[0], m_p // SSM_CHUNK, dec_batch,
                               False)
    y, y_packed = _proj_ln(yg_p, yg_s, [w_glu_v[0].astype(BF16), w_glu_g[0].astype(BF16)],
                           [y], ln_g[1, 0], ln_b[1, 0], tm=512, tn=512, packed=True)

    tm_moe = 1024
    top_idx, top_gate = _router(y, w_router[0], tm=512)
    tile_expert, tile_rows, slot_token, pair_slot = _moe_routing(top_idx, tm_moe)
    xs = y_packed.at[lax.optimization_barrier(slot_token)].get(mode="promise_in_bounds")
    ys = _ffn_moe(tile_expert, tile_rows, xs, w_moe_gate[0], w_moe_up[0], w_moe_down[0], tm_moe, tf=512, half=128)
    pairs = ys.at[pair_slot].get(mode="promise_in_bounds").reshape(TOP_K, m, d // 2)
    out_p, out_s = _combine_ln(y, pairs, top_gate, ln_g[1, 1], ln_b[1, 1], m_p, tm=512)

    return (out_p.reshape(batch, seq, d), out_s.reshape(dec_batch, dec_seq, d),
            k_prompt, v_prompt, k_sample, v_sample,
            hrp.reshape(1, batch, N_SSM_GROUPS, SSM_STATE), hip.reshape(1, batch, N_SSM_GROUPS, SSM_STATE),
            hrs.reshape(1, dec_batch, N_SSM_GROUPS, SSM_STATE), his.reshape(1, dec_batch, N_SSM_GROUPS, SSM_STATE))
```

```python
import functools
import math

import numpy as np
import jax
import jax.numpy as jnp
from jax import lax
from jax.experimental import pallas as pl
from jax.experimental.pallas import tpu as pltpu

D_MODEL = 2048
N_HEADS = 16
HEAD_DIM = D_MODEL // N_HEADS
N_KV_HEADS = 4
HEADS_PER_KV = N_HEADS // N_KV_HEADS
MOBA_BLOCK = 256
MOBA_TOPK = 3
NUM_BUCKETS = 32
MAX_DISTANCE = 128
SSM_GROUP = 16
N_SSM_GROUPS = D_MODEL // SSM_GROUP
SSM_STATE = 64
SSM_CHUNK = 8
D_FF = 7 * D_MODEL // 2
N_EXPERTS = 8
TOP_K = 2
DEPTH = 2
ALPHA = (2 * DEPTH) ** 0.25
LN_EPS = 1e-5
NEG_INF = -1e30

LANES = 128
SUBLANES = 8
VMEM_LIMIT = 62 * 1024 * 1024

F32 = jnp.float32
BF16 = jnp.bfloat16
HIGHEST = lax.Precision.HIGHEST
_NT = (((1,), (1,)), ((), ()))


def _params(*sem):
    return pltpu.CompilerParams(dimension_semantics=sem, vmem_limit_bytes=VMEM_LIMIT)


def _dot(a, b):
    return jnp.dot(a, b, preferred_element_type=F32)


def _sigmoid(x):
    return 1.0 / (1.0 + jnp.exp(-x))


def _layer_norm(y, g, b):
    mean = jnp.mean(y, axis=-1, keepdims=True)
    yc = y - mean
    var = jnp.mean(yc * yc, axis=-1, keepdims=True)
    return yc * lax.rsqrt(var + LN_EPS) * g + b


def _mm_body(x_ref, w_ref, o_ref, xb_scr):
    @pl.when(pl.program_id(1) == 0)
    def _():
        xb_scr[...] = x_ref[...].astype(BF16)

    o_ref[...] = _dot(xb_scr[...], w_ref[...])


def _matmul(x, w, tm, tn):
    m, k = x.shape
    n = w.shape[1]
    return pl.pallas_call(
        _mm_body,
        grid=(m // tm, n // tn),
        in_specs=[pl.BlockSpec((tm, k), lambda i, j: (i, 0)),
                  pl.BlockSpec((k, tn), lambda i, j: (0, j))],
        out_specs=pl.BlockSpec((tm, tn), lambda i, j: (i, j)),
        out_shape=jax.ShapeDtypeStruct((m, n), F32),
        scratch_shapes=[pltpu.VMEM((tm, k), BF16)],
        compiler_params=_params("parallel", "arbitrary"),
        name="matmul",
    )(x, w)


def _split_rows(block, n_first):
    return [pl.BlockSpec(block, lambda i, j: (jnp.minimum(i, n_first - 1), 0)),
            pl.BlockSpec(block, lambda i, j: (jnp.maximum(i - n_first, 0), 0))]


def _load_split_bf16(dst, src1, src2, n_first):
    i = pl.program_id(0)
    j = pl.program_id(1)

    @pl.when((j == 0) & (i < n_first))
    def _():
        dst[...] = src1[...].astype(BF16)

    @pl.when((j == 0) & (i >= n_first))
    def _():
        dst[...] = src2[...].astype(BF16)


def _qkv_body(x1_ref, x2_ref, w_ref, of_ref, ob_ref, xb_scr, *, nq_tiles, scale, n_first):
    _load_split_bf16(xb_scr, x1_ref, x2_ref, n_first)
    acc = _dot(xb_scr[...], w_ref[...])
    of_ref[...] = acc
    s = jnp.where(pl.program_id(1) < nq_tiles, scale, 1.0).astype(F32)
    ob_ref[...] = (acc * s).astype(BF16)


def _qkv_proj(x1, x2, w, tm, tn):
    k = x1.shape[1]
    m = x1.shape[0] + x2.shape[0]
    n = w.shape[1]
    n_first = x1.shape[0] // tm
    assert x1.shape[0] % tm == 0 and x2.shape[0] % tm == 0
    body = functools.partial(_qkv_body, nq_tiles=(N_HEADS * HEAD_DIM) // tn, scale=HEAD_DIM ** -0.5,
                             n_first=n_first)
    return pl.pallas_call(
        body,
        grid=(m // tm, n // tn),
        in_specs=_split_rows((tm, k), n_first) + [pl.BlockSpec((k, tn), lambda i, j: (0, j))],
        out_specs=[pl.BlockSpec((tm, tn), lambda i, j: (i, j)),
                   pl.BlockSpec((tm, tn), lambda i, j: (i, j))],
        out_shape=[jax.ShapeDtypeStruct((m, n), F32), jax.ShapeDtypeStruct((m, n), BF16)],
        scratch_shapes=[pltpu.VMEM((tm, k), BF16)],
        compiler_params=_params("parallel", "arbitrary"),
        name="qkv_proj",
    )(x1, x2, w)


def _kmean_body(k_ref, o_ref):
    i = pl.program_id(0)

    @pl.when(i == 0)
    def _():
        o_ref[...] = jnp.zeros_like(o_ref)

    mean = jnp.sum(k_ref[...], axis=0, keepdims=True) * (1.0 / MOBA_BLOCK)
    rows = lax.broadcasted_iota(jnp.int32, o_ref.shape, 0)
    o_ref[...] = jnp.where(rows == i, mean, o_ref[...])


def _block_means(qkv_f32, n_blocks):
    dkv = N_KV_HEADS * HEAD_DIM
    kcol = (N_HEADS * HEAD_DIM) // dkv
    return pl.pallas_call(
        _kmean_body,
        grid=(n_blocks,),
        in_specs=[pl.BlockSpec((MOBA_BLOCK, dkv), lambda i: (i, kcol))],
        out_specs=pl.BlockSpec((LANES, dkv), lambda i: (0, 0)),
        out_shape=jax.ShapeDtypeStruct((LANES, dkv), F32),
        compiler_params=_params("arbitrary"),
        name="moba_block_means",
    )(qkv_f32)


def _t5_bucket_np(dist):
    n = np.maximum(dist, 0)
    max_exact = NUM_BUCKETS // 2
    nf = np.maximum(n, 1).astype(np.float32)
    large = max_exact + (np.log(nf / np.float32(max_exact)) / np.float32(math.log(MAX_DISTANCE / max_exact))
                         * np.float32(NUM_BUCKETS - max_exact)).astype(np.int32)
    large = np.minimum(large, NUM_BUCKETS - 1)
    return np.where(n < max_exact, n, large).astype(np.int32)


def _bucket_tables():
    q = np.arange(MOBA_BLOCK)[:, None]
    k = np.arange(MOBA_BLOCK)[None, :]
    own = np.where(q - k >= 0, _t5_bucket_np(q - k), -1)
    prev = _t5_bucket_np(q - k + MOBA_BLOCK)
    return np.stack([own, prev]).astype(np.int32)


def _bias_body(rb_ref, idx_ref, o_ref, os_ref):
    h = pl.program_id(0)
    far = rb_ref[h, NUM_BUCKETS - 1]
    tabs = []
    for m in range(2):
        idx = idx_ref[m]
        acc = jnp.full(idx.shape, NEG_INF, F32)
        for b in range(NUM_BUCKETS):
            acc = jnp.where(idx == b, rb_ref[h, b] - far, acc)
        o_ref[0, m] = acc
        tabs.append(acc)
    os_ref[0] = jnp.concatenate([tabs[1][:SUBLANES, :], tabs[0][:SUBLANES, :LANES]], axis=1)


def _bias_tables(rel_bias):
    idx = jnp.asarray(_bucket_tables())
    return pl.pallas_call(
        _bias_body,
        grid=(N_HEADS,),
        in_specs=[pl.BlockSpec(memory_space=pltpu.SMEM),
                  pl.BlockSpec((2, MOBA_BLOCK, MOBA_BLOCK), lambda h: (0, 0, 0))],
        out_specs=[pl.BlockSpec((1, 2, MOBA_BLOCK, MOBA_BLOCK), lambda h: (h, 0, 0, 0)),
                   pl.BlockSpec((1, SUBLANES, MOBA_BLOCK + LANES), lambda h: (h, 0, 0))],
        out_shape=[jax.ShapeDtypeStruct((N_HEADS, 2, MOBA_BLOCK, MOBA_BLOCK), F32),
                   jax.ShapeDtypeStruct((N_HEADS, SUBLANES, MOBA_BLOCK + LANES), F32)],
        compiler_params=_params("parallel"),
        name="moba_bias_tables",
    )(rel_bias, idx)


def _select_blocks(gate, n_past):
    lane = lax.broadcasted_iota(jnp.int32, gate.shape, 1)
    lane_f = lane.astype(F32)
    g = jnp.where(lane < n_past, gate, NEG_INF)
    sel = jnp.full(gate.shape, NEG_INF, F32)
    for _ in range(MOBA_TOPK):
        mx = jnp.max(g, axis=1, keepdims=True)
        idx = jnp.min(jnp.where(g == mx, lane_f, float(gate.shape[1])), axis=1, keepdims=True)
        pick = lane_f == idx
        sel = jnp.where(pick, 0.0, sel)
        g = jnp.where(pick, -jnp.inf, g)
    return jnp.where(lane < n_past, sel, 0.0)


def _attn_prompt_body(qf_ref, qb_ref, k_ref, v_ref, km_ref, tab_ref, o_ref, qa_scr, m_scr, l_scr, acc_scr):
    i = pl.program_id(1)
    rows = HEADS_PER_KV * MOBA_BLOCK
    km = km_ref[...]
    lane = lax.broadcasted_iota(jnp.int32, (MOBA_BLOCK, LANES), 1)
    dummy = LANES - 1
    for hh in range(HEADS_PER_KV):
        cs = slice(hh * HEAD_DIM, (hh + 1) * HEAD_DIM)
        rs = slice(hh * MOBA_BLOCK, (hh + 1) * MOBA_BLOCK)
        gate = lax.dot_general(qf_ref[:, cs], km, _NT, precision=HIGHEST, preferred_element_type=F32)
        mask = jnp.where(lane == dummy, NEG_INF, _select_blocks(gate, i))
        qa_scr[rs, :HEAD_DIM] = qb_ref[:, cs]
        qa_scr[rs, HEAD_DIM:] = mask.astype(BF16)
    qa = qa_scr[...]

    def scores(j, mask_lane):
        start = pl.multiple_of(j * MOBA_BLOCK, MOBA_BLOCK)
        kj = k_ref[pl.ds(start, MOBA_BLOCK), :]
        vj = v_ref[pl.ds(start, MOBA_BLOCK), :]
        rhs = jnp.concatenate([kj, (lane == mask_lane).astype(BF16)], axis=1)
        return lax.dot_general(qa, rhs, _NT, preferred_element_type=F32), vj

    def softmax_parts(parts, m_new):
        m2 = jnp.concatenate([m_new, m_new], axis=1)
        l_add = None
        acc_add = None
        for s, vj in parts:
            p = jnp.exp(s - m2)
            ls = jnp.sum(p, axis=1, keepdims=True)
            pv = _dot(p.astype(BF16), vj)
            l_add = ls if l_add is None else l_add + ls
            acc_add = pv if acc_add is None else acc_add + pv
        return l_add, acc_add

    has_prev = i >= 1
    s_own, v_own = scores(i, -1)
    s_own = s_own + tab_ref[:, 0].reshape(rows, MOBA_BLOCK)
    s_prev, v_prev = scores(jnp.maximum(i - 1, 0), jnp.where(has_prev, i - 1, dummy))
    s_prev = s_prev + tab_ref[:, 1].reshape(rows, MOBA_BLOCK)
    m0 = jnp.maximum(jnp.max(s_own, axis=1, keepdims=True), jnp.max(s_prev, axis=1, keepdims=True))
    m0 = jnp.broadcast_to(m0, (rows, LANES))
    l0, acc0 = softmax_parts([(s_own, v_own), (s_prev, v_prev)], m0)
    m_scr[...] = m0
    l_scr[...] = jnp.broadcast_to(l0, (rows, LANES))
    acc_scr[...] = acc0

    n_far = jnp.maximum(i - 1, 0)

    def far_quad(t, carry):
        for j0 in (4 * t, 4 * t + 2):
            parts = []
            for j in (j0, j0 + 1):
                ok = j < n_far
                parts.append(scores(jnp.where(ok, j, 0), jnp.where(ok, j, dummy)))
            m_prev = m_scr[...]
            m_new = m_prev
            for s, _ in parts:
                m_new = jnp.maximum(m_new, jnp.max(s, axis=1, keepdims=True))
            alpha = jnp.exp(m_prev - m_new)
            l_add, acc_add = softmax_parts(parts, m_new)
            l_scr[...] = alpha * l_scr[...] + l_add
            acc_scr[...] = alpha * acc_scr[...] + acc_add
            m_scr[...] = m_new
        return carry

    lax.fori_loop(0, (n_far + 3) // 4, far_quad, 0)

    out = acc_scr[...] / l_scr[...]
    for hh in range(HEADS_PER_KV):
        o_ref[:, hh * HEAD_DIM:(hh + 1) * HEAD_DIM] = out[hh * MOBA_BLOCK:(hh + 1) * MOBA_BLOCK]


def _attn_sample_core(first_step, qkv_ref, kp, vp, tab_ref, o_ref, kall, vall, expand, page, dec_seq):
    n_pages = len(kp)
    past = n_pages * page
    n_past_blocks = past // MOBA_BLOCK
    pages_per_block = MOBA_BLOCK // page
    near0 = past - MOBA_BLOCK
    total = past + LANES
    dq = N_HEADS * HEAD_DIM
    dkv = N_KV_HEADS * HEAD_DIM
    hq = HEADS_PER_KV * dec_seq
    rows = N_HEADS * dec_seq
    pad = jnp.zeros((LANES - dec_seq, HEAD_DIM), F32)

    @pl.when(first_step)
    def _():
        key_blk = lax.broadcasted_iota(jnp.int32, (LANES, total), 1) // MOBA_BLOCK
        expand[...] = (key_blk == lax.broadcasted_iota(jnp.int32, (LANES, total), 0)).astype(BF16)

    kmeans = []
    for g in range(N_KV_HEADS):
        means = []
        for blk in range(n_past_blocks):
            tot = jnp.zeros((1, HEAD_DIM), F32)
            for pp in range(pages_per_block):
                pg = blk * pages_per_block + pp
                kk = kp[pg][pl.ds(g, page, stride=N_KV_HEADS), :]
                vv = vp[pg][pl.ds(g, page, stride=N_KV_HEADS), :]
                tot = tot + jnp.sum(kk, axis=0, keepdims=True)
                kall[g, pg * page:(pg + 1) * page, :] = kk.astype(BF16)
                vall[g, pg * page:(pg + 1) * page, :] = vv.astype(BF16)
            means.append(tot * (1.0 / MOBA_BLOCK))
        kmeans.append(jnp.concatenate(means + [jnp.zeros((LANES - n_past_blocks, HEAD_DIM), F32)], axis=0))
        knew = qkv_ref[:, dq + g * HEAD_DIM:dq + (g + 1) * HEAD_DIM]
        vnew = qkv_ref[:, dq + dkv + g * HEAD_DIM:dq + dkv + (g + 1) * HEAD_DIM]
        kall[g, past:total, :] = jnp.concatenate([knew, pad], axis=0).astype(BF16)
        vall[g, past:total, :] = jnp.concatenate([vnew, pad], axis=0).astype(BF16)

    qs = jnp.concatenate([qkv_ref[:, h * HEAD_DIM:(h + 1) * HEAD_DIM] for h in range(N_HEADS)], axis=0)
    gate_all = lax.dot_general(qs, jnp.concatenate(kmeans, axis=0), _NT, precision=HIGHEST,
                               preferred_element_type=F32)
    row_g = lax.broadcasted_iota(jnp.int32, (rows, LANES), 0) // hq
    gate = gate_all[:, :LANES]
    for g in range(1, N_KV_HEADS):
        gate = jnp.where(row_g == g, gate_all[:, g * LANES:(g + 1) * LANES], gate)
    selm = _select_blocks(gate, n_past_blocks).astype(BF16)

    qb = (qs * (HEAD_DIM ** -0.5)).astype(BF16)
    s = jnp.concatenate([lax.dot_general(qb[g * hq:(g + 1) * hq], kall[g], _NT, preferred_element_type=F32)
                         for g in range(N_KV_HEADS)], axis=0)
    s = s + _dot(selm, expand[...])
    s_far = s[:, :near0]
    s_near = s[:, near0:] + tab_ref[...].reshape(rows, MOBA_BLOCK + LANES)
    m = jnp.maximum(jnp.max(s_far, axis=1, keepdims=True), jnp.max(s_near, axis=1, keepdims=True))
    p_far = jnp.exp(s_far - m)
    p_near = jnp.exp(s_near - m)
    inv_l = 1.0 / (jnp.sum(p_far, axis=1, keepdims=True) + jnp.sum(p_near, axis=1, keepdims=True))
    p_far = p_far.astype(BF16)
    p_near = p_near.astype(BF16)
    for g in range(N_KV_HEADS):
        rs = slice(g * hq, (g + 1) * hq)
        out = (_dot(p_far[rs], vall[g, :near0, :]) + _dot(p_near[rs], vall[g, near0:, :])) * inv_l[rs]
        for hh in range(HEADS_PER_KV):
            c0 = (g * HEADS_PER_KV + hh) * HEAD_DIM
            o_ref[:, c0:c0 + HEAD_DIM] = out[hh * dec_seq:(hh + 1) * dec_seq]


def _attn_body(pt_ref, *refs, n_pages, page, dec_seq):
    del pt_ref
    n_prompt_in = 6
    n_sample_in = 2 + 2 * n_pages
    prompt_in = refs[:n_prompt_in]
    qkv_ref, *pages, tab_ref = refs[n_prompt_in:n_prompt_in + n_sample_in]
    op_ref, os_ref = refs[n_prompt_in + n_sample_in:n_prompt_in + n_sample_in + 2]
    scr = refs[n_prompt_in + n_sample_in + 2:]
    _attn_prompt_body(*prompt_in, op_ref, *scr[:4])
    first_step = (pl.program_id(0) == 0) & (pl.program_id(1) == 0)
    _attn_sample_core(first_step, qkv_ref, pages[:n_pages], pages[n_pages:], tab_ref, os_ref, *scr[4:],
                      page, dec_seq)


def _attention(page_table, qkv_f32, qkv_b, kmean, tabs, tab_s, cache_k, cache_v, seq, dec_seq):
    nblk = seq // MOBA_BLOCK
    assert nblk < LANES - 1
    dec_batch, n_pages = page_table.shape
    assert dec_batch == N_KV_HEADS * nblk
    page = cache_k.shape[1] // N_KV_HEADS
    total = n_pages * page + LANES
    blk0 = seq // dec_seq
    gw = HEADS_PER_KV * HEAD_DIM
    kcol = (N_HEADS * HEAD_DIM) // HEAD_DIM
    vcol = kcol + N_KV_HEADS
    rows = HEADS_PER_KV * MOBA_BLOCK

    def page_spec(p):
        return pl.BlockSpec((None, page * N_KV_HEADS, HEAD_DIM),
                            lambda g, i, pt, p=p: (pt[g * nblk + i, p], 0, 0))

    grid_spec = pltpu.PrefetchScalarGridSpec(
        num_scalar_prefetch=1,
        grid=(N_KV_HEADS, nblk),
        in_specs=([pl.BlockSpec((MOBA_BLOCK, gw), lambda g, i, pt: (i, g)),
                   pl.BlockSpec((MOBA_BLOCK, gw), lambda g, i, pt: (i, g)),
                   pl.BlockSpec((seq, HEAD_DIM), lambda g, i, pt: (0, kcol + g)),
                   pl.BlockSpec((seq, HEAD_DIM), lambda g, i, pt: (0, vcol + g)),
                   pl.BlockSpec((LANES, HEAD_DIM), lambda g, i, pt: (0, g)),
                   pl.BlockSpec((HEADS_PER_KV, 2, MOBA_BLOCK, MOBA_BLOCK), lambda g, i, pt: (g, 0, 0, 0)),
                   pl.BlockSpec((dec_seq, qkv_f32.shape[1]), lambda g, i, pt: (blk0 + g * nblk + i, 0))]
                  + [page_spec(p) for p in range(n_pages)]
                  + [page_spec(p) for p in range(n_pages)]
                  + [pl.BlockSpec(tab_s.shape, lambda g, i, pt: (0, 0, 0))]),
        out_specs=[pl.BlockSpec((MOBA_BLOCK, gw), lambda g, i, pt: (i, g)),
                   pl.BlockSpec((dec_seq, N_HEADS * HEAD_DIM), lambda g, i, pt: (g * nblk + i, 0))],
        scratch_shapes=[pltpu.VMEM((rows, 2 * HEAD_DIM), BF16),
                        pltpu.VMEM((rows, LANES), F32),
                        pltpu.VMEM((rows, LANES), F32),
                        pltpu.VMEM((rows, HEAD_DIM), F32),
                        pltpu.VMEM((N_KV_HEADS, total, HEAD_DIM), BF16),
                        pltpu.VMEM((N_KV_HEADS, total, HEAD_DIM), BF16),
                        pltpu.VMEM((LANES, total), BF16)],
    )
    return pl.pallas_call(
        functools.partial(_attn_body, n_pages=n_pages, page=page, dec_seq=dec_seq),
        grid_spec=grid_spec,
        out_shape=[jax.ShapeDtypeStruct((seq, N_HEADS * HEAD_DIM), F32),
                   jax.ShapeDtypeStruct((dec_batch * dec_seq, N_HEADS * HEAD_DIM), F32)],
        compiler_params=_params("arbitrary", "arbitrary"),
        name="moba_attention",
    )(page_table, qkv_f32, qkv_b, qkv_b, qkv_b, kmean, tabs, qkv_f32,
      *([cache_k] * n_pages), *([cache_v] * n_pages), tab_s)


def _pack_bf16_halves(x):
    c = x.shape[1] // 2
    lo = lax.bitcast_convert_type(x[:, :c].astype(BF16).astype(F32), jnp.uint32)
    hi = lax.bitcast_convert_type(x[:, c:].astype(BF16).astype(F32), jnp.uint32)
    return lax.bitcast_convert_type(hi | (lo >> 16), F32)


def _unpack_bf16_halves(w):
    u = lax.bitcast_convert_type(w, jnp.uint32)
    lo = lax.bitcast_convert_type(u << 16, F32)
    hi = lax.bitcast_convert_type(u & jnp.uint32(0xFFFF0000), F32)
    return jnp.concatenate([lo, hi], axis=1)


def _proj_ln_body(*refs, n_w, n_x, n_tiles, n_first, packed):
    a1_ref, a2_ref = refs[:2]
    w_refs = refs[2:2 + n_w]
    x_refs = refs[2 + n_w:2 + n_w + n_x]
    if packed:
        g_ref, b_ref, o_ref, op_ref, a_scr, z_scr = refs[2 + n_w + n_x:]
    else:
        g_ref, b_ref, o_ref, a_scr, z_scr = refs[2 + n_w + n_x:]
    i = pl.program_id(0)
    j = pl.program_id(1)
    _load_split_bf16(a_scr, a1_ref, a2_ref, n_first)
    a = a_scr[...]
    z = _dot(a, w_refs[0][...])
    if n_w == 2:
        z = z * _sigmoid(_dot(a, w_refs[1][...]))
    z_scr[j] = z

    @pl.when(j == n_tiles - 1)
    def _():
        zfull = jnp.concatenate([z_scr[t] for t in range(n_tiles)], axis=1)
        x = x_refs[0][...] if n_x == 1 else jnp.where(i < n_first, x_refs[0][...], x_refs[1][...])
        out = _layer_norm(ALPHA * x + zfull, g_ref[...], b_ref[...])
        o_ref[...] = out
        if packed:
            op_ref[...] = _pack_bf16_halves(out)


def _proj_ln(a1, a2, ws, xs, g, b, tm, tn, packed=False):
    k = a1.shape[1]
    m = a1.shape[0] + a2.shape[0]
    n = ws[0].shape[1]
    n_first = a1.shape[0] // tm
    assert a1.shape[0] % tm == 0 and a2.shape[0] % tm == 0
    assert len(xs) == 1 or xs[0].shape[0] == a1.shape[0]
    n_tiles = n // tn
    body = functools.partial(_proj_ln_body, n_w=len(ws), n_x=len(xs), n_tiles=n_tiles, n_first=n_first,
                             packed=packed)
    row = pl.BlockSpec((tm, n), lambda i, j: (i, 0))
    vec = pl.BlockSpec((1, n), lambda i, j: (0, 0))
    out_specs, out_shape = row, jax.ShapeDtypeStruct((m, n), F32)
    if packed:
        out_specs = [row, pl.BlockSpec((tm, n // 2), lambda i, j: (i, 0))]
        out_shape = [out_shape, jax.ShapeDtypeStruct((m, n // 2), F32)]
    return pl.pallas_call(
        body,
        grid=(m // tm, n_tiles),
        in_specs=(_split_rows((tm, k), n_first)
                  + [pl.BlockSpec((k, tn), lambda i, j: (0, j)) for _ in ws]
                  + ([row] if len(xs) == 1 else _split_rows((tm, n), n_first))
                  + [vec, vec]),
        out_specs=out_specs,
        out_shape=out_shape,
        scratch_shapes=[pltpu.VMEM((tm, k), BF16), pltpu.VMEM((n_tiles, tm, tn), F32)],
        compiler_params=_params("parallel", "arbitrary"),
        name="glu_res_ln" if len(ws) == 2 else "proj_res_ln",
    )(a1, a2, *ws, *xs, g.reshape(1, n), b.reshape(1, n))


def _ffn_body(te_ref, tr_ref, *refs, n_f, dense, half):
    del te_ref
    if dense:
        x_ref, wg_ref, wu_ref, wd_ref, g_ref, b_ref, o_ref, xb = refs
        acc = o_ref
    else:
        x_ref, wg_ref, wu_ref, wd_ref, o_ref, xb, acc = refs
    tm = x_ref.shape[0]
    t = pl.program_id(0)
    f = pl.program_id(1)
    n_rows = tr_ref[t]

    @pl.when(f == 0)
    def _():
        x = x_ref[...] if dense else _unpack_bf16_halves(x_ref[...])
        xb[...] = x.astype(BF16)
        acc[...] = jnp.zeros_like(acc)

    def swiglu_rows(n):
        x = xb[:n, :]
        hg = _dot(x, wg_ref[...].astype(BF16))
        hu = _dot(x, wu_ref[...].astype(BF16))
        h = (hg * _sigmoid(hg)) * hu
        acc[:n, :] += _dot(h.astype(BF16), wd_ref[...].astype(BF16))

    for n in range(half, tm + 1, half):
        pl.when((n_rows > n - half) & (n_rows <= n))(functools.partial(swiglu_rows, n))

    @pl.when(f == n_f - 1)
    def _():
        if dense:
            o_ref[...] = _layer_norm(ALPHA * x_ref[...] + acc[...], g_ref[...], b_ref[...])
        else:
            o_ref[...] = _pack_bf16_halves(acc[...])


def _ffn_call(tile_expert, tile_rows, x, wg, wu, wd, extra, extra_specs, tm, tf, half, dense, name):
    s = x.shape[0]
    d = wg.shape[-2]
    n_t = s // tm
    n_f = wg.shape[-1] // tf
    live = lambda t, tr: jnp.minimum(tr[t], 1)
    x_spec = pl.BlockSpec((tm, x.shape[1]), lambda t, f, te, tr: (t, 0))
    x_in = pl.BlockSpec((tm, x.shape[1]), lambda t, f, te, tr: (t, 0), pipeline_mode=pl.Buffered(1)) if dense else x_spec
    wgu = pl.BlockSpec((None, d, tf), lambda t, f, te, tr: (te[t], 0, f * live(t, tr)))
    wds = pl.BlockSpec((None, tf, d), lambda t, f, te, tr: (te[t], f * live(t, tr), 0))
    grid_spec = pltpu.PrefetchScalarGridSpec(
        num_scalar_prefetch=2, grid=(n_t, n_f),
        in_specs=[x_in, wgu, wgu, wds] + extra_specs,
        out_specs=x_spec,
        scratch_shapes=[pltpu.VMEM((tm, d), BF16)] + ([] if dense else [pltpu.VMEM((tm, d), F32)]))
    return pl.pallas_call(
        functools.partial(_ffn_body, n_f=n_f, dense=dense, half=half),
        grid_spec=grid_spec,
        out_shape=jax.ShapeDtypeStruct(x.shape, x.dtype),
        compiler_params=_params("parallel", "arbitrary"),
        name=name,
    )(tile_expert, tile_rows, x, wg, wu, wd, *extra)


def _ffn_dense(x, wg, wu, wd, g, b, tm, tf):
    m, d = x.shape
    n_t = m // tm
    vec = pl.BlockSpec((1, d), lambda t, f, te, tr: (0, 0))
    return _ffn_call(jnp.zeros((n_t,), jnp.int32), jnp.full((n_t,), tm, jnp.int32), x, wg, wu, wd,
                     [g.reshape(1, d), b.reshape(1, d)], [vec, vec], tm, tf, tm, True, "ffn_res_ln")


def _ffn_moe(tile_expert, tile_rows, xs, wg, wu, wd, tm, tf, half):
    return _ffn_call(tile_expert, tile_rows, xs, wg, wu, wd, [], [], tm, tf, half, False, "moe_ffn")


def _router_body(y_ref, w_ref, i_ref, g_ref):
    y = y_ref[...]
    w = w_ref[...]
    y_hi = y.astype(BF16)
    w_hi = w.astype(BF16)
    y_lo = (y - y_hi.astype(F32)).astype(BF16)
    w_lo = (w - w_hi.astype(F32)).astype(BF16)
    logits = _dot(y_hi, w_hi) + (_dot(y_hi, w_lo) + _dot(y_lo, w_hi))
    lane = lax.broadcasted_iota(jnp.int32, logits.shape, 1)
    lane_f = lane.astype(F32)
    l1 = jnp.where(lane < N_EXPERTS, logits, -jnp.inf)
    m1 = jnp.max(l1, axis=1, keepdims=True)
    i1 = jnp.min(jnp.where(l1 == m1, lane_f, float(LANES)), axis=1, keepdims=True)
    l2 = jnp.where(lane_f == i1, -jnp.inf, l1)
    m2 = jnp.max(l2, axis=1, keepdims=True)
    i2 = jnp.min(jnp.where(l2 == m2, lane_f, float(LANES)), axis=1, keepdims=True)
    e = jnp.exp(m2 - m1)
    g1 = 1.0 / (1.0 + e)
    g2 = e / (1.0 + e)
    i_ref[...] = jnp.where(lane == 0, i1, jnp.where(lane == 1, i2, 0.0)).astype(jnp.int32)
    g_ref[...] = jnp.where(lane == 0, g1, jnp.where(lane == 1, g2, 0.0))


def _router(y, w_router, tm):
    m, d = y.shape
    wr = jnp.zeros((d, LANES), F32).at[:, :N_EXPERTS].set(w_router)
    row = pl.BlockSpec((tm, LANES), lambda i: (i, 0))
    return pl.pallas_call(
        _router_body,
        grid=(m // tm,),
        in_specs=[pl.BlockSpec((tm, d), lambda i: (i, 0)), pl.BlockSpec((d, LANES), lambda i: (0, 0))],
        out_specs=[row, row],
        out_shape=[jax.ShapeDtypeStruct((m, LANES), jnp.int32), jax.ShapeDtypeStruct((m, LANES), F32)],
        compiler_params=_params("parallel"),
        name="moe_router",
    )(y, wr)


def _combine_ln_body(x_ref, a_ref, b2_ref, tg_ref, g_ref, b_ref, op_ref, os_ref, *, n_first):
    i = pl.program_id(0)
    tg = tg_ref[...]
    moe = tg[:, 0:1] * _unpack_bf16_halves(a_ref[...]) + tg[:, 1:2] * _unpack_bf16_halves(b2_ref[...])
    out = _layer_norm(ALPHA * x_ref[...] + moe, g_ref[...], b_ref[...])

    @pl.when(i < n_first)
    def _():
        op_ref[...] = out

    @pl.when(i >= n_first)
    def _():
        os_ref[...] = out


def _combine_ln(x, pairs, top_gate, g, b, m_first, tm):
    m, d = x.shape
    dp = pairs.shape[2]
    n_first = m_first // tm
    vec = pl.BlockSpec((1, d), lambda i: (0, 0))
    return pl.pallas_call(
        functools.partial(_combine_ln_body, n_first=n_first),
        grid=(m // tm,),
        in_specs=[pl.BlockSpec((tm, d), lambda i: (i, 0)),
                  pl.BlockSpec((None, tm, dp), lambda i: (0, i, 0)),
                  pl.BlockSpec((None, tm, dp), lambda i: (1, i, 0)),
                  pl.BlockSpec((tm, LANES), lambda i: (i, 0)),
                  vec, vec],
        out_specs=[pl.BlockSpec((tm, d), lambda i: (jnp.minimum(i, n_first - 1), 0)),
                   pl.BlockSpec((tm, d), lambda i: (jnp.maximum(i - n_first, 0), 0))],
        out_shape=[jax.ShapeDtypeStruct((m_first, d), F32), jax.ShapeDtypeStruct((m - m_first, d), F32)],
        compiler_params=_params("arbitrary"),
        name="moe_combine_ln",
    )(x, pairs, pairs, top_gate, g.reshape(1, d), b.reshape(1, d))


def _moe_routing(top_idx, tm):
    m = top_idx.shape[0]
    n_pairs = m * TOP_K
    n_tiles = n_pairs // tm + N_EXPERTS
    i32 = jnp.int32
    e_flat = jnp.concatenate([top_idx[:, k] for k in range(TOP_K)])
    order = jnp.argsort(e_flat, stable=True).astype(i32)
    inv = jnp.argsort(order).astype(i32)
    counts = jnp.sum((e_flat[None, :] == jnp.arange(N_EXPERTS, dtype=i32)[:, None]).astype(i32), axis=1)
    starts = jnp.cumsum(counts) - counts
    k_tiles = (counts + tm - 1) // tm
    n_full = jnp.maximum(k_tiles - 2, 0)
    rest = counts - n_full * tm
    rows_a = jnp.where(k_tiles >= 2, (rest + 1) // 2, rest)
    rows_b = rest - rows_a
    tile_ends = jnp.cumsum(k_tiles)
    tile_first = tile_ends - k_tiles
    tile = jnp.arange(n_tiles, dtype=i32)
    tile_expert = jnp.minimum(jnp.searchsorted(tile_ends, tile, side="right"), N_EXPERTS - 1).astype(i32)
    tile_j = tile - tile_first[tile_expert]
    nf_t, ra_t, rb_t = n_full[tile_expert], rows_a[tile_expert], rows_b[tile_expert]
    tile_rows = jnp.where(tile_j < nf_t, tm, jnp.where(tile_j == nf_t, ra_t, rb_t))
    tile_rows = jnp.where(tile < tile_ends[-1], tile_rows, 0).astype(i32)
    tile_rank0 = jnp.where(tile_j <= nf_t, tile_j * tm, nf_t * tm + ra_t)
    slot = jnp.arange(n_tiles * tm, dtype=i32)
    slot_u = slot % tm
    slot_e = jnp.repeat(tile_expert, tm)
    live = slot_u < jnp.repeat(tile_rows, tm)
    src = order[jnp.clip(starts[slot_e] + jnp.repeat(tile_rank0, tm) + slot_u, 0, n_pairs - 1)]
    slot_token = jnp.where(live, src % m, slot % m)
    pair_rank = inv - starts[e_flat]
    nf_p, ra_p = n_full[e_flat], rows_a[e_flat]
    tail = pair_rank - nf_p * tm
    in_b = tail >= ra_p
    pair_j = jnp.where(tail < 0, pair_rank // tm, nf_p + in_b.astype(i32))
    pair_u = jnp.where(tail < 0, pair_rank % tm, tail - jnp.where(in_b, ra_p, 0))
    pair_slot = (tile_first[e_flat] + pair_j) * tm + pair_u
    return tile_expert, tile_rows, slot_token, pair_slot


def _gelu_tanh(x):
    return 0.5 * x * (1.0 + jnp.tanh(math.sqrt(2.0 / math.pi) * (x + 0.044715 * (x * x * x))))


def _ssm_body(*refs, sequential):
    (u_ref, tz_ref, wb_ref, vc_ref, a8r_ref, a8i_ref, d_ref, h0r_ref, h0i_ref, y_ref, hr_ref, hi_ref) = refs[:12]
    scr = refs[12:]
    n_chunks = u_ref.shape[0] // SSM_CHUNK
    sw = a8r_ref.shape[1]
    step_rows = lambda s: pl.ds(s, n_chunks, stride=SSM_CHUNK)
    ucat = jnp.concatenate([u_ref[step_rows(s), :].astype(BF16) for s in range(SSM_CHUNK)], axis=1)
    hl = _dot(ucat, wb_ref[...])
    hl_r = hl[:, :sw]
    hl_i = hl[:, sw:]
    a8r = a8r_ref[...]
    a8i = a8i_ref[...]
    if sequential:
        hlr_scr, hli_scr, hinr_scr, hini_scr = scr
        hlr_scr[...] = hl_r
        hli_scr[...] = hl_i

        def tile_step(kb, carry):
            hr, hi = carry
            base = pl.multiple_of(kb * SUBLANES, SUBLANES)
            tr = hlr_scr[pl.ds(base, SUBLANES), :]
            ti = hli_scr[pl.ds(base, SUBLANES), :]
            rows_r, rows_i = [], []
            for r in range(SUBLANES):
                rows_r.append(hr)
                rows_i.append(hi)
                hr, hi = (a8r * hr - a8i * hi + tr[r:r + 1], a8r * hi + a8i * hr + ti[r:r + 1])
            hinr_scr[pl.ds(base, SUBLANES), :] = jnp.concatenate(rows_r, axis=0)
            hini_scr[pl.ds(base, SUBLANES), :] = jnp.concatenate(rows_i, axis=0)
            return hr, hi

        hr, hi = lax.fori_loop(0, n_chunks // SUBLANES, tile_step, (h0r_ref[...], h0i_ref[...]))
        hr_ref[...] = hr
        hi_ref[...] = hi
        hin_r = hinr_scr[...]
        hin_i = hini_scr[...]
    else:
        gpb = h0r_ref.shape[1]
        p_n = h0r_ref.shape[2]
        hin_r = jnp.concatenate([h0r_ref[:, g, :] for g in range(gpb)], axis=1)
        hin_i = jnp.concatenate([h0i_ref[:, g, :] for g in range(gpb)], axis=1)
        hr = a8r * hin_r - a8i * hin_i + hl_r
        hi = a8r * hin_i + a8i * hin_r + hl_i
        for g in range(gpb):
            hr_ref[:, g, :] = hr[:, g * p_n:(g + 1) * p_n]
            hi_ref[:, g, :] = hi[:, g * p_n:(g + 1) * p_n]
    hcat = jnp.concatenate([hin_r.astype(BF16), hin_i.astype(BF16)], axis=1)
    d = d_ref[...]
    steps_per_dot = 2
    for t0 in range(0, SSM_CHUNK, steps_per_dot):
        cols = slice(t0 * LANES, (t0 + steps_per_dot) * LANES)
        ks = t0 + steps_per_dot
        k = ks * LANES
        y2 = _dot(ucat[:, :k], tz_ref[:k, cols]) + _dot(hcat, vc_ref[:, cols])
        for t in range(t0, t0 + steps_per_dot):
            y = y2[:, (t - t0) * LANES:(t - t0 + 1) * LANES]
            y_ref[step_rows(t), :] = _gelu_tanh(y + d * u_ref[step_rows(t), :])


GROUPS_PER_BLOCK = LANES // SSM_GROUP


def _spread_groups(x, rows_per_group):
    r, w = x.shape
    wl = GROUPS_PER_BLOCK * w
    src = lax.broadcasted_iota(jnp.int32, (w, wl), 0)
    dst = lax.broadcasted_iota(jnp.int32, (w, wl), 1)
    tiled = _dot(x.astype(BF16), (dst % w == src).astype(BF16))
    row_g = (lax.broadcasted_iota(jnp.int32, (r, wl), 0) // rows_per_group) % GROUPS_PER_BLOCK
    lane_g = lax.broadcasted_iota(jnp.int32, (r, wl), 1) // w
    return jnp.where(row_g == lane_g, tiled, 0.0).astype(BF16)


def _ssm_ops_body(conv_ref, abr_ref, abi_ref, vcr_ref, vci_ref, tz_ref, wb_ref, vc_ref):
    t_n = conv_ref.shape[0]
    sw = vcr_ref.shape[1]
    bd = [_spread_groups(conv_ref[tau], SSM_GROUP) for tau in range(t_n)]
    zero = jnp.zeros_like(bd[0])
    for s in range(t_n):
        rs = slice(s * LANES, (s + 1) * LANES)
        tz_ref[rs, :] = jnp.concatenate([bd[t - s] if t >= s else zero for t in range(t_n)], axis=1)
        wb_ref[rs, :] = jnp.concatenate([_spread_groups(abr_ref[t_n - 1 - s], SSM_GROUP),
                                         _spread_groups(abi_ref[t_n - 1 - s], SSM_GROUP)], axis=1)
    for t in range(t_n):
        cs = slice(t * LANES, (t + 1) * LANES)
        vc_ref[:sw, cs] = _spread_groups(vcr_ref[t], SSM_STATE)
        vc_ref[sw:, cs] = _spread_groups(vci_ref[t], SSM_STATE)


def _ssm_operators(a_re, a_im, log_dt, b_re, b_im, c_re, c_im):
    g_n, p_n = a_re.shape
    lam_re = jnp.minimum(a_re, -1e-4)
    lam_im = a_im
    dt = jnp.exp(log_dt)[:, None]
    decay = jnp.exp(lam_re * dt)
    ar = decay * jnp.cos(lam_im * dt)
    ai = decay * jnp.sin(lam_im * dt)
    num_re = ar - 1.0
    den = lam_re * lam_re + lam_im * lam_im
    f_re = (num_re * lam_re + ai * lam_im) / den
    f_im = (ai * lam_re - num_re * lam_im) / den
    bt_re = jnp.swapaxes(b_re, 1, 2)
    bt_im = jnp.swapaxes(b_im, 1, 2)
    bb_re = f_re[:, None, :] * bt_re - f_im[:, None, :] * bt_im
    bb_im = f_re[:, None, :] * bt_im + f_im[:, None, :] * bt_re
    pw_re, pw_im = [jnp.ones_like(ar)], [jnp.zeros_like(ar)]
    for _ in range(SSM_CHUNK):
        pr, pi = pw_re[-1], pw_im[-1]
        pw_re.append(pr * ar - pi * ai)
        pw_im.append(pr * ai + pi * ar)
    pw_re = jnp.stack(pw_re)
    pw_im = jnp.stack(pw_im)
    pk_re = pw_re[:SSM_CHUNK, :, None, :]
    pk_im = pw_im[:SSM_CHUNK, :, None, :]
    ab_re = pk_re * bb_re - pk_im * bb_im
    ab_im = pk_re * bb_im + pk_im * bb_re
    conv = (jnp.einsum("gcp,tgdp->tgdc", c_re, ab_re, precision=HIGHEST)
            - jnp.einsum("gcp,tgdp->tgdc", c_im, ab_im, precision=HIGHEST))
    gpb = GROUPS_PER_BLOCK
    nb = g_n // gpb
    t_n = SSM_CHUNK
    pr = pw_re[1:, :, :, None]
    pi = pw_im[1:, :, :, None]
    ct_re = jnp.swapaxes(c_re, 1, 2)[None]
    ct_im = jnp.swapaxes(c_im, 1, 2)[None]
    tables = [conv.reshape(t_n, nb, LANES, SSM_GROUP),
              ab_re.reshape(t_n, nb, LANES, p_n), ab_im.reshape(t_n, nb, LANES, p_n),
              (ct_re * pr - ct_im * pi).reshape(t_n, nb, gpb * p_n, SSM_GROUP),
              (-(ct_re * pi + ct_im * pr)).reshape(t_n, nb, gpb * p_n, SSM_GROUP)]
    side = t_n * LANES
    op_spec = pl.BlockSpec((None, side, side), lambda j: (j, 0, 0))
    tz, wb, vc = pl.pallas_call(
        _ssm_ops_body,
        grid=(nb,),
        in_specs=[pl.BlockSpec((t_n, None) + t.shape[2:], lambda j: (0, j, 0, 0)) for t in tables],
        out_specs=[op_spec] * 3,
        out_shape=[jax.ShapeDtypeStruct((nb, side, side), BF16)] * 3,
        compiler_params=_params("parallel"),
        name="s5_chunk_operators",
    )(*tables)
    a8r = pw_re[SSM_CHUNK].reshape(1, g_n * p_n)
    a8i = pw_im[SSM_CHUNK].reshape(1, g_n * p_n)
    return tz, wb, vc, a8r, a8i


def _ssm_scan(u, ops, d_skip, h0_re, h0_im, chunk0, n_chunks, sequential):
    tz, wb, vc, a8r, a8i = ops
    nb = tz.shape[0]
    sw = wb.shape[-1] // 2
    rows = n_chunks * SSM_CHUNK
    cblk = chunk0 // n_chunks
    op3 = lambda a: pl.BlockSpec((None,) + a.shape[1:], lambda j: (j, 0, 0))
    srow = pl.BlockSpec((1, sw), lambda j: (0, j))
    if sequential:
        hspec = srow
        scratch = [pltpu.VMEM((n_chunks, sw), F32)] * 4
    else:
        hspec = pl.BlockSpec((n_chunks, sw // SSM_STATE, SSM_STATE), lambda j: (0, j, 0))
        scratch = []
    return pl.pallas_call(
        functools.partial(_ssm_body, sequential=sequential),
        grid=(nb,),
        in_specs=[pl.BlockSpec((rows, LANES), lambda j: (cblk, j)),
                  op3(tz), op3(wb), op3(vc),
                  srow, srow, pl.BlockSpec((1, LANES), lambda j: (0, j)), hspec, hspec],
        out_specs=[pl.BlockSpec((rows, LANES), lambda j: (0, j)), hspec, hspec],
        out_shape=[jax.ShapeDtypeStruct((rows, u.shape[1]), F32),
                   jax.ShapeDtypeStruct(h0_re.shape, F32), jax.ShapeDtypeStruct(h0_im.shape, F32)],
        scratch_shapes=scratch,
        compiler_params=_params("parallel"),
        name="s5_scan_seq" if sequential else "s5_scan_step",
    )(u, tz, wb, vc, a8r, a8i, d_skip.reshape(1, -1), h0_re, h0_im)


def kernel(x_prompt, x_sample, cache_k, cache_v, state_ssm_re, state_ssm_im, page_table, rel_bias, ln_g, ln_b,
           w_qkv, w_o, w_ssm_in, ssm_a_re, ssm_a_im, ssm_log_dt, ssm_b_re, ssm_b_im, ssm_c_re, ssm_c_im, ssm_d,
           w_glu_v, w_glu_g, w_ff_gate, w_ff_up, w_ff_down, w_router, w_moe_gate, w_moe_up, w_moe_down):
    batch, seq, d = x_prompt.shape
    dec_batch, dec_seq, _ = x_sample.shape
    assert batch == 1 and d == D_MODEL and dec_seq == SSM_CHUNK and seq % MOBA_BLOCK == 0
    m_p = batch * seq
    m_s = dec_batch * dec_seq
    m = m_p + m_s
    dq = N_HEADS * HEAD_DIM
    dkv = N_KV_HEADS * HEAD_DIM
    n_pool, page = cache_k.shape[1], cache_k.shape[2]

    x_p = x_prompt.reshape(m_p, d)
    x_s = x_sample.reshape(m_s, d)

    qkv_f32, qkv_b = _qkv_proj(x_p, x_s, w_qkv[0].astype(BF16), tm=1024, tn=512)
    kmean = _block_means(qkv_f32, seq // MOBA_BLOCK)
    tabs, tab_s = _bias_tables(rel_bias)
    ck = cache_k[0].reshape(n_pool, page * N_KV_HEADS, HEAD_DIM)
    cv = cache_v[0].reshape(n_pool, page * N_KV_HEADS, HEAD_DIM)
    attn_p, attn_s = _attention(page_table, qkv_f32, qkv_b, kmean, tabs, tab_s, ck, cv, seq, dec_seq)
    y = _proj_ln(attn_p, attn_s, [w_o[0].astype(BF16)], [x_p, x_s], ln_g[0, 0], ln_b[0, 0], tm=512, tn=512)
    y = _ffn_dense(y, w_ff_gate, w_ff_up, w_ff_down, ln_g[0, 1], ln_b[0, 1], tm=1024, tf=512)

    k_all = qkv_f32[:, dq:dq + dkv]
    v_all = qkv_f32[:, dq + dkv:]
    k_prompt = k_all[:m_p].reshape(1, batch, seq, N_KV_HEADS, HEAD_DIM)
    v_prompt = v_all[:m_p].reshape(1, batch, seq, N_KV_HEADS, HEAD_DIM)
    k_sample = k_all[m_p:].reshape(1, dec_batch, dec_seq, N_KV_HEADS, HEAD_DIM)
    v_sample = v_all[m_p:].reshape(1, dec_batch, dec_seq, N_KV_HEADS, HEAD_DIM)

    u = _matmul(y, w_ssm_in[0].astype(BF16), tm=1024, tn=1024)
    ops = _ssm_operators(ssm_a_re[0], ssm_a_im[0], ssm_log_dt[0], ssm_b_re[0], ssm_b_im[0],
                         ssm_c_re[0], ssm_c_im[0])
    n_state = N_SSM_GROUPS * SSM_STATE
    zero = jnp.zeros((batch, n_state), F32)
    yg_p, hrp, hip = _ssm_scan(u, ops, ssm_d[0], zero, zero, 0, m_p // SSM_CHUNK, True)
    yg_s, hrs, his = _ssm_scan(u, ops, ssm_d[0], state_ssm_re[0], state_ssm_im[0], m_p // SSM_CHUNK, dec_batch,
                               False)
    y, y_packed = _proj_ln(yg_p, yg_s, [w_glu_v[0].astype(BF16), w_glu_g[0].astype(BF16)],
                           [y], ln_g[1, 0], ln_b[1, 0], tm=512, tn=512, packed=True)

    tm_moe = 1024
    top_idx, top_gate = _router(y, w_router[0], tm=512)
    tile_expert, tile_rows, slot_token, pair_slot = _moe_routing(top_idx, tm_moe)
    xs = y_packed.at[lax.optimization_barrier(slot_token)].get(mode="promise_in_bounds")
    ys = _ffn_moe(tile_expert, tile_rows, xs, w_moe_gate[0], w_moe_up[0], w_moe_down[0], tm_moe, tf=512, half=128)
    pairs = ys.at[pair_slot].get(mode="promise_in_bounds").reshape(TOP_K, m, d // 2)
    out_p, out_s = _combine_ln(y, pairs, top_gate, ln_g[1, 1], ln_b[1, 1], m_p, tm=512)

    return (out_p.reshape(batch, seq, d), out_s.reshape(dec_batch, dec_seq, d),
            k_prompt, v_prompt, k_sample, v_sample,
            hrp.reshape(1, batch, N_SSM_GROUPS, SSM_STATE), hip.reshape(1, batch, N_SSM_GROUPS, SSM_STATE),
            hrs.reshape(1, dec_batch, N_SSM_GROUPS, SSM_STATE), his.reshape(1, dec_batch, N_SSM_GROUPS, SSM_STATE))
```

```python
import functools
import math

import numpy as np
import jax
import jax.numpy as jnp
from jax import lax
from jax.experimental import pallas as pl
from jax.experimental.pallas import tpu as pltpu

D_MODEL = 2048
N_HEADS = 16
HEAD_DIM = D_MODEL // N_HEADS
N_KV_HEADS = 4
HEADS_PER_KV = N_HEADS // N_KV_HEADS
MOBA_BLOCK = 256
MOBA_TOPK = 3
NUM_BUCKETS = 32
MAX_DISTANCE = 128
SSM_GROUP = 16
N_SSM_GROUPS = D_MODEL // SSM_GROUP
SSM_STATE = 64
SSM_CHUNK = 8
D_FF = 7 * D_MODEL // 2
N_EXPERTS = 8
TOP_K = 2
DEPTH = 2
ALPHA = (2 * DEPTH) ** 0.25
LN_EPS = 1e-5
NEG_INF = -1e30

LANES = 128
SUBLANES = 8
VMEM_LIMIT = 62 * 1024 * 1024

F32 = jnp.float32
BF16 = jnp.bfloat16
HIGHEST = lax.Precision.HIGHEST
_NT = (((1,), (1,)), ((), ()))


def _params(*sem):
    return pltpu.CompilerParams(dimension_semantics=sem, vmem_limit_bytes=VMEM_LIMIT)


def _dot(a, b):
    return jnp.dot(a, b, preferred_element_type=F32)


def _split_bf16(x):
    hi = x.astype(BF16)
    return hi, (x - hi.astype(F32)).astype(BF16)


def _dot3(x, w, dims=(((1,), (0,)), ((), ()))):
    x_hi, x_lo = _split_bf16(x)
    w_hi, w_lo = _split_bf16(w)
    mm = lambda a, b: lax.dot_general(a, b, dims, preferred_element_type=F32)
    return mm(x_hi, w_hi) + (mm(x_hi, w_lo) + mm(x_lo, w_hi))


def _sigmoid(x):
    return 1.0 / (1.0 + jnp.exp(-x))


def _layer_norm(y, g, b):
    mean = jnp.mean(y, axis=-1, keepdims=True)
    yc = y - mean
    var = jnp.mean(yc * yc, axis=-1, keepdims=True)
    return yc * lax.rsqrt(var + LN_EPS) * g + b


def _mm_body(x_ref, w_ref, o_ref, xb_scr):
    @pl.when(pl.program_id(1) == 0)
    def _():
        xb_scr[...] = x_ref[...].astype(BF16)

    o_ref[...] = _dot(xb_scr[...], w_ref[...])


def _matmul(x, w, tm, tn):
    m, k = x.shape
    n = w.shape[1]
    return pl.pallas_call(
        _mm_body,
        grid=(m // tm, n // tn),
        in_specs=[pl.BlockSpec((tm, k), lambda i, j: (i, 0)),
                  pl.BlockSpec((k, tn), lambda i, j: (0, j))],
        out_specs=pl.BlockSpec((tm, tn), lambda i, j: (i, j)),
        out_shape=jax.ShapeDtypeStruct((m, n), F32),
        scratch_shapes=[pltpu.VMEM((tm, k), BF16)],
        compiler_params=_params("parallel", "arbitrary"),
        name="matmul",
    )(x, w)


def _split_rows(block, n_first):
    return [pl.BlockSpec(block, lambda i, j: (jnp.minimum(i, n_first - 1), 0)),
            pl.BlockSpec(block, lambda i, j: (jnp.maximum(i - n_first, 0), 0))]


def _load_split_bf16(dst, src1, src2, n_first):
    i = pl.program_id(0)
    j = pl.program_id(1)

    @pl.when((j == 0) & (i < n_first))
    def _():
        dst[...] = src1[...].astype(BF16)

    @pl.when((j == 0) & (i >= n_first))
    def _():
        dst[...] = src2[...].astype(BF16)


def _qkv_body(x1_ref, x2_ref, w_ref, of_ref, ob_ref, xb_scr, *, nq_tiles, scale, n_first):
    _load_split_bf16(xb_scr, x1_ref, x2_ref, n_first)
    acc = _dot(xb_scr[...], w_ref[...])
    of_ref[...] = acc
    s = jnp.where(pl.program_id(1) < nq_tiles, scale, 1.0).astype(F32)
    ob_ref[...] = (acc * s).astype(BF16)


def _qkv_proj(x1, x2, w, tm, tn):
    k = x1.shape[1]
    m = x1.shape[0] + x2.shape[0]
    n = w.shape[1]
    n_first = x1.shape[0] // tm
    assert x1.shape[0] % tm == 0 and x2.shape[0] % tm == 0
    body = functools.partial(_qkv_body, nq_tiles=(N_HEADS * HEAD_DIM) // tn, scale=HEAD_DIM ** -0.5,
                             n_first=n_first)
    return pl.pallas_call(
        body,
        grid=(m // tm, n // tn),
        in_specs=_split_rows((tm, k), n_first) + [pl.BlockSpec((k, tn), lambda i, j: (0, j))],
        out_specs=[pl.BlockSpec((tm, tn), lambda i, j: (i, j)),
                   pl.BlockSpec((tm, tn), lambda i, j: (i, j))],
        out_shape=[jax.ShapeDtypeStruct((m, n), F32), jax.ShapeDtypeStruct((m, n), BF16)],
        scratch_shapes=[pltpu.VMEM((tm, k), BF16)],
        compiler_params=_params("parallel", "arbitrary"),
        name="qkv_proj",
    )(x1, x2, w)


def _kmean_body(k_ref, o_ref):
    i = pl.program_id(0)

    @pl.when(i == 0)
    def _():
        o_ref[...] = jnp.zeros_like(o_ref)

    mean = jnp.sum(k_ref[...], axis=0, keepdims=True) * (1.0 / MOBA_BLOCK)
    rows = lax.broadcasted_iota(jnp.int32, o_ref.shape, 0)
    o_ref[...] = jnp.where(rows == i, mean, o_ref[...])


def _block_means(qkv_f32, n_blocks):
    dkv = N_KV_HEADS * HEAD_DIM
    kcol = (N_HEADS * HEAD_DIM) // dkv
    return pl.pallas_call(
        _kmean_body,
        grid=(n_blocks,),
        in_specs=[pl.BlockSpec((MOBA_BLOCK, dkv), lambda i: (i, kcol))],
        out_specs=pl.BlockSpec((LANES, dkv), lambda i: (0, 0)),
        out_shape=jax.ShapeDtypeStruct((LANES, dkv), F32),
        compiler_params=_params("arbitrary"),
        name="moba_block_means",
    )(qkv_f32)


def _t5_bucket_np(dist):
    n = np.maximum(dist, 0)
    max_exact = NUM_BUCKETS // 2
    nf = np.maximum(n, 1).astype(np.float32)
    large = max_exact + (np.log(nf / np.float32(max_exact)) / np.float32(math.log(MAX_DISTANCE / max_exact))
                         * np.float32(NUM_BUCKETS - max_exact)).astype(np.int32)
    large = np.minimum(large, NUM_BUCKETS - 1)
    return np.where(n < max_exact, n, large).astype(np.int32)


def _bucket_tables():
    q = np.arange(MOBA_BLOCK)[:, None]
    k = np.arange(MOBA_BLOCK)[None, :]
    own = np.where(q - k >= 0, _t5_bucket_np(q - k), -1)
    prev = _t5_bucket_np(q - k + MOBA_BLOCK)
    return np.stack([own, prev]).astype(np.int32)


def _bias_body(rb_ref, idx_ref, o_ref, os_ref):
    h = pl.program_id(0)
    far = rb_ref[h, NUM_BUCKETS - 1]
    tabs = []
    for m in range(2):
        idx = idx_ref[m]
        acc = jnp.full(idx.shape, NEG_INF, F32)
        for b in range(NUM_BUCKETS):
            acc = jnp.where(idx == b, rb_ref[h, b] - far, acc)
        o_ref[0, m] = acc
        tabs.append(acc)
    os_ref[0] = jnp.concatenate([tabs[1][:SUBLANES, :], tabs[0][:SUBLANES, :LANES]], axis=1)


def _bias_tables(rel_bias):
    idx = jnp.asarray(_bucket_tables())
    return pl.pallas_call(
        _bias_body,
        grid=(N_HEADS,),
        in_specs=[pl.BlockSpec(memory_space=pltpu.SMEM),
                  pl.BlockSpec((2, MOBA_BLOCK, MOBA_BLOCK), lambda h: (0, 0, 0))],
        out_specs=[pl.BlockSpec((1, 2, MOBA_BLOCK, MOBA_BLOCK), lambda h: (h, 0, 0, 0)),
                   pl.BlockSpec((1, SUBLANES, MOBA_BLOCK + LANES), lambda h: (h, 0, 0))],
        out_shape=[jax.ShapeDtypeStruct((N_HEADS, 2, MOBA_BLOCK, MOBA_BLOCK), F32),
                   jax.ShapeDtypeStruct((N_HEADS, SUBLANES, MOBA_BLOCK + LANES), F32)],
        compiler_params=_params("parallel"),
        name="moba_bias_tables",
    )(rel_bias, idx)


def _select_blocks(gate, n_past):
    lane = lax.broadcasted_iota(jnp.int32, gate.shape, 1)
    lane_f = lane.astype(F32)
    g = jnp.where(lane < n_past, gate, NEG_INF)
    sel = jnp.full(gate.shape, NEG_INF, F32)
    for _ in range(MOBA_TOPK):
        mx = jnp.max(g, axis=1, keepdims=True)
        idx = jnp.min(jnp.where(g == mx, lane_f, float(gate.shape[1])), axis=1, keepdims=True)
        pick = lane_f == idx
        sel = jnp.where(pick, 0.0, sel)
        g = jnp.where(pick, -jnp.inf, g)
    return jnp.where(lane < n_past, sel, 0.0)


def _attn_prompt_body(qf_ref, qb_ref, k_ref, v_ref, km_ref, tab_ref, o_ref, qa_scr, m_scr, l_scr, acc_scr):
    i = pl.program_id(1)
    rows = HEADS_PER_KV * MOBA_BLOCK
    km = km_ref[...]
    lane = lax.broadcasted_iota(jnp.int32, (MOBA_BLOCK, LANES), 1)
    dummy = LANES - 1
    for hh in range(HEADS_PER_KV):
        cs = slice(hh * HEAD_DIM, (hh + 1) * HEAD_DIM)
        rs = slice(hh * MOBA_BLOCK, (hh + 1) * MOBA_BLOCK)
        gate = _dot3(qf_ref[:, cs], km, _NT)
        mask = jnp.where(lane == dummy, NEG_INF, _select_blocks(gate, i))
        qa_scr[rs, :HEAD_DIM] = qb_ref[:, cs]
        qa_scr[rs, HEAD_DIM:] = mask.astype(BF16)
    qa = qa_scr[...]

    def scores(j, mask_lane):
        start = pl.multiple_of(j * MOBA_BLOCK, MOBA_BLOCK)
        kj = k_ref[pl.ds(start, MOBA_BLOCK), :]
        vj = v_ref[pl.ds(start, MOBA_BLOCK), :]
        rhs = jnp.concatenate([kj, (lane == mask_lane).astype(BF16)], axis=1)
        return lax.dot_general(qa, rhs, _NT, preferred_element_type=F32), vj

    def softmax_parts(parts, m_new):
        m2 = jnp.concatenate([m_new, m_new], axis=1)
        l_add = None
        acc_add = None
        for s, vj in parts:
            p = jnp.exp(s - m2)
            ls = jnp.sum(p, axis=1, keepdims=True)
            pv = _dot(p.astype(BF16), vj)
            l_add = ls if l_add is None else l_add + ls
            acc_add = pv if acc_add is None else acc_add + pv
        return l_add, acc_add

    has_prev = i >= 1
    s_own, v_own = scores(i, -1)
    s_own = s_own + tab_ref[:, 0].reshape(rows, MOBA_BLOCK)
    s_prev, v_prev = scores(jnp.maximum(i - 1, 0), jnp.where(has_prev, i - 1, dummy))
    s_prev = s_prev + tab_ref[:, 1].reshape(rows, MOBA_BLOCK)
    m0 = jnp.maximum(jnp.max(s_own, axis=1, keepdims=True), jnp.max(s_prev, axis=1, keepdims=True))
    m0 = jnp.broadcast_to(m0, (rows, LANES))
    l0, acc0 = softmax_parts([(s_own, v_own), (s_prev, v_prev)], m0)
    m_scr[...] = m0
    l_scr[...] = jnp.broadcast_to(l0, (rows, LANES))
    acc_scr[...] = acc0

    n_far = jnp.maximum(i - 1, 0)

    def far_pair_update(j0):
        parts = []
        for j in (j0, j0 + 1):
            ok = j < n_far
            parts.append(scores(jnp.where(ok, j, 0), jnp.where(ok, j, dummy)))
        m_prev = m_scr[...]
        m_new = m_prev
        for s, _ in parts:
            m_new = jnp.maximum(m_new, jnp.max(s, axis=1, keepdims=True))
        alpha = jnp.exp(m_prev - m_new)
        l_add, acc_add = softmax_parts(parts, m_new)
        l_scr[...] = alpha * l_scr[...] + l_add
        acc_scr[...] = alpha * acc_scr[...] + acc_add
        m_scr[...] = m_new

    def far_quad(t, carry):
        far_pair_update(4 * t)
        far_pair_update(4 * t + 2)
        return carry

    def far_pair(t, carry):
        far_pair_update(n_quads * 4 + 2 * t)
        return carry

    n_quads = n_far // 4
    lax.fori_loop(0, n_quads, far_quad, 0)
    lax.fori_loop(0, (n_far - n_quads * 4 + 1) // 2, far_pair, 0)

    out = acc_scr[...] / l_scr[...]
    for hh in range(HEADS_PER_KV):
        o_ref[:, hh * HEAD_DIM:(hh + 1) * HEAD_DIM] = out[hh * MOBA_BLOCK:(hh + 1) * MOBA_BLOCK]


def _attn_sample_core(first_step, qkv_ref, kp, vp, tab_ref, o_ref, kall, vall, expand, page, dec_seq):
    n_pages = len(kp)
    past = n_pages * page
    n_past_blocks = past // MOBA_BLOCK
    pages_per_block = MOBA_BLOCK // page
    near0 = past - MOBA_BLOCK
    total = past + LANES
    dq = N_HEADS * HEAD_DIM
    dkv = N_KV_HEADS * HEAD_DIM
    hq = HEADS_PER_KV * dec_seq
    rows = N_HEADS * dec_seq
    pad = jnp.zeros((LANES - dec_seq, HEAD_DIM), F32)

    @pl.when(first_step)
    def _():
        key_blk = lax.broadcasted_iota(jnp.int32, (LANES, total), 1) // MOBA_BLOCK
        expand[...] = (key_blk == lax.broadcasted_iota(jnp.int32, (LANES, total), 0)).astype(BF16)

    kmeans = []
    for g in range(N_KV_HEADS):
        means = []
        for blk in range(n_past_blocks):
            tot = jnp.zeros((1, HEAD_DIM), F32)
            for pp in range(pages_per_block):
                pg = blk * pages_per_block + pp
                kk = kp[pg][pl.ds(g, page, stride=N_KV_HEADS), :]
                vv = vp[pg][pl.ds(g, page, stride=N_KV_HEADS), :]
                tot = tot + jnp.sum(kk, axis=0, keepdims=True)
                kall[g, pg * page:(pg + 1) * page, :] = kk.astype(BF16)
                vall[g, pg * page:(pg + 1) * page, :] = vv.astype(BF16)
            means.append(tot * (1.0 / MOBA_BLOCK))
        kmeans.append(jnp.concatenate(means + [jnp.zeros((LANES - n_past_blocks, HEAD_DIM), F32)], axis=0))
        knew = qkv_ref[:, dq + g * HEAD_DIM:dq + (g + 1) * HEAD_DIM]
        vnew = qkv_ref[:, dq + dkv + g * HEAD_DIM:dq + dkv + (g + 1) * HEAD_DIM]
        kall[g, past:total, :] = jnp.concatenate([knew, pad], axis=0).astype(BF16)
        vall[g, past:total, :] = jnp.concatenate([vnew, pad], axis=0).astype(BF16)

    qs = jnp.concatenate([qkv_ref[:, h * HEAD_DIM:(h + 1) * HEAD_DIM] for h in range(N_HEADS)], axis=0)
    gate_all = _dot3(qs, jnp.concatenate(kmeans, axis=0), _NT)
    row_g = lax.broadcasted_iota(jnp.int32, (rows, LANES), 0) // hq
    gate = gate_all[:, :LANES]
    for g in range(1, N_KV_HEADS):
        gate = jnp.where(row_g == g, gate_all[:, g * LANES:(g + 1) * LANES], gate)
    selm = _select_blocks(gate, n_past_blocks).astype(BF16)

    qb = (qs * (HEAD_DIM ** -0.5)).astype(BF16)
    s = jnp.concatenate([lax.dot_general(qb[g * hq:(g + 1) * hq], kall[g], _NT, preferred_element_type=F32)
                         for g in range(N_KV_HEADS)], axis=0)
    s = s + _dot(selm, expand[...])
    s_far = s[:, :near0]
    s_near = s[:, near0:] + tab_ref[...].reshape(rows, MOBA_BLOCK + LANES)
    m = jnp.maximum(jnp.max(s_far, axis=1, keepdims=True), jnp.max(s_near, axis=1, keepdims=True))
    p_far = jnp.exp(s_far - m)
    p_near = jnp.exp(s_near - m)
    inv_l = 1.0 / (jnp.sum(p_far, axis=1, keepdims=True) + jnp.sum(p_near, axis=1, keepdims=True))
    p_far = p_far.astype(BF16)
    p_near = p_near.astype(BF16)
    for g in range(N_KV_HEADS):
        rs = slice(g * hq, (g + 1) * hq)
        out = (_dot(p_far[rs], vall[g, :near0, :]) + _dot(p_near[rs], vall[g, near0:, :])) * inv_l[rs]
        for hh in range(HEADS_PER_KV):
            c0 = (g * HEADS_PER_KV + hh) * HEAD_DIM
            o_ref[:, c0:c0 + HEAD_DIM] = out[hh * dec_seq:(hh + 1) * dec_seq]


def _attn_body(pt_ref, *refs, n_pages, page, dec_seq):
    del pt_ref
    n_prompt_in = 6
    n_sample_in = 2 + 2 * n_pages
    prompt_in = refs[:n_prompt_in]
    qkv_ref, *pages, tab_ref = refs[n_prompt_in:n_prompt_in + n_sample_in]
    op_ref, os_ref = refs[n_prompt_in + n_sample_in:n_prompt_in + n_sample_in + 2]
    scr = refs[n_prompt_in + n_sample_in + 2:]
    _attn_prompt_body(*prompt_in, op_ref, *scr[:4])
    first_step = (pl.program_id(0) == 0) & (pl.program_id(1) == 0)
    _attn_sample_core(first_step, qkv_ref, pages[:n_pages], pages[n_pages:], tab_ref, os_ref, *scr[4:],
                      page, dec_seq)


def _attention(page_table, qkv_f32, qkv_b, kmean, tabs, tab_s, cache_k, cache_v, seq, dec_seq):
    nblk = seq // MOBA_BLOCK
    assert nblk < LANES - 1
    dec_batch, n_pages = page_table.shape
    assert dec_batch == N_KV_HEADS * nblk
    page = cache_k.shape[1] // N_KV_HEADS
    total = n_pages * page + LANES
    blk0 = seq // dec_seq
    gw = HEADS_PER_KV * HEAD_DIM
    kcol = (N_HEADS * HEAD_DIM) // HEAD_DIM
    vcol = kcol + N_KV_HEADS
    rows = HEADS_PER_KV * MOBA_BLOCK

    def page_spec(p):
        return pl.BlockSpec((None, page * N_KV_HEADS, HEAD_DIM),
                            lambda g, i, pt, p=p: (pt[g * nblk + i, p], 0, 0))

    grid_spec = pltpu.PrefetchScalarGridSpec(
        num_scalar_prefetch=1,
        grid=(N_KV_HEADS, nblk),
        in_specs=([pl.BlockSpec((MOBA_BLOCK, gw), lambda g, i, pt: (i, g)),
                   pl.BlockSpec((MOBA_BLOCK, gw), lambda g, i, pt: (i, g)),
                   pl.BlockSpec((seq, HEAD_DIM), lambda g, i, pt: (0, kcol + g)),
                   pl.BlockSpec((seq, HEAD_DIM), lambda g, i, pt: (0, vcol + g)),
                   pl.BlockSpec((LANES, HEAD_DIM), lambda g, i, pt: (0, g)),
                   pl.BlockSpec((HEADS_PER_KV, 2, MOBA_BLOCK, MOBA_BLOCK), lambda g, i, pt: (g, 0, 0, 0)),
                   pl.BlockSpec((dec_seq, qkv_f32.shape[1]), lambda g, i, pt: (blk0 + g * nblk + i, 0))]
                  + [page_spec(p) for p in range(n_pages)]
                  + [page_spec(p) for p in range(n_pages)]
                  + [pl.BlockSpec(tab_s.shape, lambda g, i, pt: (0, 0, 0))]),
        out_specs=[pl.BlockSpec((MOBA_BLOCK, gw), lambda g, i, pt: (i, g)),
                   pl.BlockSpec((dec_seq, N_HEADS * HEAD_DIM), lambda g, i, pt: (g * nblk + i, 0))],
        scratch_shapes=[pltpu.VMEM((rows, 2 * HEAD_DIM), BF16),
                        pltpu.VMEM((rows, LANES), F32),
                        pltpu.VMEM((rows, LANES), F32),
                        pltpu.VMEM((rows, HEAD_DIM), F32),
                        pltpu.VMEM((N_KV_HEADS, total, HEAD_DIM), BF16),
                        pltpu.VMEM((N_KV_HEADS, total, HEAD_DIM), BF16),
                        pltpu.VMEM((LANES, total), BF16)],
    )
    return pl.pallas_call(
        functools.partial(_attn_body, n_pages=n_pages, page=page, dec_seq=dec_seq),
        grid_spec=grid_spec,
        out_shape=[jax.ShapeDtypeStruct((seq, N_HEADS * HEAD_DIM), F32),
                   jax.ShapeDtypeStruct((dec_batch * dec_seq, N_HEADS * HEAD_DIM), F32)],
        compiler_params=_params("arbitrary", "arbitrary"),
        name="moba_attention",
    )(page_table, qkv_f32, qkv_b, qkv_b, qkv_b, kmean, tabs, qkv_f32,
      *([cache_k] * n_pages), *([cache_v] * n_pages), tab_s)


def _pack_bf16_halves(x):
    c = x.shape[1] // 2
    lo = lax.bitcast_convert_type(x[:, :c].astype(BF16).astype(F32), jnp.uint32)
    hi = lax.bitcast_convert_type(x[:, c:].astype(BF16).astype(F32), jnp.uint32)
    return lax.bitcast_convert_type(hi | (lo >> 16), F32)


def _unpack_bf16_halves(w):
    u = lax.bitcast_convert_type(w, jnp.uint32)
    lo = lax.bitcast_convert_type(u << 16, F32)
    hi = lax.bitcast_convert_type(u & jnp.uint32(0xFFFF0000), F32)
    return jnp.concatenate([lo, hi], axis=1)


def _proj_ln_body(*refs, n_w, n_x, n_tiles, n_first, packed):
    a1_ref, a2_ref = refs[:2]
    w_refs = refs[2:2 + n_w]
    x_refs = refs[2 + n_w:2 + n_w + n_x]
    if packed:
        g_ref, b_ref, o_ref, op_ref, a_scr, z_scr = refs[2 + n_w + n_x:]
    else:
        g_ref, b_ref, o_ref, a_scr, z_scr = refs[2 + n_w + n_x:]
    i = pl.program_id(0)
    j = pl.program_id(1)
    _load_split_bf16(a_scr, a1_ref, a2_ref, n_first)
    a = a_scr[...]
    z = _dot(a, w_refs[0][...])
    if n_w == 2:
        z = z * _sigmoid(_dot(a, w_refs[1][...]))
    z_scr[j] = z

    @pl.when(j == n_tiles - 1)
    def _():
        zfull = jnp.concatenate([z_scr[t] for t in range(n_tiles)], axis=1)
        x = x_refs[0][...] if n_x == 1 else jnp.where(i < n_first, x_refs[0][...], x_refs[1][...])
        out = _layer_norm(ALPHA * x + zfull, g_ref[...], b_ref[...])
        o_ref[...] = out
        if packed:
            op_ref[...] = _pack_bf16_halves(out)


def _proj_ln(a1, a2, ws, xs, g, b, tm, tn, packed=False):
    k = a1.shape[1]
    m = a1.shape[0] + a2.shape[0]
    n = ws[0].shape[1]
    n_first = a1.shape[0] // tm
    assert a1.shape[0] % tm == 0 and a2.shape[0] % tm == 0
    assert len(xs) == 1 or xs[0].shape[0] == a1.shape[0]
    n_tiles = n // tn
    body = functools.partial(_proj_ln_body, n_w=len(ws), n_x=len(xs), n_tiles=n_tiles, n_first=n_first,
                             packed=packed)
    row = pl.BlockSpec((tm, n), lambda i, j: (i, 0))
    vec = pl.BlockSpec((1, n), lambda i, j: (0, 0))
    out_specs, out_shape = row, jax.ShapeDtypeStruct((m, n), F32)
    if packed:
        out_specs = [row, pl.BlockSpec((tm, n // 2), lambda i, j: (i, 0))]
        out_shape = [out_shape, jax.ShapeDtypeStruct((m, n // 2), F32)]
    return pl.pallas_call(
        body,
        grid=(m // tm, n_tiles),
        in_specs=(_split_rows((tm, k), n_first)
                  + [pl.BlockSpec((k, tn), lambda i, j: (0, j)) for _ in ws]
                  + ([row] if len(xs) == 1 else _split_rows((tm, n), n_first))
                  + [vec, vec]),
        out_specs=out_specs,
        out_shape=out_shape,
        scratch_shapes=[pltpu.VMEM((tm, k), BF16), pltpu.VMEM((n_tiles, tm, tn), F32)],
        compiler_params=_params("parallel", "arbitrary"),
        name="glu_res_ln" if len(ws) == 2 else "proj_res_ln",
    )(a1, a2, *ws, *xs, g.reshape(1, n), b.reshape(1, n))


def _ffn_body(te_ref, tr_ref, *refs, n_f, dense, half):
    del te_ref
    if dense:
        x_ref, wg_ref, wu_ref, wd_ref, g_ref, b_ref, o_ref, xb = refs
        acc = o_ref
    else:
        x_ref, wg_ref, wu_ref, wd_ref, o_ref, xb, acc = refs
    tm = x_ref.shape[0]
    t = pl.program_id(0)
    f = pl.program_id(1)
    n_rows = tr_ref[t]

    @pl.when(f == 0)
    def _():
        x = x_ref[...] if dense else _unpack_bf16_halves(x_ref[...])
        xb[...] = x.astype(BF16)
        acc[...] = jnp.zeros_like(acc)

    def swiglu_rows(n):
        x = xb[:n, :]
        hg = _dot(x, wg_ref[...].astype(BF16))
        hu = _dot(x, wu_ref[...].astype(BF16))
        h = (hg * _sigmoid(hg)) * hu
        acc[:n, :] += _dot(h.astype(BF16), wd_ref[...].astype(BF16))

    for n in range(half, tm + 1, half):
        pl.when((n_rows > n - half) & (n_rows <= n))(functools.partial(swiglu_rows, n))

    @pl.when(f == n_f - 1)
    def _():
        if dense:
            o_ref[...] = _layer_norm(ALPHA * x_ref[...] + acc[...], g_ref[...], b_ref[...])
        else:
            o_ref[...] = _pack_bf16_halves(acc[...])


def _ffn_call(tile_expert, tile_rows, x, wg, wu, wd, extra, extra_specs, tm, tf, half, dense, name):
    s = x.shape[0]
    d = wg.shape[-2]
    n_t = s // tm
    n_f = wg.shape[-1] // tf
    live = lambda t, tr: jnp.minimum(tr[t], 1)
    x_spec = pl.BlockSpec((tm, x.shape[1]), lambda t, f, te, tr: (t, 0))
    x_in = pl.BlockSpec((tm, x.shape[1]), lambda t, f, te, tr: (t, 0), pipeline_mode=pl.Buffered(1)) if dense else x_spec
    wgu = pl.BlockSpec((None, d, tf), lambda t, f, te, tr: (te[t], 0, f * live(t, tr)))
    wds = pl.BlockSpec((None, tf, d), lambda t, f, te, tr: (te[t], f * live(t, tr), 0))
    grid_spec = pltpu.PrefetchScalarGridSpec(
        num_scalar_prefetch=2, grid=(n_t, n_f),
        in_specs=[x_in, wgu, wgu, wds] + extra_specs,
        out_specs=x_spec,
        scratch_shapes=[pltpu.VMEM((tm, d), BF16)] + ([] if dense else [pltpu.VMEM((tm, d), F32)]))
    return pl.pallas_call(
        functools.partial(_ffn_body, n_f=n_f, dense=dense, half=half),
        grid_spec=grid_spec,
        out_shape=jax.ShapeDtypeStruct(x.shape, x.dtype),
        compiler_params=_params("parallel", "arbitrary"),
        name=name,
    )(tile_expert, tile_rows, x, wg, wu, wd, *extra)


def _ffn_dense(x, wg, wu, wd, g, b, tm, tf):
    m, d = x.shape
    n_t = m // tm
    vec = pl.BlockSpec((1, d), lambda t, f, te, tr: (0, 0))
    return _ffn_call(jnp.zeros((n_t,), jnp.int32), jnp.full((n_t,), tm, jnp.int32), x, wg, wu, wd,
                     [g.reshape(1, d), b.reshape(1, d)], [vec, vec], tm, tf, tm, True, "ffn_res_ln")


def _ffn_moe(tile_expert, tile_rows, xs, wg, wu, wd, tm, tf, half):
    return _ffn_call(tile_expert, tile_rows, xs, wg, wu, wd, [], [], tm, tf, half, False, "moe_ffn")


def _router_body(y_ref, w_ref, i_ref, g_ref):
    logits = _dot3(y_ref[...], w_ref[...])
    lane = lax.broadcasted_iota(jnp.int32, logits.shape, 1)
    lane_f = lane.astype(F32)
    l1 = jnp.where(lane < N_EXPERTS, logits, -jnp.inf)
    m1 = jnp.max(l1, axis=1, keepdims=True)
    i1 = jnp.min(jnp.where(l1 == m1, lane_f, float(LANES)), axis=1, keepdims=True)
    l2 = jnp.where(lane_f == i1, -jnp.inf, l1)
    m2 = jnp.max(l2, axis=1, keepdims=True)
    i2 = jnp.min(jnp.where(l2 == m2, lane_f, float(LANES)), axis=1, keepdims=True)
    e = jnp.exp(m2 - m1)
    g1 = 1.0 / (1.0 + e)
    g2 = e / (1.0 + e)
    i_ref[...] = jnp.where(lane == 0, i1, jnp.where(lane == 1, i2, 0.0)).astype(jnp.int32)
    g_ref[...] = jnp.where(lane == 0, g1, jnp.where(lane == 1, g2, 0.0))


def _router(y, w_router, tm):
    m, d = y.shape
    wr = jnp.zeros((d, LANES), F32).at[:, :N_EXPERTS].set(w_router)
    row = pl.BlockSpec((tm, LANES), lambda i: (i, 0))
    return pl.pallas_call(
        _router_body,
        grid=(m // tm,),
        in_specs=[pl.BlockSpec((tm, d), lambda i: (i, 0)), pl.BlockSpec((d, LANES), lambda i: (0, 0))],
        out_specs=[row, row],
        out_shape=[jax.ShapeDtypeStruct((m, LANES), jnp.int32), jax.ShapeDtypeStruct((m, LANES), F32)],
        compiler_params=_params("parallel"),
        name="moe_router",
    )(y, wr)


def _combine_ln_body(x_ref, a_ref, b2_ref, tg_ref, g_ref, b_ref, op_ref, os_ref, *, n_first):
    i = pl.program_id(0)
    tg = tg_ref[...]
    moe = tg[:, 0:1] * _unpack_bf16_halves(a_ref[...]) + tg[:, 1:2] * _unpack_bf16_halves(b2_ref[...])
    out = _layer_norm(ALPHA * x_ref[...] + moe, g_ref[...], b_ref[...])

    @pl.when(i < n_first)
    def _():
        op_ref[...] = out

    @pl.when(i >= n_first)
    def _():
        os_ref[...] = out


def _combine_ln(x, pairs, top_gate, g, b, m_first, tm):
    m, d = x.shape
    dp = pairs.shape[2]
    n_first = m_first // tm
    vec = pl.BlockSpec((1, d), lambda i: (0, 0))
    return pl.pallas_call(
        functools.partial(_combine_ln_body, n_first=n_first),
        grid=(m // tm,),
        in_specs=[pl.BlockSpec((tm, d), lambda i: (i, 0)),
                  pl.BlockSpec((None, tm, dp), lambda i: (0, i, 0)),
                  pl.BlockSpec((None, tm, dp), lambda i: (1, i, 0)),
                  pl.BlockSpec((tm, LANES), lambda i: (i, 0)),
                  vec, vec],
        out_specs=[pl.BlockSpec((tm, d), lambda i: (jnp.minimum(i, n_first - 1), 0)),
                   pl.BlockSpec((tm, d), lambda i: (jnp.maximum(i - n_first, 0), 0))],
        out_shape=[jax.ShapeDtypeStruct((m_first, d), F32), jax.ShapeDtypeStruct((m - m_first, d), F32)],
        compiler_params=_params("arbitrary"),
        name="moe_combine_ln",
    )(x, pairs, pairs, top_gate, g.reshape(1, d), b.reshape(1, d))


def _moe_routing(top_idx, tm):
    m = top_idx.shape[0]
    n_pairs = m * TOP_K
    n_tiles = n_pairs // tm + N_EXPERTS
    i32 = jnp.int32
    e_flat = jnp.concatenate([top_idx[:, k] for k in range(TOP_K)])
    order = jnp.argsort(e_flat, stable=True).astype(i32)
    inv = jnp.argsort(order).astype(i32)
    counts = jnp.sum((e_flat[None, :] == jnp.arange(N_EXPERTS, dtype=i32)[:, None]).astype(i32), axis=1)
    starts = jnp.cumsum(counts) - counts
    k_tiles = (counts + tm - 1) // tm
    n_full = jnp.maximum(k_tiles - 2, 0)
    rest = counts - n_full * tm
    rows_a = jnp.where(k_tiles >= 2, (rest + 1) // 2, rest)
    rows_b = rest - rows_a
    tile_ends = jnp.cumsum(k_tiles)
    tile_first = tile_ends - k_tiles
    tile = jnp.arange(n_tiles, dtype=i32)
    tile_expert = jnp.minimum(jnp.searchsorted(tile_ends, tile, side="right"), N_EXPERTS - 1).astype(i32)
    tile_j = tile - tile_first[tile_expert]
    nf_t, ra_t, rb_t = n_full[tile_expert], rows_a[tile_expert], rows_b[tile_expert]
    tile_rows = jnp.where(tile_j < nf_t, tm, jnp.where(tile_j == nf_t, ra_t, rb_t))
    tile_rows = jnp.where(tile < tile_ends[-1], tile_rows, 0).astype(i32)
    tile_rank0 = jnp.where(tile_j <= nf_t, tile_j * tm, nf_t * tm + ra_t)
    slot = jnp.arange(n_tiles * tm, dtype=i32)
    slot_u = slot % tm
    slot_e = jnp.repeat(tile_expert, tm)
    live = slot_u < jnp.repeat(tile_rows, tm)
    src = order[jnp.clip(starts[slot_e] + jnp.repeat(tile_rank0, tm) + slot_u, 0, n_pairs - 1)]
    slot_token = jnp.where(live, src % m, slot % m)
    pair_rank = inv - starts[e_flat]
    nf_p, ra_p = n_full[e_flat], rows_a[e_flat]
    tail = pair_rank - nf_p * tm
    in_b = tail >= ra_p
    pair_j = jnp.where(tail < 0, pair_rank // tm, nf_p + in_b.astype(i32))
    pair_u = jnp.where(tail < 0, pair_rank % tm, tail - jnp.where(in_b, ra_p, 0))
    pair_slot = (tile_first[e_flat] + pair_j) * tm + pair_u
    return tile_expert, tile_rows, slot_token, pair_slot


def _gelu_tanh(x):
    return 0.5 * x * (1.0 + jnp.tanh(math.sqrt(2.0 / math.pi) * (x + 0.044715 * (x * x * x))))


def _ssm_body(*refs, sequential):
    (u_ref, tz_ref, wb_ref, vc_ref, a8r_ref, a8i_ref, d_ref, h0r_ref, h0i_ref, y_ref, hr_ref, hi_ref) = refs[:12]
    scr = refs[12:]
    n_chunks = u_ref.shape[0] // SSM_CHUNK
    sw = a8r_ref.shape[1]
    step_rows = lambda s: pl.ds(s, n_chunks, stride=SSM_CHUNK)
    ucat = jnp.concatenate([u_ref[step_rows(s), :].astype(BF16) for s in range(SSM_CHUNK)], axis=1)
    hl = _dot(ucat, wb_ref[...])
    hl_r = hl[:, :sw]
    hl_i = hl[:, sw:]
    a8r = a8r_ref[...]
    a8i = a8i_ref[...]
    if sequential:
        hlr_scr, hli_scr, hinr_scr, hini_scr = scr
        hlr_scr[...] = hl_r
        hli_scr[...] = hl_i

        def tile_step(kb, carry):
            hr, hi = carry
            base = pl.multiple_of(kb * SUBLANES, SUBLANES)
            tr = hlr_scr[pl.ds(base, SUBLANES), :]
            ti = hli_scr[pl.ds(base, SUBLANES), :]
            rows_r, rows_i = [], []
            for r in range(SUBLANES):
                rows_r.append(hr)
                rows_i.append(hi)
                hr, hi = (a8r * hr - a8i * hi + tr[r:r + 1], a8r * hi + a8i * hr + ti[r:r + 1])
            hinr_scr[pl.ds(base, SUBLANES), :] = jnp.concatenate(rows_r, axis=0)
            hini_scr[pl.ds(base, SUBLANES), :] = jnp.concatenate(rows_i, axis=0)
            return hr, hi

        hr, hi = lax.fori_loop(0, n_chunks // SUBLANES, tile_step, (h0r_ref[...], h0i_ref[...]))
        hr_ref[...] = hr
        hi_ref[...] = hi
        hin_r = hinr_scr[...]
        hin_i = hini_scr[...]
    else:
        gpb = h0r_ref.shape[1]
        p_n = h0r_ref.shape[2]
        hin_r = jnp.concatenate([h0r_ref[:, g, :] for g in range(gpb)], axis=1)
        hin_i = jnp.concatenate([h0i_ref[:, g, :] for g in range(gpb)], axis=1)
        hr = a8r * hin_r - a8i * hin_i + hl_r
        hi = a8r * hin_i + a8i * hin_r + hl_i
        for g in range(gpb):
            hr_ref[:, g, :] = hr[:, g * p_n:(g + 1) * p_n]
            hi_ref[:, g, :] = hi[:, g * p_n:(g + 1) * p_n]
    hcat = jnp.concatenate([hin_r.astype(BF16), hin_i.astype(BF16)], axis=1)
    d = d_ref[...]
    steps_per_dot = 2
    for t0 in range(0, SSM_CHUNK, steps_per_dot):
        cols = slice(t0 * LANES, (t0 + steps_per_dot) * LANES)
        ks = t0 + steps_per_dot
        k = ks * LANES
        y2 = _dot(ucat[:, :k], tz_ref[:k, cols]) + _dot(hcat, vc_ref[:, cols])
        for t in range(t0, t0 + steps_per_dot):
            y = y2[:, (t - t0) * LANES:(t - t0 + 1) * LANES]
            y_ref[step_rows(t), :] = _gelu_tanh(y + d * u_ref[step_rows(t), :])


GROUPS_PER_BLOCK = LANES // SSM_GROUP


def _spread_groups(x, rows_per_group):
    r, w = x.shape
    wl = GROUPS_PER_BLOCK * w
    src = lax.broadcasted_iota(jnp.int32, (w, wl), 0)
    dst = lax.broadcasted_iota(jnp.int32, (w, wl), 1)
    tiled = _dot(x.astype(BF16), (dst % w == src).astype(BF16))
    row_g = (lax.broadcasted_iota(jnp.int32, (r, wl), 0) // rows_per_group) % GROUPS_PER_BLOCK
    lane_g = lax.broadcasted_iota(jnp.int32, (r, wl), 1) // w
    return jnp.where(row_g == lane_g, tiled, 0.0).astype(BF16)


def _ssm_ops_body(conv_ref, abr_ref, abi_ref, vcr_ref, vci_ref, tz_ref, wb_ref, vc_ref):
    t_n = conv_ref.shape[0]
    sw = vcr_ref.shape[1]
    bd = [_spread_groups(conv_ref[tau], SSM_GROUP) for tau in range(t_n)]
    zero = jnp.zeros_like(bd[0])
    for s in range(t_n):
        rs = slice(s * LANES, (s + 1) * LANES)
        tz_ref[rs, :] = jnp.concatenate([bd[t - s] if t >= s else zero for t in range(t_n)], axis=1)
        wb_ref[rs, :] = jnp.concatenate([_spread_groups(abr_ref[t_n - 1 - s], SSM_GROUP),
                                         _spread_groups(abi_ref[t_n - 1 - s], SSM_GROUP)], axis=1)
    for t in range(t_n):
        cs = slice(t * LANES, (t + 1) * LANES)
        vc_ref[:sw, cs] = _spread_groups(vcr_ref[t], SSM_STATE)
        vc_ref[sw:, cs] = _spread_groups(vci_ref[t], SSM_STATE)


def _ssm_operators(a_re, a_im, log_dt, b_re, b_im, c_re, c_im):
    g_n, p_n = a_re.shape
    lam_re = jnp.minimum(a_re, -1e-4)
    lam_im = a_im
    dt = jnp.exp(log_dt)[:, None]
    decay = jnp.exp(lam_re * dt)
    ar = decay * jnp.cos(lam_im * dt)
    ai = decay * jnp.sin(lam_im * dt)
    num_re = ar - 1.0
    den = lam_re * lam_re + lam_im * lam_im
    f_re = (num_re * lam_re + ai * lam_im) / den
    f_im = (ai * lam_re - num_re * lam_im) / den
    bt_re = jnp.swapaxes(b_re, 1, 2)
    bt_im = jnp.swapaxes(b_im, 1, 2)
    bb_re = f_re[:, None, :] * bt_re - f_im[:, None, :] * bt_im
    bb_im = f_re[:, None, :] * bt_im + f_im[:, None, :] * bt_re
    pw_re, pw_im = [jnp.ones_like(ar)], [jnp.zeros_like(ar)]
    for _ in range(SSM_CHUNK):
        pr, pi = pw_re[-1], pw_im[-1]
        pw_re.append(pr * ar - pi * ai)
        pw_im.append(pr * ai + pi * ar)
    pw_re = jnp.stack(pw_re)
    pw_im = jnp.stack(pw_im)
    pk_re = pw_re[:SSM_CHUNK, :, None, :]
    pk_im = pw_im[:SSM_CHUNK, :, None, :]
    ab_re = pk_re * bb_re - pk_im * bb_im
    ab_im = pk_re * bb_im + pk_im * bb_re
    conv = (jnp.einsum("gcp,tgdp->tgdc", c_re, ab_re, precision=HIGHEST)
            - jnp.einsum("gcp,tgdp->tgdc", c_im, ab_im, precision=HIGHEST))
    gpb = GROUPS_PER_BLOCK
    nb = g_n // gpb
    t_n = SSM_CHUNK
    pr = pw_re[1:, :, :, None]
    pi = pw_im[1:, :, :, None]
    ct_re = jnp.swapaxes(c_re, 1, 2)[None]
    ct_im = jnp.swapaxes(c_im, 1, 2)[None]
    tables = [conv.reshape(t_n, nb, LANES, SSM_GROUP),
              ab_re.reshape(t_n, nb, LANES, p_n), ab_im.reshape(t_n, nb, LANES, p_n),
              (ct_re * pr - ct_im * pi).reshape(t_n, nb, gpb * p_n, SSM_GROUP),
              (-(ct_re * pi + ct_im * pr)).reshape(t_n, nb, gpb * p_n, SSM_GROUP)]
    side = t_n * LANES
    op_spec = pl.BlockSpec((None, side, side), lambda j: (j, 0, 0))
    tz, wb, vc = pl.pallas_call(
        _ssm_ops_body,
        grid=(nb,),
        in_specs=[pl.BlockSpec((t_n, None) + t.shape[2:], lambda j: (0, j, 0, 0)) for t in tables],
        out_specs=[op_spec] * 3,
        out_shape=[jax.ShapeDtypeStruct((nb, side, side), BF16)] * 3,
        compiler_params=_params("parallel"),
        name="s5_chunk_operators",
    )(*tables)
    a8r = pw_re[SSM_CHUNK].reshape(1, g_n * p_n)
    a8i = pw_im[SSM_CHUNK].reshape(1, g_n * p_n)
    return tz, wb, vc, a8r, a8i


def _ssm_scan(u, ops, d_skip, h0_re, h0_im, chunk0, n_chunks, sequential):
    tz, wb, vc, a8r, a8i = ops
    nb = tz.shape[0]
    sw = wb.shape[-1] // 2
    rows = n_chunks * SSM_CHUNK
    cblk = chunk0 // n_chunks
    op3 = lambda a: pl.BlockSpec((None,) + a.shape[1:], lambda j: (j, 0, 0))
    srow = pl.BlockSpec((1, sw), lambda j: (0, j))
    if sequential:
        hspec = srow
        scratch = [pltpu.VMEM((n_chunks, sw), F32)] * 4
    else:
        hspec = pl.BlockSpec((n_chunks, sw // SSM_STATE, SSM_STATE), lambda j: (0, j, 0))
        scratch = []
    return pl.pallas_call(
        functools.partial(_ssm_body, sequential=sequential),
        grid=(nb,),
        in_specs=[pl.BlockSpec((rows, LANES), lambda j: (cblk, j)),
                  op3(tz), op3(wb), op3(vc),
                  srow, srow, pl.BlockSpec((1, LANES), lambda j: (0, j)), hspec, hspec],
        out_specs=[pl.BlockSpec((rows, LANES), lambda j: (0, j)), hspec, hspec],
        out_shape=[jax.ShapeDtypeStruct((rows, u.shape[1]), F32),
                   jax.ShapeDtypeStruct(h0_re.shape, F32), jax.ShapeDtypeStruct(h0_im.shape, F32)],
        scratch_shapes=scratch,
        compiler_params=_params("parallel"),
        name="s5_scan_seq" if sequential else "s5_scan_step",
    )(u, tz, wb, vc, a8r, a8i, d_skip.reshape(1, -1), h0_re, h0_im)


def kernel(x_prompt, x_sample, cache_k, cache_v, state_ssm_re, state_ssm_im, page_table, rel_bias, ln_g, ln_b,
           w_qkv, w_o, w_ssm_in, ssm_a_re, ssm_a_im, ssm_log_dt, ssm_b_re, ssm_b_im, ssm_c_re, ssm_c_im, ssm_d,
           w_glu_v, w_glu_g, w_ff_gate, w_ff_up, w_ff_down, w_router, w_moe_gate, w_moe_up, w_moe_down):
    batch, seq, d = x_prompt.shape
    dec_batch, dec_seq, _ = x_sample.shape
    assert batch == 1 and d == D_MODEL and dec_seq == SSM_CHUNK and seq % MOBA_BLOCK == 0
    m_p = batch * seq
    m_s = dec_batch * dec_seq
    m = m_p + m_s
    dq = N_HEADS * HEAD_DIM
    dkv = N_KV_HEADS * HEAD_DIM
    n_pool, page = cache_k.shape[1], cache_k.shape[2]

    x_p = x_prompt.reshape(m_p, d)
    x_s = x_sample.reshape(m_s, d)

    qkv_f32, qkv_b = _qkv_proj(x_p, x_s, w_qkv[0].astype(BF16), tm=1024, tn=512)
    kmean = _block_means(qkv_f32, seq // MOBA_BLOCK)
    tabs, tab_s = _bias_tables(rel_bias)
    ck = cache_k[0].reshape(n_pool, page * N_KV_HEADS, HEAD_DIM)
    cv = cache_v[0].reshape(n_pool, page * N_KV_HEADS, HEAD_DIM)
    attn_p, attn_s = _attention(page_table, qkv_f32, qkv_b, kmean, tabs, tab_s, ck, cv, seq, dec_seq)
    y = _proj_ln(attn_p, attn_s, [w_o[0].astype(BF16)], [x_p, x_s], ln_g[0, 0], ln_b[0, 0], tm=512, tn=512)
    y = _ffn_dense(y, w_ff_gate, w_ff_up, w_ff_down, ln_g[0, 1], ln_b[0, 1], tm=1024, tf=512)

    k_all = qkv_f32[:, dq:dq + dkv]
    v_all = qkv_f32[:, dq + dkv:]
    k_prompt = k_all[:m_p].reshape(1, batch, seq, N_KV_HEADS, HEAD_DIM)
    v_prompt = v_all[:m_p].reshape(1, batch, seq, N_KV_HEADS, HEAD_DIM)
    k_sample = k_all[m_p:].reshape(1, dec_batch, dec_seq, N_KV_HEADS, HEAD_DIM)
    v_sample = v_all[m_p:].reshape(1, dec_batch, dec_seq, N_KV_HEADS, HEAD_DIM)

    u = _matmul(y, w_ssm_in[0].astype(BF16), tm=1024, tn=1024)
    ops = _ssm_operators(ssm_a_re[0], ssm_a_im[0], ssm_log_dt[0], ssm_b_re[0], ssm_b_im[0],
                         ssm_c_re[0], ssm_c_im[0])
    n_state = N_SSM_GROUPS * SSM_STATE
    zero = jnp.zeros((batch, n_state), F32)
    yg_p, hrp, hip = _ssm_scan(u, ops, ssm_d[0], zero, zero, 0, m_p // SSM_CHUNK, True)
    yg_s, hrs, his = _ssm_scan(u, ops, ssm_d[0], state_ssm_re[0], state_ssm_im[0], m_p // SSM_CHUNK, dec_batch,
                               False)
    y, y_packed = _proj_ln(yg_p, yg_s, [w_glu_v[0].astype(BF16), w_glu_g[0].astype(BF16)],
                           [y], ln_g[1, 0], ln_b[1, 0], tm=512, tn=512, packed=True)

    tm_moe = 1024
    top_idx, top_gate = _router(y, w_router[0], tm=512)
    tile_expert, tile_rows, slot_token, pair_slot = _moe_routing(top_idx, tm_moe)
    xs = y_packed.at[lax.optimization_barrier(slot_token)].get(mode="promise_in_bounds")
    ys = _ffn_moe(tile_expert, tile_rows, xs, w_moe_gate[0], w_moe_up[0], w_moe_down[0], tm_moe, tf=512, half=128)
    pairs = ys.at[pair_slot].get(mode="promise_in_bounds").reshape(TOP_K, m, d // 2)
    out_p, out_s = _combine_ln(y, pairs, top_gate, ln_g[1, 1], ln_b[1, 1], m_p, tm=512)

    return (out_p.reshape(batch, seq, d), out_s.reshape(dec_batch, dec_seq, d),
            k_prompt, v_prompt, k_sample, v_sample,
            hrp.reshape(1, batch, N_SSM_GROUPS, SSM_STATE), hip.reshape(1, batch, N_SSM_GROUPS, SSM_STATE),
            hrs.reshape(1, dec_batch, N_SSM_GROUPS, SSM_STATE), his.reshape(1, dec_batch, N_SSM_GROUPS, SSM_STATE))
```

```python
import functools
import math

import numpy as np
import jax
import jax.numpy as jnp
from jax import lax
from jax.experimental import pallas as pl
from jax.experimental.pallas import tpu as pltpu

D_MODEL = 2048
N_HEADS = 16
HEAD_DIM = D_MODEL // N_HEADS
N_KV_HEADS = 4
HEADS_PER_KV = N_HEADS // N_KV_HEADS
MOBA_BLOCK = 256
MOBA_TOPK = 3
NUM_BUCKETS = 32
MAX_DISTANCE = 128
SSM_GROUP = 16
N_SSM_GROUPS = D_MODEL // SSM_GROUP
SSM_STATE = 64
SSM_CHUNK = 8
D_FF = 7 * D_MODEL // 2
N_EXPERTS = 8
TOP_K = 2
DEPTH = 2
ALPHA = (2 * DEPTH) ** 0.25
LN_EPS = 1e-5
NEG_INF = -1e30

LANES = 128
SUBLANES = 8
VMEM_LIMIT = 62 * 1024 * 1024

F32 = jnp.float32
BF16 = jnp.bfloat16
HIGHEST = lax.Precision.HIGHEST
_NT = (((1,), (1,)), ((), ()))


def _params(*sem, fuse_inputs=None):
    return pltpu.CompilerParams(dimension_semantics=sem, vmem_limit_bytes=VMEM_LIMIT,
                                allow_input_fusion=fuse_inputs)


def _dot(a, b):
    return jnp.dot(a, b, preferred_element_type=F32)


def _split_bf16(x):
    hi = x.astype(BF16)
    return hi, (x - hi.astype(F32)).astype(BF16)


def _dot3(x, w, dims=(((1,), (0,)), ((), ()))):
    x_hi, x_lo = _split_bf16(x)
    w_hi, w_lo = _split_bf16(w)
    mm = lambda a, b: lax.dot_general(a, b, dims, preferred_element_type=F32)
    return mm(x_hi, w_hi) + (mm(x_hi, w_lo) + mm(x_lo, w_hi))


def _sigmoid(x):
    return 1.0 / (1.0 + jnp.exp(-x))


def _layer_norm(y, g, b):
    mean = jnp.mean(y, axis=-1, keepdims=True)
    yc = y - mean
    var = jnp.mean(yc * yc, axis=-1, keepdims=True)
    return yc * lax.rsqrt(var + LN_EPS) * g + b


def _mm_body(x_ref, w_ref, o_ref, xb_scr):
    @pl.when(pl.program_id(1) == 0)
    def _():
        xb_scr[...] = x_ref[...].astype(BF16)

    o_ref[...] = _dot(xb_scr[...], w_ref[...])


def _matmul(x, w, tm, tn):
    m, k = x.shape
    n = w.shape[1]
    return pl.pallas_call(
        _mm_body,
        grid=(m // tm, n // tn),
        in_specs=[pl.BlockSpec((tm, k), lambda i, j: (i, 0)),
                  pl.BlockSpec((k, tn), lambda i, j: (0, j))],
        out_specs=pl.BlockSpec((tm, tn), lambda i, j: (i, j)),
        out_shape=jax.ShapeDtypeStruct((m, n), F32),
        scratch_shapes=[pltpu.VMEM((tm, k), BF16)],
        compiler_params=_params("parallel", "arbitrary", fuse_inputs=[False, True]),
        name="matmul",
    )(x, w)


def _split_rows(block, n_first):
    return [pl.BlockSpec(block, lambda i, j: (jnp.minimum(i, n_first - 1), 0)),
            pl.BlockSpec(block, lambda i, j: (jnp.maximum(i - n_first, 0), 0))]


def _load_split_bf16(dst, src1, src2, n_first):
    i = pl.program_id(0)
    j = pl.program_id(1)

    @pl.when((j == 0) & (i < n_first))
    def _():
        dst[...] = src1[...].astype(BF16)

    @pl.when((j == 0) & (i >= n_first))
    def _():
        dst[...] = src2[...].astype(BF16)


def _qkv_body(x1_ref, x2_ref, w_ref, of_ref, ob_ref, xb_scr, *, nq_tiles, scale, n_first):
    _load_split_bf16(xb_scr, x1_ref, x2_ref, n_first)
    acc = _dot(xb_scr[...], w_ref[...])
    of_ref[...] = acc
    s = jnp.where(pl.program_id(1) < nq_tiles, scale, 1.0).astype(F32)
    ob_ref[...] = (acc * s).astype(BF16)


def _qkv_proj(x1, x2, w, tm, tn):
    k = x1.shape[1]
    m = x1.shape[0] + x2.shape[0]
    n = w.shape[1]
    n_first = x1.shape[0] // tm
    assert x1.shape[0] % tm == 0 and x2.shape[0] % tm == 0
    body = functools.partial(_qkv_body, nq_tiles=(N_HEADS * HEAD_DIM) // tn, scale=HEAD_DIM ** -0.5,
                             n_first=n_first)
    return pl.pallas_call(
        body,
        grid=(m // tm, n // tn),
        in_specs=_split_rows((tm, k), n_first) + [pl.BlockSpec((k, tn), lambda i, j: (0, j))],
        out_specs=[pl.BlockSpec((tm, tn), lambda i, j: (i, j)),
                   pl.BlockSpec((tm, tn), lambda i, j: (i, j))],
        out_shape=[jax.ShapeDtypeStruct((m, n), F32), jax.ShapeDtypeStruct((m, n), BF16)],
        scratch_shapes=[pltpu.VMEM((tm, k), BF16)],
        compiler_params=_params("parallel", "arbitrary", fuse_inputs=[False, False, True]),
        name="qkv_proj",
    )(x1, x2, w)


def _kmean_body(k_ref, o_ref):
    i = pl.program_id(0)

    @pl.when(i == 0)
    def _():
        o_ref[...] = jnp.zeros_like(o_ref)

    mean = jnp.sum(k_ref[...], axis=0, keepdims=True) * (1.0 / MOBA_BLOCK)
    rows = lax.broadcasted_iota(jnp.int32, o_ref.shape, 0)
    o_ref[...] = jnp.where(rows == i, mean, o_ref[...])


def _block_means(qkv_f32, n_blocks):
    dkv = N_KV_HEADS * HEAD_DIM
    kcol = (N_HEADS * HEAD_DIM) // dkv
    return pl.pallas_call(
        _kmean_body,
        grid=(n_blocks,),
        in_specs=[pl.BlockSpec((MOBA_BLOCK, dkv), lambda i: (i, kcol))],
        out_specs=pl.BlockSpec((LANES, dkv), lambda i: (0, 0)),
        out_shape=jax.ShapeDtypeStruct((LANES, dkv), F32),
        compiler_params=_params("arbitrary"),
        name="moba_block_means",
    )(qkv_f32)


def _t5_bucket_np(dist):
    n = np.maximum(dist, 0)
    max_exact = NUM_BUCKETS // 2
    nf = np.maximum(n, 1).astype(np.float32)
    large = max_exact + (np.log(nf / np.float32(max_exact)) / np.float32(math.log(MAX_DISTANCE / max_exact))
                         * np.float32(NUM_BUCKETS - max_exact)).astype(np.int32)
    large = np.minimum(large, NUM_BUCKETS - 1)
    return np.where(n < max_exact, n, large).astype(np.int32)


def _bucket_tables():
    q = np.arange(MOBA_BLOCK)[:, None]
    k = np.arange(MOBA_BLOCK)[None, :]
    own = np.where(q - k >= 0, _t5_bucket_np(q - k), -1)
    prev = _t5_bucket_np(q - k + MOBA_BLOCK)
    return np.stack([own, prev]).astype(np.int32)


def _bias_body(rb_ref, idx_ref, o_ref, os_ref):
    h = pl.program_id(0)
    far = rb_ref[h, NUM_BUCKETS - 1]
    tabs = []
    for m in range(2):
        idx = idx_ref[m]
        acc = jnp.full(idx.shape, NEG_INF, F32)
        for b in range(NUM_BUCKETS):
            acc = jnp.where(idx == b, rb_ref[h, b] - far, acc)
        o_ref[0, m] = acc
        tabs.append(acc)
    os_ref[0] = jnp.concatenate([tabs[1][:SUBLANES, :], tabs[0][:SUBLANES, :LANES]], axis=1)


def _bias_tables(rel_bias):
    idx = jnp.asarray(_bucket_tables())
    return pl.pallas_call(
        _bias_body,
        grid=(N_HEADS,),
        in_specs=[pl.BlockSpec(memory_space=pltpu.SMEM),
                  pl.BlockSpec((2, MOBA_BLOCK, MOBA_BLOCK), lambda h: (0, 0, 0))],
        out_specs=[pl.BlockSpec((1, 2, MOBA_BLOCK, MOBA_BLOCK), lambda h: (h, 0, 0, 0)),
                   pl.BlockSpec((1, SUBLANES, MOBA_BLOCK + LANES), lambda h: (h, 0, 0))],
        out_shape=[jax.ShapeDtypeStruct((N_HEADS, 2, MOBA_BLOCK, MOBA_BLOCK), F32),
                   jax.ShapeDtypeStruct((N_HEADS, SUBLANES, MOBA_BLOCK + LANES), F32)],
        compiler_params=_params("parallel"),
        name="moba_bias_tables",
    )(rel_bias, idx)


def _select_blocks(gate, n_past):
    lane = lax.broadcasted_iota(jnp.int32, gate.shape, 1)
    lane_f = lane.astype(F32)
    g = jnp.where(lane < n_past, gate, NEG_INF)
    sel = jnp.full(gate.shape, NEG_INF, F32)
    for _ in range(MOBA_TOPK):
        mx = jnp.max(g, axis=1, keepdims=True)
        idx = jnp.min(jnp.where(g == mx, lane_f, float(gate.shape[1])), axis=1, keepdims=True)
        pick = lane_f == idx
        sel = jnp.where(pick, 0.0, sel)
        g = jnp.where(pick, -jnp.inf, g)
    return jnp.where(lane < n_past, sel, 0.0)


def _attn_prompt_body(qf_ref, qb_ref, k_ref, v_ref, km_ref, tab_ref, o_ref, qa_scr, m_scr, l_scr, acc_scr):
    i = pl.program_id(1)
    rows = HEADS_PER_KV * MOBA_BLOCK
    km = km_ref[...]
    lane = lax.broadcasted_iota(jnp.int32, (MOBA_BLOCK, LANES), 1)
    dummy = LANES - 1
    for hh in range(HEADS_PER_KV):
        cs = slice(hh * HEAD_DIM, (hh + 1) * HEAD_DIM)
        rs = slice(hh * MOBA_BLOCK, (hh + 1) * MOBA_BLOCK)
        gate = _dot3(qf_ref[:, cs], km, _NT)
        mask = jnp.where(lane == dummy, NEG_INF, _select_blocks(gate, i))
        qa_scr[rs, :HEAD_DIM] = qb_ref[:, cs]
        qa_scr[rs, HEAD_DIM:] = mask.astype(BF16)
    qa = qa_scr[...]

    def scores(j, mask_lane):
        start = pl.multiple_of(j * MOBA_BLOCK, MOBA_BLOCK)
        kj = k_ref[pl.ds(start, MOBA_BLOCK), :]
        vj = v_ref[pl.ds(start, MOBA_BLOCK), :]
        rhs = jnp.concatenate([kj, (lane == mask_lane).astype(BF16)], axis=1)
        return lax.dot_general(qa, rhs, _NT, preferred_element_type=F32), vj

    def softmax_parts(parts, m_new):
        m2 = jnp.concatenate([m_new, m_new], axis=1)
        l_add = None
        acc_add = None
        for s, vj in parts:
            p = jnp.exp(s - m2)
            ls = jnp.sum(p, axis=1, keepdims=True)
            pv = _dot(p.astype(BF16), vj)
            l_add = ls if l_add is None else l_add + ls
            acc_add = pv if acc_add is None else acc_add + pv
        return l_add, acc_add

    has_prev = i >= 1
    s_own, v_own = scores(i, -1)
    s_own = s_own + tab_ref[:, 0].reshape(rows, MOBA_BLOCK)
    s_prev, v_prev = scores(jnp.maximum(i - 1, 0), jnp.where(has_prev, i - 1, dummy))
    s_prev = s_prev + tab_ref[:, 1].reshape(rows, MOBA_BLOCK)
    m0 = jnp.maximum(jnp.max(s_own, axis=1, keepdims=True), jnp.max(s_prev, axis=1, keepdims=True))
    m0 = jnp.broadcast_to(m0, (rows, LANES))
    l0, acc0 = softmax_parts([(s_own, v_own), (s_prev, v_prev)], m0)
    m_scr[...] = m0
    l_scr[...] = jnp.broadcast_to(l0, (rows, LANES))
    acc_scr[...] = acc0

    n_far = jnp.maximum(i - 1, 0)

    def far_pair_update(j0):
        parts = []
        for j in (j0, j0 + 1):
            ok = j < n_far
            parts.append(scores(jnp.where(ok, j, 0), jnp.where(ok, j, dummy)))
        m_prev = m_scr[...]
        m_new = m_prev
        for s, _ in parts:
            m_new = jnp.maximum(m_new, jnp.max(s, axis=1, keepdims=True))
        alpha = jnp.exp(m_prev - m_new)
        l_add, acc_add = softmax_parts(parts, m_new)
        l_scr[...] = alpha * l_scr[...] + l_add
        acc_scr[...] = alpha * acc_scr[...] + acc_add
        m_scr[...] = m_new

    def far_quad(t, carry):
        far_pair_update(4 * t)
        far_pair_update(4 * t + 2)
        return carry

    def far_pair(t, carry):
        far_pair_update(n_quads * 4 + 2 * t)
        return carry

    n_quads = n_far // 4
    lax.fori_loop(0, n_quads, far_quad, 0)
    lax.fori_loop(0, (n_far - n_quads * 4 + 1) // 2, far_pair, 0)

    out = acc_scr[...] / l_scr[...]
    for hh in range(HEADS_PER_KV):
        o_ref[:, hh * HEAD_DIM:(hh + 1) * HEAD_DIM] = out[hh * MOBA_BLOCK:(hh + 1) * MOBA_BLOCK]


def _attn_sample_core(first_step, qkv_ref, kp, vp, tab_ref, o_ref, kall, vall, expand, page, dec_seq):
    n_pages = len(kp)
    past = n_pages * page
    n_past_blocks = past // MOBA_BLOCK
    pages_per_block = MOBA_BLOCK // page
    near0 = past - MOBA_BLOCK
    total = past + LANES
    dq = N_HEADS * HEAD_DIM
    dkv = N_KV_HEADS * HEAD_DIM
    hq = HEADS_PER_KV * dec_seq
    rows = N_HEADS * dec_seq
    pad = jnp.zeros((LANES - dec_seq, HEAD_DIM), F32)

    @pl.when(first_step)
    def _():
        key_blk = lax.broadcasted_iota(jnp.int32, (LANES, total), 1) // MOBA_BLOCK
        expand[...] = (key_blk == lax.broadcasted_iota(jnp.int32, (LANES, total), 0)).astype(BF16)

    kmeans = []
    for g in range(N_KV_HEADS):
        means = []
        for blk in range(n_past_blocks):
            tot = jnp.zeros((1, HEAD_DIM), F32)
            for pp in range(pages_per_block):
                pg = blk * pages_per_block + pp
                kk = kp[pg][pl.ds(g, page, stride=N_KV_HEADS), :]
                vv = vp[pg][pl.ds(g, page, stride=N_KV_HEADS), :]
                tot = tot + jnp.sum(kk, axis=0, keepdims=True)
                kall[g, pg * page:(pg + 1) * page, :] = kk.astype(BF16)
                vall[g, pg * page:(pg + 1) * page, :] = vv.astype(BF16)
            means.append(tot * (1.0 / MOBA_BLOCK))
        kmeans.append(jnp.concatenate(means + [jnp.zeros((LANES - n_past_blocks, HEAD_DIM), F32)], axis=0))
        knew = qkv_ref[:, dq + g * HEAD_DIM:dq + (g + 1) * HEAD_DIM]
        vnew = qkv_ref[:, dq + dkv + g * HEAD_DIM:dq + dkv + (g + 1) * HEAD_DIM]
        kall[g, past:total, :] = jnp.concatenate([knew, pad], axis=0).astype(BF16)
        vall[g, past:total, :] = jnp.concatenate([vnew, pad], axis=0).astype(BF16)

    qs = jnp.concatenate([qkv_ref[:, h * HEAD_DIM:(h + 1) * HEAD_DIM] for h in range(N_HEADS)], axis=0)
    gate_all = _dot3(qs, jnp.concatenate(kmeans, axis=0), _NT)
    row_g = lax.broadcasted_iota(jnp.int32, (rows, LANES), 0) // hq
    gate = gate_all[:, :LANES]
    for g in range(1, N_KV_HEADS):
        gate = jnp.where(row_g == g, gate_all[:, g * LANES:(g + 1) * LANES], gate)
    selm = _select_blocks(gate, n_past_blocks).astype(BF16)

    qb = (qs * (HEAD_DIM ** -0.5)).astype(BF16)
    s = jnp.concatenate([lax.dot_general(qb[g * hq:(g + 1) * hq], kall[g], _NT, preferred_element_type=F32)
                         for g in range(N_KV_HEADS)], axis=0)
    s = s + _dot(selm, expand[...])
    s_far = s[:, :near0]
    s_near = s[:, near0:] + tab_ref[...].reshape(rows, MOBA_BLOCK + LANES)
    m = jnp.maximum(jnp.max(s_far, axis=1, keepdims=True), jnp.max(s_near, axis=1, keepdims=True))
    p_far = jnp.exp(s_far - m)
    p_near = jnp.exp(s_near - m)
    inv_l = 1.0 / (jnp.sum(p_far, axis=1, keepdims=True) + jnp.sum(p_near, axis=1, keepdims=True))
    p_far = p_far.astype(BF16)
    p_near = p_near.astype(BF16)
    for g in range(N_KV_HEADS):
        rs = slice(g * hq, (g + 1) * hq)
        out = (_dot(p_far[rs], vall[g, :near0, :]) + _dot(p_near[rs], vall[g, near0:, :])) * inv_l[rs]
        for hh in range(HEADS_PER_KV):
            c0 = (g * HEADS_PER_KV + hh) * HEAD_DIM
            o_ref[:, c0:c0 + HEAD_DIM] = out[hh * dec_seq:(hh + 1) * dec_seq]


def _attn_body(pt_ref, *refs, n_pages, page, dec_seq):
    del pt_ref
    n_prompt_in = 6
    n_sample_in = 2 + 2 * n_pages
    prompt_in = refs[:n_prompt_in]
    qkv_ref, *pages, tab_ref = refs[n_prompt_in:n_prompt_in + n_sample_in]
    op_ref, os_ref = refs[n_prompt_in + n_sample_in:n_prompt_in + n_sample_in + 2]
    scr = refs[n_prompt_in + n_sample_in + 2:]
    _attn_prompt_body(*prompt_in, op_ref, *scr[:4])
    first_step = (pl.program_id(0) == 0) & (pl.program_id(1) == 0)
    _attn_sample_core(first_step, qkv_ref, pages[:n_pages], pages[n_pages:], tab_ref, os_ref, *scr[4:],
                      page, dec_seq)


def _attention(page_table, qkv_f32, qkv_b, kmean, tabs, tab_s, cache_k, cache_v, seq, dec_seq):
    nblk = seq // MOBA_BLOCK
    assert nblk < LANES - 1
    dec_batch, n_pages = page_table.shape
    assert dec_batch == N_KV_HEADS * nblk
    page = cache_k.shape[1] // N_KV_HEADS
    total = n_pages * page + LANES
    blk0 = seq // dec_seq
    gw = HEADS_PER_KV * HEAD_DIM
    kcol = (N_HEADS * HEAD_DIM) // HEAD_DIM
    vcol = kcol + N_KV_HEADS
    rows = HEADS_PER_KV * MOBA_BLOCK

    def page_spec(p):
        return pl.BlockSpec((None, page * N_KV_HEADS, HEAD_DIM),
                            lambda g, i, pt, p=p: (pt[g * nblk + i, p], 0, 0))

    grid_spec = pltpu.PrefetchScalarGridSpec(
        num_scalar_prefetch=1,
        grid=(N_KV_HEADS, nblk),
        in_specs=([pl.BlockSpec((MOBA_BLOCK, gw), lambda g, i, pt: (i, g)),
                   pl.BlockSpec((MOBA_BLOCK, gw), lambda g, i, pt: (i, g)),
                   pl.BlockSpec((seq, HEAD_DIM), lambda g, i, pt: (0, kcol + g)),
                   pl.BlockSpec((seq, HEAD_DIM), lambda g, i, pt: (0, vcol + g)),
                   pl.BlockSpec((LANES, HEAD_DIM), lambda g, i, pt: (0, g)),
                   pl.BlockSpec((HEADS_PER_KV, 2, MOBA_BLOCK, MOBA_BLOCK), lambda g, i, pt: (g, 0, 0, 0)),
                   pl.BlockSpec((dec_seq, qkv_f32.shape[1]), lambda g, i, pt: (blk0 + g * nblk + i, 0))]
                  + [page_spec(p) for p in range(n_pages)]
                  + [page_spec(p) for p in range(n_pages)]
                  + [pl.BlockSpec(tab_s.shape, lambda g, i, pt: (0, 0, 0))]),
        out_specs=[pl.BlockSpec((MOBA_BLOCK, gw), lambda g, i, pt: (i, g)),
                   pl.BlockSpec((dec_seq, N_HEADS * HEAD_DIM), lambda g, i, pt: (g * nblk + i, 0))],
        scratch_shapes=[pltpu.VMEM((rows, 2 * HEAD_DIM), BF16),
                        pltpu.VMEM((rows, LANES), F32),
                        pltpu.VMEM((rows, LANES), F32),
                        pltpu.VMEM((rows, HEAD_DIM), F32),
                        pltpu.VMEM((N_KV_HEADS, total, HEAD_DIM), BF16),
                        pltpu.VMEM((N_KV_HEADS, total, HEAD_DIM), BF16),
                        pltpu.VMEM((LANES, total), BF16)],
    )
    return pl.pallas_call(
        functools.partial(_attn_body, n_pages=n_pages, page=page, dec_seq=dec_seq),
        grid_spec=grid_spec,
        out_shape=[jax.ShapeDtypeStruct((seq, N_HEADS * HEAD_DIM), F32),
                   jax.ShapeDtypeStruct((dec_batch * dec_seq, N_HEADS * HEAD_DIM), F32)],
        compiler_params=_params("arbitrary", "arbitrary"),
        name="moba_attention",
    )(page_table, qkv_f32, qkv_b, qkv_b, qkv_b, kmean, tabs, qkv_f32,
      *([cache_k] * n_pages), *([cache_v] * n_pages), tab_s)


def _pack_bf16_halves(x):
    c = x.shape[1] // 2
    lo = lax.bitcast_convert_type(x[:, :c].astype(BF16).astype(F32), jnp.uint32)
    hi = lax.bitcast_convert_type(x[:, c:].astype(BF16).astype(F32), jnp.uint32)
    return lax.bitcast_convert_type(hi | (lo >> 16), F32)


def _unpack_bf16_halves(w):
    u = lax.bitcast_convert_type(w, jnp.uint32)
    lo = lax.bitcast_convert_type(u << 16, F32)
    hi = lax.bitcast_convert_type(u & jnp.uint32(0xFFFF0000), F32)
    return jnp.concatenate([lo, hi], axis=1)


def _proj_ln_body(*refs, n_w, n_x, n_tiles, n_first, packed):
    a1_ref, a2_ref = refs[:2]
    w_refs = refs[2:2 + n_w]
    x_refs = refs[2 + n_w:2 + n_w + n_x]
    if packed:
        g_ref, b_ref, o_ref, op_ref, a_scr, z_scr = refs[2 + n_w + n_x:]
    else:
        g_ref, b_ref, o_ref, a_scr, z_scr = refs[2 + n_w + n_x:]
    i = pl.program_id(0)
    j = pl.program_id(1)
    _load_split_bf16(a_scr, a1_ref, a2_ref, n_first)
    a = a_scr[...]
    z = _dot(a, w_refs[0][...])
    if n_w == 2:
        z = z * _sigmoid(_dot(a, w_refs[1][...]))
    z_scr[j] = z

    @pl.when(j == n_tiles - 1)
    def _():
        zfull = jnp.concatenate([z_scr[t] for t in range(n_tiles)], axis=1)
        x = x_refs[0][...] if n_x == 1 else jnp.where(i < n_first, x_refs[0][...], x_refs[1][...])
        out = _layer_norm(ALPHA * x + zfull, g_ref[...], b_ref[...])
        o_ref[...] = out
        if packed:
            op_ref[...] = _pack_bf16_halves(out)


def _proj_ln(a1, a2, ws, xs, g, b, tm, tn, packed=False):
    k = a1.shape[1]
    m = a1.shape[0] + a2.shape[0]
    n = ws[0].shape[1]
    n_first = a1.shape[0] // tm
    assert a1.shape[0] % tm == 0 and a2.shape[0] % tm == 0
    assert len(xs) == 1 or xs[0].shape[0] == a1.shape[0]
    n_tiles = n // tn
    body = functools.partial(_proj_ln_body, n_w=len(ws), n_x=len(xs), n_tiles=n_tiles, n_first=n_first,
                             packed=packed)
    row = pl.BlockSpec((tm, n), lambda i, j: (i, 0))
    vec = pl.BlockSpec((1, n), lambda i, j: (0, 0))
    out_specs, out_shape = row, jax.ShapeDtypeStruct((m, n), F32)
    if packed:
        out_specs = [row, pl.BlockSpec((tm, n // 2), lambda i, j: (i, 0))]
        out_shape = [out_shape, jax.ShapeDtypeStruct((m, n // 2), F32)]
    return pl.pallas_call(
        body,
        grid=(m // tm, n_tiles),
        in_specs=(_split_rows((tm, k), n_first)
                  + [pl.BlockSpec((k, tn), lambda i, j: (0, j)) for _ in ws]
                  + ([row] if len(xs) == 1 else _split_rows((tm, n), n_first))
                  + [vec, vec]),
        out_specs=out_specs,
        out_shape=out_shape,
        scratch_shapes=[pltpu.VMEM((tm, k), BF16), pltpu.VMEM((n_tiles, tm, tn), F32)],
        compiler_params=_params("parallel", "arbitrary",
                                fuse_inputs=[False, False] + [True] * len(ws) + [False] * (len(xs) + 2)),
        name="glu_res_ln" if len(ws) == 2 else "proj_res_ln",
    )(a1, a2, *ws, *xs, g.reshape(1, n), b.reshape(1, n))


def _ffn_body(te_ref, tr_ref, *refs, n_f, dense, half):
    del te_ref
    if dense:
        x_ref, wg_ref, wu_ref, wd_ref, g_ref, b_ref, o_ref, xb = refs
        acc = o_ref
    else:
        x_ref, wg_ref, wu_ref, wd_ref, o_ref, xb, acc = refs
    tm = x_ref.shape[0]
    t = pl.program_id(0)
    f = pl.program_id(1)
    n_rows = tr_ref[t]

    @pl.when(f == 0)
    def _():
        x = x_ref[...] if dense else _unpack_bf16_halves(x_ref[...])
        xb[...] = x.astype(BF16)
        acc[...] = jnp.zeros_like(acc)

    def swiglu_rows(n):
        x = xb[:n, :]
        hg = _dot(x, wg_ref[...].astype(BF16))
        hu = _dot(x, wu_ref[...].astype(BF16))
        h = (hg * _sigmoid(hg)) * hu
        acc[:n, :] += _dot(h.astype(BF16), wd_ref[...].astype(BF16))

    for n in range(half, tm + 1, half):
        pl.when((n_rows > n - half) & (n_rows <= n))(functools.partial(swiglu_rows, n))

    @pl.when(f == n_f - 1)
    def _():
        if dense:
            o_ref[...] = _layer_norm(ALPHA * x_ref[...] + acc[...], g_ref[...], b_ref[...])
        else:
            o_ref[...] = _pack_bf16_halves(acc[...])


def _ffn_call(tile_expert, tile_rows, x, wg, wu, wd, extra, extra_specs, tm, tf, half, dense, name):
    s = x.shape[0]
    d = wg.shape[-2]
    n_t = s // tm
    n_f = wg.shape[-1] // tf
    live = lambda t, tr: jnp.minimum(tr[t], 1)
    x_spec = pl.BlockSpec((tm, x.shape[1]), lambda t, f, te, tr: (t, 0))
    x_in = pl.BlockSpec((tm, x.shape[1]), lambda t, f, te, tr: (t, 0), pipeline_mode=pl.Buffered(1)) if dense else x_spec
    wgu = pl.BlockSpec((None, d, tf), lambda t, f, te, tr: (te[t], 0, f * live(t, tr)))
    wds = pl.BlockSpec((None, tf, d), lambda t, f, te, tr: (te[t], f * live(t, tr), 0))
    grid_spec = pltpu.PrefetchScalarGridSpec(
        num_scalar_prefetch=2, grid=(n_t, n_f),
        in_specs=[x_in, wgu, wgu, wds] + extra_specs,
        out_specs=x_spec,
        scratch_shapes=[pltpu.VMEM((tm, d), BF16)] + ([] if dense else [pltpu.VMEM((tm, d), F32)]))
    return pl.pallas_call(
        functools.partial(_ffn_body, n_f=n_f, dense=dense, half=half),
        grid_spec=grid_spec,
        out_shape=jax.ShapeDtypeStruct(x.shape, x.dtype),
        compiler_params=_params("parallel", "arbitrary"),
        name=name,
    )(tile_expert, tile_rows, x, wg, wu, wd, *extra)


def _ffn_dense(x, wg, wu, wd, g, b, tm, tf):
    m, d = x.shape
    n_t = m // tm
    vec = pl.BlockSpec((1, d), lambda t, f, te, tr: (0, 0))
    return _ffn_call(jnp.zeros((n_t,), jnp.int32), jnp.full((n_t,), tm, jnp.int32), x, wg, wu, wd,
                     [g.reshape(1, d), b.reshape(1, d)], [vec, vec], tm, tf, tm, True, "ffn_res_ln")


def _ffn_moe(tile_expert, tile_rows, xs, wg, wu, wd, tm, tf, half):
    return _ffn_call(tile_expert, tile_rows, xs, wg, wu, wd, [], [], tm, tf, half, False, "moe_ffn")


def _router_body(y_ref, w_ref, i_ref, g_ref):
    logits = _dot3(y_ref[...], w_ref[...])
    lane = lax.broadcasted_iota(jnp.int32, logits.shape, 1)
    lane_f = lane.astype(F32)
    l1 = jnp.where(lane < N_EXPERTS, logits, -jnp.inf)
    m1 = jnp.max(l1, axis=1, keepdims=True)
    i1 = jnp.min(jnp.where(l1 == m1, lane_f, float(LANES)), axis=1, keepdims=True)
    l2 = jnp.where(lane_f == i1, -jnp.inf, l1)
    m2 = jnp.max(l2, axis=1, keepdims=True)
    i2 = jnp.min(jnp.where(l2 == m2, lane_f, float(LANES)), axis=1, keepdims=True)
    e = jnp.exp(m2 - m1)
    g1 = 1.0 / (1.0 + e)
    g2 = e / (1.0 + e)
    i_ref[...] = jnp.where(lane == 0, i1, jnp.where(lane == 1, i2, 0.0)).astype(jnp.int32)
    g_ref[...] = jnp.where(lane == 0, g1, jnp.where(lane == 1, g2, 0.0))


def _router(y, w_router, tm):
    m, d = y.shape
    wr = jnp.zeros((d, LANES), F32).at[:, :N_EXPERTS].set(w_router)
    row = pl.BlockSpec((tm, LANES), lambda i: (i, 0))
    return pl.pallas_call(
        _router_body,
        grid=(m // tm,),
        in_specs=[pl.BlockSpec((tm, d), lambda i: (i, 0)), pl.BlockSpec((d, LANES), lambda i: (0, 0))],
        out_specs=[row, row],
        out_shape=[jax.ShapeDtypeStruct((m, LANES), jnp.int32), jax.ShapeDtypeStruct((m, LANES), F32)],
        compiler_params=_params("parallel"),
        name="moe_router",
    )(y, wr)


def _combine_ln_body(x_ref, a_ref, b2_ref, tg_ref, g_ref, b_ref, op_ref, os_ref, *, n_first):
    i = pl.program_id(0)
    tg = tg_ref[...]
    moe = tg[:, 0:1] * _unpack_bf16_halves(a_ref[...]) + tg[:, 1:2] * _unpack_bf16_halves(b2_ref[...])
    out = _layer_norm(ALPHA * x_ref[...] + moe, g_ref[...], b_ref[...])

    @pl.when(i < n_first)
    def _():
        op_ref[...] = out

    @pl.when(i >= n_first)
    def _():
        os_ref[...] = out


def _combine_ln(x, pairs, top_gate, g, b, m_first, tm):
    m, d = x.shape
    dp = pairs.shape[2]
    n_first = m_first // tm
    vec = pl.BlockSpec((1, d), lambda i: (0, 0))
    return pl.pallas_call(
        functools.partial(_combine_ln_body, n_first=n_first),
        grid=(m // tm,),
        in_specs=[pl.BlockSpec((tm, d), lambda i: (i, 0)),
                  pl.BlockSpec((None, tm, dp), lambda i: (0, i, 0)),
                  pl.BlockSpec((None, tm, dp), lambda i: (1, i, 0)),
                  pl.BlockSpec((tm, LANES), lambda i: (i, 0)),
                  vec, vec],
        out_specs=[pl.BlockSpec((tm, d), lambda i: (jnp.minimum(i, n_first - 1), 0)),
                   pl.BlockSpec((tm, d), lambda i: (jnp.maximum(i - n_first, 0), 0))],
        out_shape=[jax.ShapeDtypeStruct((m_first, d), F32), jax.ShapeDtypeStruct((m - m_first, d), F32)],
        compiler_params=_params("arbitrary"),
        name="moe_combine_ln",
    )(x, pairs, pairs, top_gate, g.reshape(1, d), b.reshape(1, d))


def _moe_routing(top_idx, tm):
    m = top_idx.shape[0]
    n_pairs = m * TOP_K
    n_tiles = n_pairs // tm + N_EXPERTS
    i32 = jnp.int32
    e_flat = jnp.concatenate([top_idx[:, k] for k in range(TOP_K)])
    order = jnp.argsort(e_flat, stable=True).astype(i32)
    inv = jnp.argsort(order).astype(i32)
    counts = jnp.sum((e_flat[None, :] == jnp.arange(N_EXPERTS, dtype=i32)[:, None]).astype(i32), axis=1)
    starts = jnp.cumsum(counts) - counts
    k_tiles = (counts + tm - 1) // tm
    n_full = jnp.maximum(k_tiles - 2, 0)
    rest = counts - n_full * tm
    rows_a = jnp.where(k_tiles >= 2, (rest + 1) // 2, rest)
    rows_b = rest - rows_a
    tile_ends = jnp.cumsum(k_tiles)
    tile_first = tile_ends - k_tiles
    tile = jnp.arange(n_tiles, dtype=i32)
    tile_expert = jnp.minimum(jnp.searchsorted(tile_ends, tile, side="right"), N_EXPERTS - 1).astype(i32)
    tile_j = tile - tile_first[tile_expert]
    nf_t, ra_t, rb_t = n_full[tile_expert], rows_a[tile_expert], rows_b[tile_expert]
    tile_rows = jnp.where(tile_j < nf_t, tm, jnp.where(tile_j == nf_t, ra_t, rb_t))
    tile_rows = jnp.where(tile < tile_ends[-1], tile_rows, 0).astype(i32)
    tile_rank0 = jnp.where(tile_j <= nf_t, tile_j * tm, nf_t * tm + ra_t)
    slot = jnp.arange(n_tiles * tm, dtype=i32)
    slot_u = slot % tm
    slot_e = jnp.repeat(tile_expert, tm)
    live = slot_u < jnp.repeat(tile_rows, tm)
    src = order[jnp.clip(starts[slot_e] + jnp.repeat(tile_rank0, tm) + slot_u, 0, n_pairs - 1)]
    slot_token = jnp.where(live, src % m, slot % m)
    pair_rank = inv - starts[e_flat]
    nf_p, ra_p = n_full[e_flat], rows_a[e_flat]
    tail = pair_rank - nf_p * tm
    in_b = tail >= ra_p
    pair_j = jnp.where(tail < 0, pair_rank // tm, nf_p + in_b.astype(i32))
    pair_u = jnp.where(tail < 0, pair_rank % tm, tail - jnp.where(in_b, ra_p, 0))
    pair_slot = (tile_first[e_flat] + pair_j) * tm + pair_u
    return tile_expert, tile_rows, slot_token, pair_slot


def _gelu_tanh(x):
    return 0.5 * x * (1.0 + jnp.tanh(math.sqrt(2.0 / math.pi) * (x + 0.044715 * (x * x * x))))


def _ssm_body(*refs, sequential):
    (u_ref, tz_ref, wb_ref, vc_ref, a8r_ref, a8i_ref, d_ref, h0r_ref, h0i_ref, y_ref, hr_ref, hi_ref) = refs[:12]
    scr = refs[12:]
    n_chunks = u_ref.shape[0] // SSM_CHUNK
    sw = a8r_ref.shape[1]
    step_rows = lambda s: pl.ds(s, n_chunks, stride=SSM_CHUNK)
    ucat = jnp.concatenate([u_ref[step_rows(s), :].astype(BF16) for s in range(SSM_CHUNK)], axis=1)
    hl = _dot(ucat, wb_ref[...])
    hl_r = hl[:, :sw]
    hl_i = hl[:, sw:]
    a8r = a8r_ref[...]
    a8i = a8i_ref[...]
    if sequential:
        hlr_scr, hli_scr, hinr_scr, hini_scr = scr
        hlr_scr[...] = hl_r
        hli_scr[...] = hl_i

        def tile_step(kb, carry):
            hr, hi = carry
            base = pl.multiple_of(kb * SUBLANES, SUBLANES)
            tr = hlr_scr[pl.ds(base, SUBLANES), :]
            ti = hli_scr[pl.ds(base, SUBLANES), :]
            rows_r, rows_i = [], []
            for r in range(SUBLANES):
                rows_r.append(hr)
                rows_i.append(hi)
                hr, hi = (a8r * hr - a8i * hi + tr[r:r + 1], a8r * hi + a8i * hr + ti[r:r + 1])
            hinr_scr[pl.ds(base, SUBLANES), :] = jnp.concatenate(rows_r, axis=0)
            hini_scr[pl.ds(base, SUBLANES), :] = jnp.concatenate(rows_i, axis=0)
            return hr, hi

        hr, hi = lax.fori_loop(0, n_chunks // SUBLANES, tile_step, (h0r_ref[...], h0i_ref[...]))
        hr_ref[...] = hr
        hi_ref[...] = hi
        hin_r = hinr_scr[...]
        hin_i = hini_scr[...]
    else:
        gpb = h0r_ref.shape[1]
        p_n = h0r_ref.shape[2]
        hin_r = jnp.concatenate([h0r_ref[:, g, :] for g in range(gpb)], axis=1)
        hin_i = jnp.concatenate([h0i_ref[:, g, :] for g in range(gpb)], axis=1)
        hr = a8r * hin_r - a8i * hin_i + hl_r
        hi = a8r * hin_i + a8i * hin_r + hl_i
        for g in range(gpb):
            hr_ref[:, g, :] = hr[:, g * p_n:(g + 1) * p_n]
            hi_ref[:, g, :] = hi[:, g * p_n:(g + 1) * p_n]
    hcat = jnp.concatenate([hin_r.astype(BF16), hin_i.astype(BF16)], axis=1)
    d = d_ref[...]
    steps_per_dot = 2
    for t0 in range(0, SSM_CHUNK, steps_per_dot):
        cols = slice(t0 * LANES, (t0 + steps_per_dot) * LANES)
        ks = t0 + steps_per_dot
        k = ks * LANES
        y2 = _dot(ucat[:, :k], tz_ref[:k, cols]) + _dot(hcat, vc_ref[:, cols])
        for t in range(t0, t0 + steps_per_dot):
            y = y2[:, (t - t0) * LANES:(t - t0 + 1) * LANES]
            y_ref[step_rows(t), :] = _gelu_tanh(y + d * u_ref[step_rows(t), :])


GROUPS_PER_BLOCK = LANES // SSM_GROUP


def _spread_groups(x, rows_per_group):
    r, w = x.shape
    wl = GROUPS_PER_BLOCK * w
    src = lax.broadcasted_iota(jnp.int32, (w, wl), 0)
    dst = lax.broadcasted_iota(jnp.int32, (w, wl), 1)
    tiled = _dot(x.astype(BF16), (dst % w == src).astype(BF16))
    row_g = (lax.broadcasted_iota(jnp.int32, (r, wl), 0) // rows_per_group) % GROUPS_PER_BLOCK
    lane_g = lax.broadcasted_iota(jnp.int32, (r, wl), 1) // w
    return jnp.where(row_g == lane_g, tiled, 0.0).astype(BF16)


def _ssm_ops_body(conv_ref, abr_ref, abi_ref, vcr_ref, vci_ref, tz_ref, wb_ref, vc_ref):
    t_n = conv_ref.shape[0]
    sw = vcr_ref.shape[1]
    bd = [_spread_groups(conv_ref[tau], SSM_GROUP) for tau in range(t_n)]
    zero = jnp.zeros_like(bd[0])
    for s in range(t_n):
        rs = slice(s * LANES, (s + 1) * LANES)
        tz_ref[rs, :] = jnp.concatenate([bd[t - s] if t >= s else zero for t in range(t_n)], axis=1)
        wb_ref[rs, :] = jnp.concatenate([_spread_groups(abr_ref[t_n - 1 - s], SSM_GROUP),
                                         _spread_groups(abi_ref[t_n - 1 - s], SSM_GROUP)], axis=1)
    for t in range(t_n):
        cs = slice(t * LANES, (t + 1) * LANES)
        vc_ref[:sw, cs] = _spread_groups(vcr_ref[t], SSM_STATE)
        vc_ref[sw:, cs] = _spread_groups(vci_ref[t], SSM_STATE)


def _ssm_operators(a_re, a_im, log_dt, b_re, b_im, c_re, c_im):
    g_n, p_n = a_re.shape
    lam_re = jnp.minimum(a_re, -1e-4)
    lam_im = a_im
    dt = jnp.exp(log_dt)[:, None]
    decay = jnp.exp(lam_re * dt)
    ar = decay * jnp.cos(lam_im * dt)
    ai = decay * jnp.sin(lam_im * dt)
    num_re = ar - 1.0
    den = lam_re * lam_re + lam_im * lam_im
    f_re = (num_re * lam_re + ai * lam_im) / den
    f_im = (ai * lam_re - num_re * lam_im) / den
    bt_re = jnp.swapaxes(b_re, 1, 2)
    bt_im = jnp.swapaxes(b_im, 1, 2)
    bb_re = f_re[:, None, :] * bt_re - f_im[:, None, :] * bt_im
    bb_im = f_re[:, None, :] * bt_im + f_im[:, None, :] * bt_re
    pw_re, pw_im = [jnp.ones_like(ar)], [jnp.zeros_like(ar)]
    for _ in range(SSM_CHUNK):
        pr, pi = pw_re[-1], pw_im[-1]
        pw_re.append(pr * ar - pi * ai)
        pw_im.append(pr * ai + pi * ar)
    pw_re = jnp.stack(pw_re)
    pw_im = jnp.stack(pw_im)
    pk_re = pw_re[:SSM_CHUNK, :, None, :]
    pk_im = pw_im[:SSM_CHUNK, :, None, :]
    ab_re = pk_re * bb_re - pk_im * bb_im
    ab_im = pk_re * bb_im + pk_im * bb_re
    conv = (jnp.einsum("gcp,tgdp->tgdc", c_re, ab_re, precision=HIGHEST)
            - jnp.einsum("gcp,tgdp->tgdc", c_im, ab_im, precision=HIGHEST))
    gpb = GROUPS_PER_BLOCK
    nb = g_n // gpb
    t_n = SSM_CHUNK
    pr = pw_re[1:, :, :, None]
    pi = pw_im[1:, :, :, None]
    ct_re = jnp.swapaxes(c_re, 1, 2)[None]
    ct_im = jnp.swapaxes(c_im, 1, 2)[None]
    tables = [conv.reshape(t_n, nb, LANES, SSM_GROUP),
              ab_re.reshape(t_n, nb, LANES, p_n), ab_im.reshape(t_n, nb, LANES, p_n),
              (ct_re * pr - ct_im * pi).reshape(t_n, nb, gpb * p_n, SSM_GROUP),
              (-(ct_re * pi + ct_im * pr)).reshape(t_n, nb, gpb * p_n, SSM_GROUP)]
    side = t_n * LANES
    op_spec = pl.BlockSpec((None, side, side), lambda j: (j, 0, 0))
    tz, wb, vc = pl.pallas_call(
        _ssm_ops_body,
        grid=(nb,),
        in_specs=[pl.BlockSpec((t_n, None) + t.shape[2:], lambda j: (0, j, 0, 0)) for t in tables],
        out_specs=[op_spec] * 3,
        out_shape=[jax.ShapeDtypeStruct((nb, side, side), BF16)] * 3,
        compiler_params=_params("parallel"),
        name="s5_chunk_operators",
    )(*tables)
    a8r = pw_re[SSM_CHUNK].reshape(1, g_n * p_n)
    a8i = pw_im[SSM_CHUNK].reshape(1, g_n * p_n)
    return tz, wb, vc, a8r, a8i


def _ssm_scan(u, ops, d_skip, h0_re, h0_im, chunk0, n_chunks, sequential):
    tz, wb, vc, a8r, a8i = ops
    nb = tz.shape[0]
    sw = wb.shape[-1] // 2
    rows = n_chunks * SSM_CHUNK
    cblk = chunk0 // n_chunks
    op3 = lambda a: pl.BlockSpec((None,) + a.shape[1:], lambda j: (j, 0, 0))
    srow = pl.BlockSpec((1, sw), lambda j: (0, j))
    if sequential:
        hspec = srow
        scratch = [pltpu.VMEM((n_chunks, sw), F32)] * 4
    else:
        hspec = pl.BlockSpec((n_chunks, sw // SSM_STATE, SSM_STATE), lambda j: (0, j, 0))
        scratch = []
    return pl.pallas_call(
        functools.partial(_ssm_body, sequential=sequential),
        grid=(nb,),
        in_specs=[pl.BlockSpec((rows, LANES), lambda j: (cblk, j)),
                  op3(tz), op3(wb), op3(vc),
                  srow, srow, pl.BlockSpec((1, LANES), lambda j: (0, j)), hspec, hspec],
        out_specs=[pl.BlockSpec((rows, LANES), lambda j: (0, j)), hspec, hspec],
        out_shape=[jax.ShapeDtypeStruct((rows, u.shape[1]), F32),
                   jax.ShapeDtypeStruct(h0_re.shape, F32), jax.ShapeDtypeStruct(h0_im.shape, F32)],
        scratch_shapes=scratch,
        compiler_params=_params("parallel"),
        name="s5_scan_seq" if sequential else "s5_scan_step",
    )(u, tz, wb, vc, a8r, a8i, d_skip.reshape(1, -1), h0_re, h0_im)


def kernel(x_prompt, x_sample, cache_k, cache_v, state_ssm_re, state_ssm_im, page_table, rel_bias, ln_g, ln_b,
           w_qkv, w_o, w_ssm_in, ssm_a_re, ssm_a_im, ssm_log_dt, ssm_b_re, ssm_b_im, ssm_c_re, ssm_c_im, ssm_d,
           w_glu_v, w_glu_g, w_ff_gate, w_ff_up, w_ff_down, w_router, w_moe_gate, w_moe_up, w_moe_down):
    batch, seq, d = x_prompt.shape
    dec_batch, dec_seq, _ = x_sample.shape
    assert batch == 1 and d == D_MODEL and dec_seq == SSM_CHUNK and seq % MOBA_BLOCK == 0
    m_p = batch * seq
    m_s = dec_batch * dec_seq
    m = m_p + m_s
    dq = N_HEADS * HEAD_DIM
    dkv = N_KV_HEADS * HEAD_DIM
    n_pool, page = cache_k.shape[1], cache_k.shape[2]

    x_p = x_prompt.reshape(m_p, d)
    x_s = x_sample.reshape(m_s, d)

    qkv_f32, qkv_b = _qkv_proj(x_p, x_s, w_qkv[0].astype(BF16), tm=1024, tn=512)
    kmean = _block_means(qkv_f32, seq // MOBA_BLOCK)
    tabs, tab_s = _bias_tables(rel_bias)
    ck = cache_k[0].reshape(n_pool, page * N_KV_HEADS, HEAD_DIM)
    cv = cache_v[0].reshape(n_pool, page * N_KV_HEADS, HEAD_DIM)
    attn_p, attn_s = _attention(page_table, qkv_f32, qkv_b, kmean, tabs, tab_s, ck, cv, seq, dec_seq)
    y = _proj_ln(attn_p, attn_s, [w_o[0].astype(BF16)], [x_p, x_s], ln_g[0, 0], ln_b[0, 0], tm=512, tn=512)
    y = _ffn_dense(y, w_ff_gate, w_ff_up, w_ff_down, ln_g[0, 1], ln_b[0, 1], tm=1024, tf=512)

    k_all = qkv_f32[:, dq:dq + dkv]
    v_all = qkv_f32[:, dq + dkv:]
    k_prompt = k_all[:m_p].reshape(1, batch, seq, N_KV_HEADS, HEAD_DIM)
    v_prompt = v_all[:m_p].reshape(1, batch, seq, N_KV_HEADS, HEAD_DIM)
    k_sample = k_all[m_p:].reshape(1, dec_batch, dec_seq, N_KV_HEADS, HEAD_DIM)
    v_sample = v_all[m_p:].reshape(1, dec_batch, dec_seq, N_KV_HEADS, HEAD_DIM)

    u = _matmul(y, w_ssm_in[0].astype(BF16), tm=1024, tn=1024)
    ops = _ssm_operators(ssm_a_re[0], ssm_a_im[0], ssm_log_dt[0], ssm_b_re[0], ssm_b_im[0],
                         ssm_c_re[0], ssm_c_im[0])
    n_state = N_SSM_GROUPS * SSM_STATE
    zero = jnp.zeros((batch, n_state), F32)
    yg_p, hrp, hip = _ssm_scan(u, ops, ssm_d[0], zero, zero, 0, m_p // SSM_CHUNK, True)
    yg_s, hrs, his = _ssm_scan(u, ops, ssm_d[0], state_ssm_re[0], state_ssm_im[0], m_p // SSM_CHUNK, dec_batch,
                               False)
    y, y_packed = _proj_ln(yg_p, yg_s, [w_glu_v[0].astype(BF16), w_glu_g[0].astype(BF16)],
                           [y], ln_g[1, 0], ln_b[1, 0], tm=512, tn=512, packed=True)

    tm_moe = 1024
    top_idx, top_gate = _router(y, w_router[0], tm=512)
    tile_expert, tile_rows, slot_token, pair_slot = _moe_routing(top_idx, tm_moe)
    xs = y_packed.at[lax.optimization_barrier(slot_token)].get(mode="promise_in_bounds")
    ys = _ffn_moe(tile_expert, tile_rows, xs, w_moe_gate[0], w_moe_up[0], w_moe_down[0], tm_moe, tf=512, half=128)
    pairs = ys.at[pair_slot].get(mode="promise_in_bounds").reshape(TOP_K, m, d // 2)
    out_p, out_s = _combine_ln(y, pairs, top_gate, ln_g[1, 1], ln_b[1, 1], m_p, tm=512)

    return (out_p.reshape(batch, seq, d), out_s.reshape(dec_batch, dec_seq, d),
            k_prompt, v_prompt, k_sample, v_sample,
            hrp.reshape(1, batch, N_SSM_GROUPS, SSM_STATE), hip.reshape(1, batch, N_SSM_GROUPS, SSM_STATE),
            hrs.reshape(1, dec_batch, N_SSM_GROUPS, SSM_STATE), his.reshape(1, dec_batch, N_SSM_GROUPS, SSM_STATE))
```
